```python
import math
import jax, jax.numpy as jnp
from jax import lax
import numpy as np

D_MODEL = 1024
BATCH = 4
SEQ = 4096
DEPTH = 4
DEC_BATCH = 32
DEC_SEQ = 8
PAST_LEN = 8192
PAGE_SIZE = 128

N_MIXERS = 3
N_RET = (DEPTH + 2) // 3
N_POOL = (DEPTH + 1) // 3
N_FOX = DEPTH // 3
N_DENSE = (DEPTH + 1) // 2
N_MOE = DEPTH // 2

RET_HEADS = 4
RET_DK = D_MODEL // RET_HEADS
RET_DV = 2 * D_MODEL // RET_HEADS
RET_QK = RET_HEADS * RET_DK
RET_V = RET_HEADS * RET_DV
RET_CHUNK = 128
ROPE_BASE = 10000.0

POOL_WINDOWS = (2, 4, 8, 16)
POOL_GROUPS = 4
POOL_GW = D_MODEL // POOL_GROUPS
POOL_BUF = max(POOL_WINDOWS) - 1

FOX_HEADS = 16
FOX_DH = D_MODEL // FOX_HEADS
FOX_QBLOCK = 128
FORGET_BIAS_INIT = 2.0

FFN_DIM = 2816
N_EXPERTS = 8
TOP_K = 2
EPS = 1e-6

kernel_name = 'hybrid_retention_pool_fox_adaln_step'


def rms_norm(x):
    xf = x.astype(jnp.float32)
    return (xf * lax.rsqrt(jnp.mean(xf * xf, axis=-1, keepdims=True) + EPS)).astype(x.dtype)


def modulate(x, shift, scale):
    return rms_norm(x) * (1.0 + scale[:, None, :]) + shift[:, None, :]


def rotary(x, pos):
    d = x.shape[-1]
    inv_freq = ROPE_BASE ** (-jnp.arange(0, d, 2, dtype=jnp.float32) / d)
    ang = pos.astype(jnp.float32)[:, None] * inv_freq[None, :]
    cos = jnp.cos(ang)[None, :, None, :]
    sin = jnp.sin(ang)[None, :, None, :]
    x1, x2 = x[..., 0::2], x[..., 1::2]
    return jnp.stack([x1 * cos - x2 * sin, x1 * sin + x2 * cos], axis=-1).reshape(x.shape)


def retention(h, s0, pos0, w_in, w_out):
    B, L, _ = h.shape
    H, DK, DV = RET_HEADS, RET_DK, RET_DV
    q, k, v, g = jnp.split(h @ w_in, [RET_QK, 2 * RET_QK, 2 * RET_QK + RET_V], axis=-1)
    pos = pos0 + jnp.arange(L)
    q = rotary(q.reshape(B, L, H, DK).astype(jnp.float32), pos)
    k = rotary(k.reshape(B, L, H, DK).astype(jnp.float32), pos) * (DK ** -0.5)
    v = v.reshape(B, L, H, DV).astype(jnp.float32)
    C = math.gcd(L, RET_CHUNK)
    n_chunks = L // C
    log_gamma = jnp.log1p(-(2.0 ** (-5.0 - jnp.arange(H, dtype=jnp.float32))))
    idx = jnp.arange(C, dtype=jnp.float32)
    diff = idx[:, None] - idx[None, :]
    inner_decay = jnp.where(diff >= 0, jnp.exp(log_gamma[:, None, None] * jnp.maximum(diff, 0.0)), 0.0)
    q_decay = jnp.exp(log_gamma[None, :] * (idx[:, None] + 1.0))
    k_decay = jnp.exp(log_gamma[None, :] * (C - 1.0 - idx[:, None]))
    chunk_decay = jnp.exp(log_gamma * C)

    def to_chunks(t):
        return t.reshape(B, n_chunks, C, H, t.shape[-1]).swapaxes(0, 1)

    def step(S, qkv):
        qc, kc, vc = qkv
        att = jnp.einsum('bihd,bjhd->bhij', qc, kc) * inner_decay[None]
        inner = jnp.einsum('bhij,bjhe->bihe', att, vc)
        cross = jnp.einsum('bihd,bhde->bihe', qc, S) * q_decay[None, :, :, None]
        S = S * chunk_decay[None, :, None, None] + jnp.einsum('bjhd,bjhe->bhde', kc * k_decay[None, :, :, None], vc)
        return S, inner + cross

    S, o = lax.scan(step, s0.astype(jnp.float32), (to_chunks(q), to_chunks(k), to_chunks(v)))
    o = rms_norm(o.swapaxes(0, 1).reshape(B, L, H, DV))
    o = jax.nn.silu(g.astype(jnp.float32)) * o.reshape(B, L, RET_V)
    return o.astype(h.dtype) @ w_out, S.astype(s0.dtype)


def pool_mixer(h, buf, pos0, w_group, chan_scale):
    B, L, D = h.shape
    P = POOL_BUF
    ext = jnp.concatenate([buf.astype(jnp.float32), h.astype(jnp.float32)], axis=1)
    cs = jnp.concatenate([jnp.zeros((B, 1, D), jnp.float32), jnp.cumsum(ext, axis=1)], axis=1)
    pos = (pos0 + jnp.arange(L)).astype(jnp.float32)[None, :, None]
    cur = ext[:, P:]
    groups = []
    for gi, w in enumerate(POOL_WINDOWS):
        c0, c1 = gi * POOL_GW, (gi + 1) * POOL_GW
        win = cs[:, P + 1:P + 1 + L, c0:c1] - cs[:, P + 1 - w:P + 1 - w + L, c0:c1]
        mean = win / jnp.minimum(jnp.float32(w), pos + 1.0)
        groups.append(mean - cur[..., c0:c1])
    d = jnp.stack(groups, axis=2).astype(h.dtype)
    y = jnp.einsum('blgc,gce->blge', d, w_group).reshape(B, L, D) * chan_scale
    return y, ext[:, -P:].astype(buf.dtype)


def fox_project(h, w_in, b_f):
    B, L, _ = h.shape
    H, Dh = FOX_HEADS, FOX_DH
    q, k, v, f_logit = jnp.split(h @ w_in, [H * Dh, 2 * H * Dh, 3 * H * Dh], axis=-1)
    logf = jax.nn.log_sigmoid(f_logit.astype(jnp.float32) + b_f.astype(jnp.float32))
    return q.reshape(B, L, H, Dh), k.reshape(B, L, H, Dh), v.reshape(B, L, H, Dh), logf


def fox_attend(q, k, v, Fq, Fk, qpos, kpos):
    s = jnp.einsum('bqhd,bkhd->bhqk', q, k, preferred_element_type=jnp.float32) * (FOX_DH ** -0.5)
    bias = Fq.transpose(0, 2, 1)[:, :, :, None] - Fk.transpose(0, 2, 1)[:, :, None, :]
    mask = kpos[None, :] <= qpos[:, None]
    p = jax.nn.softmax(jnp.where(mask[None, None], s + bias, -jnp.inf), axis=-1)
    return jnp.einsum('bhqk,bkhd->bqhd', p.astype(v.dtype), v)


def fox_prompt(h, w_in, b_f, w_out):
    B, L, D = h.shape
    q, k, v, logf = fox_project(h, w_in, b_f)
    F = jnp.cumsum(logf, axis=1)
    kpos = jnp.arange(L)

    def block(i):
        start = i * FOX_QBLOCK
        qb = lax.dynamic_slice_in_dim(q, start, FOX_QBLOCK, axis=1)
        Fb = lax.dynamic_slice_in_dim(F, start, FOX_QBLOCK, axis=1)
        return fox_attend(qb, k, v, Fb, F, start + jnp.arange(FOX_QBLOCK), kpos)

    o = lax.map(block, jnp.arange(L // FOX_QBLOCK))
    o = o.swapaxes(0, 1).reshape(B, L, D)
    return o @ w_out, k, v, logf.astype(h.dtype)


def fox_sample(h, cache_k, cache_v, cache_logf, page_table, w_in, b_f, w_out):
    B, L, D = h.shape
    q, k, v, logf = fox_project(h, w_in, b_f)
    n_past = page_table.shape[1] * PAGE_SIZE
    kp = cache_k[page_table].reshape(B, n_past, FOX_HEADS, FOX_DH)
    vp = cache_v[page_table].reshape(B, n_past, FOX_HEADS, FOX_DH)
    lp = cache_logf[page_table].reshape(B, n_past, FOX_HEADS).astype(jnp.float32)
    k_all = jnp.concatenate([kp, k.astype(kp.dtype)], axis=1)
    v_all = jnp.concatenate([vp, v.astype(vp.dtype)], axis=1)
    F_all = jnp.cumsum(jnp.concatenate([lp, logf], axis=1), axis=1)
    o = fox_attend(q.astype(kp.dtype), k_all, v_all, F_all[:, n_past:], F_all,
                   n_past + jnp.arange(L), jnp.arange(n_past + L))
    return o.reshape(B, L, D).astype(h.dtype) @ w_out, k, v, logf.astype(h.dtype)


def swiglu(h, w_gu, w_down):
    a, b = jnp.split(h @ w_gu, 2, axis=-1)
    return (jax.nn.silu(a) * b) @ w_down


def moe(h, w_router, b_router, w_gu, w_down):
    logits = (h @ w_router).astype(jnp.float32) + b_router.astype(jnp.float32)
    top_v, top_i = lax.top_k(logits, TOP_K)
    top_w = jax.nn.softmax(top_v, axis=-1)
    gates = jnp.sum(jax.nn.one_hot(top_i, N_EXPERTS, dtype=jnp.float32) * top_w[..., None], axis=-2)
    out = jnp.zeros_like(h)
    for e in range(N_EXPERTS):
        out = out + gates[..., e:e + 1].astype(h.dtype) * swiglu(h, w_gu[e], w_down[e])
    return out


def trunk(x, c, ret_state, pool_state, fox_past, pos0, params):
    (ada_w, ada_b, ret_w_in, ret_w_out, pool_w, pool_scale, fox_w_in, fox_b_f, fox_w_out,
     ffn_w_gu, ffn_w_down, moe_w_router, moe_b_router, moe_w_gu, moe_w_down, final_g) = params
    cond = jax.nn.silu(c)
    ret_new, pool_new, k_new, v_new, l_new = [], [], [], [], []
    for i in range(DEPTH):
        mod = cond @ ada_w[i] + ada_b[i]
        sh_a, sc_a, g_a, sh_f, sc_f, g_f = jnp.split(mod, 6, axis=-1)
        h = modulate(x, sh_a, sc_a)
        kind, j = i % N_MIXERS, i // N_MIXERS
        if kind == 0:
            y, s = retention(h, ret_state[j], pos0, ret_w_in[j], ret_w_out[j])
            ret_new.append(s)
        elif kind == 1:
            y, b = pool_mixer(h, pool_state[j], pos0, pool_w[j], pool_scale[j])
            pool_new.append(b)
        else:
            if fox_past is None:
                y, k, v, lf = fox_prompt(h, fox_w_in[j], fox_b_f[j], fox_w_out[j])
            else:
                ck, cv, cl, pt = fox_past
                y, k, v, lf = fox_sample(h, ck[j], cv[j], cl[j], pt, fox_w_in[j], fox_b_f[j], fox_w_out[j])
            k_new.append(k)
            v_new.append(v)
            l_new.append(lf)
        x = x + g_a[:, None, :] * y
        h = modulate(x, sh_f, sc_f)
        m = i // 2
        if i % 2 == 0:
            y = swiglu(h, ffn_w_gu[m], ffn_w_down[m])
        else:
            y = moe(h, moe_w_router[m], moe_b_router[m], moe_w_gu[m], moe_w_down[m])
        x = x + g_f[:, None, :] * y
    out = rms_norm(x) * final_g
    return (out, jnp.stack(ret_new), jnp.stack(pool_new), jnp.stack(k_new), jnp.stack(v_new), jnp.stack(l_new))


def setup_inputs(seed: int = 0) -> dict:
    key = jax.random.key(seed)
    ks = jax.random.split(key, 32)
    D = D_MODEL
    n_pages = PAST_LEN // PAGE_SIZE
    n_used = DEC_BATCH * n_pages
    n_phys = n_used + max(1, n_used // 4)

    def nrm(k, shape, scale):
        return jax.random.normal(k, shape, jnp.float32) * scale

    perm = jax.random.permutation(ks[0], n_phys)
    page_table = perm[:n_used].reshape(DEC_BATCH, n_pages).astype(jnp.int32)
    return {
        'x_prompt': nrm(ks[1], (BATCH, SEQ, D), 1.0),
        'x_sample': nrm(ks[2], (DEC_BATCH, DEC_SEQ, D), 1.0),
        'state_ret': nrm(ks[3], (N_RET, DEC_BATCH, RET_HEADS, RET_DK, RET_DV), 0.3),
        'state_pool': nrm(ks[4], (N_POOL, DEC_BATCH, POOL_BUF, D), 1.0),
        'cache_fox_k': nrm(ks[5], (N_FOX, n_phys, PAGE_SIZE, FOX_HEADS, FOX_DH), 1.0),
        'cache_fox_v': nrm(ks[6], (N_FOX, n_phys, PAGE_SIZE, FOX_HEADS, FOX_DH), 1.0),
        'cache_fox_logf': jax.nn.log_sigmoid(FORGET_BIAS_INIT + nrm(ks[7], (N_FOX, n_phys, PAGE_SIZE, FOX_HEADS), 1.0)),
        'page_table': page_table,
        'c_prompt': nrm(ks[8], (BATCH, D), 1.0),
        'c_sample': nrm(ks[9], (DEC_BATCH, D), 1.0),
        'ada_w': nrm(ks[10], (DEPTH, D, 6 * D), 0.5 * D ** -0.5),
        'ada_b': nrm(ks[11], (DEPTH, 6 * D), 0.02),
        'ret_w_in': nrm(ks[12], (N_RET, D, 2 * RET_QK + 2 * RET_V), D ** -0.5),
        'ret_w_out': nrm(ks[13], (N_RET, RET_V, D), RET_V ** -0.5),
        'pool_w': nrm(ks[14], (N_POOL, POOL_GROUPS, POOL_GW, POOL_GW), POOL_GW ** -0.5),
        'pool_scale': 1.0 + nrm(ks[15], (N_POOL, D), 0.02),
        'fox_w_in': nrm(ks[16], (N_FOX, D, 3 * FOX_HEADS * FOX_DH + FOX_HEADS), D ** -0.5),
        'fox_b_f': FORGET_BIAS_INIT + nrm(ks[17], (N_FOX, FOX_HEADS), 0.1),
        'fox_w_out': nrm(ks[18], (N_FOX, FOX_HEADS * FOX_DH, D), D ** -0.5),
        'ffn_w_gu': nrm(ks[19], (N_DENSE, D, 2 * FFN_DIM), D ** -0.5),
        'ffn_w_down': nrm(ks[20], (N_DENSE, FFN_DIM, D), FFN_DIM ** -0.5),
        'moe_w_router': nrm(ks[21], (N_MOE, D, N_EXPERTS), D ** -0.5),
        'moe_b_router': nrm(ks[22], (N_MOE, N_EXPERTS), 0.01),
        'moe_w_gu': nrm(ks[23], (N_MOE, N_EXPERTS, D, 2 * FFN_DIM), D ** -0.5),
        'moe_w_down': nrm(ks[24], (N_MOE, N_EXPERTS, FFN_DIM, D), FFN_DIM ** -0.5),
        'final_g': 1.0 + nrm(ks[25], (D,), 0.02),
    }


def reference(x_prompt, x_sample, state_ret, state_pool, cache_fox_k, cache_fox_v, cache_fox_logf,
              page_table, c_prompt, c_sample, ada_w, ada_b, ret_w_in, ret_w_out, pool_w, pool_scale,
              fox_w_in, fox_b_f, fox_w_out, ffn_w_gu, ffn_w_down, moe_w_router, moe_b_router,
              moe_w_gu, moe_w_down, final_g):
    params = (ada_w, ada_b, ret_w_in, ret_w_out, pool_w, pool_scale, fox_w_in, fox_b_f, fox_w_out,
              ffn_w_gu, ffn_w_down, moe_w_router, moe_b_router, moe_w_gu, moe_w_down, final_g)
    bp = x_prompt.shape[0]
    ret0 = jnp.zeros((N_RET, bp, RET_HEADS, RET_DK, RET_DV), x_prompt.dtype)
    pool0 = jnp.zeros((N_POOL, bp, POOL_BUF, D_MODEL), x_prompt.dtype)
    y_prompt, ret_p, pool_p, k_p, v_p, l_p = trunk(x_prompt, c_prompt, ret0, pool0, None, 0, params)
    n_past = page_table.shape[1] * PAGE_SIZE
    y_sample, ret_s, pool_s, k_s, v_s, l_s = trunk(
        x_sample, c_sample, state_ret, state_pool,
        (cache_fox_k, cache_fox_v, cache_fox_logf, page_table), n_past, params)
    return (y_prompt, y_sample, ret_p, ret_s, pool_p, pool_s, k_p, k_s, v_p, v_s, l_p, l_s)
```

```python
import functools
import math

import jax
import jax.numpy as jnp
from jax import lax
from jax.experimental import pallas as pl
from jax.experimental.pallas import tpu as pltpu

F32 = jnp.float32
BF16 = jnp.bfloat16

D_MODEL = 1024
DEPTH = 4
PAGE_SIZE = 128
N_MIXERS = 3
RET_HEADS = 4
RET_DK = D_MODEL // RET_HEADS
RET_DV = 2 * D_MODEL // RET_HEADS
RET_QK = RET_HEADS * RET_DK
RET_V = RET_HEADS * RET_DV
ROPE_BASE = 10000.0
POOL_WINDOWS = (2, 4, 8, 16)
POOL_GW = D_MODEL // len(POOL_WINDOWS)
POOL_BUF = max(POOL_WINDOWS) - 1
POOL_HALO = POOL_BUF + 1
FOX_HEADS = 16
FOX_DH = D_MODEL // FOX_HEADS
FFN_DIM = 2816
N_EXPERTS = 8
EPS = 1e-6
NEG_BIG = -1e30

V7X_VMEM_BYTES = 64 * 1024 * 1024
VMEM_LIMIT = V7X_VMEM_BYTES - 8 * 1024 * 1024
LANES = 128
FFN_TF = 256


def _cparams(n_axes):
    return pltpu.CompilerParams(dimension_semantics=("arbitrary",) * n_axes, vmem_limit_bytes=VMEM_LIMIT)


def _sigmoid(x):
    return 1.0 / (1.0 + jnp.exp(-x))


def _modulate(x, shift, scale):
    xf = x.astype(F32)
    ms = jnp.mean(xf * xf, axis=-1, keepdims=True)
    return (xf * lax.rsqrt(ms + EPS)) * (1.0 + scale) + shift


def _row_spec(arr, tm, width, col_fn):
    if arr.shape[1] == 1:
        return pl.BlockSpec((1, 1, width), lambda b, m, *r: (b, 0, col_fn(*r)))
    return pl.BlockSpec((1, tm, width), lambda b, m, *r: (b, m, col_fn(*r)))


def _ada_body(c_ref, w_ref, b_ref, o_ref):
    c = c_ref[...]
    cond = c * _sigmoid(c)
    o_ref[0] = jnp.dot(cond.astype(BF16), w_ref[0].astype(BF16), preferred_element_type=F32) + b_ref[0]


def ada_mods(c_all, ada_w, ada_b):
    rows = c_all.shape[0]
    n_out = ada_w.shape[2]
    tn = 1024
    return pl.pallas_call(
        _ada_body,
        grid=(DEPTH, n_out // tn),
        in_specs=[
            pl.BlockSpec((rows, D_MODEL), lambda i, n: (0, 0)),
            pl.BlockSpec((1, D_MODEL, tn), lambda i, n: (i, 0, n)),
            pl.BlockSpec((1, 1, tn), lambda i, n: (i, 0, n)),
        ],
        out_specs=pl.BlockSpec((1, rows, tn), lambda i, n: (i, 0, n)),
        out_shape=jax.ShapeDtypeStruct((DEPTH, rows, n_out), F32),
        compiler_params=_cparams(2),
        name="ada_mods",
    )(c_all, ada_w, ada_b.reshape(DEPTH, 1, n_out))


def _mm_body(*refs, has_mod, epi, w_t, out_t, scale):
    it = iter(refs)
    x_ref = next(it)
    if has_mod:
        sh_ref, sc_ref = next(it), next(it)
    w_ref = next(it)
    if epi == "rot":
        cos_ref, sin_ref = next(it), next(it)
    elif epi == "res":
        res_ref, gate_ref = next(it), next(it)
    elif epi == "logsig":
        b_ref = next(it)
    o_ref = next(it)
    n = pl.program_id(2)
    if has_mod:
        h_scr = next(it)

        @pl.when(n == 0)
        def _():
            h_scr[...] = _modulate(x_ref[0], sh_ref[0], sc_ref[0]).astype(BF16)

        lhs = h_scr[...]
    else:
        lhs = x_ref[0].astype(BF16)
    w = w_ref[0].astype(BF16)
    last = (((1,), (1,)), ((), ()))
    if not w_t:
        acc = jnp.dot(lhs, w, preferred_element_type=F32)
    elif not out_t:
        acc = lax.dot_general(lhs, w, last, preferred_element_type=F32)
    else:
        acc = lax.dot_general(w, lhs, last, preferred_element_type=F32)
    if epi == "rot":
        width = acc.shape[1]
        lane = lax.broadcasted_iota(jnp.int32, acc.shape, 1)
        nxt = pltpu.roll(acc, width - 1, axis=1)
        prv = pltpu.roll(acc, 1, axis=1)
        partner = jnp.where(lane % 2 == 0, nxt, prv)
        acc = acc * cos_ref[...] + partner * sin_ref[...]
    elif epi == "res":
        acc = res_ref[0] + gate_ref[0] * acc
    elif epi == "logsig":
        z = acc + b_ref[...]
        acc = jnp.minimum(z, 0.0) - jnp.log1p(jnp.exp(-jnp.abs(z)))
    if scale != 1.0:
        acc = acc * scale
    o_ref[0] = acc.astype(o_ref.dtype)


def mm(x3, w3, wl, *, n0, n_out, out_dtype, mod=None, epi="plain", w_t=False, out_t=False, scale=1.0,
       rot=None, res=None, bias=None, tm=None, tn=None, name="mm"):
    bx, L, K = x3.shape
    tm = tm or min(L, 512 if epi == "rot" else 1024)
    tn = tn or min(n_out, 512 if K > 1024 else 1024)
    assert L % tm == 0 and n_out % tn == 0 and n0 % tn == 0
    nb0 = n0 // tn
    has_mod = mod is not None
    in_specs = [pl.BlockSpec((1, tm, K), lambda b, m, n: (b, m, 0))]
    args = [x3]
    if has_mod:
        for a in mod:
            in_specs.append(_row_spec(a, tm, K, lambda n: 0))
            args.append(a)
    if w_t:
        in_specs.append(pl.BlockSpec((1, tn, K), lambda b, m, n: (wl, nb0 + n, 0)))
    else:
        in_specs.append(pl.BlockSpec((1, K, tn), lambda b, m, n: (wl, 0, nb0 + n)))
    args.append(w3)
    if epi == "rot":
        for a in rot:
            in_specs.append(pl.BlockSpec((tm, tn), lambda b, m, n: (m, n)))
            args.append(a)
    elif epi == "res":
        in_specs.append(pl.BlockSpec((1, tm, tn), lambda b, m, n: (b, m, n)))
        in_specs.append(_row_spec(res[1], tm, tn, lambda n: n))
        args.extend(res)
    elif epi == "logsig":
        in_specs.append(pl.BlockSpec(bias.shape, lambda b, m, n: (0, 0)))
        args.append(bias)
    if out_t:
        out_spec = pl.BlockSpec((1, tn, tm), lambda b, m, n: (b, n, m))
        out_shape = jax.ShapeDtypeStruct((bx, n_out, L), out_dtype)
    else:
        out_spec = pl.BlockSpec((1, tm, tn), lambda b, m, n: (b, m, n))
        out_shape = jax.ShapeDtypeStruct((bx, L, n_out), out_dtype)
    return pl.pallas_call(
        functools.partial(_mm_body, has_mod=has_mod, epi=epi, w_t=w_t, out_t=out_t, scale=scale),
        grid=(bx, L // tm, n_out // tn),
        in_specs=in_specs,
        out_specs=out_spec,
        out_shape=out_shape,
        scratch_shapes=[pltpu.VMEM((tm, K), BF16)] if has_mod else [],
        compiler_params=_cparams(3),
        name=name,
    )(*args)


def _ffn_body(*refs, moe, nf, ne):
    it = iter(refs)
    x_ref, sh_ref, sc_ref, gate_ref = next(it), next(it), next(it), next(it)
    gw_ref = next(it) if moe else None
    wg_ref, wu_ref, wd_ref, o_ref, h_scr, acc_scr = next(it), next(it), next(it), next(it), next(it), next(it)
    tot_scr = next(it) if moe else None
    e = pl.program_id(2)
    f = pl.program_id(3)

    @pl.when((e == 0) & (f == 0))
    def _():
        h_scr[...] = _modulate(x_ref[0], sh_ref[0], sc_ref[0]).astype(BF16)

    h = h_scr[...]
    a = jnp.dot(h, wg_ref[0].astype(BF16), preferred_element_type=F32)
    b = jnp.dot(h, wu_ref[0].astype(BF16), preferred_element_type=F32)
    mid = (a * _sigmoid(a) * b).astype(BF16)
    y = jnp.dot(mid, wd_ref[0].astype(BF16), preferred_element_type=F32)

    @pl.when(f == 0)
    def _():
        acc_scr[...] = y

    @pl.when(f > 0)
    def _():
        acc_scr[...] += y

    if not moe:

        @pl.when(f == nf - 1)
        def _():
            o_ref[0] = x_ref[0] + gate_ref[0] * acc_scr[...]

    else:

        @pl.when(f == nf - 1)
        def _():
            gw = gw_ref[0]
            lane = lax.broadcasted_iota(jnp.int32, gw.shape, 1)
            col = jnp.sum(jnp.where(lane == e, gw, 0.0), axis=1, keepdims=True)
            contrib = col * acc_scr[...]

            @pl.when(e == 0)
            def _():
                tot_scr[...] = contrib

            @pl.when(e > 0)
            def _():
                tot_scr[...] += contrib

            @pl.when(e == ne - 1)
            def _():
                o_ref[0] = x_ref[0] + gate_ref[0] * tot_scr[...]


def ffn(x3, mod, gate, w_gu3, w_down3, wl, *, gates=None, tm=None, name="ffn"):
    bx, L, D = x3.shape
    moe = gates is not None
    ne = N_EXPERTS if moe else 1
    tm = tm or min(L, 1024)
    tf = FFN_TF
    nf = FFN_DIM // tf
    assert L % tm == 0 and FFN_DIM % tf == 0
    in_specs = [pl.BlockSpec((1, tm, D), lambda b, m, e, f: (b, m, 0))]
    args = [x3]
    for a in (*mod, gate):
        in_specs.append(_row_spec(a, tm, D, lambda e, f: 0))
        args.append(a)
    if moe:
        in_specs.append(pl.BlockSpec((1, tm, LANES), lambda b, m, e, f: (b, m, 0)))
        args.append(gates)
    in_specs += [
        pl.BlockSpec((1, D, tf), lambda b, m, e, f: (wl + e, 0, f)),
        pl.BlockSpec((1, D, tf), lambda b, m, e, f: (wl + e, 0, nf + f)),
        pl.BlockSpec((1, tf, D), lambda b, m, e, f: (wl + e, f, 0)),
    ]
    args += [w_gu3, w_gu3, w_down3]
    scratch = [pltpu.VMEM((tm, D), BF16), pltpu.VMEM((tm, D), F32)]
    if moe:
        scratch.append(pltpu.VMEM((tm, D), F32))
    return pl.pallas_call(
        functools.partial(_ffn_body, moe=moe, nf=nf, ne=ne),
        grid=(bx, L // tm, ne, nf),
        in_specs=in_specs,
        out_specs=pl.BlockSpec((1, tm, D), lambda b, m, e, f: (b, m, 0)),
        out_shape=jax.ShapeDtypeStruct((bx, L, D), F32),
        scratch_shapes=scratch,
        compiler_params=_cparams(4),
        name=name,
    )(*args)


def _router_body(x_ref, sh_ref, sc_ref, w_ref, b_ref, o_ref):
    h = _modulate(x_ref[0], sh_ref[0], sc_ref[0]).astype(BF16)
    logits = lax.dot_general(h, w_ref[0].astype(BF16), (((1,), (1,)), ((), ())), preferred_element_type=F32)
    logits = logits + b_ref[0]
    lane = lax.broadcasted_iota(jnp.int32, logits.shape, 1).astype(F32)
    lg = jnp.where(lane < N_EXPERTS, logits, -jnp.inf)
    m1 = jnp.max(lg, axis=1, keepdims=True)
    i1 = jnp.min(jnp.where(lg == m1, lane, float(LANES)), axis=1, keepdims=True)
    lg2 = jnp.where(lane == i1, -jnp.inf, lg)
    m2 = jnp.max(lg2, axis=1, keepdims=True)
    i2 = jnp.min(jnp.where(lg2 == m2, lane, float(LANES)), axis=1, keepdims=True)
    e2 = jnp.exp(m2 - m1)
    den = 1.0 + e2
    o_ref[0] = jnp.where(lane == i1, 1.0 / den, 0.0) + jnp.where(lane == i2, e2 / den, 0.0)


def router(x3, mod, w_rt_pad, b_pad, wl, *, tm=None):
    bx, L, D = x3.shape
    tm = tm or min(L, 1024)
    in_specs = [pl.BlockSpec((1, tm, D), lambda b, m: (b, m, 0))]
    args = [x3]
    for a in mod:
        in_specs.append(_row_spec(a, tm, D, lambda: 0))
        args.append(a)
    in_specs += [
        pl.BlockSpec((1, LANES, D), lambda b, m: (wl, 0, 0)),
        pl.BlockSpec((1, 1, LANES), lambda b, m: (wl, 0, 0)),
    ]
    args += [w_rt_pad, b_pad]
    return pl.pallas_call(
        _router_body,
        grid=(bx, L // tm),
        in_specs=in_specs,
        out_specs=pl.BlockSpec((1, tm, LANES), lambda b, m: (b, m, 0)),
        out_shape=jax.ShapeDtypeStruct((bx, L, LANES), F32),
        compiler_params=_cparams(2),
        name="router",
    )(*args)


def _ret_body(*refs, zero_init, nc):
    it = iter(refs)
    q_ref, k_ref, v_ref, g_ref = next(it), next(it), next(it), next(it)
    s0_ref = None if zero_init else next(it)
    inner_ref, qd_ref, kd_ref, cd_ref = next(it), next(it), next(it), next(it)
    o_ref, sout_ref, s_scr = next(it), next(it), next(it)
    c = pl.program_id(2)

    @pl.when(c == 0)
    def _():
        if zero_init:
            s_scr[...] = jnp.zeros_like(s_scr)
        else:
            s_scr[...] = s0_ref[0, 0, 0]

    q = q_ref[0]
    k = k_ref[0]
    v = v_ref[0]
    s = s_scr[...]
    att = lax.dot_general(q, k, (((1,), (1,)), ((), ())), preferred_element_type=F32) * inner_ref[0]
    inner = jnp.dot(att.astype(BF16), v, preferred_element_type=F32)
    cross = jnp.dot(q, s.astype(BF16), preferred_element_type=F32) * qd_ref[0]
    kdt = (k.astype(F32) * kd_ref[0]).T.astype(BF16)
    s_new = s * cd_ref[0] + jnp.dot(kdt, v, preferred_element_type=F32)
    s_scr[...] = s_new
    o = inner + cross
    on = o * lax.rsqrt(jnp.mean(o * o, axis=-1, keepdims=True) + EPS)
    g = g_ref[0]
    o_ref[0] = (g * _sigmoid(g) * on).astype(o_ref.dtype)

    @pl.when(c == nc - 1)
    def _():
        sout_ref[0, 0] = s_new


def retention_scan(q, k, v, g, tables, chunk, *, s0=None, s0_layer=0):
    B, L, _ = q.shape
    nc = L // chunk
    inner, qd, kd, cd = tables
    zero_init = s0 is None
    in_specs = [
        pl.BlockSpec((1, chunk, RET_DK), lambda b, h, c: (b, c, h)),
        pl.BlockSpec((1, chunk, RET_DK), lambda b, h, c: (b, c, h)),
        pl.BlockSpec((1, chunk, RET_DV), lambda b, h, c: (b, c, h)),
        pl.BlockSpec((1, chunk, RET_DV), lambda b, h, c: (b, c, h)),
    ]
    args = [q, k, v, g]
    if not zero_init:
        in_specs.append(pl.BlockSpec((1, 1, 1, RET_DK, RET_DV), lambda b, h, c: (s0_layer, b, h, 0, 0)))
        args.append(s0)
    in_specs += [
        pl.BlockSpec((1, chunk, chunk), lambda b, h, c: (h, 0, 0)),
        pl.BlockSpec((1, chunk, RET_DV), lambda b, h, c: (h, 0, 0)),
        pl.BlockSpec((1, chunk, RET_DK), lambda b, h, c: (h, 0, 0)),
        pl.BlockSpec((1, 1, RET_DV), lambda b, h, c: (h, 0, 0)),
    ]
    args += [inner, qd, kd, cd]
    return pl.pallas_call(
        functools.partial(_ret_body, zero_init=zero_init, nc=nc),
        grid=(B, RET_HEADS, nc),
        in_specs=in_specs,
        out_specs=[
            pl.BlockSpec((1, chunk, RET_DV), lambda b, h, c: (b, c, h)),
            pl.BlockSpec((1, 1, RET_DK, RET_DV), lambda b, h, c: (b, h, 0, 0)),
        ],
        out_shape=[
            jax.ShapeDtypeStruct((B, L, RET_V), BF16),
            jax.ShapeDtypeStruct((B, RET_HEADS, RET_DK, RET_DV), F32),
        ],
        scratch_shapes=[pltpu.VMEM((RET_DK, RET_DV), F32)],
        compiler_params=_cparams(3),
        name="retention_scan",
    )(*args)


def retention_tables(n_real, n_pad):
    log_gamma = jnp.log1p(-(2.0 ** (-5.0 - jnp.arange(RET_HEADS, dtype=F32))))
    idx = jnp.arange(n_pad, dtype=F32)
    valid = idx < n_real
    diff = idx[:, None] - idx[None, :]
    ok = (diff >= 0) & valid[:, None] & valid[None, :]
    inner = jnp.where(ok[None], jnp.exp(log_gamma[:, None, None] * jnp.maximum(diff, 0.0)[None]), 0.0)
    qd = jnp.exp(log_gamma[:, None] * (idx[None, :] + 1.0))
    kd = jnp.where(valid[None, :], jnp.exp(log_gamma[:, None] * (n_real - 1.0 - idx[None, :])), 0.0)
    cd = jnp.exp(log_gamma * n_real)
    return (inner,
            jnp.broadcast_to(qd[:, :, None], (RET_HEADS, n_pad, RET_DV)),
            jnp.broadcast_to(kd[:, :, None], (RET_HEADS, n_pad, RET_DK)),
            jnp.broadcast_to(cd[:, None, None], (RET_HEADS, 1, RET_DV)))


def rotary_tables(pos, reps):
    inv_freq = ROPE_BASE ** (-jnp.arange(0, RET_DK, 2, dtype=F32) / RET_DK)
    ang = pos.astype(F32)[:, None] * inv_freq[None, :]
    cos = jnp.repeat(jnp.cos(ang), 2, axis=1)
    sin = jnp.sin(ang)
    sin_signed = jnp.stack([-sin, sin], axis=-1).reshape(ang.shape[0], RET_DK)
    return jnp.tile(cos, (1, reps)), jnp.tile(sin_signed, (1, reps))


def _pool_body(x_ref, xp_ref, buf_ref, sh_ref, sc_ref, gate_ref, w_ref, cs_ref, o_ref, tail_ref, ext_scr,
               *, tm, pos0, has_prev):
    m = pl.program_id(1)
    sh, sc = sh_ref[0], sc_ref[0]
    h = _modulate(x_ref[0], sh, sc)

    @pl.when(m == 0)
    def _():
        ext_scr[0:POOL_HALO, :] = buf_ref[0]

    if has_prev:

        @pl.when(m > 0)
        def _():
            ext_scr[0:POOL_HALO, :] = _modulate(xp_ref[0], sh, sc)

    ext_scr[POOL_HALO:POOL_HALO + tm, :] = h
    tail_ref[0] = ext_scr[tm:tm + POOL_HALO, :]
    row = lax.broadcasted_iota(jnp.int32, (tm, 1), 0)
    pos1 = (pos0 + m * tm + row + 1).astype(F32)
    rows = max(tm, 16)
    ys = []
    for gi, w in enumerate(POOL_WINDOWS):
        c0, c1 = gi * POOL_GW, (gi + 1) * POOL_GW
        win = ext_scr[POOL_HALO:POOL_HALO + tm, c0:c1]
        for j in range(1, w):
            win = win + ext_scr[POOL_HALO - j:POOL_HALO - j + tm, c0:c1]
        d = win / jnp.minimum(jnp.float32(w), pos1) - h[:, c0:c1]
        if rows != tm:
            d = jnp.concatenate([d, jnp.zeros((rows - tm, POOL_GW), F32)], axis=0)
        y = jnp.dot(d.astype(BF16), w_ref[0, gi].astype(BF16), preferred_element_type=F32)
        ys.append(y[0:tm])
    y = jnp.concatenate(ys, axis=1) * cs_ref[...]
    o_ref[0] = x_ref[0] + gate_ref[0] * y


def pool_layer(x3, buf16, mod, gate, pool_w, pool_scale, wl, pos0, *, tm):
    B, L, D = x3.shape
    assert L % tm == 0 and (L == tm or tm % POOL_HALO == 0)
    has_prev = L > tm
    ph = POOL_HALO if has_prev else min(L, POOL_HALO)
    per = tm // POOL_HALO if has_prev else 1
    vec = lambda: pl.BlockSpec((1, 1, D), lambda b, m: (b, 0, 0))
    return pl.pallas_call(
        functools.partial(_pool_body, tm=tm, pos0=pos0, has_prev=has_prev),
        grid=(B, L // tm),
        in_specs=[
            pl.BlockSpec((1, tm, D), lambda b, m: (b, m, 0)),
            pl.BlockSpec((1, ph, D), lambda b, m: (b, jnp.maximum(m * per - 1, 0), 0)),
            pl.BlockSpec((1, POOL_HALO, D), lambda b, m: (b, 0, 0)),
            vec(), vec(), vec(),
            pl.BlockSpec((1,) + pool_w.shape[1:], lambda b, m: (wl, 0, 0, 0)),
            pl.BlockSpec((1, D), lambda b, m: (wl, 0)),
        ],
        out_specs=[
            pl.BlockSpec((1, tm, D), lambda b, m: (b, m, 0)),
            pl.BlockSpec((1, POOL_HALO, D), lambda b, m: (b, 0, 0)),
        ],
        out_shape=[
            jax.ShapeDtypeStruct((B, L, D), F32),
            jax.ShapeDtypeStruct((B, POOL_HALO, D), F32),
        ],
        scratch_shapes=[pltpu.VMEM((tm + POOL_HALO, D), F32)],
        compiler_params=_cparams(2),
        name="pool_layer",
    )(x3, x3, buf16, mod[0], mod[1], gate, pool_w, pool_scale)


def _split3(x):
    p0 = x.astype(BF16)
    r1 = x - p0.astype(F32)
    p1 = r1.astype(BF16)
    p2 = (r1 - p1.astype(F32)).astype(BF16)
    return p0, p1, p2


def _lane_cumsum(x, tri):
    p0, p1, p2 = _split3(x)
    dot = lambda p: jnp.dot(p, tri, preferred_element_type=F32)
    return (dot(p0) + dot(p1)) + dot(p2)


def _upper_tri(t):
    r = lax.broadcasted_iota(jnp.int32, (t, t), 0)
    c = lax.broadcasted_iota(jnp.int32, (t, t), 1)
    return jnp.where(r <= c, 1.0, 0.0).astype(BF16)


def _negcum_body(lf_ref, o_ref, carry_scr, *, tc):
    @pl.when(pl.program_id(1) == 0)
    def _():
        carry_scr[...] = jnp.zeros_like(carry_scr)

    f = carry_scr[...] + _lane_cumsum(lf_ref[0], _upper_tri(tc))
    o_ref[0] = -f
    carry_scr[...] = jnp.broadcast_to(f[:, tc - 1:tc], f.shape)


def neg_cumsum(lft, *, tc=512):
    B, H, L = lft.shape
    tc = min(tc, L)
    return pl.pallas_call(
        functools.partial(_negcum_body, tc=tc),
        grid=(B, L // tc),
        in_specs=[pl.BlockSpec((1, H, tc), lambda b, c: (b, 0, c))],
        out_specs=pl.BlockSpec((1, H, tc), lambda b, c: (b, 0, c)),
        out_shape=jax.ShapeDtypeStruct((B, H, L), F32),
        scratch_shapes=[pltpu.VMEM((H, tc), F32)],
        compiler_params=_cparams(2),
        name="neg_cumsum",
    )(lft)


def _flash_body(q_ref, kt_ref, vt_ref, nf_ref, o_ref, m_scr, l_scr, acc_scr, *, tq):
    hp = pl.program_id(1)
    qi = pl.program_id(2)
    m_scr[...] = jnp.full_like(m_scr, NEG_BIG)
    l_scr[...] = jnp.zeros_like(l_scr)
    acc_scr[...] = jnp.zeros_like(acc_scr)
    qs = [q_ref[0, :, i * FOX_DH:(i + 1) * FOX_DH] for i in range(2)]
    rowi = lax.broadcasted_iota(jnp.int32, (tq, tq), 0)
    coli = lax.broadcasted_iota(jnp.int32, (tq, tq), 1)
    causal = coli <= rowi
    last = (((1,), (1,)), ((), ()))

    def step(j, masked):
        k0 = pl.multiple_of(j * tq, tq)
        for i in range(2):
            kt = kt_ref[0, i * FOX_DH:(i + 1) * FOX_DH, pl.ds(k0, tq)].astype(BF16)
            vt = vt_ref[0, i * FOX_DH:(i + 1) * FOX_DH, pl.ds(k0, tq)].astype(BF16)
            s = jnp.dot(qs[i], kt, preferred_element_type=F32) + nf_ref[0, pl.ds(2 * hp + i, 1), pl.ds(k0, tq)]
            if masked:
                s = jnp.where(causal, s, NEG_BIG)
            m_old = m_scr[i]
            m_new = jnp.maximum(m_old, jnp.max(s, axis=1, keepdims=True))
            alpha = jnp.exp(m_old - m_new)
            p = jnp.exp(s - m_new)
            l_scr[i] = alpha * l_scr[i] + jnp.sum(p, axis=1, keepdims=True)
            acc_scr[i] = alpha * acc_scr[i] + lax.dot_general(p.astype(BF16), vt, last, preferred_element_type=F32)
            m_scr[i] = m_new

    def body(j, carry):
        step(j, False)
        return carry

    lax.fori_loop(0, qi, body, 0)
    step(qi, True)
    o_ref[0] = jnp.concatenate([acc_scr[i] / l_scr[i] for i in range(2)], axis=1).astype(o_ref.dtype)


def fox_flash(q, kt, vt, negf, *, tq=512):
    B, L, D = q.shape
    tq = min(tq, L)
    pair = 2 * FOX_DH
    return pl.pallas_call(
        functools.partial(_flash_body, tq=tq),
        grid=(B, FOX_HEADS // 2, L // tq),
        in_specs=[
            pl.BlockSpec((1, tq, pair), lambda b, hp, qi: (b, qi, hp)),
            pl.BlockSpec((1, pair, L), lambda b, hp, qi: (b, hp, 0)),
            pl.BlockSpec((1, pair, L), lambda b, hp, qi: (b, hp, 0)),
            pl.BlockSpec((1, FOX_HEADS, L), lambda b, hp, qi: (b, 0, 0)),
        ],
        out_specs=pl.BlockSpec((1, tq, pair), lambda b, hp, qi: (b, qi, hp)),
        out_shape=jax.ShapeDtypeStruct((B, L, D), BF16),
        scratch_shapes=[
            pltpu.VMEM((2, tq, 1), F32),
            pltpu.VMEM((2, tq, 1), F32),
            pltpu.VMEM((2, tq, FOX_DH), F32),
        ],
        compiler_params=_cparams(3),
        name="fox_flash",
    )(q, kt, vt, negf)


DEC_PAGES = 4


def _decode_body(pt_ref, q_ref, kn_ref, vn_ref, lfn_ref, *refs, n_steps, lq):
    kc = refs[0:DEC_PAGES]
    vc = refs[DEC_PAGES:2 * DEC_PAGES]
    lc = refs[2 * DEC_PAGES:3 * DEC_PAGES]
    o_ref, qbd_scr, m_scr, l_scr, acc_scr, carry_scr = refs[3 * DEC_PAGES:]
    b = pl.program_id(0)
    st = pl.program_id(1)
    rows = lq * FOX_HEADS
    last = (((1,), (1,)), ((), ()))
    tri = _upper_tri(PAGE_SIZE)

    @pl.when(st == 0)
    def _():
        head = lax.broadcasted_iota(jnp.int32, (FOX_HEADS, D_MODEL), 0)
        lane_head = lax.broadcasted_iota(jnp.int32, (FOX_HEADS, D_MODEL), 1) // FOX_DH
        blocks = [jnp.where(head == lane_head, jnp.broadcast_to(q_ref[0, t:t + 1, :], (FOX_HEADS, D_MODEL)), 0.0)
                  for t in range(lq)]
        qbd_scr[...] = jnp.concatenate(blocks, axis=0).astype(BF16)
        m_scr[...] = jnp.full_like(m_scr, NEG_BIG)
        l_scr[...] = jnp.zeros_like(l_scr)
        acc_scr[...] = jnp.zeros_like(acc_scr)
        carry_scr[...] = jnp.zeros_like(carry_scr)

    def absorb(s, v_mat, v_is_t):
        m_old = m_scr[...]
        m_new = jnp.maximum(m_old, jnp.max(s, axis=1, keepdims=True))
        alpha = jnp.exp(m_old - m_new)
        p = jnp.exp(s - m_new)
        l_scr[...] = alpha * l_scr[...] + jnp.sum(p, axis=1, keepdims=True)
        if v_is_t:
            pv = lax.dot_general(p.astype(BF16), v_mat, last, preferred_element_type=F32)
        else:
            pv = jnp.dot(p.astype(BF16), v_mat, preferred_element_type=F32)
        acc_scr[...] = alpha * acc_scr[...] + pv
        m_scr[...] = m_new

    @pl.when(st < n_steps)
    def _():
        for i in range(DEC_PAGES):
            kt = kc[i][0].reshape(D_MODEL, PAGE_SIZE).astype(BF16)
            vt = vc[i][0].reshape(D_MODEL, PAGE_SIZE).astype(BF16)
            f = carry_scr[...] + _lane_cumsum(lc[i][0], tri)
            carry_scr[...] = jnp.broadcast_to(f[:, PAGE_SIZE - 1:PAGE_SIZE], f.shape)
            s = jnp.dot(qbd_scr[...], kt, preferred_element_type=F32) - jnp.tile(f, (lq, 1))
            absorb(s, vt, True)

    @pl.when(st == n_steps)
    def _():
        pad = jnp.zeros((PAGE_SIZE - lq, D_MODEL), F32)
        kn = jnp.concatenate([kn_ref[0], pad], axis=0).astype(BF16)
        vn = jnp.concatenate([vn_ref[0], pad], axis=0).astype(BF16)
        n_tok = lfn_ref.shape[1]
        tok = lax.broadcasted_iota(jnp.int32, (n_tok, PAGE_SIZE), 0)
        key = lax.broadcasted_iota(jnp.int32, (n_tok, PAGE_SIZE), 1)
        sel = jnp.where((tok // lq == b) & (tok % lq <= key) & (key < lq), 1.0, 0.0).astype(BF16)
        p0, p1, p2 = _split3(lfn_ref[...])
        dot = lambda p: jnp.dot(p, sel, preferred_element_type=F32)
        f = carry_scr[...] + ((dot(p0) + dot(p1)) + dot(p2))
        s = lax.dot_general(qbd_scr[...], kn, last, preferred_element_type=F32) - jnp.tile(f, (lq, 1))
        rq = lax.broadcasted_iota(jnp.int32, (rows, PAGE_SIZE), 0) // FOX_HEADS
        kk = lax.broadcasted_iota(jnp.int32, (rows, PAGE_SIZE), 1)
        s = jnp.where(kk <= rq, s, NEG_BIG)
        absorb(s, vn, False)
        o = acc_scr[...] / l_scr[...]
        head = lax.broadcasted_iota(jnp.int32, (FOX_HEADS, D_MODEL), 0)
        lane_head = lax.broadcasted_iota(jnp.int32, (FOX_HEADS, D_MODEL), 1) // FOX_DH
        outs = [jnp.sum(jnp.where(head == lane_head, o[t * FOX_HEADS:(t + 1) * FOX_HEADS, :], 0.0), axis=0,
                        keepdims=True) for t in range(lq)]
        o_ref[0] = jnp.concatenate(outs, axis=0).astype(o_ref.dtype)


def fox_decode(q, k_new, v_new, lft_new, cache_kt, cache_vt, cache_lt, page_table):
    B, lq, D = q.shape
    n_pages = page_table.shape[1]
    assert n_pages % DEC_PAGES == 0
    n_steps = n_pages // DEC_PAGES
    rows = lq * FOX_HEADS

    def page_idx(i):
        return lambda b, s, pt: (pt[b * n_pages + jnp.minimum(s, n_steps - 1) * DEC_PAGES + i], 0, 0, 0)

    def page_idx3(i):
        return lambda b, s, pt: (pt[b * n_pages + jnp.minimum(s, n_steps - 1) * DEC_PAGES + i], 0, 0)

    seq = lambda: pl.BlockSpec((1, lq, D), lambda b, s, pt: (b, 0, 0))
    in_specs = [seq(), seq(), seq(), pl.BlockSpec(lft_new.shape, lambda b, s, pt: (0, 0))]
    in_specs += [pl.BlockSpec((1, FOX_HEADS, FOX_DH, PAGE_SIZE), page_idx(i)) for i in range(DEC_PAGES)]
    in_specs += [pl.BlockSpec((1, FOX_HEADS, FOX_DH, PAGE_SIZE), page_idx(i)) for i in range(DEC_PAGES)]
    in_specs += [pl.BlockSpec((1, FOX_HEADS, PAGE_SIZE), page_idx3(i)) for i in range(DEC_PAGES)]
    grid_spec = pltpu.PrefetchScalarGridSpec(
        num_scalar_prefetch=1,
        grid=(B, n_steps + 1),
        in_specs=in_specs,
        out_specs=pl.BlockSpec((1, lq, D), lambda b, s, pt: (b, 0, 0)),
        scratch_shapes=[
            pltpu.VMEM((rows, D), BF16),
            pltpu.VMEM((rows, 1), F32),
            pltpu.VMEM((rows, 1), F32),
            pltpu.VMEM((rows, D), F32),
            pltpu.VMEM((FOX_HEADS, PAGE_SIZE), F32),
        ],
    )
    return pl.pallas_call(
        functools.partial(_decode_body, n_steps=n_steps, lq=lq),
        grid_spec=grid_spec,
        out_shape=jax.ShapeDtypeStruct((B, lq, D), F32),
        compiler_params=_cparams(2),
        name="fox_decode",
    )(page_table.reshape(-1), q, k_new, v_new, lft_new,
      *([cache_kt] * DEC_PAGES), *([cache_vt] * DEC_PAGES), *([cache_lt] * DEC_PAGES))


def _final_body(x_ref, g_ref, o_ref):
    xf = x_ref[0]
    o_ref[0] = xf * lax.rsqrt(jnp.mean(xf * xf, axis=-1, keepdims=True) + EPS) * g_ref[...]


def final_norm(x3, final_g, *, tm=None):
    bx, L, D = x3.shape
    tm = tm or min(L, 1024)
    return pl.pallas_call(
        _final_body,
        grid=(bx, L // tm),
        in_specs=[pl.BlockSpec((1, tm, D), lambda b, m: (b, m, 0)), pl.BlockSpec((1, D), lambda b, m: (0, 0))],
        out_specs=pl.BlockSpec((1, tm, D), lambda b, m: (b, m, 0)),
        out_shape=jax.ShapeDtypeStruct((bx, L, D), F32),
        compiler_params=_cparams(2),
        name="final_norm",
    )(x3, final_g.reshape(1, D))


def _trunk(x, mods, ret_state, pool_state, fox_past, pos0, params):
    (ret_w_in, ret_w_out, pool_w, pool_scale, fox_wt, fox_b_f, fox_w_out, ffn_w_gu, ffn_w_down,
     w_rt_pad, b_rt_pad, moe_w_gu, moe_w_down, final_g) = params
    B, L, D = x.shape
    decode = fox_past is not None
    if decode:
        x3 = x.reshape(1, B * L, D)
        expand = lambda v: jnp.repeat(v, L, axis=0)[None]
    else:
        x3 = x
        expand = lambda v: v[:, None, :]
    n_rows = x3.shape[1]
    pos = pos0 + jnp.arange(L)
    cos, sin = rotary_tables(pos, RET_HEADS)
    if decode:
        cos, sin = jnp.tile(cos, (B, 1)), jnp.tile(sin, (B, 1))
    chunk = min(L, 256)
    chunk_pad = max(chunk, PAGE_SIZE)
    tables = retention_tables(chunk, chunk_pad)
    ret_new, extras = [], {}
    for i in range(DEPTH):
        sh_a, sc_a, g_a, sh_f, sc_f, g_f = [expand(v) for v in jnp.split(mods[i], 6, axis=-1)]
        kind, j = i % N_MIXERS, i // N_MIXERS
        if kind == 0:
            proj = functools.partial(mm, x3, ret_w_in, j, mod=(sh_a, sc_a))
            q = proj(n0=0, n_out=RET_QK, out_dtype=BF16, epi="rot", rot=(cos, sin), name="ret_q")
            k = proj(n0=RET_QK, n_out=RET_QK, out_dtype=BF16, epi="rot", rot=(cos, sin), scale=RET_DK ** -0.5,
                     name="ret_k")
            v = proj(n0=2 * RET_QK, n_out=RET_V, out_dtype=BF16, name="ret_v")
            g = proj(n0=2 * RET_QK + RET_V, n_out=RET_V, out_dtype=F32, name="ret_g")
            if decode:
                padr = lambda t: jnp.pad(t.reshape(B, L, -1), ((0, 0), (0, chunk_pad - L), (0, 0)))
                o, s = retention_scan(padr(q), padr(k), padr(v), padr(g), tables, chunk_pad, s0=ret_state,
                                      s0_layer=j)
                o = o[:, :L].reshape(1, n_rows, RET_V)
            else:
                o, s = retention_scan(q, k, v, g, tables, chunk_pad)
            ret_new.append(s)
            x3 = mm(o, ret_w_out, j, n0=0, n_out=D, out_dtype=F32, epi="res", res=(x3, g_a), name="ret_out")
        elif kind == 1:
            vecs = [v[:, None, :] for v in jnp.split(mods[i], 6, axis=-1)[:3]]
            if decode:
                buf16 = jnp.pad(pool_state[j], ((0, 0), (1, 0), (0, 0)))
                tm = L
            else:
                buf16 = jnp.zeros((B, POOL_HALO, D), F32)
                tm = min(L, 512)
            xn, tail = pool_layer(x3.reshape(B, L, D), buf16, (vecs[0], vecs[1]), vecs[2], pool_w, pool_scale, j,
                                  pos0, tm=tm)
            x3 = xn.reshape(x3.shape)
            extras["pool"] = tail[:, 1:, :]
        else:
            fproj = functools.partial(mm, x3, fox_wt, j, mod=(sh_a, sc_a), w_t=True)
            q = fproj(n0=0, n_out=D, out_dtype=F32 if decode else BF16, scale=FOX_DH ** -0.5, name="fox_q")
            lft = fproj(n0=3 * D, n_out=FOX_HEADS, out_dtype=F32, out_t=True, epi="logsig",
                        bias=fox_b_f[j].reshape(FOX_HEADS, 1), name="fox_logf")
            if decode:
                ck, cv, cl, pt = fox_past
                k = fproj(n0=D, n_out=D, out_dtype=F32, name="fox_k")
                v = fproj(n0=2 * D, n_out=D, out_dtype=F32, name="fox_v")
                o = fox_decode(q.reshape(B, L, D), k.reshape(B, L, D), v.reshape(B, L, D), lft[0],
                               jnp.transpose(ck[j], (0, 2, 3, 1)), jnp.transpose(cv[j], (0, 2, 3, 1)),
                               jnp.transpose(cl[j], (0, 2, 1)), pt)
                o = o.reshape(1, n_rows, D)
                extras["k"] = k.reshape(B, L, FOX_HEADS, FOX_DH)
                extras["v"] = v.reshape(B, L, FOX_HEADS, FOX_DH)
                extras["l"] = jnp.transpose(lft[0].reshape(FOX_HEADS, B, L), (1, 2, 0))
            else:
                kt = fproj(n0=D, n_out=D, out_dtype=F32, out_t=True, name="fox_kt")
                vt = fproj(n0=2 * D, n_out=D, out_dtype=F32, out_t=True, name="fox_vt")
                o = fox_flash(q, kt, vt, neg_cumsum(lft))
                unt = lambda t: jnp.transpose(t.reshape(B, FOX_HEADS, FOX_DH, L), (0, 3, 1, 2))
                extras["k"], extras["v"] = unt(kt), unt(vt)
                extras["l"] = jnp.transpose(lft, (0, 2, 1))
            x3 = mm(o, fox_w_out, j, n0=0, n_out=D, out_dtype=F32, epi="res", res=(x3, g_a), name="fox_out")
        ml = i // 2
        if i % 2 == 0:
            x3 = ffn(x3, (sh_f, sc_f), g_f, ffn_w_gu, ffn_w_down, ml, name="ffn_dense")
        else:
            gates = router(x3, (sh_f, sc_f), w_rt_pad, b_rt_pad, ml)
            x3 = ffn(x3, (sh_f, sc_f), g_f, moe_w_gu, moe_w_down, ml * N_EXPERTS, gates=gates, name="ffn_moe")
    out = final_norm(x3, final_g).reshape(B, L, D)
    return (out, jnp.stack(ret_new), extras["pool"][None], extras["k"][None], extras["v"][None], extras["l"][None])


def kernel(x_prompt, x_sample, state_ret, state_pool, cache_fox_k, cache_fox_v, cache_fox_logf, page_table,
           c_prompt, c_sample, ada_w, ada_b, ret_w_in, ret_w_out, pool_w, pool_scale, fox_w_in, fox_b_f, fox_w_out,
           ffn_w_gu, ffn_w_down, moe_w_router, moe_b_router, moe_w_gu, moe_w_down, final_g):
    bp, bs = x_prompt.shape[0], x_sample.shape[0]
    rows = -(-(bp + bs) // 8) * 8
    c_all = jnp.concatenate([c_prompt, c_sample, jnp.zeros((rows - bp - bs, D_MODEL), F32)], axis=0)
    mods = ada_mods(c_all, ada_w, ada_b)
    n_moe = moe_w_router.shape[0]
    params = (
        ret_w_in, ret_w_out, pool_w, pool_scale,
        jnp.swapaxes(fox_w_in, 1, 2),
        fox_b_f, fox_w_out, ffn_w_gu, ffn_w_down,
        jnp.pad(jnp.swapaxes(moe_w_router, 1, 2), ((0, 0), (0, LANES - N_EXPERTS), (0, 0))),
        jnp.pad(moe_b_router, ((0, 0), (0, LANES - N_EXPERTS))).reshape(n_moe, 1, LANES),
        moe_w_gu.reshape((n_moe * N_EXPERTS,) + moe_w_gu.shape[2:]),
        moe_w_down.reshape((n_moe * N_EXPERTS,) + moe_w_down.shape[2:]),
        final_g,
    )
    y_p, ret_p, pool_p, k_p, v_p, l_p = _trunk(x_prompt, mods[:, :bp], None, None, None, 0, params)
    n_past = page_table.shape[1] * PAGE_SIZE
    y_s, ret_s, pool_s, k_s, v_s, l_s = _trunk(
        x_sample, mods[:, bp:bp + bs], state_ret, state_pool,
        (cache_fox_k, cache_fox_v, cache_fox_logf, page_table), n_past, params)
    return (y_p, y_s, ret_p, ret_s, pool_p, pool_s, k_p, k_s, v_p, v_s, l_p, l_s)
```

```python
import functools
import math

import jax
import jax.numpy as jnp
from jax import lax
from jax.experimental import pallas as pl
from jax.experimental.pallas import tpu as pltpu

F32 = jnp.float32
BF16 = jnp.bfloat16

D_MODEL = 1024
DEPTH = 4
PAGE_SIZE = 128
N_MIXERS = 3
RET_HEADS = 4
RET_DK = D_MODEL // RET_HEADS
RET_DV = 2 * D_MODEL // RET_HEADS
RET_QK = RET_HEADS * RET_DK
RET_V = RET_HEADS * RET_DV
ROPE_BASE = 10000.0
POOL_WINDOWS = (2, 4, 8, 16)
POOL_GW = D_MODEL // len(POOL_WINDOWS)
POOL_BUF = max(POOL_WINDOWS) - 1
POOL_HALO = POOL_BUF + 1
FOX_HEADS = 16
FOX_DH = D_MODEL // FOX_HEADS
FFN_DIM = 2816
N_EXPERTS = 8
EPS = 1e-6
NEG_BIG = -1e30

V7X_VMEM_BYTES = 64 * 1024 * 1024
VMEM_LIMIT = V7X_VMEM_BYTES - 8 * 1024 * 1024
LANES = 128
FFN_TF = 256


def _cparams(n_axes):
    return pltpu.CompilerParams(dimension_semantics=("arbitrary",) * n_axes, vmem_limit_bytes=VMEM_LIMIT)


def _sigmoid(x):
    return 1.0 / (1.0 + jnp.exp(-x))


def _modulate(x, shift, scale):
    xf = x.astype(F32)
    ms = jnp.mean(xf * xf, axis=-1, keepdims=True)
    return (xf * lax.rsqrt(ms + EPS)) * (1.0 + scale) + shift


def _row_spec(arr, tm, width, col_fn):
    if arr.shape[1] == 1:
        return pl.BlockSpec((1, 1, width), lambda b, m, *r: (b, 0, col_fn(*r)))
    return pl.BlockSpec((1, tm, width), lambda b, m, *r: (b, m, col_fn(*r)))


def _ada_body(c_ref, w_ref, b_ref, o_ref):
    c = c_ref[...]
    cond = c * _sigmoid(c)
    o_ref[0] = jnp.dot(cond.astype(BF16), w_ref[0].astype(BF16), preferred_element_type=F32) + b_ref[0]


def ada_mods(c_all, ada_w, ada_b):
    rows = c_all.shape[0]
    n_out = ada_w.shape[2]
    tn = 1024
    return pl.pallas_call(
        _ada_body,
        grid=(DEPTH, n_out // tn),
        in_specs=[
            pl.BlockSpec((rows, D_MODEL), lambda i, n: (0, 0)),
            pl.BlockSpec((1, D_MODEL, tn), lambda i, n: (i, 0, n)),
            pl.BlockSpec((1, 1, tn), lambda i, n: (i, 0, n)),
        ],
        out_specs=pl.BlockSpec((1, rows, tn), lambda i, n: (i, 0, n)),
        out_shape=jax.ShapeDtypeStruct((DEPTH, rows, n_out), F32),
        compiler_params=_cparams(2),
        name="ada_mods",
    )(c_all, ada_w, ada_b.reshape(DEPTH, 1, n_out))


def _mm_body(*refs, has_mod, epi, w_t, out_t, scale):
    it = iter(refs)
    x_ref = next(it)
    if has_mod:
        sh_ref, sc_ref = next(it), next(it)
    w_ref = next(it)
    if epi == "rot":
        cos_ref, sin_ref = next(it), next(it)
    elif epi == "res":
        res_ref, gate_ref = next(it), next(it)
    elif epi == "logsig":
        b_ref = next(it)
    o_ref = next(it)
    n = pl.program_id(2)
    if has_mod:
        h_scr = next(it)

        @pl.when(n == 0)
        def _():
            h_scr[...] = _modulate(x_ref[0], sh_ref[0], sc_ref[0]).astype(BF16)

        lhs = h_scr[...]
    else:
        lhs = x_ref[0].astype(BF16)
    w = w_ref[0].astype(BF16)
    last = (((1,), (1,)), ((), ()))
    if not w_t:
        acc = jnp.dot(lhs, w, preferred_element_type=F32)
    elif not out_t:
        acc = lax.dot_general(lhs, w, last, preferred_element_type=F32)
    else:
        acc = lax.dot_general(w, lhs, last, preferred_element_type=F32)
    if epi == "rot":
        width = acc.shape[1]
        lane = lax.broadcasted_iota(jnp.int32, acc.shape, 1)
        nxt = pltpu.roll(acc, width - 1, axis=1)
        prv = pltpu.roll(acc, 1, axis=1)
        partner = jnp.where(lane % 2 == 0, nxt, prv)
        acc = acc * cos_ref[...] + partner * sin_ref[...]
    elif epi == "res":
        acc = res_ref[0] + gate_ref[0] * acc
    elif epi == "logsig":
        z = acc + b_ref[...]
        acc = jnp.minimum(z, 0.0) - jnp.log1p(jnp.exp(-jnp.abs(z)))
    if scale != 1.0:
        acc = acc * scale
    o_ref[0] = acc.astype(o_ref.dtype)


def mm(x3, w3, wl, *, n0, n_out, out_dtype, mod=None, epi="plain", w_t=False, out_t=False, scale=1.0,
       rot=None, res=None, bias=None, tm=None, tn=None, name="mm"):
    bx, L, K = x3.shape
    tm = tm or min(L, 512 if epi == "rot" else 1024)
    tn = tn or min(n_out, 512 if K > 1024 else 1024)
    assert L % tm == 0 and n_out % tn == 0 and n0 % tn == 0
    nb0 = n0 // tn
    has_mod = mod is not None
    in_specs = [pl.BlockSpec((1, tm, K), lambda b, m, n: (b, m, 0))]
    args = [x3]
    if has_mod:
        for a in mod:
            in_specs.append(_row_spec(a, tm, K, lambda n: 0))
            args.append(a)
    if w_t:
        in_specs.append(pl.BlockSpec((1, tn, K), lambda b, m, n: (wl, nb0 + n, 0)))
    else:
        in_specs.append(pl.BlockSpec((1, K, tn), lambda b, m, n: (wl, 0, nb0 + n)))
    args.append(w3)
    if epi == "rot":
        for a in rot:
            in_specs.append(pl.BlockSpec((tm, tn), lambda b, m, n: (m, n)))
            args.append(a)
    elif epi == "res":
        in_specs.append(pl.BlockSpec((1, tm, tn), lambda b, m, n: (b, m, n)))
        in_specs.append(_row_spec(res[1], tm, tn, lambda n: n))
        args.extend(res)
    elif epi == "logsig":
        in_specs.append(pl.BlockSpec(bias.shape, lambda b, m, n: (0, 0)))
        args.append(bias)
    if out_t:
        out_spec = pl.BlockSpec((1, tn, tm), lambda b, m, n: (b, n, m))
        out_shape = jax.ShapeDtypeStruct((bx, n_out, L), out_dtype)
    else:
        out_spec = pl.BlockSpec((1, tm, tn), lambda b, m, n: (b, m, n))
        out_shape = jax.ShapeDtypeStruct((bx, L, n_out), out_dtype)
    return pl.pallas_call(
        functools.partial(_mm_body, has_mod=has_mod, epi=epi, w_t=w_t, out_t=out_t, scale=scale),
        grid=(bx, L // tm, n_out // tn),
        in_specs=in_specs,
        out_specs=out_spec,
        out_shape=out_shape,
        scratch_shapes=[pltpu.VMEM((tm, K), BF16)] if has_mod else [],
        compiler_params=_cparams(3),
        name=name,
    )(*args)


def _ffn_body(*refs, moe, nf, ne):
    it = iter(refs)
    x_ref, sh_ref, sc_ref, gate_ref = next(it), next(it), next(it), next(it)
    gw_ref = next(it) if moe else None
    wg_ref, wu_ref, wd_ref, o_ref, h_scr, acc_scr = next(it), next(it), next(it), next(it), next(it), next(it)
    tot_scr = next(it) if moe else None
    e = pl.program_id(2)
    f = pl.program_id(3)

    @pl.when((e == 0) & (f == 0))
    def _():
        h_scr[...] = _modulate(x_ref[0], sh_ref[0], sc_ref[0]).astype(BF16)

    h = h_scr[...]
    a = jnp.dot(h, wg_ref[0].astype(BF16), preferred_element_type=F32)
    b = jnp.dot(h, wu_ref[0].astype(BF16), preferred_element_type=F32)
    mid = (a * _sigmoid(a) * b).astype(BF16)
    y = jnp.dot(mid, wd_ref[0].astype(BF16), preferred_element_type=F32)

    @pl.when(f == 0)
    def _():
        acc_scr[...] = y

    @pl.when(f > 0)
    def _():
        acc_scr[...] += y

    if not moe:

        @pl.when(f == nf - 1)
        def _():
            o_ref[0] = x_ref[0] + gate_ref[0] * acc_scr[...]

    else:

        @pl.when(f == nf - 1)
        def _():
            gw = gw_ref[0]
            lane = lax.broadcasted_iota(jnp.int32, gw.shape, 1)
            col = jnp.sum(jnp.where(lane == e, gw, 0.0), axis=1, keepdims=True)
            contrib = col * acc_scr[...]

            @pl.when(e == 0)
            def _():
                tot_scr[...] = contrib

            @pl.when(e > 0)
            def _():
                tot_scr[...] += contrib

            @pl.when(e == ne - 1)
            def _():
                o_ref[0] = x_ref[0] + gate_ref[0] * tot_scr[...]


def ffn(x3, mod, gate, w_gu3, w_down3, wl, *, gates=None, tm=None, name="ffn"):
    bx, L, D = x3.shape
    moe = gates is not None
    ne = N_EXPERTS if moe else 1
    tm = tm or min(L, 1024)
    tf = FFN_TF
    nf = FFN_DIM // tf
    assert L % tm == 0 and FFN_DIM % tf == 0
    in_specs = [pl.BlockSpec((1, tm, D), lambda b, m, e, f: (b, m, 0))]
    args = [x3]
    for a in (*mod, gate):
        in_specs.append(_row_spec(a, tm, D, lambda e, f: 0))
        args.append(a)
    if moe:
        in_specs.append(pl.BlockSpec((1, tm, LANES), lambda b, m, e, f: (b, m, 0)))
        args.append(gates)
    in_specs += [
        pl.BlockSpec((1, D, tf), lambda b, m, e, f: (wl + e, 0, f)),
        pl.BlockSpec((1, D, tf), lambda b, m, e, f: (wl + e, 0, nf + f)),
        pl.BlockSpec((1, tf, D), lambda b, m, e, f: (wl + e, f, 0)),
    ]
    args += [w_gu3, w_gu3, w_down3]
    scratch = [pltpu.VMEM((tm, D), BF16), pltpu.VMEM((tm, D), F32)]
    if moe:
        scratch.append(pltpu.VMEM((tm, D), F32))
    return pl.pallas_call(
        functools.partial(_ffn_body, moe=moe, nf=nf, ne=ne),
        grid=(bx, L // tm, ne, nf),
        in_specs=in_specs,
        out_specs=pl.BlockSpec((1, tm, D), lambda b, m, e, f: (b, m, 0)),
        out_shape=jax.ShapeDtypeStruct((bx, L, D), F32),
        scratch_shapes=scratch,
        compiler_params=_cparams(4),
        name=name,
    )(*args)


def _router_body(x_ref, sh_ref, sc_ref, w_ref, b_ref, o_ref):
    h = _modulate(x_ref[0], sh_ref[0], sc_ref[0]).astype(BF16)
    logits = lax.dot_general(h, w_ref[0].astype(BF16), (((1,), (1,)), ((), ())), preferred_element_type=F32)
    logits = logits + b_ref[0]
    lane = lax.broadcasted_iota(jnp.int32, logits.shape, 1).astype(F32)
    lg = jnp.where(lane < N_EXPERTS, logits, -jnp.inf)
    m1 = jnp.max(lg, axis=1, keepdims=True)
    i1 = jnp.min(jnp.where(lg == m1, lane, float(LANES)), axis=1, keepdims=True)
    lg2 = jnp.where(lane == i1, -jnp.inf, lg)
    m2 = jnp.max(lg2, axis=1, keepdims=True)
    i2 = jnp.min(jnp.where(lg2 == m2, lane, float(LANES)), axis=1, keepdims=True)
    e2 = jnp.exp(m2 - m1)
    den = 1.0 + e2
    o_ref[0] = jnp.where(lane == i1, 1.0 / den, 0.0) + jnp.where(lane == i2, e2 / den, 0.0)


def router(x3, mod, w_rt_pad, b_pad, wl, *, tm=None):
    bx, L, D = x3.shape
    tm = tm or min(L, 1024)
    in_specs = [pl.BlockSpec((1, tm, D), lambda b, m: (b, m, 0))]
    args = [x3]
    for a in mod:
        in_specs.append(_row_spec(a, tm, D, lambda: 0))
        args.append(a)
    in_specs += [
        pl.BlockSpec((1, LANES, D), lambda b, m: (wl, 0, 0)),
        pl.BlockSpec((1, 1, LANES), lambda b, m: (wl, 0, 0)),
    ]
    args += [w_rt_pad, b_pad]
    return pl.pallas_call(
        _router_body,
        grid=(bx, L // tm),
        in_specs=in_specs,
        out_specs=pl.BlockSpec((1, tm, LANES), lambda b, m: (b, m, 0)),
        out_shape=jax.ShapeDtypeStruct((bx, L, LANES), F32),
        compiler_params=_cparams(2),
        name="router",
    )(*args)


def _ret_body(*refs, zero_init, nc):
    it = iter(refs)
    q_ref, k_ref, v_ref, g_ref = next(it), next(it), next(it), next(it)
    s0_ref = None if zero_init else next(it)
    inner_ref, qd_ref, kd_ref, cd_ref = next(it), next(it), next(it), next(it)
    o_ref, sout_ref, s_scr = next(it), next(it), next(it)
    c = pl.program_id(2)

    @pl.when(c == 0)
    def _():
        if zero_init:
            s_scr[...] = jnp.zeros_like(s_scr)
        else:
            s_scr[...] = s0_ref[0, 0, 0]

    q = q_ref[0]
    k = k_ref[0]
    v = v_ref[0]
    s = s_scr[...]
    att = lax.dot_general(q, k, (((1,), (1,)), ((), ())), preferred_element_type=F32) * inner_ref[0]
    inner = jnp.dot(att.astype(BF16), v, preferred_element_type=F32)
    cross = jnp.dot(q, s.astype(BF16), preferred_element_type=F32) * qd_ref[0]
    kdt = (k.astype(F32) * kd_ref[0]).T.astype(BF16)
    s_new = s * cd_ref[0] + jnp.dot(kdt, v, preferred_element_type=F32)
    s_scr[...] = s_new
    o = inner + cross
    on = o * lax.rsqrt(jnp.mean(o * o, axis=-1, keepdims=True) + EPS)
    g = g_ref[0]
    o_ref[0] = (g * _sigmoid(g) * on).astype(o_ref.dtype)

    @pl.when(c == nc - 1)
    def _():
        sout_ref[0, 0] = s_new


def retention_scan(q, k, v, g, tables, chunk, *, s0=None, s0_layer=0):
    B, L, _ = q.shape
    nc = L // chunk
    inner, qd, kd, cd = tables
    zero_init = s0 is None
    in_specs = [
        pl.BlockSpec((1, chunk, RET_DK), lambda b, h, c: (b, c, h)),
        pl.BlockSpec((1, chunk, RET_DK), lambda b, h, c: (b, c, h)),
        pl.BlockSpec((1, chunk, RET_DV), lambda b, h, c: (b, c, h)),
        pl.BlockSpec((1, chunk, RET_DV), lambda b, h, c: (b, c, h)),
    ]
    args = [q, k, v, g]
    if not zero_init:
        in_specs.append(pl.BlockSpec((1, 1, 1, RET_DK, RET_DV), lambda b, h, c: (s0_layer, b, h, 0, 0)))
        args.append(s0)
    in_specs += [
        pl.BlockSpec((1, chunk, chunk), lambda b, h, c: (h, 0, 0)),
        pl.BlockSpec((1, chunk, RET_DV), lambda b, h, c: (h, 0, 0)),
        pl.BlockSpec((1, chunk, RET_DK), lambda b, h, c: (h, 0, 0)),
        pl.BlockSpec((1, 1, RET_DV), lambda b, h, c: (h, 0, 0)),
    ]
    args += [inner, qd, kd, cd]
    return pl.pallas_call(
        functools.partial(_ret_body, zero_init=zero_init, nc=nc),
        grid=(B, RET_HEADS, nc),
        in_specs=in_specs,
        out_specs=[
            pl.BlockSpec((1, chunk, RET_DV), lambda b, h, c: (b, c, h)),
            pl.BlockSpec((1, 1, RET_DK, RET_DV), lambda b, h, c: (b, h, 0, 0)),
        ],
        out_shape=[
            jax.ShapeDtypeStruct((B, L, RET_V), BF16),
            jax.ShapeDtypeStruct((B, RET_HEADS, RET_DK, RET_DV), F32),
        ],
        scratch_shapes=[pltpu.VMEM((RET_DK, RET_DV), F32)],
        compiler_params=_cparams(3),
        name="retention_scan",
    )(*args)


def retention_tables(n_real, n_pad):
    log_gamma = jnp.log1p(-(2.0 ** (-5.0 - jnp.arange(RET_HEADS, dtype=F32))))
    idx = jnp.arange(n_pad, dtype=F32)
    valid = idx < n_real
    diff = idx[:, None] - idx[None, :]
    ok = (diff >= 0) & valid[:, None] & valid[None, :]
    inner = jnp.where(ok[None], jnp.exp(log_gamma[:, None, None] * jnp.maximum(diff, 0.0)[None]), 0.0)
    qd = jnp.exp(log_gamma[:, None] * (idx[None, :] + 1.0))
    kd = jnp.where(valid[None, :], jnp.exp(log_gamma[:, None] * (n_real - 1.0 - idx[None, :])), 0.0)
    cd = jnp.exp(log_gamma * n_real)
    return (inner,
            jnp.broadcast_to(qd[:, :, None], (RET_HEADS, n_pad, RET_DV)),
            jnp.broadcast_to(kd[:, :, None], (RET_HEADS, n_pad, RET_DK)),
            jnp.broadcast_to(cd[:, None, None], (RET_HEADS, 1, RET_DV)))


def rotary_tables(pos, reps):
    inv_freq = ROPE_BASE ** (-jnp.arange(0, RET_DK, 2, dtype=F32) / RET_DK)
    ang = pos.astype(F32)[:, None] * inv_freq[None, :]
    cos = jnp.repeat(jnp.cos(ang), 2, axis=1)
    sin = jnp.sin(ang)
    sin_signed = jnp.stack([-sin, sin], axis=-1).reshape(ang.shape[0], RET_DK)
    return jnp.tile(cos, (1, reps)), jnp.tile(sin_signed, (1, reps))


def _pool_body(x_ref, xp_ref, buf_ref, sh_ref, sc_ref, gate_ref, w_ref, cs_ref, o_ref, tail_ref, ext_scr,
               *, tm, pos0, has_prev):
    m = pl.program_id(1)
    sh, sc = sh_ref[0], sc_ref[0]
    h = _modulate(x_ref[0], sh, sc)

    @pl.when(m == 0)
    def _():
        ext_scr[0:POOL_HALO, :] = buf_ref[0]

    if has_prev:

        @pl.when(m > 0)
        def _():
            ext_scr[0:POOL_HALO, :] = _modulate(xp_ref[0], sh, sc)

    ext_scr[POOL_HALO:POOL_HALO + tm, :] = h
    tail_ref[0] = ext_scr[tm:tm + POOL_HALO, :]
    row = lax.broadcasted_iota(jnp.int32, (tm, 1), 0)
    pos1 = (pos0 + m * tm + row + 1).astype(F32)
    rows = max(tm, 16)
    ys = []
    for gi, w in enumerate(POOL_WINDOWS):
        c0, c1 = gi * POOL_GW, (gi + 1) * POOL_GW
        win = ext_scr[POOL_HALO:POOL_HALO + tm, c0:c1]
        for j in range(1, w):
            win = win + ext_scr[POOL_HALO - j:POOL_HALO - j + tm, c0:c1]
        d = win / jnp.minimum(jnp.float32(w), pos1) - h[:, c0:c1]
        if rows != tm:
            d = jnp.concatenate([d, jnp.zeros((rows - tm, POOL_GW), F32)], axis=0)
        y = jnp.dot(d.astype(BF16), w_ref[0, gi].astype(BF16), preferred_element_type=F32)
        ys.append(y[0:tm])
    y = jnp.concatenate(ys, axis=1) * cs_ref[...]
    o_ref[0] = x_ref[0] + gate_ref[0] * y


def pool_layer(x3, buf16, mod, gate, pool_w, pool_scale, wl, pos0, *, tm):
    B, L, D = x3.shape
    assert L % tm == 0 and (L == tm or tm % POOL_HALO == 0)
    has_prev = L > tm
    ph = POOL_HALO if has_prev else min(L, POOL_HALO)
    per = tm // POOL_HALO if has_prev else 1
    vec = lambda: pl.BlockSpec((1, 1, D), lambda b, m: (b, 0, 0))
    return pl.pallas_call(
        functools.partial(_pool_body, tm=tm, pos0=pos0, has_prev=has_prev),
        grid=(B, L // tm),
        in_specs=[
            pl.BlockSpec((1, tm, D), lambda b, m: (b, m, 0)),
            pl.BlockSpec((1, ph, D), lambda b, m: (b, jnp.maximum(m * per - 1, 0), 0)),
            pl.BlockSpec((1, POOL_HALO, D), lambda b, m: (b, 0, 0)),
            vec(), vec(), vec(),
            pl.BlockSpec((1,) + pool_w.shape[1:], lambda b, m: (wl, 0, 0, 0)),
            pl.BlockSpec((1, D), lambda b, m: (wl, 0)),
        ],
        out_specs=[
            pl.BlockSpec((1, tm, D), lambda b, m: (b, m, 0)),
            pl.BlockSpec((1, POOL_HALO, D), lambda b, m: (b, 0, 0)),
        ],
        out_shape=[
            jax.ShapeDtypeStruct((B, L, D), F32),
            jax.ShapeDtypeStruct((B, POOL_HALO, D), F32),
        ],
        scratch_shapes=[pltpu.VMEM((tm + POOL_HALO, D), F32)],
        compiler_params=_cparams(2),
        name="pool_layer",
    )(x3, x3, buf16, mod[0], mod[1], gate, pool_w, pool_scale)


def _split3(x):
    p0 = x.astype(BF16)
    r1 = x - p0.astype(F32)
    p1 = r1.astype(BF16)
    p2 = (r1 - p1.astype(F32)).astype(BF16)
    return p0, p1, p2


def _lane_cumsum(x, tri):
    p0, p1, p2 = _split3(x)
    dot = lambda p: jnp.dot(p, tri, preferred_element_type=F32)
    return (dot(p0) + dot(p1)) + dot(p2)


def _upper_tri(t):
    r = lax.broadcasted_iota(jnp.int32, (t, t), 0)
    c = lax.broadcasted_iota(jnp.int32, (t, t), 1)
    return jnp.where(r <= c, 1.0, 0.0).astype(BF16)


BIAS_PIECES = 3


def _fbias_body(lf_ref, o_ref, carry_scr, *, tc):
    @pl.when(pl.program_id(1) == 0)
    def _():
        carry_scr[...] = jnp.zeros_like(carry_scr)

    r = lax.broadcasted_iota(jnp.int32, (tc, tc), 0)
    c = lax.broadcasted_iota(jnp.int32, (tc, tc), 1)
    tril = jnp.where(c <= r, 1.0, 0.0).astype(BF16)
    p0, p1, p2 = _split3(lf_ref[0])
    dot = lambda p: jnp.dot(tril, p, preferred_element_type=F32)
    f = carry_scr[...] + ((dot(p0) + dot(p1)) + dot(p2))
    carry_scr[...] = f[tc - 1:tc, :]
    head = lax.broadcasted_iota(jnp.int32, (FOX_HEADS, LANES), 0)
    lane = lax.broadcasted_iota(jnp.int32, (FOX_HEADS, LANES), 1)
    out = None
    for p, piece in enumerate(_split3(-f)):
        place = jnp.where(lane == BIAS_PIECES * head + p, 1.0, 0.0).astype(BF16)
        term = jnp.dot(piece, place, preferred_element_type=F32)
        out = term if out is None else out + term
    o_ref[0] = out.astype(BF16)


def fox_bias_features(lf, *, tc=512):
    B, L, H = lf.shape
    tc = min(tc, L)
    return pl.pallas_call(
        functools.partial(_fbias_body, tc=tc),
        grid=(B, L // tc),
        in_specs=[pl.BlockSpec((1, tc, H), lambda b, c: (b, c, 0))],
        out_specs=pl.BlockSpec((1, tc, LANES), lambda b, c: (b, c, 0)),
        out_shape=jax.ShapeDtypeStruct((B, L, LANES), BF16),
        scratch_shapes=[pltpu.VMEM((1, H), F32)],
        compiler_params=_cparams(2),
        name="fox_bias_features",
    )(lf)


def _flash_body(qt_ref, k_ref, fb_ref, vt_ref, o_ref, *, tq):
    hp = pl.program_id(1)
    qi = pl.program_id(2)
    pair = 2 * FOX_DH
    row = lax.broadcasted_iota(jnp.int32, (pair, tq), 0)
    qt = qt_ref[0]
    qaug = []
    for i in range(2):
        q_head = jnp.where(row // FOX_DH == i, qt, jnp.zeros_like(qt))
        pick = jnp.where(row // BIAS_PIECES == 2 * hp + i, 1.0, 0.0).astype(BF16)
        qaug.append(jnp.concatenate([q_head, pick], axis=0))
    key_i = lax.broadcasted_iota(jnp.int32, (tq, tq), 0)
    qry_i = lax.broadcasted_iota(jnp.int32, (tq, tq), 1)
    causal = key_i <= qry_i

    def step(j, carry, masked):
        k0 = pl.multiple_of(j * tq, tq)
        kaug = jnp.concatenate([k_ref[0, pl.ds(k0, tq), :], fb_ref[0, pl.ds(k0, tq), :]], axis=1)
        new = []
        for i in range(2):
            m_old, l_old, acc = carry[i]
            st = jnp.dot(kaug, qaug[i], preferred_element_type=F32)
            if masked:
                st = jnp.where(causal, st, NEG_BIG)
            m_new = jnp.maximum(m_old, jnp.max(st, axis=0, keepdims=True))
            alpha = jnp.exp(m_old - m_new)
            p = jnp.exp(st - m_new)
            l_new = alpha * l_old + jnp.sum(p, axis=0, keepdims=True)
            vt = vt_ref[0, i * FOX_DH:(i + 1) * FOX_DH, pl.ds(k0, tq)].astype(BF16)
            acc = alpha * acc + jnp.dot(vt, p.astype(BF16), preferred_element_type=F32)
            new.append((m_new, l_new, acc))
        return tuple(new)

    init = tuple((jnp.full((1, tq), NEG_BIG, F32), jnp.zeros((1, tq), F32), jnp.zeros((FOX_DH, tq), F32))
                 for _ in range(2))
    carry = lax.fori_loop(0, qi, lambda j, c: step(j, c, False), init)
    carry = step(qi, carry, True)
    o_ref[0] = jnp.concatenate([(acc / l).T for _, l, acc in carry], axis=1).astype(o_ref.dtype)


def fox_flash(qt, k, fb, vt, *, tq=512):
    B, L, D = k.shape
    tq = min(tq, L)
    pair = 2 * FOX_DH
    return pl.pallas_call(
        functools.partial(_flash_body, tq=tq),
        grid=(B, FOX_HEADS // 2, L // tq),
        in_specs=[
            pl.BlockSpec((1, pair, tq), lambda b, hp, qi: (b, hp, qi)),
            pl.BlockSpec((1, L, pair), lambda b, hp, qi: (b, 0, hp)),
            pl.BlockSpec((1, L, LANES), lambda b, hp, qi: (b, 0, 0)),
            pl.BlockSpec((1, pair, L), lambda b, hp, qi: (b, hp, 0)),
        ],
        out_specs=pl.BlockSpec((1, tq, pair), lambda b, hp, qi: (b, qi, hp)),
        out_shape=jax.ShapeDtypeStruct((B, L, D), BF16),
        compiler_params=_cparams(3),
        name="fox_flash",
    )(qt, k, fb, vt)


DEC_PAGES = 8


def _decode_body(pt_ref, q_ref, kn_ref, vn_ref, lfn_ref, *refs, n_steps, lq):
    kc = refs[0:DEC_PAGES]
    vc = refs[DEC_PAGES:2 * DEC_PAGES]
    lc = refs[2 * DEC_PAGES:3 * DEC_PAGES]
    o_ref, qbd_scr, m_scr, l_scr, acc_scr, carry_scr = refs[3 * DEC_PAGES:]
    b = pl.program_id(0)
    st = pl.program_id(1)
    rows = lq * FOX_HEADS
    last = (((1,), (1,)), ((), ()))
    tri = _upper_tri(PAGE_SIZE)

    @pl.when(st == 0)
    def _():
        head = lax.broadcasted_iota(jnp.int32, (FOX_HEADS, D_MODEL), 0)
        lane_head = lax.broadcasted_iota(jnp.int32, (FOX_HEADS, D_MODEL), 1) // FOX_DH
        blocks = [jnp.where(head == lane_head, jnp.broadcast_to(q_ref[0, t:t + 1, :], (FOX_HEADS, D_MODEL)), 0.0)
                  for t in range(lq)]
        qbd_scr[...] = jnp.concatenate(blocks, axis=0).astype(BF16)
        m_scr[...] = jnp.full_like(m_scr, NEG_BIG)
        l_scr[...] = jnp.zeros_like(l_scr)
        acc_scr[...] = jnp.zeros_like(acc_scr)
        carry_scr[...] = jnp.zeros_like(carry_scr)

    def absorb(s, v_mat, v_is_t):
        m_old = m_scr[...]
        m_new = jnp.maximum(m_old, jnp.max(s, axis=1, keepdims=True))
        alpha = jnp.exp(m_old - m_new)
        p = jnp.exp(s - m_new)
        l_scr[...] = alpha * l_scr[...] + jnp.sum(p, axis=1, keepdims=True)
        if v_is_t:
            pv = lax.dot_general(p.astype(BF16), v_mat, last, preferred_element_type=F32)
        else:
            pv = jnp.dot(p.astype(BF16), v_mat, preferred_element_type=F32)
        acc_scr[...] = alpha * acc_scr[...] + pv
        m_scr[...] = m_new

    @pl.when(st < n_steps)
    def _():
        kt = jnp.concatenate([kc[i][0].reshape(D_MODEL, PAGE_SIZE).astype(BF16) for i in range(DEC_PAGES)], axis=1)
        vt = jnp.concatenate([vc[i][0].reshape(D_MODEL, PAGE_SIZE).astype(BF16) for i in range(DEC_PAGES)], axis=1)
        within = _lane_cumsum(jnp.concatenate([lc[i][0] for i in range(DEC_PAGES)], axis=0), tri)
        f = carry_scr[...]
        biases = []
        for i in range(DEC_PAGES):
            f_page = f + within[i * FOX_HEADS:(i + 1) * FOX_HEADS, :]
            biases.append(jnp.tile(f_page, (lq, 1)))
            f = jnp.broadcast_to(f_page[:, PAGE_SIZE - 1:PAGE_SIZE], f_page.shape)
        carry_scr[...] = f
        s = jnp.dot(qbd_scr[...], kt, preferred_element_type=F32) - jnp.concatenate(biases, axis=1)
        absorb(s, vt, True)

    @pl.when(st == n_steps)
    def _():
        pad = jnp.zeros((PAGE_SIZE - lq, D_MODEL), F32)
        kn = jnp.concatenate([kn_ref[0], pad], axis=0).astype(BF16)
        vn = jnp.concatenate([vn_ref[0], pad], axis=0).astype(BF16)
        n_tok = lfn_ref.shape[1]
        tok = lax.broadcasted_iota(jnp.int32, (n_tok, PAGE_SIZE), 0)
        key = lax.broadcasted_iota(jnp.int32, (n_tok, PAGE_SIZE), 1)
        sel = jnp.where((tok // lq == b) & (tok % lq <= key) & (key < lq), 1.0, 0.0).astype(BF16)
        p0, p1, p2 = _split3(lfn_ref[...])
        dot = lambda p: jnp.dot(p, sel, preferred_element_type=F32)
        f = carry_scr[...] + ((dot(p0) + dot(p1)) + dot(p2))
        s = lax.dot_general(qbd_scr[...], kn, last, preferred_element_type=F32) - jnp.tile(f, (lq, 1))
        rq = lax.broadcasted_iota(jnp.int32, (rows, PAGE_SIZE), 0) // FOX_HEADS
        kk = lax.broadcasted_iota(jnp.int32, (rows, PAGE_SIZE), 1)
        s = jnp.where(kk <= rq, s, NEG_BIG)
        absorb(s, vn, False)
        o = acc_scr[...] / l_scr[...]
        head = lax.broadcasted_iota(jnp.int32, (FOX_HEADS, D_MODEL), 0)
        lane_head = lax.broadcasted_iota(jnp.int32, (FOX_HEADS, D_MODEL), 1) // FOX_DH
        outs = [jnp.sum(jnp.where(head == lane_head, o[t * FOX_HEADS:(t + 1) * FOX_HEADS, :], 0.0), axis=0,
                        keepdims=True) for t in range(lq)]
        o_ref[0] = jnp.concatenate(outs, axis=0).astype(o_ref.dtype)


def fox_decode(q, k_new, v_new, lft_new, cache_kt, cache_vt, cache_lt, page_table):
    B, lq, D = q.shape
    n_pages = page_table.shape[1]
    assert n_pages % DEC_PAGES == 0
    n_steps = n_pages // DEC_PAGES
    rows = lq * FOX_HEADS

    def page_idx(i):
        return lambda b, s, pt: (pt[b * n_pages + jnp.minimum(s, n_steps - 1) * DEC_PAGES + i], 0, 0, 0)

    def page_idx3(i):
        return lambda b, s, pt: (pt[b * n_pages + jnp.minimum(s, n_steps - 1) * DEC_PAGES + i], 0, 0)

    seq = lambda: pl.BlockSpec((1, lq, D), lambda b, s, pt: (b, 0, 0))
    in_specs = [seq(), seq(), seq(), pl.BlockSpec(lft_new.shape, lambda b, s, pt: (0, 0))]
    in_specs += [pl.BlockSpec((1, FOX_HEADS, FOX_DH, PAGE_SIZE), page_idx(i)) for i in range(DEC_PAGES)]
    in_specs += [pl.BlockSpec((1, FOX_HEADS, FOX_DH, PAGE_SIZE), page_idx(i)) for i in range(DEC_PAGES)]
    in_specs += [pl.BlockSpec((1, FOX_HEADS, PAGE_SIZE), page_idx3(i)) for i in range(DEC_PAGES)]
    grid_spec = pltpu.PrefetchScalarGridSpec(
        num_scalar_prefetch=1,
        grid=(B, n_steps + 1),
        in_specs=in_specs,
        out_specs=pl.BlockSpec((1, lq, D), lambda b, s, pt: (b, 0, 0)),
        scratch_shapes=[
            pltpu.VMEM((rows, D), BF16),
            pltpu.VMEM((rows, 1), F32),
            pltpu.VMEM((rows, 1), F32),
            pltpu.VMEM((rows, D), F32),
            pltpu.VMEM((FOX_HEADS, PAGE_SIZE), F32),
        ],
    )
    return pl.pallas_call(
        functools.partial(_decode_body, n_steps=n_steps, lq=lq),
        grid_spec=grid_spec,
        out_shape=jax.ShapeDtypeStruct((B, lq, D), F32),
        compiler_params=_cparams(2),
        name="fox_decode",
    )(page_table.reshape(-1), q, k_new, v_new, lft_new,
      *([cache_kt] * DEC_PAGES), *([cache_vt] * DEC_PAGES), *([cache_lt] * DEC_PAGES))


def _final_body(x_ref, g_ref, o_ref):
    xf = x_ref[0]
    o_ref[0] = xf * lax.rsqrt(jnp.mean(xf * xf, axis=-1, keepdims=True) + EPS) * g_ref[...]


def final_norm(x3, final_g, *, tm=None):
    bx, L, D = x3.shape
    tm = tm or min(L, 1024)
    return pl.pallas_call(
        _final_body,
        grid=(bx, L // tm),
        in_specs=[pl.BlockSpec((1, tm, D), lambda b, m: (b, m, 0)), pl.BlockSpec((1, D), lambda b, m: (0, 0))],
        out_specs=pl.BlockSpec((1, tm, D), lambda b, m: (b, m, 0)),
        out_shape=jax.ShapeDtypeStruct((bx, L, D), F32),
        compiler_params=_cparams(2),
        name="final_norm",
    )(x3, final_g.reshape(1, D))


def _trunk(x, mods, ret_state, pool_state, fox_past, pos0, params):
    (ret_w_in, ret_w_out, pool_w, pool_scale, fox_wt, fox_b_f, fox_w_out, ffn_w_gu, ffn_w_down,
     w_rt_pad, b_rt_pad, moe_w_gu, moe_w_down, final_g) = params
    B, L, D = x.shape
    decode = fox_past is not None
    if decode:
        x3 = x.reshape(1, B * L, D)
        expand = lambda v: jnp.repeat(v, L, axis=0)[None]
    else:
        x3 = x
        expand = lambda v: v[:, None, :]
    n_rows = x3.shape[1]
    pos = pos0 + jnp.arange(L)
    cos, sin = rotary_tables(pos, RET_HEADS)
    if decode:
        cos, sin = jnp.tile(cos, (B, 1)), jnp.tile(sin, (B, 1))
    chunk = min(L, 256)
    chunk_pad = max(chunk, PAGE_SIZE)
    tables = retention_tables(chunk, chunk_pad)
    ret_new, extras = [], {}
    for i in range(DEPTH):
        sh_a, sc_a, g_a, sh_f, sc_f, g_f = [expand(v) for v in jnp.split(mods[i], 6, axis=-1)]
        kind, j = i % N_MIXERS, i // N_MIXERS
        if kind == 0:
            proj = functools.partial(mm, x3, ret_w_in, j, mod=(sh_a, sc_a))
            q = proj(n0=0, n_out=RET_QK, out_dtype=BF16, epi="rot", rot=(cos, sin), name="ret_q")
            k = proj(n0=RET_QK, n_out=RET_QK, out_dtype=BF16, epi="rot", rot=(cos, sin), scale=RET_DK ** -0.5,
                     name="ret_k")
            v = proj(n0=2 * RET_QK, n_out=RET_V, out_dtype=BF16, name="ret_v")
            g = proj(n0=2 * RET_QK + RET_V, n_out=RET_V, out_dtype=F32, name="ret_g")
            if decode:
                padr = lambda t: jnp.pad(t.reshape(B, L, -1), ((0, 0), (0, chunk_pad - L), (0, 0)))
                o, s = retention_scan(padr(q), padr(k), padr(v), padr(g), tables, chunk_pad, s0=ret_state,
                                      s0_layer=j)
                o = o[:, :L].reshape(1, n_rows, RET_V)
            else:
                o, s = retention_scan(q, k, v, g, tables, chunk_pad)
            ret_new.append(s)
            x3 = mm(o, ret_w_out, j, n0=0, n_out=D, out_dtype=F32, epi="res", res=(x3, g_a), name="ret_out")
        elif kind == 1:
            vecs = [v[:, None, :] for v in jnp.split(mods[i], 6, axis=-1)[:3]]
            if decode:
                buf16 = jnp.pad(pool_state[j], ((0, 0), (1, 0), (0, 0)))
                tm = L
            else:
                buf16 = jnp.zeros((B, POOL_HALO, D), F32)
                tm = min(L, 512)
            xn, tail = pool_layer(x3.reshape(B, L, D), buf16, (vecs[0], vecs[1]), vecs[2], pool_w, pool_scale, j,
                                  pos0, tm=tm)
            x3 = xn.reshape(x3.shape)
            extras["pool"] = tail[:, 1:, :]
        else:
            fproj = functools.partial(mm, x3, fox_wt, j, mod=(sh_a, sc_a), w_t=True)
            if decode:
                ck, cv, cl, pt = fox_past
                q = fproj(n0=0, n_out=D, out_dtype=F32, scale=FOX_DH ** -0.5, name="fox_q")
                lft = fproj(n0=3 * D, n_out=FOX_HEADS, out_dtype=F32, out_t=True, epi="logsig",
                            bias=fox_b_f[j].reshape(FOX_HEADS, 1), name="fox_logft")
                k = fproj(n0=D, n_out=D, out_dtype=F32, name="fox_k")
                v = fproj(n0=2 * D, n_out=D, out_dtype=F32, name="fox_v")
                o = fox_decode(q.reshape(B, L, D), k.reshape(B, L, D), v.reshape(B, L, D), lft[0],
                               jnp.transpose(ck[j], (0, 2, 3, 1)), jnp.transpose(cv[j], (0, 2, 3, 1)),
                               jnp.transpose(cl[j], (0, 2, 1)), pt)
                o = o.reshape(1, n_rows, D)
                extras["k"] = k.reshape(B, L, FOX_HEADS, FOX_DH)
                extras["v"] = v.reshape(B, L, FOX_HEADS, FOX_DH)
                extras["l"] = jnp.transpose(lft[0].reshape(FOX_HEADS, B, L), (1, 2, 0))
            else:
                qt = fproj(n0=0, n_out=D, out_dtype=BF16, out_t=True, scale=FOX_DH ** -0.5, name="fox_qt")
                kb = fproj(n0=D, n_out=D, out_dtype=BF16, name="fox_kb")
                kt = fproj(n0=D, n_out=D, out_dtype=F32, out_t=True, name="fox_kt")
                vt = fproj(n0=2 * D, n_out=D, out_dtype=F32, out_t=True, name="fox_vt")
                lf = fproj(n0=3 * D, n_out=FOX_HEADS, out_dtype=F32, epi="logsig",
                           bias=fox_b_f[j].reshape(1, FOX_HEADS), name="fox_logf")
                o = fox_flash(qt, kb, fox_bias_features(lf), vt)
                unt = lambda t: jnp.transpose(t.reshape(B, FOX_HEADS, FOX_DH, L), (0, 3, 1, 2))
                extras["k"], extras["v"] = unt(kt), unt(vt)
                extras["l"] = lf
            x3 = mm(o, fox_w_out, j, n0=0, n_out=D, out_dtype=F32, epi="res", res=(x3, g_a), name="fox_out")
        ml = i // 2
        if i % 2 == 0:
            x3 = ffn(x3, (sh_f, sc_f), g_f, ffn_w_gu, ffn_w_down, ml, name="ffn_dense")
        else:
            gates = router(x3, (sh_f, sc_f), w_rt_pad, b_rt_pad, ml)
            x3 = ffn(x3, (sh_f, sc_f), g_f, moe_w_gu, moe_w_down, ml * N_EXPERTS, gates=gates, name="ffn_moe")
    out = final_norm(x3, final_g).reshape(B, L, D)
    return (out, jnp.stack(ret_new), extras["pool"][None], extras["k"][None], extras["v"][None], extras["l"][None])


def kernel(x_prompt, x_sample, state_ret, state_pool, cache_fox_k, cache_fox_v, cache_fox_logf, page_table,
           c_prompt, c_sample, ada_w, ada_b, ret_w_in, ret_w_out, pool_w, pool_scale, fox_w_in, fox_b_f, fox_w_out,
           ffn_w_gu, ffn_w_down, moe_w_router, moe_b_router, moe_w_gu, moe_w_down, final_g):
    bp, bs = x_prompt.shape[0], x_sample.shape[0]
    rows = -(-(bp + bs) // 8) * 8
    c_all = jnp.concatenate([c_prompt, c_sample, jnp.zeros((rows - bp - bs, D_MODEL), F32)], axis=0)
    mods = ada_mods(c_all, ada_w, ada_b)
    n_moe = moe_w_router.shape[0]
    params = (
        ret_w_in, ret_w_out, pool_w, pool_scale,
        jnp.swapaxes(fox_w_in, 1, 2),
        fox_b_f, fox_w_out, ffn_w_gu, ffn_w_down,
        jnp.pad(jnp.swapaxes(moe_w_router, 1, 2), ((0, 0), (0, LANES - N_EXPERTS), (0, 0))),
        jnp.pad(moe_b_router, ((0, 0), (0, LANES - N_EXPERTS))).reshape(n_moe, 1, LANES),
        moe_w_gu.reshape((n_moe * N_EXPERTS,) + moe_w_gu.shape[2:]),
        moe_w_down.reshape((n_moe * N_EXPERTS,) + moe_w_down.shape[2:]),
        final_g,
    )
    y_p, ret_p, pool_p, k_p, v_p, l_p = _trunk(x_prompt, mods[:, :bp], None, None, None, 0, params)
    n_past = page_table.shape[1] * PAGE_SIZE
    y_s, ret_s, pool_s, k_s, v_s, l_s = _trunk(
        x_sample, mods[:, bp:bp + bs], state_ret, state_pool,
        (cache_fox_k, cache_fox_v, cache_fox_logf, page_table), n_past, params)
    return (y_p, y_s, ret_p, ret_s, pool_p, pool_s, k_p, k_s, v_p, v_s, l_p, l_s)
```

```python
import functools
import math

import jax
import jax.numpy as jnp
from jax import lax
from jax.experimental import pallas as pl
from jax.experimental.pallas import tpu as pltpu

F32 = jnp.float32
BF16 = jnp.bfloat16

D_MODEL = 1024
DEPTH = 4
PAGE_SIZE = 128
N_MIXERS = 3
RET_HEADS = 4
RET_DK = D_MODEL // RET_HEADS
RET_DV = 2 * D_MODEL // RET_HEADS
RET_QK = RET_HEADS * RET_DK
RET_V = RET_HEADS * RET_DV
ROPE_BASE = 10000.0
POOL_WINDOWS = (2, 4, 8, 16)
POOL_GW = D_MODEL // len(POOL_WINDOWS)
POOL_BUF = max(POOL_WINDOWS) - 1
POOL_HALO = POOL_BUF + 1
FOX_HEADS = 16
FOX_DH = D_MODEL // FOX_HEADS
FFN_DIM = 2816
N_EXPERTS = 8
EPS = 1e-6
NEG_BIG = -1e30

V7X_VMEM_BYTES = 64 * 1024 * 1024
VMEM_LIMIT = V7X_VMEM_BYTES - 8 * 1024 * 1024
LANES = 128
FFN_TF = 256


def _cparams(n_axes):
    return pltpu.CompilerParams(dimension_semantics=("arbitrary",) * n_axes, vmem_limit_bytes=VMEM_LIMIT)


def _sigmoid(x):
    return 1.0 / (1.0 + jnp.exp(-x))


def _modulate(x, shift, scale):
    xf = x.astype(F32)
    ms = jnp.mean(xf * xf, axis=-1, keepdims=True)
    return (xf * lax.rsqrt(ms + EPS)) * (1.0 + scale) + shift


def _row_spec(arr, tm, width, col_fn):
    if arr.shape[1] == 1:
        return pl.BlockSpec((1, 1, width), lambda b, m, *r: (b, 0, col_fn(*r)))
    return pl.BlockSpec((1, tm, width), lambda b, m, *r: (b, m, col_fn(*r)))


def _ada_body(c_ref, w_ref, b_ref, o_ref):
    c = c_ref[...]
    cond = c * _sigmoid(c)
    o_ref[0] = jnp.dot(cond.astype(BF16), w_ref[0].astype(BF16), preferred_element_type=F32) + b_ref[0]


def ada_mods(c_all, ada_w, ada_b):
    rows = c_all.shape[0]
    n_out = ada_w.shape[2]
    tn = 1024
    return pl.pallas_call(
        _ada_body,
        grid=(DEPTH, n_out // tn),
        in_specs=[
            pl.BlockSpec((rows, D_MODEL), lambda i, n: (0, 0)),
            pl.BlockSpec((1, D_MODEL, tn), lambda i, n: (i, 0, n)),
            pl.BlockSpec((1, 1, tn), lambda i, n: (i, 0, n)),
        ],
        out_specs=pl.BlockSpec((1, rows, tn), lambda i, n: (i, 0, n)),
        out_shape=jax.ShapeDtypeStruct((DEPTH, rows, n_out), F32),
        compiler_params=_cparams(2),
        name="ada_mods",
    )(c_all, ada_w, ada_b.reshape(DEPTH, 1, n_out))


def _mm_body(*refs, has_mod, epi, w_t, out_t, scale):
    it = iter(refs)
    x_ref = next(it)
    if has_mod:
        sh_ref, sc_ref = next(it), next(it)
    w_ref = next(it)
    if epi == "rot":
        cos_ref, sin_ref = next(it), next(it)
    elif epi == "res":
        res_ref, gate_ref = next(it), next(it)
    elif epi == "logsig":
        b_ref = next(it)
    o_ref = next(it)
    n = pl.program_id(2)
    if has_mod:
        h_scr = next(it)

        @pl.when(n == 0)
        def _():
            h_scr[...] = _modulate(x_ref[0], sh_ref[0], sc_ref[0]).astype(BF16)

        lhs = h_scr[...]
    else:
        lhs = x_ref[0].astype(BF16)
    w = w_ref[0].astype(BF16)
    last = (((1,), (1,)), ((), ()))
    if not w_t:
        acc = jnp.dot(lhs, w, preferred_element_type=F32)
    elif not out_t:
        acc = lax.dot_general(lhs, w, last, preferred_element_type=F32)
    else:
        acc = lax.dot_general(w, lhs, last, preferred_element_type=F32)
    if epi == "rot":
        width = acc.shape[1]
        lane = lax.broadcasted_iota(jnp.int32, acc.shape, 1)
        nxt = pltpu.roll(acc, width - 1, axis=1)
        prv = pltpu.roll(acc, 1, axis=1)
        partner = jnp.where(lane % 2 == 0, nxt, prv)
        acc = acc * cos_ref[...] + partner * sin_ref[...]
    elif epi == "res":
        acc = res_ref[0] + gate_ref[0] * acc
    elif epi == "logsig":
        z = acc + b_ref[...]
        acc = jnp.minimum(z, 0.0) - jnp.log1p(jnp.exp(-jnp.abs(z)))
    if scale != 1.0:
        acc = acc * scale
    o_ref[0] = acc.astype(o_ref.dtype)


def mm(x3, w3, wl, *, n0, n_out, out_dtype, mod=None, epi="plain", w_t=False, out_t=False, scale=1.0,
       rot=None, res=None, bias=None, tm=None, tn=None, name="mm"):
    bx, L, K = x3.shape
    tm = tm or min(L, 512 if epi == "rot" else 1024)
    tn = tn or min(n_out, 512 if K > 1024 else 1024)
    assert L % tm == 0 and n_out % tn == 0 and n0 % tn == 0
    nb0 = n0 // tn
    has_mod = mod is not None
    in_specs = [pl.BlockSpec((1, tm, K), lambda b, m, n: (b, m, 0))]
    args = [x3]
    if has_mod:
        for a in mod:
            in_specs.append(_row_spec(a, tm, K, lambda n: 0))
            args.append(a)
    if w_t:
        in_specs.append(pl.BlockSpec((1, tn, K), lambda b, m, n: (wl, nb0 + n, 0)))
    else:
        in_specs.append(pl.BlockSpec((1, K, tn), lambda b, m, n: (wl, 0, nb0 + n)))
    args.append(w3)
    if epi == "rot":
        for a in rot:
            in_specs.append(pl.BlockSpec((tm, tn), lambda b, m, n: (m, n)))
            args.append(a)
    elif epi == "res":
        in_specs.append(pl.BlockSpec((1, tm, tn), lambda b, m, n: (b, m, n)))
        in_specs.append(_row_spec(res[1], tm, tn, lambda n: n))
        args.extend(res)
    elif epi == "logsig":
        in_specs.append(pl.BlockSpec(bias.shape, lambda b, m, n: (0, 0)))
        args.append(bias)
    if out_t:
        out_spec = pl.BlockSpec((1, tn, tm), lambda b, m, n: (b, n, m))
        out_shape = jax.ShapeDtypeStruct((bx, n_out, L), out_dtype)
    else:
        out_spec = pl.BlockSpec((1, tm, tn), lambda b, m, n: (b, m, n))
        out_shape = jax.ShapeDtypeStruct((bx, L, n_out), out_dtype)
    return pl.pallas_call(
        functools.partial(_mm_body, has_mod=has_mod, epi=epi, w_t=w_t, out_t=out_t, scale=scale),
        grid=(bx, L // tm, n_out // tn),
        in_specs=in_specs,
        out_specs=out_spec,
        out_shape=out_shape,
        scratch_shapes=[pltpu.VMEM((tm, K), BF16)] if has_mod else [],
        compiler_params=_cparams(3),
        name=name,
    )(*args)


def _ffn_body(*refs, moe, nf, ne):
    it = iter(refs)
    x_ref, sh_ref, sc_ref, gate_ref = next(it), next(it), next(it), next(it)
    gw_ref = next(it) if moe else None
    wg_ref, wu_ref, wd_ref, o_ref, h_scr, acc_scr = next(it), next(it), next(it), next(it), next(it), next(it)
    tot_scr = next(it) if moe else None
    e = pl.program_id(2)
    f = pl.program_id(3)

    @pl.when((e == 0) & (f == 0))
    def _():
        h_scr[...] = _modulate(x_ref[0], sh_ref[0], sc_ref[0]).astype(BF16)

    h = h_scr[...]
    a = jnp.dot(h, wg_ref[0].astype(BF16), preferred_element_type=F32)
    b = jnp.dot(h, wu_ref[0].astype(BF16), preferred_element_type=F32)
    mid = (a * _sigmoid(a) * b).astype(BF16)
    y = jnp.dot(mid, wd_ref[0].astype(BF16), preferred_element_type=F32)

    @pl.when(f == 0)
    def _():
        acc_scr[...] = y

    @pl.when(f > 0)
    def _():
        acc_scr[...] += y

    if not moe:

        @pl.when(f == nf - 1)
        def _():
            o_ref[0] = x_ref[0] + gate_ref[0] * acc_scr[...]

    else:

        @pl.when(f == nf - 1)
        def _():
            gw = gw_ref[0]
            lane = lax.broadcasted_iota(jnp.int32, gw.shape, 1)
            col = jnp.sum(jnp.where(lane == e, gw, 0.0), axis=1, keepdims=True)
            contrib = col * acc_scr[...]

            @pl.when(e == 0)
            def _():
                tot_scr[...] = contrib

            @pl.when(e > 0)
            def _():
                tot_scr[...] += contrib

            @pl.when(e == ne - 1)
            def _():
                o_ref[0] = x_ref[0] + gate_ref[0] * tot_scr[...]


def ffn(x3, mod, gate, w_gu3, w_down3, wl, *, gates=None, tm=None, name="ffn"):
    bx, L, D = x3.shape
    moe = gates is not None
    ne = N_EXPERTS if moe else 1
    tm = tm or min(L, 1024)
    tf = FFN_TF
    nf = FFN_DIM // tf
    assert L % tm == 0 and FFN_DIM % tf == 0
    in_specs = [pl.BlockSpec((1, tm, D), lambda b, m, e, f: (b, m, 0))]
    args = [x3]
    for a in (*mod, gate):
        in_specs.append(_row_spec(a, tm, D, lambda e, f: 0))
        args.append(a)
    if moe:
        in_specs.append(pl.BlockSpec((1, tm, LANES), lambda b, m, e, f: (b, m, 0)))
        args.append(gates)
    in_specs += [
        pl.BlockSpec((1, D, tf), lambda b, m, e, f: (wl + e, 0, f)),
        pl.BlockSpec((1, D, tf), lambda b, m, e, f: (wl + e, 0, nf + f)),
        pl.BlockSpec((1, tf, D), lambda b, m, e, f: (wl + e, f, 0)),
    ]
    args += [w_gu3, w_gu3, w_down3]
    scratch = [pltpu.VMEM((tm, D), BF16), pltpu.VMEM((tm, D), F32)]
    if moe:
        scratch.append(pltpu.VMEM((tm, D), F32))
    return pl.pallas_call(
        functools.partial(_ffn_body, moe=moe, nf=nf, ne=ne),
        grid=(bx, L // tm, ne, nf),
        in_specs=in_specs,
        out_specs=pl.BlockSpec((1, tm, D), lambda b, m, e, f: (b, m, 0)),
        out_shape=jax.ShapeDtypeStruct((bx, L, D), F32),
        scratch_shapes=scratch,
        compiler_params=_cparams(4),
        name=name,
    )(*args)


def _router_body(x_ref, sh_ref, sc_ref, w_ref, b_ref, o_ref):
    h = _modulate(x_ref[0], sh_ref[0], sc_ref[0]).astype(BF16)
    logits = lax.dot_general(h, w_ref[0].astype(BF16), (((1,), (1,)), ((), ())), preferred_element_type=F32)
    logits = logits + b_ref[0]
    lane = lax.broadcasted_iota(jnp.int32, logits.shape, 1).astype(F32)
    lg = jnp.where(lane < N_EXPERTS, logits, -jnp.inf)
    m1 = jnp.max(lg, axis=1, keepdims=True)
    i1 = jnp.min(jnp.where(lg == m1, lane, float(LANES)), axis=1, keepdims=True)
    lg2 = jnp.where(lane == i1, -jnp.inf, lg)
    m2 = jnp.max(lg2, axis=1, keepdims=True)
    i2 = jnp.min(jnp.where(lg2 == m2, lane, float(LANES)), axis=1, keepdims=True)
    e2 = jnp.exp(m2 - m1)
    den = 1.0 + e2
    o_ref[0] = jnp.where(lane == i1, 1.0 / den, 0.0) + jnp.where(lane == i2, e2 / den, 0.0)


def router(x3, mod, w_rt_pad, b_pad, wl, *, tm=None):
    bx, L, D = x3.shape
    tm = tm or min(L, 1024)
    in_specs = [pl.BlockSpec((1, tm, D), lambda b, m: (b, m, 0))]
    args = [x3]
    for a in mod:
        in_specs.append(_row_spec(a, tm, D, lambda: 0))
        args.append(a)
    in_specs += [
        pl.BlockSpec((1, LANES, D), lambda b, m: (wl, 0, 0)),
        pl.BlockSpec((1, 1, LANES), lambda b, m: (wl, 0, 0)),
    ]
    args += [w_rt_pad, b_pad]
    return pl.pallas_call(
        _router_body,
        grid=(bx, L // tm),
        in_specs=in_specs,
        out_specs=pl.BlockSpec((1, tm, LANES), lambda b, m: (b, m, 0)),
        out_shape=jax.ShapeDtypeStruct((bx, L, LANES), F32),
        compiler_params=_cparams(2),
        name="router",
    )(*args)


MOE_T = 256
MOE_CH = 128
MOE_TM = 256
MOE_TF = FFN_DIM // 2
MOE_ALIGN = 8
MOE_STAGE = -(-(2 * MOE_T + N_EXPERTS * (MOE_ALIGN - 1) + MOE_CH) // 8) * 8


def _moe_plan(gates2, n_tokens):
    nb = n_tokens // MOE_T
    routed = (gates2[:, :N_EXPERTS] > 0).reshape(nb, MOE_T, N_EXPERTS)
    ri = routed.astype(jnp.int32)
    rank = jnp.cumsum(ri, axis=1) - ri
    cnt = jnp.sum(ri, axis=1)
    cnt_al = -(-cnt // MOE_ALIGN) * MOE_ALIGN
    lo = jnp.cumsum(cnt_al, axis=1) - cnt_al
    total = jnp.sum(cnt_al, axis=0)
    region = -(-(total + MOE_CH) // MOE_TM) * MOE_TM
    ends = jnp.cumsum(region)
    off = ends - region
    pos = off[None, :] + jnp.cumsum(cnt_al, axis=0) - cnt_al
    nch = -(-cnt_al // MOE_CH)
    m_pad = -(-(2 * n_tokens + nb * N_EXPERTS * (MOE_ALIGN - 1) + N_EXPERTS * (MOE_CH + MOE_TM)) // MOE_TM) * MOE_TM
    n_tiles = m_pad // MOE_TM
    tile_start = jnp.arange(n_tiles, dtype=jnp.int32) * MOE_TM
    tile_e = jnp.minimum(jnp.sum(tile_start[:, None] >= ends[None, :], axis=1), N_EXPERTS - 1).astype(jnp.int32)
    n_used = (ends[-1] // MOE_TM).astype(jnp.int32).reshape(1)
    dest = jnp.where(routed, lo[:, None, :] + rank, -1)
    d_hi = jnp.max(dest, axis=2)
    d_lo = jnp.min(jnp.where(routed, dest, MOE_STAGE), axis=2)
    dd = jnp.stack([d_hi, d_lo], axis=1).astype(jnp.int32)
    seg_rank = jnp.where(routed, rank, -1).astype(F32).reshape(n_tokens, N_EXPERTS)
    seg_rank = jnp.pad(seg_rank, ((0, 0), (0, LANES - N_EXPERTS)), constant_values=-1.0)
    flat = lambda a: a.reshape(-1).astype(jnp.int32)
    return dict(lo=flat(lo // MOE_ALIGN), pos=flat(pos // MOE_ALIGN), nch=flat(nch), tile_e=tile_e, n_used=n_used,
                dd=dd, seg_rank=seg_rank, m_pad=m_pad, n_tiles=n_tiles, nb=nb)


def _dispatch_body(lo_ref, pos_ref, nch_ref, x_ref, sh_ref, sc_ref, dd_ref, xs_in_ref, xs_ref, stage_scr, sem,
                   *, nb):
    del xs_in_ref
    b = pl.program_id(0)
    slot = b % 2
    h = _modulate(x_ref[...], sh_ref[0], sc_ref[0]).astype(BF16)
    r = lax.broadcasted_iota(jnp.int32, (MOE_STAGE, MOE_T), 0)
    dd = dd_ref[0]
    onehot = jnp.where((r == dd[0:1, :]) | (r == dd[1:2, :]), 1.0, 0.0).astype(BF16)
    stage_scr[slot] = jnp.dot(onehot, h, preferred_element_type=F32)

    def seg_copy(blk, e, c):
        src0 = pl.multiple_of(lo_ref[blk * N_EXPERTS + e] * MOE_ALIGN + c * MOE_CH, MOE_ALIGN)
        dst0 = pl.multiple_of(pos_ref[blk * N_EXPERTS + e] * MOE_ALIGN + c * MOE_CH, MOE_ALIGN)
        return pltpu.make_async_copy(stage_scr.at[blk % 2, pl.ds(src0, MOE_CH)], xs_ref.at[pl.ds(dst0, MOE_CH)],
                                     sem.at[blk % 2, e, c])

    def for_segments(blk, fn):
        for e in range(N_EXPERTS):
            for c in range(2):
                @pl.when(c < nch_ref[blk * N_EXPERTS + e])
                def _():
                    fn(seg_copy(blk, e, c))

    @pl.when(b > 0)
    def _():
        for_segments(b - 1, lambda cp: cp.wait())

    for_segments(b, lambda cp: cp.start())

    @pl.when(b == nb - 1)
    def _():
        for_segments(b, lambda cp: cp.wait())


def moe_dispatch(x2, mod, plan, seq_len):
    n_tokens, D = x2.shape
    per_seq = seq_len // MOE_T
    nb = plan["nb"]
    vec = lambda: pl.BlockSpec((1, 1, D), lambda b, *_: (b // per_seq, 0, 0))
    grid_spec = pltpu.PrefetchScalarGridSpec(
        num_scalar_prefetch=3,
        grid=(nb,),
        in_specs=[
            pl.BlockSpec((MOE_T, D), lambda b, *_: (b, 0)),
            vec(), vec(),
            pl.BlockSpec((1, 2, MOE_T), lambda b, *_: (b, 0, 0)),
            pl.BlockSpec(memory_space=pltpu.MemorySpace.HBM),
        ],
        out_specs=pl.BlockSpec(memory_space=pltpu.MemorySpace.HBM),
        scratch_shapes=[pltpu.VMEM((2, MOE_STAGE, D), F32), pltpu.SemaphoreType.DMA((2, N_EXPERTS, 2))],
    )
    return pl.pallas_call(
        functools.partial(_dispatch_body, nb=nb),
        grid_spec=grid_spec,
        out_shape=jax.ShapeDtypeStruct((plan["m_pad"], D), F32),
        input_output_aliases={7: 0},
        compiler_params=_cparams(1),
        name="moe_dispatch",
    )(plan["lo"], plan["pos"], plan["nch"], x2, mod[0], mod[1], plan["dd"], jnp.zeros((plan["m_pad"], D), F32))


def _gffn_body(te_ref, nu_ref, *refs, has_prev):
    it = iter(refs)
    xs_ref = next(it)
    yp_ref = next(it) if has_prev else None
    wg_ref, wu_ref, wd_ref, o_ref, wg_scr, wu_scr, wd_scr = (next(it) for _ in range(7))
    t = pl.program_id(0)
    e_here = te_ref[t]
    e_prev = te_ref[jnp.maximum(t - 1, 0)]

    @pl.when(t < nu_ref[0])
    def _():
        @pl.when((t == 0) | (e_here != e_prev))
        def _():
            wg_scr[...] = wg_ref[0].astype(BF16)
            wu_scr[...] = wu_ref[0].astype(BF16)
            wd_scr[...] = wd_ref[0].astype(BF16)

        h = xs_ref[...].astype(BF16)
        a = jnp.dot(h, wg_scr[...], preferred_element_type=F32)
        b = jnp.dot(h, wu_scr[...], preferred_element_type=F32)
        mid = (a * _sigmoid(a) * b).astype(BF16)
        y = jnp.dot(mid, wd_scr[...], preferred_element_type=F32)
        o_ref[...] = (yp_ref[...] + y) if has_prev else y

    @pl.when(t >= nu_ref[0])
    def _():
        o_ref[...] = jnp.zeros_like(o_ref)


def moe_grouped_ffn(xs, w_gu3, w_down3, wl, plan, f, y_prev=None):
    m_pad, D = xs.shape
    nf = FFN_DIM // MOE_TF
    has_prev = y_prev is not None
    row = lambda t, te, nu: (jnp.minimum(t, nu[0] - 1), 0)
    tile = lambda: pl.BlockSpec((MOE_TM, D), row)
    in_specs = [tile()] + ([tile()] if has_prev else [])
    in_specs += [
        pl.BlockSpec((1, D, MOE_TF), lambda t, te, nu: (wl + te[t], 0, f)),
        pl.BlockSpec((1, D, MOE_TF), lambda t, te, nu: (wl + te[t], 0, nf + f)),
        pl.BlockSpec((1, MOE_TF, D), lambda t, te, nu: (wl + te[t], f, 0), pipeline_mode=pl.Buffered(1)),
    ]
    grid_spec = pltpu.PrefetchScalarGridSpec(
        num_scalar_prefetch=2,
        grid=(plan["n_tiles"],),
        in_specs=in_specs,
        out_specs=pl.BlockSpec((MOE_TM, D), lambda t, te, nu: (t, 0)),
        scratch_shapes=[pltpu.VMEM((D, MOE_TF), BF16), pltpu.VMEM((D, MOE_TF), BF16), pltpu.VMEM((MOE_TF, D), BF16)],
    )
    args = [xs] + ([y_prev] if has_prev else []) + [w_gu3, w_gu3, w_down3]
    return pl.pallas_call(
        functools.partial(_gffn_body, has_prev=has_prev),
        grid_spec=grid_spec,
        out_shape=jax.ShapeDtypeStruct((m_pad, D), F32),
        compiler_params=_cparams(1),
        name=f"moe_grouped_ffn{f}",
    )(plan["tile_e"], plan["n_used"], *args)


def _combine_body(pos_ref, nch_ref, x_ref, gate_ref, gw_ref, rk_ref, *refs):
    y_refs = refs[:N_EXPERTS]
    o_ref, acc_scr = refs[N_EXPERTS:]
    b = pl.program_id(0)
    c = pl.program_id(1)

    @pl.when(c == 0)
    def _():
        acc_scr[...] = jnp.zeros_like(acc_scr)

    lane = lax.broadcasted_iota(jnp.int32, (MOE_T, LANES), 1)
    col = lax.broadcasted_iota(jnp.int32, (MOE_T, MOE_CH), 1).astype(F32)
    for e in range(N_EXPERTS):
        @pl.when(c < nch_ref[b * N_EXPERTS + e])
        def _():
            pick = lambda ref: jnp.sum(jnp.where(lane == e, ref[...], 0.0), axis=1, keepdims=True)
            rank = pick(rk_ref) - (c * MOE_CH).astype(F32)
            onehot = jnp.where(col == rank, 1.0, 0.0).astype(BF16)
            y = y_refs[e][...]
            y_hi = y.astype(BF16)
            y_lo = (y - y_hi.astype(F32)).astype(BF16)
            rows = (jnp.dot(onehot, y_hi, preferred_element_type=F32)
                    + jnp.dot(onehot, y_lo, preferred_element_type=F32))
            acc_scr[...] += pick(gw_ref) * rows

    @pl.when(c == 1)
    def _():
        o_ref[...] = x_ref[...] + gate_ref[0] * acc_scr[...]


def moe_combine(x2, gate, gates2, y, plan, seq_len):
    n_tokens, D = x2.shape
    per_seq = seq_len // MOE_T

    def window(e):
        def idx(b, c, pos, nch):
            s = b * N_EXPERTS + e
            return ((pos[s] + jnp.where(c < nch[s], c, 0) * (MOE_CH // MOE_ALIGN)) * MOE_ALIGN, 0)
        return pl.BlockSpec((pl.Element(MOE_CH), pl.Element(D)), idx)

    blk = lambda w: pl.BlockSpec((MOE_T, w), lambda b, c, *_: (b, 0))
    grid_spec = pltpu.PrefetchScalarGridSpec(
        num_scalar_prefetch=2,
        grid=(plan["nb"], 2),
        in_specs=[blk(D), pl.BlockSpec((1, 1, D), lambda b, c, *_: (b // per_seq, 0, 0)), blk(LANES), blk(LANES)]
        + [window(e) for e in range(N_EXPERTS)],
        out_specs=blk(D),
        scratch_shapes=[pltpu.VMEM((MOE_T, D), F32)],
    )
    return pl.pallas_call(
        _combine_body,
        grid_spec=grid_spec,
        out_shape=jax.ShapeDtypeStruct((n_tokens, D), F32),
        compiler_params=_cparams(2),
        name="moe_combine",
    )(plan["pos"], plan["nch"], x2, gate, gates2, plan["seg_rank"], *([y] * N_EXPERTS))


def moe_sparse(x3, mod, gate, gates, w_gu3, w_down3, wl):
    B, L, D = x3.shape
    n_tokens = B * L
    assert L % MOE_T == 0 and FFN_DIM % MOE_TF == 0
    x2 = x3.reshape(n_tokens, D)
    gates2 = gates.reshape(n_tokens, LANES)
    plan = _moe_plan(gates2, n_tokens)
    xs = moe_dispatch(x2, mod, plan, L)
    y = None
    for f in range(FFN_DIM // MOE_TF):
        y = moe_grouped_ffn(xs, w_gu3, w_down3, wl, plan, f, y_prev=y)
    return moe_combine(x2, gate, gates2, y, plan, L).reshape(B, L, D)


def _ret_body(*refs, zero_init, nc):
    it = iter(refs)
    q_ref, k_ref, v_ref, g_ref = next(it), next(it), next(it), next(it)
    s0_ref = None if zero_init else next(it)
    inner_ref, qd_ref, kd_ref, cd_ref = next(it), next(it), next(it), next(it)
    o_ref, sout_ref, s_scr = next(it), next(it), next(it)
    c = pl.program_id(2)

    @pl.when(c == 0)
    def _():
        if zero_init:
            s_scr[...] = jnp.zeros_like(s_scr)
        else:
            s_scr[...] = s0_ref[0, 0, 0]

    q = q_ref[0]
    k = k_ref[0]
    v = v_ref[0]
    s = s_scr[...]
    att = lax.dot_general(q, k, (((1,), (1,)), ((), ())), preferred_element_type=F32) * inner_ref[0]
    inner = jnp.dot(att.astype(BF16), v, preferred_element_type=F32)
    cross = jnp.dot(q, s.astype(BF16), preferred_element_type=F32) * qd_ref[0]
    kdt = (k.astype(F32) * kd_ref[0]).T.astype(BF16)
    s_new = s * cd_ref[0] + jnp.dot(kdt, v, preferred_element_type=F32)
    s_scr[...] = s_new
    o = inner + cross
    on = o * lax.rsqrt(jnp.mean(o * o, axis=-1, keepdims=True) + EPS)
    g = g_ref[0]
    o_ref[0] = (g * _sigmoid(g) * on).astype(o_ref.dtype)

    @pl.when(c == nc - 1)
    def _():
        sout_ref[0, 0] = s_new


def retention_scan(q, k, v, g, tables, chunk, *, s0=None, s0_layer=0):
    B, L, _ = q.shape
    nc = L // chunk
    inner, qd, kd, cd = tables
    zero_init = s0 is None
    in_specs = [
        pl.BlockSpec((1, chunk, RET_DK), lambda b, h, c: (b, c, h)),
        pl.BlockSpec((1, chunk, RET_DK), lambda b, h, c: (b, c, h)),
        pl.BlockSpec((1, chunk, RET_DV), lambda b, h, c: (b, c, h)),
        pl.BlockSpec((1, chunk, RET_DV), lambda b, h, c: (b, c, h)),
    ]
    args = [q, k, v, g]
    if not zero_init:
        in_specs.append(pl.BlockSpec((1, 1, 1, RET_DK, RET_DV), lambda b, h, c: (s0_layer, b, h, 0, 0)))
        args.append(s0)
    in_specs += [
        pl.BlockSpec((1, chunk, chunk), lambda b, h, c: (h, 0, 0)),
        pl.BlockSpec((1, chunk, RET_DV), lambda b, h, c: (h, 0, 0)),
        pl.BlockSpec((1, chunk, RET_DK), lambda b, h, c: (h, 0, 0)),
        pl.BlockSpec((1, 1, RET_DV), lambda b, h, c: (h, 0, 0)),
    ]
    args += [inner, qd, kd, cd]
    return pl.pallas_call(
        functools.partial(_ret_body, zero_init=zero_init, nc=nc),
        grid=(B, RET_HEADS, nc),
        in_specs=in_specs,
        out_specs=[
            pl.BlockSpec((1, chunk, RET_DV), lambda b, h, c: (b, c, h)),
            pl.BlockSpec((1, 1, RET_DK, RET_DV), lambda b, h, c: (b, h, 0, 0)),
        ],
        out_shape=[
            jax.ShapeDtypeStruct((B, L, RET_V), BF16),
            jax.ShapeDtypeStruct((B, RET_HEADS, RET_DK, RET_DV), F32),
        ],
        scratch_shapes=[pltpu.VMEM((RET_DK, RET_DV), F32)],
        compiler_params=_cparams(3),
        name="retention_scan",
    )(*args)


def retention_tables(n_real, n_pad):
    log_gamma = jnp.log1p(-(2.0 ** (-5.0 - jnp.arange(RET_HEADS, dtype=F32))))
    idx = jnp.arange(n_pad, dtype=F32)
    valid = idx < n_real
    diff = idx[:, None] - idx[None, :]
    ok = (diff >= 0) & valid[:, None] & valid[None, :]
    inner = jnp.where(ok[None], jnp.exp(log_gamma[:, None, None] * jnp.maximum(diff, 0.0)[None]), 0.0)
    qd = jnp.exp(log_gamma[:, None] * (idx[None, :] + 1.0))
    kd = jnp.where(valid[None, :], jnp.exp(log_gamma[:, None] * (n_real - 1.0 - idx[None, :])), 0.0)
    cd = jnp.exp(log_gamma * n_real)
    return (inner,
            jnp.broadcast_to(qd[:, :, None], (RET_HEADS, n_pad, RET_DV)),
            jnp.broadcast_to(kd[:, :, None], (RET_HEADS, n_pad, RET_DK)),
            jnp.broadcast_to(cd[:, None, None], (RET_HEADS, 1, RET_DV)))


def rotary_tables(pos, reps):
    inv_freq = ROPE_BASE ** (-jnp.arange(0, RET_DK, 2, dtype=F32) / RET_DK)
    ang = pos.astype(F32)[:, None] * inv_freq[None, :]
    cos = jnp.repeat(jnp.cos(ang), 2, axis=1)
    sin = jnp.sin(ang)
    sin_signed = jnp.stack([-sin, sin], axis=-1).reshape(ang.shape[0], RET_DK)
    return jnp.tile(cos, (1, reps)), jnp.tile(sin_signed, (1, reps))


def _pool_body(x_ref, xp_ref, buf_ref, sh_ref, sc_ref, gate_ref, w_ref, cs_ref, o_ref, tail_ref, ext_scr,
               *, tm, pos0, has_prev):
    m = pl.program_id(1)
    sh, sc = sh_ref[0], sc_ref[0]
    h = _modulate(x_ref[0], sh, sc)

    @pl.when(m == 0)
    def _():
        ext_scr[0:POOL_HALO, :] = buf_ref[0]

    if has_prev:

        @pl.when(m > 0)
        def _():
            ext_scr[0:POOL_HALO, :] = _modulate(xp_ref[0], sh, sc)

    ext_scr[POOL_HALO:POOL_HALO + tm, :] = h
    tail_ref[0] = ext_scr[tm:tm + POOL_HALO, :]
    row = lax.broadcasted_iota(jnp.int32, (tm, 1), 0)
    pos1 = (pos0 + m * tm + row + 1).astype(F32)
    rows = max(tm, 16)
    ys = []
    for gi, w in enumerate(POOL_WINDOWS):
        c0, c1 = gi * POOL_GW, (gi + 1) * POOL_GW
        win = ext_scr[POOL_HALO:POOL_HALO + tm, c0:c1]
        for j in range(1, w):
            win = win + ext_scr[POOL_HALO - j:POOL_HALO - j + tm, c0:c1]
        d = win / jnp.minimum(jnp.float32(w), pos1) - h[:, c0:c1]
        if rows != tm:
            d = jnp.concatenate([d, jnp.zeros((rows - tm, POOL_GW), F32)], axis=0)
        y = jnp.dot(d.astype(BF16), w_ref[0, gi].astype(BF16), preferred_element_type=F32)
        ys.append(y[0:tm])
    y = jnp.concatenate(ys, axis=1) * cs_ref[...]
    o_ref[0] = x_ref[0] + gate_ref[0] * y


def pool_layer(x3, buf16, mod, gate, pool_w, pool_scale, wl, pos0, *, tm):
    B, L, D = x3.shape
    assert L % tm == 0 and (L == tm or tm % POOL_HALO == 0)
    has_prev = L > tm
    ph = POOL_HALO if has_prev else min(L, POOL_HALO)
    per = tm // POOL_HALO if has_prev else 1
    vec = lambda: pl.BlockSpec((1, 1, D), lambda b, m: (b, 0, 0))
    return pl.pallas_call(
        functools.partial(_pool_body, tm=tm, pos0=pos0, has_prev=has_prev),
        grid=(B, L // tm),
        in_specs=[
            pl.BlockSpec((1, tm, D), lambda b, m: (b, m, 0)),
            pl.BlockSpec((1, ph, D), lambda b, m: (b, jnp.maximum(m * per - 1, 0), 0)),
            pl.BlockSpec((1, POOL_HALO, D), lambda b, m: (b, 0, 0)),
            vec(), vec(), vec(),
            pl.BlockSpec((1,) + pool_w.shape[1:], lambda b, m: (wl, 0, 0, 0)),
            pl.BlockSpec((1, D), lambda b, m: (wl, 0)),
        ],
        out_specs=[
            pl.BlockSpec((1, tm, D), lambda b, m: (b, m, 0)),
            pl.BlockSpec((1, POOL_HALO, D), lambda b, m: (b, 0, 0)),
        ],
        out_shape=[
            jax.ShapeDtypeStruct((B, L, D), F32),
            jax.ShapeDtypeStruct((B, POOL_HALO, D), F32),
        ],
        scratch_shapes=[pltpu.VMEM((tm + POOL_HALO, D), F32)],
        compiler_params=_cparams(2),
        name="pool_layer",
    )(x3, x3, buf16, mod[0], mod[1], gate, pool_w, pool_scale)


def _split3(x):
    p0 = x.astype(BF16)
    r1 = x - p0.astype(F32)
    p1 = r1.astype(BF16)
    p2 = (r1 - p1.astype(F32)).astype(BF16)
    return p0, p1, p2


def _lane_cumsum(x, tri):
    p0, p1, p2 = _split3(x)
    dot = lambda p: jnp.dot(p, tri, preferred_element_type=F32)
    return (dot(p0) + dot(p1)) + dot(p2)


def _upper_tri(t):
    r = lax.broadcasted_iota(jnp.int32, (t, t), 0)
    c = lax.broadcasted_iota(jnp.int32, (t, t), 1)
    return jnp.where(r <= c, 1.0, 0.0).astype(BF16)


BIAS_PIECES = 3


def _fbias_body(lf_ref, o_ref, carry_scr, *, tc):
    @pl.when(pl.program_id(1) == 0)
    def _():
        carry_scr[...] = jnp.zeros_like(carry_scr)

    r = lax.broadcasted_iota(jnp.int32, (tc, tc), 0)
    c = lax.broadcasted_iota(jnp.int32, (tc, tc), 1)
    tril = jnp.where(c <= r, 1.0, 0.0).astype(BF16)
    p0, p1, p2 = _split3(lf_ref[0])
    dot = lambda p: jnp.dot(tril, p, preferred_element_type=F32)
    f = carry_scr[...] + ((dot(p0) + dot(p1)) + dot(p2))
    carry_scr[...] = f[tc - 1:tc, :]
    head = lax.broadcasted_iota(jnp.int32, (FOX_HEADS, LANES), 0)
    lane = lax.broadcasted_iota(jnp.int32, (FOX_HEADS, LANES), 1)
    out = None
    for p, piece in enumerate(_split3(-f)):
        place = jnp.where(lane == BIAS_PIECES * head + p, 1.0, 0.0).astype(BF16)
        term = jnp.dot(piece, place, preferred_element_type=F32)
        out = term if out is None else out + term
    o_ref[0] = out.astype(BF16)


def fox_bias_features(lf, *, tc=512):
    B, L, H = lf.shape
    tc = min(tc, L)
    return pl.pallas_call(
        functools.partial(_fbias_body, tc=tc),
        grid=(B, L // tc),
        in_specs=[pl.BlockSpec((1, tc, H), lambda b, c: (b, c, 0))],
        out_specs=pl.BlockSpec((1, tc, LANES), lambda b, c: (b, c, 0)),
        out_shape=jax.ShapeDtypeStruct((B, L, LANES), BF16),
        scratch_shapes=[pltpu.VMEM((1, H), F32)],
        compiler_params=_cparams(2),
        name="fox_bias_features",
    )(lf)


def _flash_body(qt_ref, k_ref, fb_ref, vt_ref, o_ref, *, tq):
    hp = pl.program_id(1)
    qi = pl.program_id(2)
    pair = 2 * FOX_DH
    row = lax.broadcasted_iota(jnp.int32, (pair, tq), 0)
    qt = qt_ref[0]
    qaug = []
    for i in range(2):
        q_head = jnp.where(row // FOX_DH == i, qt, jnp.zeros_like(qt))
        pick = jnp.where(row // BIAS_PIECES == 2 * hp + i, 1.0, 0.0).astype(BF16)
        qaug.append(jnp.concatenate([q_head, pick], axis=0))
    key_i = lax.broadcasted_iota(jnp.int32, (tq, tq), 0)
    qry_i = lax.broadcasted_iota(jnp.int32, (tq, tq), 1)
    causal = key_i <= qry_i

    def step(j, carry, masked):
        k0 = pl.multiple_of(j * tq, tq)
        kaug = jnp.concatenate([k_ref[0, pl.ds(k0, tq), :], fb_ref[0, pl.ds(k0, tq), :]], axis=1)
        new = []
        for i in range(2):
            m_old, l_old, acc = carry[i]
            st = jnp.dot(kaug, qaug[i], preferred_element_type=F32)
            if masked:
                st = jnp.where(causal, st, NEG_BIG)
            m_new = jnp.maximum(m_old, jnp.max(st, axis=0, keepdims=True))
            alpha = jnp.exp(m_old - m_new)
            p = jnp.exp(st - m_new)
            l_new = alpha * l_old + jnp.sum(p, axis=0, keepdims=True)
            vt = vt_ref[0, i * FOX_DH:(i + 1) * FOX_DH, pl.ds(k0, tq)].astype(BF16)
            acc = alpha * acc + jnp.dot(vt, p.astype(BF16), preferred_element_type=F32)
            new.append((m_new, l_new, acc))
        return tuple(new)

    init = tuple((jnp.full((1, tq), NEG_BIG, F32), jnp.zeros((1, tq), F32), jnp.zeros((FOX_DH, tq), F32))
                 for _ in range(2))
    carry = lax.fori_loop(0, qi, lambda j, c: step(j, c, False), init)
    carry = step(qi, carry, True)
    o_ref[0] = jnp.concatenate([(acc / l).T for _, l, acc in carry], axis=1).astype(o_ref.dtype)


def fox_flash(qt, k, fb, vt, *, tq=512):
    B, L, D = k.shape
    tq = min(tq, L)
    pair = 2 * FOX_DH
    return pl.pallas_call(
        functools.partial(_flash_body, tq=tq),
        grid=(B, FOX_HEADS // 2, L // tq),
        in_specs=[
            pl.BlockSpec((1, pair, tq), lambda b, hp, qi: (b, hp, qi)),
            pl.BlockSpec((1, L, pair), lambda b, hp, qi: (b, 0, hp)),
            pl.BlockSpec((1, L, LANES), lambda b, hp, qi: (b, 0, 0)),
            pl.BlockSpec((1, pair, L), lambda b, hp, qi: (b, hp, 0)),
        ],
        out_specs=pl.BlockSpec((1, tq, pair), lambda b, hp, qi: (b, qi, hp)),
        out_shape=jax.ShapeDtypeStruct((B, L, D), BF16),
        compiler_params=_cparams(3),
        name="fox_flash",
    )(qt, k, fb, vt)


DEC_PAGES = 8


def _decode_body(pt_ref, q_ref, kn_ref, vn_ref, lfn_ref, *refs, n_steps, lq):
    kc = refs[0:DEC_PAGES]
    vc = refs[DEC_PAGES:2 * DEC_PAGES]
    lc = refs[2 * DEC_PAGES:3 * DEC_PAGES]
    o_ref, qbd_scr, m_scr, l_scr, acc_scr, carry_scr = refs[3 * DEC_PAGES:]
    b = pl.program_id(0)
    st = pl.program_id(1)
    rows = lq * FOX_HEADS
    last = (((1,), (1,)), ((), ()))
    tri = _upper_tri(PAGE_SIZE)

    @pl.when(st == 0)
    def _():
        head = lax.broadcasted_iota(jnp.int32, (FOX_HEADS, D_MODEL), 0)
        lane_head = lax.broadcasted_iota(jnp.int32, (FOX_HEADS, D_MODEL), 1) // FOX_DH
        blocks = [jnp.where(head == lane_head, jnp.broadcast_to(q_ref[0, t:t + 1, :], (FOX_HEADS, D_MODEL)), 0.0)
                  for t in range(lq)]
        qbd_scr[...] = jnp.concatenate(blocks, axis=0).astype(BF16)
        m_scr[...] = jnp.full_like(m_scr, NEG_BIG)
        l_scr[...] = jnp.zeros_like(l_scr)
        acc_scr[...] = jnp.zeros_like(acc_scr)
        carry_scr[...] = jnp.zeros_like(carry_scr)

    def absorb(s, v_mat, v_is_t):
        m_old = m_scr[...]
        m_new = jnp.maximum(m_old, jnp.max(s, axis=1, keepdims=True))
        alpha = jnp.exp(m_old - m_new)
        p = jnp.exp(s - m_new)
        l_scr[...] = alpha * l_scr[...] + jnp.sum(p, axis=1, keepdims=True)
        if v_is_t:
            pv = lax.dot_general(p.astype(BF16), v_mat, last, preferred_element_type=F32)
        else:
            pv = jnp.dot(p.astype(BF16), v_mat, preferred_element_type=F32)
        acc_scr[...] = alpha * acc_scr[...] + pv
        m_scr[...] = m_new

    @pl.when(st < n_steps)
    def _():
        kt = jnp.concatenate([kc[i][0].reshape(D_MODEL, PAGE_SIZE).astype(BF16) for i in range(DEC_PAGES)], axis=1)
        vt = jnp.concatenate([vc[i][0].reshape(D_MODEL, PAGE_SIZE).astype(BF16) for i in range(DEC_PAGES)], axis=1)
        within = _lane_cumsum(jnp.concatenate([lc[i][0] for i in range(DEC_PAGES)], axis=0), tri)
        f = carry_scr[...]
        biases = []
        for i in range(DEC_PAGES):
            f_page = f + within[i * FOX_HEADS:(i + 1) * FOX_HEADS, :]
            biases.append(jnp.tile(f_page, (lq, 1)))
            f = jnp.broadcast_to(f_page[:, PAGE_SIZE - 1:PAGE_SIZE], f_page.shape)
        carry_scr[...] = f
        s = jnp.dot(qbd_scr[...], kt, preferred_element_type=F32) - jnp.concatenate(biases, axis=1)
        absorb(s, vt, True)

    @pl.when(st == n_steps)
    def _():
        pad = jnp.zeros((PAGE_SIZE - lq, D_MODEL), F32)
        kn = jnp.concatenate([kn_ref[0], pad], axis=0).astype(BF16)
        vn = jnp.concatenate([vn_ref[0], pad], axis=0).astype(BF16)
        n_tok = lfn_ref.shape[1]
        tok = lax.broadcasted_iota(jnp.int32, (n_tok, PAGE_SIZE), 0)
        key = lax.broadcasted_iota(jnp.int32, (n_tok, PAGE_SIZE), 1)
        sel = jnp.where((tok // lq == b) & (tok % lq <= key) & (key < lq), 1.0, 0.0).astype(BF16)
        p0, p1, p2 = _split3(lfn_ref[...])
        dot = lambda p: jnp.dot(p, sel, preferred_element_type=F32)
        f = carry_scr[...] + ((dot(p0) + dot(p1)) + dot(p2))
        s = lax.dot_general(qbd_scr[...], kn, last, preferred_element_type=F32) - jnp.tile(f, (lq, 1))
        rq = lax.broadcasted_iota(jnp.int32, (rows, PAGE_SIZE), 0) // FOX_HEADS
        kk = lax.broadcasted_iota(jnp.int32, (rows, PAGE_SIZE), 1)
        s = jnp.where(kk <= rq, s, NEG_BIG)
        absorb(s, vn, False)
        o = acc_scr[...] / l_scr[...]
        head = lax.broadcasted_iota(jnp.int32, (FOX_HEADS, D_MODEL), 0)
        lane_head = lax.broadcasted_iota(jnp.int32, (FOX_HEADS, D_MODEL), 1) // FOX_DH
        outs = [jnp.sum(jnp.where(head == lane_head, o[t * FOX_HEADS:(t + 1) * FOX_HEADS, :], 0.0), axis=0,
                        keepdims=True) for t in range(lq)]
        o_ref[0] = jnp.concatenate(outs, axis=0).astype(o_ref.dtype)


def fox_decode(q, k_new, v_new, lft_new, cache_kt, cache_vt, cache_lt, page_table):
    B, lq, D = q.shape
    n_pages = page_table.shape[1]
    assert n_pages % DEC_PAGES == 0
    n_steps = n_pages // DEC_PAGES
    rows = lq * FOX_HEADS

    def page_idx(i):
        return lambda b, s, pt: (pt[b * n_pages + jnp.minimum(s, n_steps - 1) * DEC_PAGES + i], 0, 0, 0)

    def page_idx3(i):
        return lambda b, s, pt: (pt[b * n_pages + jnp.minimum(s, n_steps - 1) * DEC_PAGES + i], 0, 0)

    seq = lambda: pl.BlockSpec((1, lq, D), lambda b, s, pt: (b, 0, 0))
    in_specs = [seq(), seq(), seq(), pl.BlockSpec(lft_new.shape, lambda b, s, pt: (0, 0))]
    in_specs += [pl.BlockSpec((1, FOX_HEADS, FOX_DH, PAGE_SIZE), page_idx(i)) for i in range(DEC_PAGES)]
    in_specs += [pl.BlockSpec((1, FOX_HEADS, FOX_DH, PAGE_SIZE), page_idx(i)) for i in range(DEC_PAGES)]
    in_specs += [pl.BlockSpec((1, FOX_HEADS, PAGE_SIZE), page_idx3(i)) for i in range(DEC_PAGES)]
    grid_spec = pltpu.PrefetchScalarGridSpec(
        num_scalar_prefetch=1,
        grid=(B, n_steps + 1),
        in_specs=in_specs,
        out_specs=pl.BlockSpec((1, lq, D), lambda b, s, pt: (b, 0, 0)),
        scratch_shapes=[
            pltpu.VMEM((rows, D), BF16),
            pltpu.VMEM((rows, 1), F32),
            pltpu.VMEM((rows, 1), F32),
            pltpu.VMEM((rows, D), F32),
            pltpu.VMEM((FOX_HEADS, PAGE_SIZE), F32),
        ],
    )
    return pl.pallas_call(
        functools.partial(_decode_body, n_steps=n_steps, lq=lq),
        grid_spec=grid_spec,
        out_shape=jax.ShapeDtypeStruct((B, lq, D), F32),
        compiler_params=_cparams(2),
        name="fox_decode",
    )(page_table.reshape(-1), q, k_new, v_new, lft_new,
      *([cache_kt] * DEC_PAGES), *([cache_vt] * DEC_PAGES), *([cache_lt] * DEC_PAGES))


def _final_body(x_ref, g_ref, o_ref):
    xf = x_ref[0]
    o_ref[0] = xf * lax.rsqrt(jnp.mean(xf * xf, axis=-1, keepdims=True) + EPS) * g_ref[...]


def final_norm(x3, final_g, *, tm=None):
    bx, L, D = x3.shape
    tm = tm or min(L, 1024)
    return pl.pallas_call(
        _final_body,
        grid=(bx, L // tm),
        in_specs=[pl.BlockSpec((1, tm, D), lambda b, m: (b, m, 0)), pl.BlockSpec((1, D), lambda b, m: (0, 0))],
        out_specs=pl.BlockSpec((1, tm, D), lambda b, m: (b, m, 0)),
        out_shape=jax.ShapeDtypeStruct((bx, L, D), F32),
        compiler_params=_cparams(2),
        name="final_norm",
    )(x3, final_g.reshape(1, D))


def _trunk(x, mods, ret_state, pool_state, fox_past, pos0, params):
    (ret_w_in, ret_w_out, pool_w, pool_scale, fox_wt, fox_b_f, fox_w_out, ffn_w_gu, ffn_w_down,
     w_rt_pad, b_rt_pad, moe_w_gu, moe_w_down, final_g) = params
    B, L, D = x.shape
    decode = fox_past is not None
    if decode:
        x3 = x.reshape(1, B * L, D)
        expand = lambda v: jnp.repeat(v, L, axis=0)[None]
    else:
        x3 = x
        expand = lambda v: v[:, None, :]
    n_rows = x3.shape[1]
    pos = pos0 + jnp.arange(L)
    cos, sin = rotary_tables(pos, RET_HEADS)
    if decode:
        cos, sin = jnp.tile(cos, (B, 1)), jnp.tile(sin, (B, 1))
    chunk = min(L, 256)
    chunk_pad = max(chunk, PAGE_SIZE)
    tables = retention_tables(chunk, chunk_pad)
    ret_new, extras = [], {}
    for i in range(DEPTH):
        sh_a, sc_a, g_a, sh_f, sc_f, g_f = [expand(v) for v in jnp.split(mods[i], 6, axis=-1)]
        kind, j = i % N_MIXERS, i // N_MIXERS
        if kind == 0:
            proj = functools.partial(mm, x3, ret_w_in, j, mod=(sh_a, sc_a))
            q = proj(n0=0, n_out=RET_QK, out_dtype=BF16, epi="rot", rot=(cos, sin), name="ret_q")
            k = proj(n0=RET_QK, n_out=RET_QK, out_dtype=BF16, epi="rot", rot=(cos, sin), scale=RET_DK ** -0.5,
                     name="ret_k")
            v = proj(n0=2 * RET_QK, n_out=RET_V, out_dtype=BF16, name="ret_v")
            g = proj(n0=2 * RET_QK + RET_V, n_out=RET_V, out_dtype=F32, name="ret_g")
            if decode:
                padr = lambda t: jnp.pad(t.reshape(B, L, -1), ((0, 0), (0, chunk_pad - L), (0, 0)))
                o, s = retention_scan(padr(q), padr(k), padr(v), padr(g), tables, chunk_pad, s0=ret_state,
                                      s0_layer=j)
                o = o[:, :L].reshape(1, n_rows, RET_V)
            else:
                o, s = retention_scan(q, k, v, g, tables, chunk_pad)
            ret_new.append(s)
            x3 = mm(o, ret_w_out, j, n0=0, n_out=D, out_dtype=F32, epi="res", res=(x3, g_a), name="ret_out")
        elif kind == 1:
            vecs = [v[:, None, :] for v in jnp.split(mods[i], 6, axis=-1)[:3]]
            if decode:
                buf16 = jnp.pad(pool_state[j], ((0, 0), (1, 0), (0, 0)))
                tm = L
            else:
                buf16 = jnp.zeros((B, POOL_HALO, D), F32)
                tm = min(L, 512)
            xn, tail = pool_layer(x3.reshape(B, L, D), buf16, (vecs[0], vecs[1]), vecs[2], pool_w, pool_scale, j,
                                  pos0, tm=tm)
            x3 = xn.reshape(x3.shape)
            extras["pool"] = tail[:, 1:, :]
        else:
            fproj = functools.partial(mm, x3, fox_wt, j, mod=(sh_a, sc_a), w_t=True)
            if decode:
                ck, cv, cl, pt = fox_past
                q = fproj(n0=0, n_out=D, out_dtype=F32, scale=FOX_DH ** -0.5, name="fox_q")
                lft = fproj(n0=3 * D, n_out=FOX_HEADS, out_dtype=F32, out_t=True, epi="logsig",
                            bias=fox_b_f[j].reshape(FOX_HEADS, 1), name="fox_logft")
                k = fproj(n0=D, n_out=D, out_dtype=F32, name="fox_k")
                v = fproj(n0=2 * D, n_out=D, out_dtype=F32, name="fox_v")
                o = fox_decode(q.reshape(B, L, D), k.reshape(B, L, D), v.reshape(B, L, D), lft[0],
                               jnp.transpose(ck[j], (0, 2, 3, 1)), jnp.transpose(cv[j], (0, 2, 3, 1)),
                               jnp.transpose(cl[j], (0, 2, 1)), pt)
                o = o.reshape(1, n_rows, D)
                extras["k"] = k.reshape(B, L, FOX_HEADS, FOX_DH)
                extras["v"] = v.reshape(B, L, FOX_HEADS, FOX_DH)
                extras["l"] = jnp.transpose(lft[0].reshape(FOX_HEADS, B, L), (1, 2, 0))
            else:
                qt = fproj(n0=0, n_out=D, out_dtype=BF16, out_t=True, scale=FOX_DH ** -0.5, name="fox_qt")
                kb = fproj(n0=D, n_out=D, out_dtype=BF16, name="fox_kb")
                kt = fproj(n0=D, n_out=D, out_dtype=F32, out_t=True, name="fox_kt")
                vt = fproj(n0=2 * D, n_out=D, out_dtype=F32, out_t=True, name="fox_vt")
                lf = fproj(n0=3 * D, n_out=FOX_HEADS, out_dtype=F32, epi="logsig",
                           bias=fox_b_f[j].reshape(1, FOX_HEADS), name="fox_logf")
                o = fox_flash(qt, kb, fox_bias_features(lf), vt)
                unt = lambda t: jnp.transpose(t.reshape(B, FOX_HEADS, FOX_DH, L), (0, 3, 1, 2))
                extras["k"], extras["v"] = unt(kt), unt(vt)
                extras["l"] = lf
            x3 = mm(o, fox_w_out, j, n0=0, n_out=D, out_dtype=F32, epi="res", res=(x3, g_a), name="fox_out")
        ml = i // 2
        if i % 2 == 0:
            x3 = ffn(x3, (sh_f, sc_f), g_f, ffn_w_gu, ffn_w_down, ml, name="ffn_dense")
        else:
            gates = router(x3, (sh_f, sc_f), w_rt_pad, b_rt_pad, ml)
            if decode:
                x3 = ffn(x3, (sh_f, sc_f), g_f, moe_w_gu, moe_w_down, ml * N_EXPERTS, gates=gates, name="ffn_moe")
            else:
                x3 = moe_sparse(x3, (sh_f, sc_f), g_f, gates, moe_w_gu, moe_w_down, ml * N_EXPERTS)
    out = final_norm(x3, final_g).reshape(B, L, D)
    return (out, jnp.stack(ret_new), extras["pool"][None], extras["k"][None], extras["v"][None], extras["l"][None])


def kernel(x_prompt, x_sample, state_ret, state_pool, cache_fox_k, cache_fox_v, cache_fox_logf, page_table,
           c_prompt, c_sample, ada_w, ada_b, ret_w_in, ret_w_out, pool_w, pool_scale, fox_w_in, fox_b_f, fox_w_out,
           ffn_w_gu, ffn_w_down, moe_w_router, moe_b_router, moe_w_gu, moe_w_down, final_g):
    bp, bs = x_prompt.shape[0], x_sample.shape[0]
    rows = -(-(bp + bs) // 8) * 8
    c_all = jnp.concatenate([c_prompt, c_sample, jnp.zeros((rows - bp - bs, D_MODEL), F32)], axis=0)
    mods = ada_mods(c_all, ada_w, ada_b)
    n_moe = moe_w_router.shape[0]
    params = (
        ret_w_in, ret_w_out, pool_w, pool_scale,
        jnp.swapaxes(fox_w_in, 1, 2),
        fox_b_f, fox_w_out, ffn_w_gu, ffn_w_down,
        jnp.pad(jnp.swapaxes(moe_w_router, 1, 2), ((0, 0), (0, LANES - N_EXPERTS), (0, 0))),
        jnp.pad(moe_b_router, ((0, 0), (0, LANES - N_EXPERTS))).reshape(n_moe, 1, LANES),
        moe_w_gu.reshape((n_moe * N_EXPERTS,) + moe_w_gu.shape[2:]),
        moe_w_down.reshape((n_moe * N_EXPERTS,) + moe_w_down.shape[2:]),
        final_g,
    )
    y_p, ret_p, pool_p, k_p, v_p, l_p = _trunk(x_prompt, mods[:, :bp], None, None, None, 0, params)
    n_past = page_table.shape[1] * PAGE_SIZE
    y_s, ret_s, pool_s, k_s, v_s, l_s = _trunk(
        x_sample, mods[:, bp:bp + bs], state_ret, state_pool,
        (cache_fox_k, cache_fox_v, cache_fox_logf, page_table), n_past, params)
    return (y_p, y_s, ret_p, ret_s, pool_p, pool_s, k_p, k_s, v_p, v_s, l_p, l_s)
```

```python
import functools
import math

import jax
import jax.numpy as jnp
from jax import lax
from jax.experimental import pallas as pl
from jax.experimental.pallas import tpu as pltpu

F32 = jnp.float32
BF16 = jnp.bfloat16

D_MODEL = 1024
DEPTH = 4
PAGE_SIZE = 128
N_MIXERS = 3
RET_HEADS = 4
RET_DK = D_MODEL // RET_HEADS
RET_DV = 2 * D_MODEL // RET_HEADS
RET_QK = RET_HEADS * RET_DK
RET_V = RET_HEADS * RET_DV
ROPE_BASE = 10000.0
POOL_WINDOWS = (2, 4, 8, 16)
POOL_GW = D_MODEL // len(POOL_WINDOWS)
POOL_BUF = max(POOL_WINDOWS) - 1
POOL_HALO = POOL_BUF + 1
FOX_HEADS = 16
FOX_DH = D_MODEL // FOX_HEADS
FFN_DIM = 2816
N_EXPERTS = 8
EPS = 1e-6
NEG_BIG = -1e30

V7X_VMEM_BYTES = 64 * 1024 * 1024
VMEM_LIMIT = V7X_VMEM_BYTES - 8 * 1024 * 1024
LANES = 128
FFN_TF = 256


def _cparams(n_axes):
    return pltpu.CompilerParams(dimension_semantics=("arbitrary",) * n_axes, vmem_limit_bytes=VMEM_LIMIT)


def _sigmoid(x):
    return 1.0 / (1.0 + jnp.exp(-x))


def _modulate(x, shift, scale):
    xf = x.astype(F32)
    ms = jnp.mean(xf * xf, axis=-1, keepdims=True)
    return (xf * lax.rsqrt(ms + EPS)) * (1.0 + scale) + shift


def _row_spec(arr, tm, width, col_fn):
    if arr.shape[1] == 1:
        return pl.BlockSpec((1, 1, width), lambda b, m, *r: (b, 0, col_fn(*r)))
    return pl.BlockSpec((1, tm, width), lambda b, m, *r: (b, m, col_fn(*r)))


def _ada_body(c_ref, w_ref, b_ref, o_ref):
    c = c_ref[...]
    cond = c * _sigmoid(c)
    o_ref[0] = jnp.dot(cond.astype(BF16), w_ref[0].astype(BF16), preferred_element_type=F32) + b_ref[0]


def ada_mods(c_all, ada_w, ada_b):
    rows = c_all.shape[0]
    n_out = ada_w.shape[2]
    tn = 1024
    return pl.pallas_call(
        _ada_body,
        grid=(DEPTH, n_out // tn),
        in_specs=[
            pl.BlockSpec((rows, D_MODEL), lambda i, n: (0, 0)),
            pl.BlockSpec((1, D_MODEL, tn), lambda i, n: (i, 0, n)),
            pl.BlockSpec((1, 1, tn), lambda i, n: (i, 0, n)),
        ],
        out_specs=pl.BlockSpec((1, rows, tn), lambda i, n: (i, 0, n)),
        out_shape=jax.ShapeDtypeStruct((DEPTH, rows, n_out), F32),
        compiler_params=_cparams(2),
        name="ada_mods",
    )(c_all, ada_w, ada_b.reshape(DEPTH, 1, n_out))


def _mm_body(*refs, has_mod, epi, w_t, out_t, scale):
    it = iter(refs)
    x_ref = next(it)
    if has_mod:
        sh_ref, sc_ref = next(it), next(it)
    w_ref = next(it)
    if epi == "rot":
        cos_ref, sin_ref = next(it), next(it)
    elif epi == "res":
        res_ref, gate_ref = next(it), next(it)
    elif epi == "logsig":
        b_ref = next(it)
    o_ref = next(it)
    n = pl.program_id(2)
    if has_mod:
        h_scr = next(it)

        @pl.when(n == 0)
        def _():
            h_scr[...] = _modulate(x_ref[0], sh_ref[0], sc_ref[0]).astype(BF16)

        lhs = h_scr[...]
    else:
        lhs = x_ref[0].astype(BF16)
    w = w_ref[0].astype(BF16)
    last = (((1,), (1,)), ((), ()))
    if not w_t:
        acc = jnp.dot(lhs, w, preferred_element_type=F32)
    elif not out_t:
        acc = lax.dot_general(lhs, w, last, preferred_element_type=F32)
    else:
        acc = lax.dot_general(w, lhs, last, preferred_element_type=F32)
    if epi == "rot":
        acc = _rotary(acc, cos_ref[...], sin_ref[...])
    elif epi == "res":
        acc = res_ref[0] + gate_ref[0] * acc
    elif epi == "logsig":
        z = acc + b_ref[...]
        acc = jnp.minimum(z, 0.0) - jnp.log1p(jnp.exp(-jnp.abs(z)))
    if scale != 1.0:
        acc = acc * scale
    o_ref[0] = acc.astype(o_ref.dtype)


def mm(x3, w3, wl, *, n0, n_out, out_dtype, mod=None, epi="plain", w_t=False, out_t=False, scale=1.0,
       rot=None, res=None, bias=None, tm=None, tn=None, name="mm"):
    bx, L, K = x3.shape
    tm = tm or min(L, 512 if epi == "rot" else 1024)
    tn = tn or min(n_out, 512 if K > 1024 else 1024)
    assert L % tm == 0 and n_out % tn == 0 and n0 % tn == 0
    nb0 = n0 // tn
    has_mod = mod is not None
    in_specs = [pl.BlockSpec((1, tm, K), lambda b, m, n: (b, m, 0))]
    args = [x3]
    if has_mod:
        for a in mod:
            in_specs.append(_row_spec(a, tm, K, lambda n: 0))
            args.append(a)
    if w_t:
        in_specs.append(pl.BlockSpec((1, tn, K), lambda b, m, n: (wl, nb0 + n, 0)))
    else:
        in_specs.append(pl.BlockSpec((1, K, tn), lambda b, m, n: (wl, 0, nb0 + n)))
    args.append(w3)
    if epi == "rot":
        for a in rot:
            in_specs.append(pl.BlockSpec((tm, tn), lambda b, m, n: (m, n)))
            args.append(a)
    elif epi == "res":
        in_specs.append(pl.BlockSpec((1, tm, tn), lambda b, m, n: (b, m, n)))
        in_specs.append(_row_spec(res[1], tm, tn, lambda n: n))
        args.extend(res)
    elif epi == "logsig":
        in_specs.append(pl.BlockSpec(bias.shape, lambda b, m, n: (0, 0)))
        args.append(bias)
    if out_t:
        out_spec = pl.BlockSpec((1, tn, tm), lambda b, m, n: (b, n, m))
        out_shape = jax.ShapeDtypeStruct((bx, n_out, L), out_dtype)
    else:
        out_spec = pl.BlockSpec((1, tm, tn), lambda b, m, n: (b, m, n))
        out_shape = jax.ShapeDtypeStruct((bx, L, n_out), out_dtype)
    return pl.pallas_call(
        functools.partial(_mm_body, has_mod=has_mod, epi=epi, w_t=w_t, out_t=out_t, scale=scale),
        grid=(bx, L // tm, n_out // tn),
        in_specs=in_specs,
        out_specs=out_spec,
        out_shape=out_shape,
        scratch_shapes=[pltpu.VMEM((tm, K), BF16)] if has_mod else [],
        compiler_params=_cparams(3),
        name=name,
    )(*args)


def _rotary(acc, cos, sin_signed):
    width = acc.shape[1]
    lane = lax.broadcasted_iota(jnp.int32, acc.shape, 1)
    partner = jnp.where(lane % 2 == 0, pltpu.roll(acc, width - 1, axis=1), pltpu.roll(acc, 1, axis=1))
    return acc * cos + partner * sin_signed


def _retproj_body(x_ref, sh_ref, sc_ref, w_ref, cos_ref, sin_ref, q_ref, k_ref, v_ref, g_ref, h_scr, *, nv):
    n = pl.program_id(2)

    @pl.when(n == 0)
    def _():
        h_scr[...] = _modulate(x_ref[0], sh_ref[0], sc_ref[0]).astype(BF16)

    acc = jnp.dot(h_scr[...], w_ref[0].astype(BF16), preferred_element_type=F32)

    @pl.when(n == 0)
    def _():
        q_ref[0] = _rotary(acc, cos_ref[...], sin_ref[...]).astype(q_ref.dtype)

    @pl.when(n == 1)
    def _():
        k_ref[0] = (_rotary(acc, cos_ref[...], sin_ref[...]) * (RET_DK ** -0.5)).astype(k_ref.dtype)

    @pl.when((n >= 2) & (n < 2 + nv))
    def _():
        v_ref[0] = acc.astype(v_ref.dtype)

    @pl.when(n >= 2 + nv)
    def _():
        g_ref[0] = acc


def ret_proj(x3, mod, w3, wl, rot, *, tm=None):
    bx, L, K = x3.shape
    tn = RET_QK
    nv = RET_V // tn
    tm = tm or min(L, 512)
    assert L % tm == 0 and RET_V % tn == 0
    row = lambda a: _row_spec(a, tm, K, lambda n: 0)
    tab = lambda: pl.BlockSpec((tm, tn), lambda b, m, n: (m, 0))
    clip = lambda n, lo: jnp.clip(n - lo, 0, nv - 1)
    return pl.pallas_call(
        functools.partial(_retproj_body, nv=nv),
        grid=(bx, L // tm, 2 + 2 * nv),
        in_specs=[
            pl.BlockSpec((1, tm, K), lambda b, m, n: (b, m, 0)), row(mod[0]), row(mod[1]),
            pl.BlockSpec((1, K, tn), lambda b, m, n: (wl, 0, n)),
            tab(), tab(),
        ],
        out_specs=[
            pl.BlockSpec((1, tm, tn), lambda b, m, n: (b, m, 0)),
            pl.BlockSpec((1, tm, tn), lambda b, m, n: (b, m, 0)),
            pl.BlockSpec((1, tm, tn), lambda b, m, n: (b, m, clip(n, 2))),
            pl.BlockSpec((1, tm, tn), lambda b, m, n: (b, m, clip(n, 2 + nv))),
        ],
        out_shape=[
            jax.ShapeDtypeStruct((bx, L, RET_QK), BF16),
            jax.ShapeDtypeStruct((bx, L, RET_QK), BF16),
            jax.ShapeDtypeStruct((bx, L, RET_V), BF16),
            jax.ShapeDtypeStruct((bx, L, RET_V), F32),
        ],
        scratch_shapes=[pltpu.VMEM((tm, K), BF16)],
        compiler_params=_cparams(3),
        name="ret_proj",
    )(x3, mod[0], mod[1], w3, rot[0], rot[1])


def _ffn_body(*refs, moe, nf, ne):
    it = iter(refs)
    x_ref, sh_ref, sc_ref, gate_ref = next(it), next(it), next(it), next(it)
    gw_ref = next(it) if moe else None
    wg_ref, wu_ref, wd_ref, o_ref, h_scr, acc_scr = next(it), next(it), next(it), next(it), next(it), next(it)
    tot_scr = next(it) if moe else None
    e = pl.program_id(2)
    f = pl.program_id(3)

    @pl.when((e == 0) & (f == 0))
    def _():
        h_scr[...] = _modulate(x_ref[0], sh_ref[0], sc_ref[0]).astype(BF16)

    h = h_scr[...]
    a = jnp.dot(h, wg_ref[0].astype(BF16), preferred_element_type=F32)
    b = jnp.dot(h, wu_ref[0].astype(BF16), preferred_element_type=F32)
    mid = (a * _sigmoid(a) * b).astype(BF16)
    y = jnp.dot(mid, wd_ref[0].astype(BF16), preferred_element_type=F32)

    @pl.when(f == 0)
    def _():
        acc_scr[...] = y

    @pl.when(f > 0)
    def _():
        acc_scr[...] += y

    if not moe:

        @pl.when(f == nf - 1)
        def _():
            o_ref[0] = x_ref[0] + gate_ref[0] * acc_scr[...]

    else:

        @pl.when(f == nf - 1)
        def _():
            gw = gw_ref[0]
            lane = lax.broadcasted_iota(jnp.int32, gw.shape, 1)
            col = jnp.sum(jnp.where(lane == e, gw, 0.0), axis=1, keepdims=True)
            contrib = col * acc_scr[...]

            @pl.when(e == 0)
            def _():
                tot_scr[...] = contrib

            @pl.when(e > 0)
            def _():
                tot_scr[...] += contrib

            @pl.when(e == ne - 1)
            def _():
                o_ref[0] = x_ref[0] + gate_ref[0] * tot_scr[...]


def ffn(x3, mod, gate, w_gu3, w_down3, wl, *, gates=None, tm=None, name="ffn"):
    bx, L, D = x3.shape
    moe = gates is not None
    ne = N_EXPERTS if moe else 1
    tm = tm or min(L, 1024)
    tf = FFN_TF
    nf = FFN_DIM // tf
    assert L % tm == 0 and FFN_DIM % tf == 0
    in_specs = [pl.BlockSpec((1, tm, D), lambda b, m, e, f: (b, m, 0))]
    args = [x3]
    for a in (*mod, gate):
        in_specs.append(_row_spec(a, tm, D, lambda e, f: 0))
        args.append(a)
    if moe:
        in_specs.append(pl.BlockSpec((1, tm, LANES), lambda b, m, e, f: (b, m, 0)))
        args.append(gates)
    in_specs += [
        pl.BlockSpec((1, D, tf), lambda b, m, e, f: (wl + e, 0, f)),
        pl.BlockSpec((1, D, tf), lambda b, m, e, f: (wl + e, 0, nf + f)),
        pl.BlockSpec((1, tf, D), lambda b, m, e, f: (wl + e, f, 0)),
    ]
    args += [w_gu3, w_gu3, w_down3]
    scratch = [pltpu.VMEM((tm, D), BF16), pltpu.VMEM((tm, D), F32)]
    if moe:
        scratch.append(pltpu.VMEM((tm, D), F32))
    return pl.pallas_call(
        functools.partial(_ffn_body, moe=moe, nf=nf, ne=ne),
        grid=(bx, L // tm, ne, nf),
        in_specs=in_specs,
        out_specs=pl.BlockSpec((1, tm, D), lambda b, m, e, f: (b, m, 0)),
        out_shape=jax.ShapeDtypeStruct((bx, L, D), F32),
        scratch_shapes=scratch,
        compiler_params=_cparams(4),
        name=name,
    )(*args)


def _router_body(x_ref, sh_ref, sc_ref, w_ref, b_ref, o_ref):
    h = _modulate(x_ref[0], sh_ref[0], sc_ref[0]).astype(BF16)
    logits = lax.dot_general(h, w_ref[0].astype(BF16), (((1,), (1,)), ((), ())), preferred_element_type=F32)
    logits = logits + b_ref[0]
    lane = lax.broadcasted_iota(jnp.int32, logits.shape, 1).astype(F32)
    lg = jnp.where(lane < N_EXPERTS, logits, -jnp.inf)
    m1 = jnp.max(lg, axis=1, keepdims=True)
    i1 = jnp.min(jnp.where(lg == m1, lane, float(LANES)), axis=1, keepdims=True)
    lg2 = jnp.where(lane == i1, -jnp.inf, lg)
    m2 = jnp.max(lg2, axis=1, keepdims=True)
    i2 = jnp.min(jnp.where(lg2 == m2, lane, float(LANES)), axis=1, keepdims=True)
    e2 = jnp.exp(m2 - m1)
    den = 1.0 + e2
    o_ref[0] = jnp.where(lane == i1, 1.0 / den, 0.0) + jnp.where(lane == i2, e2 / den, 0.0)


def router(x3, mod, w_rt_pad, b_pad, wl, *, tm=None):
    bx, L, D = x3.shape
    tm = tm or min(L, 1024)
    in_specs = [pl.BlockSpec((1, tm, D), lambda b, m: (b, m, 0))]
    args = [x3]
    for a in mod:
        in_specs.append(_row_spec(a, tm, D, lambda: 0))
        args.append(a)
    in_specs += [
        pl.BlockSpec((1, LANES, D), lambda b, m: (wl, 0, 0)),
        pl.BlockSpec((1, 1, LANES), lambda b, m: (wl, 0, 0)),
    ]
    args += [w_rt_pad, b_pad]
    return pl.pallas_call(
        _router_body,
        grid=(bx, L // tm),
        in_specs=in_specs,
        out_specs=pl.BlockSpec((1, tm, LANES), lambda b, m: (b, m, 0)),
        out_shape=jax.ShapeDtypeStruct((bx, L, LANES), F32),
        compiler_params=_cparams(2),
        name="router",
    )(*args)


MOE_T = 256
MOE_CH = 128
MOE_TM = 256
MOE_TF = FFN_DIM // 2
MOE_ALIGN = 8
MOE_STAGE = -(-(2 * MOE_T + N_EXPERTS * (MOE_ALIGN - 1) + MOE_CH) // 8) * 8


def _moe_plan(gates2, n_tokens):
    nb = n_tokens // MOE_T
    routed = (gates2[:, :N_EXPERTS] > 0).reshape(nb, MOE_T, N_EXPERTS)
    ri = routed.astype(jnp.int32)
    rank = jnp.cumsum(ri, axis=1) - ri
    cnt = jnp.sum(ri, axis=1)
    cnt_al = -(-cnt // MOE_ALIGN) * MOE_ALIGN
    lo = jnp.cumsum(cnt_al, axis=1) - cnt_al
    total = jnp.sum(cnt_al, axis=0)
    region = -(-(total + MOE_CH) // MOE_TM) * MOE_TM
    ends = jnp.cumsum(region)
    off = ends - region
    pos = off[None, :] + jnp.cumsum(cnt_al, axis=0) - cnt_al
    nch = -(-cnt_al // MOE_CH)
    m_pad = -(-(2 * n_tokens + nb * N_EXPERTS * (MOE_ALIGN - 1) + N_EXPERTS * (MOE_CH + MOE_TM)) // MOE_TM) * MOE_TM
    n_tiles = m_pad // MOE_TM
    tile_start = jnp.arange(n_tiles, dtype=jnp.int32) * MOE_TM
    tile_e = jnp.minimum(jnp.sum(tile_start[:, None] >= ends[None, :], axis=1), N_EXPERTS - 1).astype(jnp.int32)
    n_used = (ends[-1] // MOE_TM).astype(jnp.int32).reshape(1)
    dest = jnp.where(routed, lo[:, None, :] + rank, -1)
    d_hi = jnp.max(dest, axis=2)
    d_lo = jnp.min(jnp.where(routed, dest, MOE_STAGE), axis=2)
    dd = jnp.stack([d_hi, d_lo], axis=1).astype(jnp.int32)
    eidx = jnp.arange(N_EXPERTS, dtype=jnp.int32)
    e_a = jnp.min(jnp.where(routed, eidx, N_EXPERTS), axis=2)
    e_b = jnp.max(jnp.where(routed, eidx, -1), axis=2)
    g3 = gates2[:, :N_EXPERTS].reshape(nb, MOE_T, N_EXPERTS)
    at = lambda a, e: jnp.sum(jnp.where(eidx == e[..., None], a, 0), axis=2)
    r_a, r_b = at(rank, e_a), at(rank, e_b)
    w_a, w_b = at(g3, e_a), jnp.where(e_b != e_a, at(g3, e_b), 0.0)
    col = lambda e, r, c: jnp.where(r // MOE_CH == c, e * MOE_CH + r % MOE_CH, -1).astype(F32)
    cmeta = jnp.stack([col(e_a, r_a, 0), col(e_b, r_b, 0), col(e_a, r_a, 1), col(e_b, r_b, 1), w_a, w_b], axis=-1)
    cmeta = jnp.pad(cmeta.reshape(n_tokens, 6), ((0, 0), (0, LANES - 6)))
    two = (jnp.max(nch, axis=1) > 1).astype(jnp.int32)
    flat = lambda a: a.reshape(-1).astype(jnp.int32)
    return dict(lo=flat(lo // MOE_ALIGN), pos=flat(pos // MOE_ALIGN), nch=flat(nch), tile_e=tile_e, n_used=n_used,
                dd=dd, cmeta=cmeta, two=two, m_pad=m_pad, n_tiles=n_tiles, nb=nb)


def _dispatch_body(lo_ref, pos_ref, nch_ref, x_ref, sh_ref, sc_ref, dd_ref, xs_in_ref, xs_ref, stage_scr, sem,
                   *, nb):
    del xs_in_ref
    b = pl.program_id(0)
    slot = b % 2
    h = _modulate(x_ref[...], sh_ref[0], sc_ref[0]).astype(BF16)
    r = lax.broadcasted_iota(jnp.int32, (MOE_STAGE, MOE_T), 0)
    dd = dd_ref[0]
    onehot = jnp.where((r == dd[0:1, :]) | (r == dd[1:2, :]), 1.0, 0.0).astype(BF16)
    stage_scr[slot] = jnp.dot(onehot, h, preferred_element_type=F32)

    def seg_copy(blk, e, c):
        src0 = pl.multiple_of(lo_ref[blk * N_EXPERTS + e] * MOE_ALIGN + c * MOE_CH, MOE_ALIGN)
        dst0 = pl.multiple_of(pos_ref[blk * N_EXPERTS + e] * MOE_ALIGN + c * MOE_CH, MOE_ALIGN)
        return pltpu.make_async_copy(stage_scr.at[blk % 2, pl.ds(src0, MOE_CH)], xs_ref.at[pl.ds(dst0, MOE_CH)],
                                     sem.at[blk % 2, e, c])

    def for_segments(blk, fn):
        for e in range(N_EXPERTS):
            for c in range(2):
                @pl.when(c < nch_ref[blk * N_EXPERTS + e])
                def _():
                    fn(seg_copy(blk, e, c))

    @pl.when(b > 0)
    def _():
        for_segments(b - 1, lambda cp: cp.wait())

    for_segments(b, lambda cp: cp.start())

    @pl.when(b == nb - 1)
    def _():
        for_segments(b, lambda cp: cp.wait())


def moe_dispatch(x2, mod, plan, seq_len):
    n_tokens, D = x2.shape
    per_seq = seq_len // MOE_T
    nb = plan["nb"]
    vec = lambda: pl.BlockSpec((1, 1, D), lambda b, *_: (b // per_seq, 0, 0))
    grid_spec = pltpu.PrefetchScalarGridSpec(
        num_scalar_prefetch=3,
        grid=(nb,),
        in_specs=[
            pl.BlockSpec((MOE_T, D), lambda b, *_: (b, 0)),
            vec(), vec(),
            pl.BlockSpec((1, 2, MOE_T), lambda b, *_: (b, 0, 0)),
            pl.BlockSpec(memory_space=pltpu.MemorySpace.HBM),
        ],
        out_specs=pl.BlockSpec(memory_space=pltpu.MemorySpace.HBM),
        scratch_shapes=[pltpu.VMEM((2, MOE_STAGE, D), F32), pltpu.SemaphoreType.DMA((2, N_EXPERTS, 2))],
    )
    return pl.pallas_call(
        functools.partial(_dispatch_body, nb=nb),
        grid_spec=grid_spec,
        out_shape=jax.ShapeDtypeStruct((plan["m_pad"], D), F32),
        input_output_aliases={7: 0},
        compiler_params=_cparams(1),
        name="moe_dispatch",
    )(plan["lo"], plan["pos"], plan["nch"], x2, mod[0], mod[1], plan["dd"], jnp.zeros((plan["m_pad"], D), F32))


def _gffn_body(te_ref, nu_ref, *refs, has_prev):
    it = iter(refs)
    xs_ref = next(it)
    yp_ref = next(it) if has_prev else None
    wg_ref, wu_ref, wd_ref, o_ref, wg_scr, wu_scr, wd_scr = (next(it) for _ in range(7))
    t = pl.program_id(0)
    e_here = te_ref[t]
    e_prev = te_ref[jnp.maximum(t - 1, 0)]

    @pl.when(t < nu_ref[0])
    def _():
        @pl.when((t == 0) | (e_here != e_prev))
        def _():
            wg_scr[...] = wg_ref[0].astype(BF16)
            wu_scr[...] = wu_ref[0].astype(BF16)
            wd_scr[...] = wd_ref[0].astype(BF16)

        h = xs_ref[...].astype(BF16)
        a = jnp.dot(h, wg_scr[...], preferred_element_type=F32)
        b = jnp.dot(h, wu_scr[...], preferred_element_type=F32)
        mid = (a * _sigmoid(a) * b).astype(BF16)
        y = jnp.dot(mid, wd_scr[...], preferred_element_type=F32)
        o_ref[...] = (yp_ref[...] + y) if has_prev else y

    @pl.when(t >= nu_ref[0])
    def _():
        o_ref[...] = jnp.zeros_like(o_ref)


def moe_grouped_ffn(xs, w_gu3, w_down3, wl, plan, f, y_prev=None):
    m_pad, D = xs.shape
    nf = FFN_DIM // MOE_TF
    has_prev = y_prev is not None
    row = lambda t, te, nu: (jnp.minimum(t, nu[0] - 1), 0)
    tile = lambda: pl.BlockSpec((MOE_TM, D), row)
    in_specs = [tile()] + ([tile()] if has_prev else [])
    in_specs += [
        pl.BlockSpec((1, D, MOE_TF), lambda t, te, nu: (wl + te[t], 0, f)),
        pl.BlockSpec((1, D, MOE_TF), lambda t, te, nu: (wl + te[t], 0, nf + f)),
        pl.BlockSpec((1, MOE_TF, D), lambda t, te, nu: (wl + te[t], f, 0), pipeline_mode=pl.Buffered(1)),
    ]
    grid_spec = pltpu.PrefetchScalarGridSpec(
        num_scalar_prefetch=2,
        grid=(plan["n_tiles"],),
        in_specs=in_specs,
        out_specs=pl.BlockSpec((MOE_TM, D), lambda t, te, nu: (t, 0)),
        scratch_shapes=[pltpu.VMEM((D, MOE_TF), BF16), pltpu.VMEM((D, MOE_TF), BF16), pltpu.VMEM((MOE_TF, D), BF16)],
    )
    args = [xs] + ([y_prev] if has_prev else []) + [w_gu3, w_gu3, w_down3]
    return pl.pallas_call(
        functools.partial(_gffn_body, has_prev=has_prev),
        grid_spec=grid_spec,
        out_shape=jax.ShapeDtypeStruct((m_pad, D), F32),
        compiler_params=_cparams(1),
        name=f"moe_grouped_ffn{f}",
    )(plan["tile_e"], plan["n_used"], *args)


def _combine_body(pos_ref, nch_ref, two_ref, x_ref, gate_ref, cm_ref, *refs):
    y_refs = refs[:N_EXPERTS]
    o_ref, acc_scr, yhi_scr, ylo_scr = refs[N_EXPERTS:]
    del pos_ref, nch_ref
    b = pl.program_id(0)
    c = pl.program_id(1)

    def gathered():
        for e in range(N_EXPERTS):
            y = y_refs[e][...]
            y_hi = y.astype(BF16)
            yhi_scr[e * MOE_CH:(e + 1) * MOE_CH, :] = y_hi
            ylo_scr[e * MOE_CH:(e + 1) * MOE_CH, :] = (y - y_hi.astype(F32)).astype(BF16)
        cm = cm_ref[...]
        lane = lax.broadcasted_iota(jnp.int32, (MOE_T, N_EXPERTS * MOE_CH), 1).astype(F32)
        total = None
        for k in range(2):
            want = jnp.where(c == 0, cm[:, k:k + 1], cm[:, 2 + k:3 + k])
            onehot = jnp.where(lane == want, 1.0, 0.0).astype(BF16)
            rows = (jnp.dot(onehot, yhi_scr[...], preferred_element_type=F32)
                    + jnp.dot(onehot, ylo_scr[...], preferred_element_type=F32))
            term = cm[:, 4 + k:5 + k] * rows
            total = term if total is None else total + term
        return total

    @pl.when(c == 0)
    def _():
        acc_scr[...] = gathered()

    @pl.when(c == 1)
    def _():
        @pl.when(two_ref[b] > 0)
        def _():
            acc_scr[...] += gathered()

        o_ref[...] = x_ref[...] + gate_ref[0] * acc_scr[...]


def moe_combine(x2, gate, y, plan, seq_len):
    n_tokens, D = x2.shape
    per_seq = seq_len // MOE_T

    def window(e):
        def idx(b, c, pos, nch, two):
            s = b * N_EXPERTS + e
            return ((pos[s] + jnp.where(c < nch[s], c, 0) * (MOE_CH // MOE_ALIGN)) * MOE_ALIGN, 0)
        return pl.BlockSpec((pl.Element(MOE_CH), pl.Element(D)), idx)

    blk = lambda w: pl.BlockSpec((MOE_T, w), lambda b, c, *_: (b, 0))
    grid_spec = pltpu.PrefetchScalarGridSpec(
        num_scalar_prefetch=3,
        grid=(plan["nb"], 2),
        in_specs=[blk(D), pl.BlockSpec((1, 1, D), lambda b, c, *_: (b // per_seq, 0, 0)), blk(LANES)]
        + [window(e) for e in range(N_EXPERTS)],
        out_specs=blk(D),
        scratch_shapes=[pltpu.VMEM((MOE_T, D), F32), pltpu.VMEM((N_EXPERTS * MOE_CH, D), BF16),
                        pltpu.VMEM((N_EXPERTS * MOE_CH, D), BF16)],
    )
    return pl.pallas_call(
        _combine_body,
        grid_spec=grid_spec,
        out_shape=jax.ShapeDtypeStruct((n_tokens, D), F32),
        compiler_params=_cparams(2),
        name="moe_combine",
    )(plan["pos"], plan["nch"], plan["two"], x2, gate, plan["cmeta"], *([y] * N_EXPERTS))


def moe_sparse(x3, mod, gate, gates, w_gu3, w_down3, wl):
    B, L, D = x3.shape
    n_tokens = B * L
    assert L % MOE_T == 0 and FFN_DIM % MOE_TF == 0
    x2 = x3.reshape(n_tokens, D)
    gates2 = gates.reshape(n_tokens, LANES)
    plan = _moe_plan(gates2, n_tokens)
    xs = moe_dispatch(x2, mod, plan, L)
    y = None
    for f in range(FFN_DIM // MOE_TF):
        y = moe_grouped_ffn(xs, w_gu3, w_down3, wl, plan, f, y_prev=y)
    return moe_combine(x2, gate, y, plan, L).reshape(B, L, D)


def _ret_body(*refs, zero_init, nc):
    it = iter(refs)
    q_ref, k_ref, v_ref, g_ref = next(it), next(it), next(it), next(it)
    s0_ref = None if zero_init else next(it)
    inner_ref, qd_ref, kd_ref, cd_ref = next(it), next(it), next(it), next(it)
    o_ref, sout_ref, s_scr = next(it), next(it), next(it)
    c = pl.program_id(2)

    @pl.when(c == 0)
    def _():
        if zero_init:
            s_scr[...] = jnp.zeros_like(s_scr)
        else:
            s_scr[...] = s0_ref[0, 0, 0]

    q = q_ref[0]
    k = k_ref[0]
    v = v_ref[0]
    s = s_scr[...]
    att = lax.dot_general(q, k, (((1,), (1,)), ((), ())), preferred_element_type=F32) * inner_ref[0]
    inner = jnp.dot(att.astype(BF16), v, preferred_element_type=F32)
    cross = jnp.dot(q, s.astype(BF16), preferred_element_type=F32) * qd_ref[0]
    kdt = (k.astype(F32) * kd_ref[0]).T.astype(BF16)
    s_new = s * cd_ref[0] + jnp.dot(kdt, v, preferred_element_type=F32)
    s_scr[...] = s_new
    o = inner + cross
    on = o * lax.rsqrt(jnp.mean(o * o, axis=-1, keepdims=True) + EPS)
    g = g_ref[0]
    o_ref[0] = (g * _sigmoid(g) * on).astype(o_ref.dtype)

    @pl.when(c == nc - 1)
    def _():
        sout_ref[0, 0] = s_new


def retention_scan(q, k, v, g, tables, chunk, *, s0=None, s0_layer=0):
    B, L, _ = q.shape
    nc = L // chunk
    inner, qd, kd, cd = tables
    zero_init = s0 is None
    in_specs = [
        pl.BlockSpec((1, chunk, RET_DK), lambda h, b, c: (b, c, h)),
        pl.BlockSpec((1, chunk, RET_DK), lambda h, b, c: (b, c, h)),
        pl.BlockSpec((1, chunk, RET_DV), lambda h, b, c: (b, c, h)),
        pl.BlockSpec((1, chunk, RET_DV), lambda h, b, c: (b, c, h)),
    ]
    args = [q, k, v, g]
    if not zero_init:
        in_specs.append(pl.BlockSpec((1, 1, 1, RET_DK, RET_DV), lambda h, b, c: (s0_layer, b, h, 0, 0)))
        args.append(s0)
    in_specs += [
        pl.BlockSpec((1, chunk, chunk), lambda h, b, c: (h, 0, 0)),
        pl.BlockSpec((1, chunk, RET_DV), lambda h, b, c: (h, 0, 0)),
        pl.BlockSpec((1, chunk, RET_DK), lambda h, b, c: (h, 0, 0)),
        pl.BlockSpec((1, 1, RET_DV), lambda h, b, c: (h, 0, 0)),
    ]
    args += [inner, qd, kd, cd]
    return pl.pallas_call(
        functools.partial(_ret_body, zero_init=zero_init, nc=nc),
        grid=(RET_HEADS, B, nc),
        in_specs=in_specs,
        out_specs=[
            pl.BlockSpec((1, chunk, RET_DV), lambda h, b, c: (b, c, h)),
            pl.BlockSpec((1, 1, RET_DK, RET_DV), lambda h, b, c: (b, h, 0, 0)),
        ],
        out_shape=[
            jax.ShapeDtypeStruct((B, L, RET_V), BF16),
            jax.ShapeDtypeStruct((B, RET_HEADS, RET_DK, RET_DV), F32),
        ],
        scratch_shapes=[pltpu.VMEM((RET_DK, RET_DV), F32)],
        compiler_params=_cparams(3),
        name="retention_scan",
    )(*args)


def retention_tables(n_real, n_pad):
    log_gamma = jnp.log1p(-(2.0 ** (-5.0 - jnp.arange(RET_HEADS, dtype=F32))))
    idx = jnp.arange(n_pad, dtype=F32)
    valid = idx < n_real
    diff = idx[:, None] - idx[None, :]
    ok = (diff >= 0) & valid[:, None] & valid[None, :]
    inner = jnp.where(ok[None], jnp.exp(log_gamma[:, None, None] * jnp.maximum(diff, 0.0)[None]), 0.0)
    qd = jnp.exp(log_gamma[:, None] * (idx[None, :] + 1.0))
    kd = jnp.where(valid[None, :], jnp.exp(log_gamma[:, None] * (n_real - 1.0 - idx[None, :])), 0.0)
    cd = jnp.exp(log_gamma * n_real)
    return (inner,
            jnp.broadcast_to(qd[:, :, None], (RET_HEADS, n_pad, RET_DV)),
            jnp.broadcast_to(kd[:, :, None], (RET_HEADS, n_pad, RET_DK)),
            jnp.broadcast_to(cd[:, None, None], (RET_HEADS, 1, RET_DV)))


def rotary_tables(pos, reps):
    inv_freq = ROPE_BASE ** (-jnp.arange(0, RET_DK, 2, dtype=F32) / RET_DK)
    ang = pos.astype(F32)[:, None] * inv_freq[None, :]
    cos = jnp.repeat(jnp.cos(ang), 2, axis=1)
    sin = jnp.sin(ang)
    sin_signed = jnp.stack([-sin, sin], axis=-1).reshape(ang.shape[0], RET_DK)
    return jnp.tile(cos, (1, reps)), jnp.tile(sin_signed, (1, reps))


def _pool_body(x_ref, xp_ref, buf_ref, sh_ref, sc_ref, gate_ref, w_ref, cs_ref, o_ref, tail_ref, ext_scr,
               *, tm, pos0, has_prev):
    m = pl.program_id(1)
    sh, sc = sh_ref[0], sc_ref[0]
    h = _modulate(x_ref[0], sh, sc)

    @pl.when(m == 0)
    def _():
        ext_scr[0:POOL_HALO, :] = buf_ref[0]

    if has_prev:

        @pl.when(m > 0)
        def _():
            ext_scr[0:POOL_HALO, :] = _modulate(xp_ref[0], sh, sc)

    ext_scr[POOL_HALO:POOL_HALO + tm, :] = h
    tail_ref[0] = ext_scr[tm:tm + POOL_HALO, :]
    row = lax.broadcasted_iota(jnp.int32, (tm, 1), 0)
    pos1 = (pos0 + m * tm + row + 1).astype(F32)
    rows = max(tm, 16)
    ys = []
    for gi, w in enumerate(POOL_WINDOWS):
        c0, c1 = gi * POOL_GW, (gi + 1) * POOL_GW
        win = ext_scr[POOL_HALO:POOL_HALO + tm, c0:c1]
        for j in range(1, w):
            win = win + ext_scr[POOL_HALO - j:POOL_HALO - j + tm, c0:c1]
        d = win / jnp.minimum(jnp.float32(w), pos1) - h[:, c0:c1]
        if rows != tm:
            d = jnp.concatenate([d, jnp.zeros((rows - tm, POOL_GW), F32)], axis=0)
        y = jnp.dot(d.astype(BF16), w_ref[0, gi].astype(BF16), preferred_element_type=F32)
        ys.append(y[0:tm])
    y = jnp.concatenate(ys, axis=1) * cs_ref[...]
    o_ref[0] = x_ref[0] + gate_ref[0] * y


def pool_layer(x3, buf16, mod, gate, pool_w, pool_scale, wl, pos0, *, tm):
    B, L, D = x3.shape
    assert L % tm == 0 and (L == tm or tm % POOL_HALO == 0)
    has_prev = L > tm
    ph = POOL_HALO if has_prev else min(L, POOL_HALO)
    per = tm // POOL_HALO if has_prev else 1
    vec = lambda: pl.BlockSpec((1, 1, D), lambda b, m: (b, 0, 0))
    return pl.pallas_call(
        functools.partial(_pool_body, tm=tm, pos0=pos0, has_prev=has_prev),
        grid=(B, L // tm),
        in_specs=[
            pl.BlockSpec((1, tm, D), lambda b, m: (b, m, 0)),
            pl.BlockSpec((1, ph, D), lambda b, m: (b, jnp.maximum(m * per - 1, 0), 0)),
            pl.BlockSpec((1, POOL_HALO, D), lambda b, m: (b, 0, 0)),
            vec(), vec(), vec(),
            pl.BlockSpec((1,) + pool_w.shape[1:], lambda b, m: (wl, 0, 0, 0)),
            pl.BlockSpec((1, D), lambda b, m: (wl, 0)),
        ],
        out_specs=[
            pl.BlockSpec((1, tm, D), lambda b, m: (b, m, 0)),
            pl.BlockSpec((1, POOL_HALO, D), lambda b, m: (b, 0, 0)),
        ],
        out_shape=[
            jax.ShapeDtypeStruct((B, L, D), F32),
            jax.ShapeDtypeStruct((B, POOL_HALO, D), F32),
        ],
        scratch_shapes=[pltpu.VMEM((tm + POOL_HALO, D), F32)],
        compiler_params=_cparams(2),
        name="pool_layer",
    )(x3, x3, buf16, mod[0], mod[1], gate, pool_w, pool_scale)


def _split3(x):
    p0 = x.astype(BF16)
    r1 = x - p0.astype(F32)
    p1 = r1.astype(BF16)
    p2 = (r1 - p1.astype(F32)).astype(BF16)
    return p0, p1, p2


def _lane_cumsum(x, tri):
    p0, p1, p2 = _split3(x)
    dot = lambda p: jnp.dot(p, tri, preferred_element_type=F32)
    return (dot(p0) + dot(p1)) + dot(p2)


def _upper_tri(t):
    r = lax.broadcasted_iota(jnp.int32, (t, t), 0)
    c = lax.broadcasted_iota(jnp.int32, (t, t), 1)
    return jnp.where(r <= c, 1.0, 0.0).astype(BF16)


BIAS_PIECES = 3


def _fbias_body(lf_ref, o_ref, carry_scr, *, tc):
    @pl.when(pl.program_id(1) == 0)
    def _():
        carry_scr[...] = jnp.zeros_like(carry_scr)

    r = lax.broadcasted_iota(jnp.int32, (tc, tc), 0)
    c = lax.broadcasted_iota(jnp.int32, (tc, tc), 1)
    tril = jnp.where(c <= r, 1.0, 0.0).astype(BF16)
    p0, p1, p2 = _split3(lf_ref[0])
    dot = lambda p: jnp.dot(tril, p, preferred_element_type=F32)
    f = carry_scr[...] + ((dot(p0) + dot(p1)) + dot(p2))
    carry_scr[...] = f[tc - 1:tc, :]
    head = lax.broadcasted_iota(jnp.int32, (FOX_HEADS, LANES), 0)
    lane = lax.broadcasted_iota(jnp.int32, (FOX_HEADS, LANES), 1)
    out = None
    for p, piece in enumerate(_split3(-f)):
        place = jnp.where(lane == BIAS_PIECES * head + p, 1.0, 0.0).astype(BF16)
        term = jnp.dot(piece, place, preferred_element_type=F32)
        out = term if out is None else out + term
    o_ref[0] = out.astype(BF16)


def fox_bias_features(lf, *, tc=512):
    B, L, H = lf.shape
    tc = min(tc, L)
    return pl.pallas_call(
        functools.partial(_fbias_body, tc=tc),
        grid=(B, L // tc),
        in_specs=[pl.BlockSpec((1, tc, H), lambda b, c: (b, c, 0))],
        out_specs=pl.BlockSpec((1, tc, LANES), lambda b, c: (b, c, 0)),
        out_shape=jax.ShapeDtypeStruct((B, L, LANES), BF16),
        scratch_shapes=[pltpu.VMEM((1, H), F32)],
        compiler_params=_cparams(2),
        name="fox_bias_features",
    )(lf)


def _flash_body(qt_ref, k_ref, fb_ref, vt_ref, o_ref, *, tq, tk):
    hp = pl.program_id(1)
    qi = pl.program_id(2)
    pair = 2 * FOX_DH
    row = lax.broadcasted_iota(jnp.int32, (pair, tq), 0)
    qt = qt_ref[0]
    qaug = []
    for i in range(2):
        q_head = jnp.where(row // FOX_DH == i, qt, jnp.zeros_like(qt))
        pick = jnp.where(row // BIAS_PIECES == 2 * hp + i, 1.0, 0.0).astype(BF16)
        qaug.append(jnp.concatenate([q_head, pick], axis=0))
    key_i = lax.broadcasted_iota(jnp.int32, (tk, tq), 0)
    qry_i = lax.broadcasted_iota(jnp.int32, (tk, tq), 1)
    per_q = tq // tk

    def step(j, carry, diag):
        k0 = pl.multiple_of(j * tk, tk)
        kaug = jnp.concatenate([k_ref[0, pl.ds(k0, tk), :], fb_ref[0, pl.ds(k0, tk), :]], axis=1)
        new = []
        for i in range(2):
            m_old, l_old, acc = carry[i]
            st = jnp.dot(kaug, qaug[i], preferred_element_type=F32)
            if diag is not None:
                st = jnp.where(key_i + diag * tk <= qry_i, st, NEG_BIG)
            m_new = jnp.maximum(m_old, jnp.max(st, axis=0, keepdims=True))
            alpha = jnp.exp(m_old - m_new)
            p = jnp.exp(st - m_new)
            l_new = alpha * l_old + jnp.sum(p, axis=0, keepdims=True)
            vt = vt_ref[0, i * FOX_DH:(i + 1) * FOX_DH, pl.ds(k0, tk)].astype(BF16)
            acc = alpha * acc + jnp.dot(vt, p.astype(BF16), preferred_element_type=F32)
            new.append((m_new, l_new, acc))
        return tuple(new)

    init = tuple((jnp.full((1, tq), NEG_BIG, F32), jnp.zeros((1, tq), F32), jnp.zeros((FOX_DH, tq), F32))
                 for _ in range(2))
    carry = lax.fori_loop(0, qi * per_q, lambda j, c: step(j, c, None), init)
    for d in range(per_q):
        carry = step(qi * per_q + d, carry, d)
    o_ref[0] = jnp.concatenate([(acc / l).T for _, l, acc in carry], axis=1).astype(o_ref.dtype)


def fox_flash(qt, k, fb, vt, *, tq=1024, tk=1024):
    B, L, D = k.shape
    tq = min(tq, L)
    tk = min(tk, tq)
    assert L % tq == 0 and tq % tk == 0
    pair = 2 * FOX_DH
    return pl.pallas_call(
        functools.partial(_flash_body, tq=tq, tk=tk),
        grid=(B, FOX_HEADS // 2, L // tq),
        in_specs=[
            pl.BlockSpec((1, pair, tq), lambda b, hp, qi: (b, hp, qi)),
            pl.BlockSpec((1, L, pair), lambda b, hp, qi: (b, 0, hp)),
            pl.BlockSpec((1, L, LANES), lambda b, hp, qi: (b, 0, 0)),
            pl.BlockSpec((1, pair, L), lambda b, hp, qi: (b, hp, 0)),
        ],
        out_specs=pl.BlockSpec((1, tq, pair), lambda b, hp, qi: (b, qi, hp)),
        out_shape=jax.ShapeDtypeStruct((B, L, D), BF16),
        compiler_params=_cparams(3),
        name="fox_flash",
    )(qt, k, fb, vt)


DEC_PAGES = 8


def _decode_body(pt_ref, q_ref, kn_ref, vn_ref, lfn_ref, *refs, n_steps, lq):
    kc = refs[0:DEC_PAGES]
    vc = refs[DEC_PAGES:2 * DEC_PAGES]
    lc = refs[2 * DEC_PAGES:3 * DEC_PAGES]
    o_ref, qbd_scr, m_scr, l_scr, acc_scr, carry_scr = refs[3 * DEC_PAGES:]
    b = pl.program_id(0)
    st = pl.program_id(1)
    rows = lq * FOX_HEADS
    last = (((1,), (1,)), ((), ()))
    tri = _upper_tri(PAGE_SIZE)

    @pl.when(st == 0)
    def _():
        head = lax.broadcasted_iota(jnp.int32, (FOX_HEADS, D_MODEL), 0)
        lane_head = lax.broadcasted_iota(jnp.int32, (FOX_HEADS, D_MODEL), 1) // FOX_DH
        blocks = [jnp.where(head == lane_head, jnp.broadcast_to(q_ref[0, t:t + 1, :], (FOX_HEADS, D_MODEL)), 0.0)
                  for t in range(lq)]
        qbd_scr[...] = jnp.concatenate(blocks, axis=0).astype(BF16)
        m_scr[...] = jnp.full_like(m_scr, NEG_BIG)
        l_scr[...] = jnp.zeros_like(l_scr)
        acc_scr[...] = jnp.zeros_like(acc_scr)
        carry_scr[...] = jnp.zeros_like(carry_scr)

    def absorb(s, v_mat, v_is_t):
        m_old = m_scr[...]
        m_new = jnp.maximum(m_old, jnp.max(s, axis=1, keepdims=True))
        alpha = jnp.exp(m_old - m_new)
        p = jnp.exp(s - m_new)
        l_scr[...] = alpha * l_scr[...] + jnp.sum(p, axis=1, keepdims=True)
        if v_is_t:
            pv = lax.dot_general(p.astype(BF16), v_mat, last, preferred_element_type=F32)
        else:
            pv = jnp.dot(p.astype(BF16), v_mat, preferred_element_type=F32)
        acc_scr[...] = alpha * acc_scr[...] + pv
        m_scr[...] = m_new

    @pl.when(st < n_steps)
    def _():
        kt = jnp.concatenate([kc[i][0].reshape(D_MODEL, PAGE_SIZE).astype(BF16) for i in range(DEC_PAGES)], axis=1)
        vt = jnp.concatenate([vc[i][0].reshape(D_MODEL, PAGE_SIZE).astype(BF16) for i in range(DEC_PAGES)], axis=1)
        within = _lane_cumsum(jnp.concatenate([lc[i][0] for i in range(DEC_PAGES)], axis=0), tri)
        f = carry_scr[...]
        biases = []
        for i in range(DEC_PAGES):
            f_page = f + within[i * FOX_HEADS:(i + 1) * FOX_HEADS, :]
            biases.append(jnp.tile(f_page, (lq, 1)))
            f = jnp.broadcast_to(f_page[:, PAGE_SIZE - 1:PAGE_SIZE], f_page.shape)
        carry_scr[...] = f
        s = jnp.dot(qbd_scr[...], kt, preferred_element_type=F32) - jnp.concatenate(biases, axis=1)
        absorb(s, vt, True)

    @pl.when(st == n_steps)
    def _():
        pad = jnp.zeros((PAGE_SIZE - lq, D_MODEL), F32)
        kn = jnp.concatenate([kn_ref[0], pad], axis=0).astype(BF16)
        vn = jnp.concatenate([vn_ref[0], pad], axis=0).astype(BF16)
        n_tok = lfn_ref.shape[1]
        tok = lax.broadcasted_iota(jnp.int32, (n_tok, PAGE_SIZE), 0)
        key = lax.broadcasted_iota(jnp.int32, (n_tok, PAGE_SIZE), 1)
        sel = jnp.where((tok // lq == b) & (tok % lq <= key) & (key < lq), 1.0, 0.0).astype(BF16)
        p0, p1, p2 = _split3(lfn_ref[...])
        dot = lambda p: jnp.dot(p, sel, preferred_element_type=F32)
        f = carry_scr[...] + ((dot(p0) + dot(p1)) + dot(p2))
        s = lax.dot_general(qbd_scr[...], kn, last, preferred_element_type=F32) - jnp.tile(f, (lq, 1))
        rq = lax.broadcasted_iota(jnp.int32, (rows, PAGE_SIZE), 0) // FOX_HEADS
        kk = lax.broadcasted_iota(jnp.int32, (rows, PAGE_SIZE), 1)
        s = jnp.where(kk <= rq, s, NEG_BIG)
        absorb(s, vn, False)
        o = acc_scr[...] / l_scr[...]
        head = lax.broadcasted_iota(jnp.int32, (FOX_HEADS, D_MODEL), 0)
        lane_head = lax.broadcasted_iota(jnp.int32, (FOX_HEADS, D_MODEL), 1) // FOX_DH
        outs = [jnp.sum(jnp.where(head == lane_head, o[t * FOX_HEADS:(t + 1) * FOX_HEADS, :], 0.0), axis=0,
                        keepdims=True) for t in range(lq)]
        o_ref[0] = jnp.concatenate(outs, axis=0).astype(o_ref.dtype)


def fox_decode(q, k_new, v_new, lft_new, cache_kt, cache_vt, cache_lt, page_table):
    B, lq, D = q.shape
    n_pages = page_table.shape[1]
    assert n_pages % DEC_PAGES == 0
    n_steps = n_pages // DEC_PAGES
    rows = lq * FOX_HEADS

    def page_idx(i):
        return lambda b, s, pt: (pt[b * n_pages + jnp.minimum(s, n_steps - 1) * DEC_PAGES + i], 0, 0, 0)

    def page_idx3(i):
        return lambda b, s, pt: (pt[b * n_pages + jnp.minimum(s, n_steps - 1) * DEC_PAGES + i], 0, 0)

    seq = lambda: pl.BlockSpec((1, lq, D), lambda b, s, pt: (b, 0, 0))
    in_specs = [seq(), seq(), seq(), pl.BlockSpec(lft_new.shape, lambda b, s, pt: (0, 0))]
    in_specs += [pl.BlockSpec((1, FOX_HEADS, FOX_DH, PAGE_SIZE), page_idx(i)) for i in range(DEC_PAGES)]
    in_specs += [pl.BlockSpec((1, FOX_HEADS, FOX_DH, PAGE_SIZE), page_idx(i)) for i in range(DEC_PAGES)]
    in_specs += [pl.BlockSpec((1, FOX_HEADS, PAGE_SIZE), page_idx3(i)) for i in range(DEC_PAGES)]
    grid_spec = pltpu.PrefetchScalarGridSpec(
        num_scalar_prefetch=1,
        grid=(B, n_steps + 1),
        in_specs=in_specs,
        out_specs=pl.BlockSpec((1, lq, D), lambda b, s, pt: (b, 0, 0)),
        scratch_shapes=[
            pltpu.VMEM((rows, D), BF16),
            pltpu.VMEM((rows, 1), F32),
            pltpu.VMEM((rows, 1), F32),
            pltpu.VMEM((rows, D), F32),
            pltpu.VMEM((FOX_HEADS, PAGE_SIZE), F32),
        ],
    )
    return pl.pallas_call(
        functools.partial(_decode_body, n_steps=n_steps, lq=lq),
        grid_spec=grid_spec,
        out_shape=jax.ShapeDtypeStruct((B, lq, D), F32),
        compiler_params=_cparams(2),
        name="fox_decode",
    )(page_table.reshape(-1), q, k_new, v_new, lft_new,
      *([cache_kt] * DEC_PAGES), *([cache_vt] * DEC_PAGES), *([cache_lt] * DEC_PAGES))


def _final_body(x_ref, g_ref, o_ref):
    xf = x_ref[0]
    o_ref[0] = xf * lax.rsqrt(jnp.mean(xf * xf, axis=-1, keepdims=True) + EPS) * g_ref[...]


def final_norm(x3, final_g, *, tm=None):
    bx, L, D = x3.shape
    tm = tm or min(L, 1024)
    return pl.pallas_call(
        _final_body,
        grid=(bx, L // tm),
        in_specs=[pl.BlockSpec((1, tm, D), lambda b, m: (b, m, 0)), pl.BlockSpec((1, D), lambda b, m: (0, 0))],
        out_specs=pl.BlockSpec((1, tm, D), lambda b, m: (b, m, 0)),
        out_shape=jax.ShapeDtypeStruct((bx, L, D), F32),
        compiler_params=_cparams(2),
        name="final_norm",
    )(x3, final_g.reshape(1, D))


def _trunk(x, mods, ret_state, pool_state, fox_past, pos0, params):
    (ret_w_in, ret_w_out, pool_w, pool_scale, fox_wt, fox_b_f, fox_w_out, ffn_w_gu, ffn_w_down,
     w_rt_pad, b_rt_pad, moe_w_gu, moe_w_down, final_g) = params
    B, L, D = x.shape
    decode = fox_past is not None
    if decode:
        x3 = x.reshape(1, B * L, D)
        expand = lambda v: jnp.repeat(v, L, axis=0)[None]
    else:
        x3 = x
        expand = lambda v: v[:, None, :]
    n_rows = x3.shape[1]
    pos = pos0 + jnp.arange(L)
    cos, sin = rotary_tables(pos, RET_HEADS)
    if decode:
        cos, sin = jnp.tile(cos, (B, 1)), jnp.tile(sin, (B, 1))
    chunk = min(L, 256)
    chunk_pad = max(chunk, PAGE_SIZE)
    tables = retention_tables(chunk, chunk_pad)
    ret_new, extras = [], {}
    for i in range(DEPTH):
        sh_a, sc_a, g_a, sh_f, sc_f, g_f = [expand(v) for v in jnp.split(mods[i], 6, axis=-1)]
        kind, j = i % N_MIXERS, i // N_MIXERS
        if kind == 0:
            q, k, v, g = ret_proj(x3, (sh_a, sc_a), ret_w_in, j, (cos, sin))
            if decode:
                padr = lambda t: jnp.pad(t.reshape(B, L, -1), ((0, 0), (0, chunk_pad - L), (0, 0)))
                o, s = retention_scan(padr(q), padr(k), padr(v), padr(g), tables, chunk_pad, s0=ret_state,
                                      s0_layer=j)
                o = o[:, :L].reshape(1, n_rows, RET_V)
            else:
                o, s = retention_scan(q, k, v, g, tables, chunk_pad)
            ret_new.append(s)
            x3 = mm(o, ret_w_out, j, n0=0, n_out=D, out_dtype=F32, epi="res", res=(x3, g_a), name="ret_out")
        elif kind == 1:
            vecs = [v[:, None, :] for v in jnp.split(mods[i], 6, axis=-1)[:3]]
            if decode:
                buf16 = jnp.pad(pool_state[j], ((0, 0), (1, 0), (0, 0)))
                tm = L
            else:
                buf16 = jnp.zeros((B, POOL_HALO, D), F32)
                tm = min(L, 512)
            xn, tail = pool_layer(x3.reshape(B, L, D), buf16, (vecs[0], vecs[1]), vecs[2], pool_w, pool_scale, j,
                                  pos0, tm=tm)
            x3 = xn.reshape(x3.shape)
            extras["pool"] = tail[:, 1:, :]
        else:
            fproj = functools.partial(mm, x3, fox_wt, j, mod=(sh_a, sc_a), w_t=True)
            if decode:
                ck, cv, cl, pt = fox_past
                q = fproj(n0=0, n_out=D, out_dtype=F32, scale=FOX_DH ** -0.5, name="fox_q")
                lft = fproj(n0=3 * D, n_out=FOX_HEADS, out_dtype=F32, out_t=True, epi="logsig",
                            bias=fox_b_f[j].reshape(FOX_HEADS, 1), name="fox_logft")
                k = fproj(n0=D, n_out=D, out_dtype=F32, name="fox_k")
                v = fproj(n0=2 * D, n_out=D, out_dtype=F32, name="fox_v")
                o = fox_decode(q.reshape(B, L, D), k.reshape(B, L, D), v.reshape(B, L, D), lft[0],
                               jnp.transpose(ck[j], (0, 2, 3, 1)), jnp.transpose(cv[j], (0, 2, 3, 1)),
                               jnp.transpose(cl[j], (0, 2, 1)), pt)
                o = o.reshape(1, n_rows, D)
                extras["k"] = k.reshape(B, L, FOX_HEADS, FOX_DH)
                extras["v"] = v.reshape(B, L, FOX_HEADS, FOX_DH)
                extras["l"] = jnp.transpose(lft[0].reshape(FOX_HEADS, B, L), (1, 2, 0))
            else:
                qt = fproj(n0=0, n_out=D, out_dtype=BF16, out_t=True, scale=FOX_DH ** -0.5, name="fox_qt")
                kb = fproj(n0=D, n_out=D, out_dtype=BF16, name="fox_kb")
                kt = fproj(n0=D, n_out=D, out_dtype=F32, out_t=True, name="fox_kt")
                vt = fproj(n0=2 * D, n_out=D, out_dtype=F32, out_t=True, name="fox_vt")
                lf = fproj(n0=3 * D, n_out=FOX_HEADS, out_dtype=F32, epi="logsig",
                           bias=fox_b_f[j].reshape(1, FOX_HEADS), name="fox_logf")
                o = fox_flash(qt, kb, fox_bias_features(lf), vt)
                unt = lambda t: jnp.transpose(t.reshape(B, FOX_HEADS, FOX_DH, L), (0, 3, 1, 2))
                extras["k"], extras["v"] = unt(kt), unt(vt)
                extras["l"] = lf
            x3 = mm(o, fox_w_out, j, n0=0, n_out=D, out_dtype=F32, epi="res", res=(x3, g_a), name="fox_out")
        ml = i // 2
        if i % 2 == 0:
            x3 = ffn(x3, (sh_f, sc_f), g_f, ffn_w_gu, ffn_w_down, ml, name="ffn_dense")
        else:
            gates = router(x3, (sh_f, sc_f), w_rt_pad, b_rt_pad, ml)
            if decode:
                x3 = ffn(x3, (sh_f, sc_f), g_f, moe_w_gu, moe_w_down, ml * N_EXPERTS, gates=gates, name="ffn_moe")
            else:
                x3 = moe_sparse(x3, (sh_f, sc_f), g_f, gates, moe_w_gu, moe_w_down, ml * N_EXPERTS)
    out = final_norm(x3, final_g).reshape(B, L, D)
    return (out, jnp.stack(ret_new), extras["pool"][None], extras["k"][None], extras["v"][None], extras["l"][None])


def kernel(x_prompt, x_sample, state_ret, state_pool, cache_fox_k, cache_fox_v, cache_fox_logf, page_table,
           c_prompt, c_sample, ada_w, ada_b, ret_w_in, ret_w_out, pool_w, pool_scale, fox_w_in, fox_b_f, fox_w_out,
           ffn_w_gu, ffn_w_down, moe_w_router, moe_b_router, moe_w_gu, moe_w_down, final_g):
    bp, bs = x_prompt.shape[0], x_sample.shape[0]
    rows = -(-(bp + bs) // 8) * 8
    c_all = jnp.concatenate([c_prompt, c_sample, jnp.zeros((rows - bp - bs, D_MODEL), F32)], axis=0)
    mods = ada_mods(c_all, ada_w, ada_b)
    n_moe = moe_w_router.shape[0]
    params = (
        ret_w_in, ret_w_out, pool_w, pool_scale,
        jnp.swapaxes(fox_w_in, 1, 2),
        fox_b_f, fox_w_out, ffn_w_gu, ffn_w_down,
        jnp.pad(jnp.swapaxes(moe_w_router, 1, 2), ((0, 0), (0, LANES - N_EXPERTS), (0, 0))),
        jnp.pad(moe_b_router, ((0, 0), (0, LANES - N_EXPERTS))).reshape(n_moe, 1, LANES),
        moe_w_gu.reshape((n_moe * N_EXPERTS,) + moe_w_gu.shape[2:]),
        moe_w_down.reshape((n_moe * N_EXPERTS,) + moe_w_down.shape[2:]),
        final_g,
    )
    y_p, ret_p, pool_p, k_p, v_p, l_p = _trunk(x_prompt, mods[:, :bp], None, None, None, 0, params)
    n_past = page_table.shape[1] * PAGE_SIZE
    y_s, ret_s, pool_s, k_s, v_s, l_s = _trunk(
        x_sample, mods[:, bp:bp + bs], state_ret, state_pool,
        (cache_fox_k, cache_fox_v, cache_fox_logf, page_table), n_past, params)
    return (y_p, y_s, ret_p, ret_s, pool_p, pool_s, k_p, k_s, v_p, v_s, l_p, l_s)
```

```python
import functools
import math

import jax
import jax.numpy as jnp
from jax import lax
from jax.experimental import pallas as pl
from jax.experimental.pallas import tpu as pltpu

F32 = jnp.float32
BF16 = jnp.bfloat16

D_MODEL = 1024
DEPTH = 4
PAGE_SIZE = 128
N_MIXERS = 3
RET_HEADS = 4
RET_DK = D_MODEL // RET_HEADS
RET_DV = 2 * D_MODEL // RET_HEADS
RET_QK = RET_HEADS * RET_DK
RET_V = RET_HEADS * RET_DV
ROPE_BASE = 10000.0
POOL_WINDOWS = (2, 4, 8, 16)
POOL_GW = D_MODEL // len(POOL_WINDOWS)
POOL_BUF = max(POOL_WINDOWS) - 1
POOL_HALO = POOL_BUF + 1
FOX_HEADS = 16
FOX_DH = D_MODEL // FOX_HEADS
FFN_DIM = 2816
N_EXPERTS = 8
EPS = 1e-6
NEG_BIG = -1e30

V7X_VMEM_BYTES = 64 * 1024 * 1024
VMEM_LIMIT = V7X_VMEM_BYTES - 8 * 1024 * 1024
LANES = 128
FFN_TF = 256


def _cparams(n_axes):
    return pltpu.CompilerParams(dimension_semantics=("arbitrary",) * n_axes, vmem_limit_bytes=VMEM_LIMIT)


def _sigmoid(x):
    return 1.0 / (1.0 + jnp.exp(-x))


def _modulate(x, shift, scale):
    xf = x.astype(F32)
    ms = jnp.mean(xf * xf, axis=-1, keepdims=True)
    return (xf * lax.rsqrt(ms + EPS)) * (1.0 + scale) + shift


def _row_spec(arr, tm, width, col_fn):
    if arr.shape[1] == 1:
        return pl.BlockSpec((1, 1, width), lambda b, m, *r: (b, 0, col_fn(*r)))
    return pl.BlockSpec((1, tm, width), lambda b, m, *r: (b, m, col_fn(*r)))


def _ada_body(c_ref, w_ref, b_ref, o_ref):
    c = c_ref[...]
    cond = c * _sigmoid(c)
    o_ref[0] = jnp.dot(cond.astype(BF16), w_ref[0].astype(BF16), preferred_element_type=F32) + b_ref[0]


def ada_mods(c_all, ada_w, ada_b):
    rows = c_all.shape[0]
    n_out = ada_w.shape[2]
    tn = 1024
    return pl.pallas_call(
        _ada_body,
        grid=(DEPTH, n_out // tn),
        in_specs=[
            pl.BlockSpec((rows, D_MODEL), lambda i, n: (0, 0)),
            pl.BlockSpec((1, D_MODEL, tn), lambda i, n: (i, 0, n)),
            pl.BlockSpec((1, 1, tn), lambda i, n: (i, 0, n)),
        ],
        out_specs=pl.BlockSpec((1, rows, tn), lambda i, n: (i, 0, n)),
        out_shape=jax.ShapeDtypeStruct((DEPTH, rows, n_out), F32),
        compiler_params=_cparams(2),
        name="ada_mods",
    )(c_all, ada_w, ada_b.reshape(DEPTH, 1, n_out))


def _mm_body(*refs, has_mod, epi, w_t, out_t, scale):
    it = iter(refs)
    x_ref = next(it)
    if has_mod:
        sh_ref, sc_ref = next(it), next(it)
    w_ref = next(it)
    if epi == "rot":
        cos_ref, sin_ref = next(it), next(it)
    elif epi == "res":
        res_ref, gate_ref = next(it), next(it)
    elif epi == "logsig":
        b_ref = next(it)
    o_ref = next(it)
    n = pl.program_id(2)
    if has_mod:
        h_scr = next(it)

        @pl.when(n == 0)
        def _():
            h_scr[...] = _modulate(x_ref[0], sh_ref[0], sc_ref[0]).astype(BF16)

        lhs = h_scr[...]
    else:
        lhs = x_ref[0].astype(BF16)
    w = w_ref[0].astype(BF16)
    last = (((1,), (1,)), ((), ()))
    if not w_t:
        acc = jnp.dot(lhs, w, preferred_element_type=F32)
    elif not out_t:
        acc = lax.dot_general(lhs, w, last, preferred_element_type=F32)
    else:
        acc = lax.dot_general(w, lhs, last, preferred_element_type=F32)
    if epi == "rot":
        acc = _rotary(acc, cos_ref[...], sin_ref[...])
    elif epi == "res":
        acc = res_ref[0] + gate_ref[0] * acc
    elif epi == "logsig":
        z = acc + b_ref[...]
        acc = jnp.minimum(z, 0.0) - jnp.log1p(jnp.exp(-jnp.abs(z)))
    if scale != 1.0:
        acc = acc * scale
    o_ref[0] = acc.astype(o_ref.dtype)


def mm(x3, w3, wl, *, n0, n_out, out_dtype, mod=None, epi="plain", w_t=False, out_t=False, scale=1.0,
       rot=None, res=None, bias=None, tm=None, tn=None, name="mm"):
    bx, L, K = x3.shape
    tm = tm or min(L, 512 if epi == "rot" else 1024)
    tn = tn or min(n_out, 512 if K > 1024 else 1024)
    assert L % tm == 0 and n_out % tn == 0 and n0 % tn == 0
    nb0 = n0 // tn
    has_mod = mod is not None
    in_specs = [pl.BlockSpec((1, tm, K), lambda b, m, n: (b, m, 0))]
    args = [x3]
    if has_mod:
        for a in mod:
            in_specs.append(_row_spec(a, tm, K, lambda n: 0))
            args.append(a)
    if w_t:
        in_specs.append(pl.BlockSpec((1, tn, K), lambda b, m, n: (wl, nb0 + n, 0)))
    else:
        in_specs.append(pl.BlockSpec((1, K, tn), lambda b, m, n: (wl, 0, nb0 + n)))
    args.append(w3)
    if epi == "rot":
        for a in rot:
            in_specs.append(pl.BlockSpec((tm, tn), lambda b, m, n: (m, n)))
            args.append(a)
    elif epi == "res":
        in_specs.append(pl.BlockSpec((1, tm, tn), lambda b, m, n: (b, m, n)))
        in_specs.append(_row_spec(res[1], tm, tn, lambda n: n))
        args.extend(res)
    elif epi == "logsig":
        in_specs.append(pl.BlockSpec(bias.shape, lambda b, m, n: (0, 0)))
        args.append(bias)
    if out_t:
        out_spec = pl.BlockSpec((1, tn, tm), lambda b, m, n: (b, n, m))
        out_shape = jax.ShapeDtypeStruct((bx, n_out, L), out_dtype)
    else:
        out_spec = pl.BlockSpec((1, tm, tn), lambda b, m, n: (b, m, n))
        out_shape = jax.ShapeDtypeStruct((bx, L, n_out), out_dtype)
    return pl.pallas_call(
        functools.partial(_mm_body, has_mod=has_mod, epi=epi, w_t=w_t, out_t=out_t, scale=scale),
        grid=(bx, L // tm, n_out // tn),
        in_specs=in_specs,
        out_specs=out_spec,
        out_shape=out_shape,
        scratch_shapes=[pltpu.VMEM((tm, K), BF16)] if has_mod else [],
        compiler_params=_cparams(3),
        name=name,
    )(*args)


def _rotary(acc, cos, sin_signed):
    width = acc.shape[1]
    lane = lax.broadcasted_iota(jnp.int32, acc.shape, 1)
    partner = jnp.where(lane % 2 == 0, pltpu.roll(acc, width - 1, axis=1), pltpu.roll(acc, 1, axis=1))
    return acc * cos + partner * sin_signed


def _retproj_body(x_ref, sh_ref, sc_ref, w_ref, cos_ref, sin_ref, q_ref, k_ref, v_ref, g_ref, h_scr, *, nv):
    n = pl.program_id(2)

    @pl.when(n == 0)
    def _():
        h_scr[...] = _modulate(x_ref[0], sh_ref[0], sc_ref[0]).astype(BF16)

    proj = lambda: jnp.dot(h_scr[...], w_ref[0].astype(BF16), preferred_element_type=F32)

    @pl.when(n == 0)
    def _():
        q_ref[0] = _rotary(proj(), cos_ref[...], sin_ref[...]).astype(q_ref.dtype)

    @pl.when(n == 1)
    def _():
        k_ref[0] = (_rotary(proj(), cos_ref[...], sin_ref[...]) * (RET_DK ** -0.5)).astype(k_ref.dtype)

    @pl.when((n >= 2) & (n < 2 + nv))
    def _():
        v_ref[0] = proj().astype(v_ref.dtype)

    @pl.when(n >= 2 + nv)
    def _():
        g_ref[0] = proj()


def ret_proj(x3, mod, w3, wl, rot, *, tm=None):
    bx, L, K = x3.shape
    tn = RET_QK
    nv = RET_V // tn
    tm = tm or min(L, 512)
    assert L % tm == 0 and RET_V % tn == 0
    row = lambda a: _row_spec(a, tm, K, lambda n: 0)
    tab = lambda: pl.BlockSpec((tm, tn), lambda b, m, n: (m, 0))
    clip = lambda n, lo: jnp.clip(n - lo, 0, nv - 1)
    return pl.pallas_call(
        functools.partial(_retproj_body, nv=nv),
        grid=(bx, L // tm, 2 + 2 * nv),
        in_specs=[
            pl.BlockSpec((1, tm, K), lambda b, m, n: (b, m, 0)), row(mod[0]), row(mod[1]),
            pl.BlockSpec((1, K, tn), lambda b, m, n: (wl, 0, n)),
            tab(), tab(),
        ],
        out_specs=[
            pl.BlockSpec((1, tm, tn), lambda b, m, n: (b, m, 0)),
            pl.BlockSpec((1, tm, tn), lambda b, m, n: (b, m, 0)),
            pl.BlockSpec((1, tm, tn), lambda b, m, n: (b, m, clip(n, 2))),
            pl.BlockSpec((1, tm, tn), lambda b, m, n: (b, m, clip(n, 2 + nv))),
        ],
        out_shape=[
            jax.ShapeDtypeStruct((bx, L, RET_QK), BF16),
            jax.ShapeDtypeStruct((bx, L, RET_QK), BF16),
            jax.ShapeDtypeStruct((bx, L, RET_V), BF16),
            jax.ShapeDtypeStruct((bx, L, RET_V), F32),
        ],
        scratch_shapes=[pltpu.VMEM((tm, K), BF16)],
        compiler_params=_cparams(3),
        name="ret_proj",
    )(x3, mod[0], mod[1], w3, rot[0], rot[1])


def _ffn_body(*refs, moe, nf, ne):
    it = iter(refs)
    x_ref, sh_ref, sc_ref, gate_ref = next(it), next(it), next(it), next(it)
    gw_ref = next(it) if moe else None
    wg_ref, wu_ref, wd_ref, o_ref, h_scr, acc_scr = next(it), next(it), next(it), next(it), next(it), next(it)
    tot_scr = next(it) if moe else None
    e = pl.program_id(2)
    f = pl.program_id(3)

    @pl.when((e == 0) & (f == 0))
    def _():
        h_scr[...] = _modulate(x_ref[0], sh_ref[0], sc_ref[0]).astype(BF16)

    h = h_scr[...]
    a = jnp.dot(h, wg_ref[0].astype(BF16), preferred_element_type=F32)
    b = jnp.dot(h, wu_ref[0].astype(BF16), preferred_element_type=F32)
    mid = (a * _sigmoid(a) * b).astype(BF16)
    y = jnp.dot(mid, wd_ref[0].astype(BF16), preferred_element_type=F32)

    @pl.when(f == 0)
    def _():
        acc_scr[...] = y

    @pl.when(f > 0)
    def _():
        acc_scr[...] += y

    if not moe:

        @pl.when(f == nf - 1)
        def _():
            o_ref[0] = x_ref[0] + gate_ref[0] * acc_scr[...]

    else:

        @pl.when(f == nf - 1)
        def _():
            gw = gw_ref[0]
            lane = lax.broadcasted_iota(jnp.int32, gw.shape, 1)
            col = jnp.sum(jnp.where(lane == e, gw, 0.0), axis=1, keepdims=True)
            contrib = col * acc_scr[...]

            @pl.when(e == 0)
            def _():
                tot_scr[...] = contrib

            @pl.when(e > 0)
            def _():
                tot_scr[...] += contrib

            @pl.when(e == ne - 1)
            def _():
                o_ref[0] = x_ref[0] + gate_ref[0] * tot_scr[...]


def ffn(x3, mod, gate, w_gu3, w_down3, wl, *, gates=None, tm=None, name="ffn"):
    bx, L, D = x3.shape
    moe = gates is not None
    ne = N_EXPERTS if moe else 1
    tm = tm or min(L, 1024)
    tf = FFN_TF
    nf = FFN_DIM // tf
    assert L % tm == 0 and FFN_DIM % tf == 0
    in_specs = [pl.BlockSpec((1, tm, D), lambda b, m, e, f: (b, m, 0))]
    args = [x3]
    for a in (*mod, gate):
        in_specs.append(_row_spec(a, tm, D, lambda e, f: 0))
        args.append(a)
    if moe:
        in_specs.append(pl.BlockSpec((1, tm, LANES), lambda b, m, e, f: (b, m, 0)))
        args.append(gates)
    in_specs += [
        pl.BlockSpec((1, D, tf), lambda b, m, e, f: (wl + e, 0, f)),
        pl.BlockSpec((1, D, tf), lambda b, m, e, f: (wl + e, 0, nf + f)),
        pl.BlockSpec((1, tf, D), lambda b, m, e, f: (wl + e, f, 0)),
    ]
    args += [w_gu3, w_gu3, w_down3]
    scratch = [pltpu.VMEM((tm, D), BF16), pltpu.VMEM((tm, D), F32)]
    if moe:
        scratch.append(pltpu.VMEM((tm, D), F32))
    return pl.pallas_call(
        functools.partial(_ffn_body, moe=moe, nf=nf, ne=ne),
        grid=(bx, L // tm, ne, nf),
        in_specs=in_specs,
        out_specs=pl.BlockSpec((1, tm, D), lambda b, m, e, f: (b, m, 0)),
        out_shape=jax.ShapeDtypeStruct((bx, L, D), F32),
        scratch_shapes=scratch,
        compiler_params=_cparams(4),
        name=name,
    )(*args)


def _router_body(x_ref, sh_ref, sc_ref, w_ref, b_ref, o_ref):
    h = _modulate(x_ref[0], sh_ref[0], sc_ref[0]).astype(BF16)
    logits = lax.dot_general(h, w_ref[0].astype(BF16), (((1,), (1,)), ((), ())), preferred_element_type=F32)
    logits = logits + b_ref[0]
    lane = lax.broadcasted_iota(jnp.int32, logits.shape, 1).astype(F32)
    lg = jnp.where(lane < N_EXPERTS, logits, -jnp.inf)
    m1 = jnp.max(lg, axis=1, keepdims=True)
    i1 = jnp.min(jnp.where(lg == m1, lane, float(LANES)), axis=1, keepdims=True)
    lg2 = jnp.where(lane == i1, -jnp.inf, lg)
    m2 = jnp.max(lg2, axis=1, keepdims=True)
    i2 = jnp.min(jnp.where(lg2 == m2, lane, float(LANES)), axis=1, keepdims=True)
    e2 = jnp.exp(m2 - m1)
    den = 1.0 + e2
    o_ref[0] = jnp.where(lane == i1, 1.0 / den, 0.0) + jnp.where(lane == i2, e2 / den, 0.0)


def router(x3, mod, w_rt_pad, b_pad, wl, *, tm=None):
    bx, L, D = x3.shape
    tm = tm or min(L, 1024)
    in_specs = [pl.BlockSpec((1, tm, D), lambda b, m: (b, m, 0))]
    args = [x3]
    for a in mod:
        in_specs.append(_row_spec(a, tm, D, lambda: 0))
        args.append(a)
    in_specs += [
        pl.BlockSpec((1, LANES, D), lambda b, m: (wl, 0, 0)),
        pl.BlockSpec((1, 1, LANES), lambda b, m: (wl, 0, 0)),
    ]
    args += [w_rt_pad, b_pad]
    return pl.pallas_call(
        _router_body,
        grid=(bx, L // tm),
        in_specs=in_specs,
        out_specs=pl.BlockSpec((1, tm, LANES), lambda b, m: (b, m, 0)),
        out_shape=jax.ShapeDtypeStruct((bx, L, LANES), F32),
        compiler_params=_cparams(2),
        name="router",
    )(*args)


MOE_T = 256
MOE_CH = 128
MOE_TM = 256
MOE_TF = FFN_DIM // 2
MOE_ALIGN = 8
MOE_STAGE = -(-(2 * MOE_T + N_EXPERTS * (MOE_ALIGN - 1) + MOE_CH) // 8) * 8


def _moe_plan(gates2, n_tokens):
    nb = n_tokens // MOE_T
    routed = (gates2[:, :N_EXPERTS] > 0).reshape(nb, MOE_T, N_EXPERTS)
    ri = routed.astype(jnp.int32)
    rank = jnp.cumsum(ri, axis=1) - ri
    cnt = jnp.sum(ri, axis=1)
    cnt_al = -(-cnt // MOE_ALIGN) * MOE_ALIGN
    lo = jnp.cumsum(cnt_al, axis=1) - cnt_al
    total = jnp.sum(cnt_al, axis=0)
    region = -(-(total + MOE_CH) // MOE_TM) * MOE_TM
    ends = jnp.cumsum(region)
    off = ends - region
    pos = off[None, :] + jnp.cumsum(cnt_al, axis=0) - cnt_al
    nch = -(-cnt_al // MOE_CH)
    m_pad = -(-(2 * n_tokens + nb * N_EXPERTS * (MOE_ALIGN - 1) + N_EXPERTS * (MOE_CH + MOE_TM)) // MOE_TM) * MOE_TM
    n_tiles = m_pad // MOE_TM
    tile_start = jnp.arange(n_tiles, dtype=jnp.int32) * MOE_TM
    tile_e = jnp.minimum(jnp.sum(tile_start[:, None] >= ends[None, :], axis=1), N_EXPERTS - 1).astype(jnp.int32)
    n_used = (ends[-1] // MOE_TM).astype(jnp.int32).reshape(1)
    dest = jnp.where(routed, lo[:, None, :] + rank, -1)
    d_hi = jnp.max(dest, axis=2)
    d_lo = jnp.min(jnp.where(routed, dest, MOE_STAGE), axis=2)
    dd = jnp.stack([d_hi, d_lo], axis=1).astype(jnp.int32)
    eidx = jnp.arange(N_EXPERTS, dtype=jnp.int32)
    e_a = jnp.min(jnp.where(routed, eidx, N_EXPERTS), axis=2)
    e_b = jnp.max(jnp.where(routed, eidx, -1), axis=2)
    g3 = gates2[:, :N_EXPERTS].reshape(nb, MOE_T, N_EXPERTS)
    at = lambda a, e: jnp.sum(jnp.where(eidx == e[..., None], a, 0), axis=2)
    r_a, r_b = at(rank, e_a), at(rank, e_b)
    w_a, w_b = at(g3, e_a), jnp.where(e_b != e_a, at(g3, e_b), 0.0)
    col = lambda e, r, c: jnp.where(r // MOE_CH == c, e * MOE_CH + r % MOE_CH, -1).astype(F32)
    cmeta = jnp.stack([col(e_a, r_a, 0), col(e_b, r_b, 0), col(e_a, r_a, 1), col(e_b, r_b, 1), w_a, w_b], axis=-1)
    cmeta = jnp.pad(cmeta.reshape(n_tokens, 6), ((0, 0), (0, LANES - 6)))
    two = (jnp.max(nch, axis=1) > 1).astype(jnp.int32)
    flat = lambda a: a.reshape(-1).astype(jnp.int32)
    return dict(lo=flat(lo // MOE_ALIGN), pos=flat(pos // MOE_ALIGN), nch=flat(nch), tile_e=tile_e, n_used=n_used,
                dd=dd, cmeta=cmeta, two=two, m_pad=m_pad, n_tiles=n_tiles, nb=nb)


def _dispatch_body(lo_ref, pos_ref, nch_ref, x_ref, sh_ref, sc_ref, dd_ref, xs_in_ref, xs_ref, stage_scr, sem,
                   *, nb):
    del xs_in_ref
    b = pl.program_id(0)
    slot = b % 2
    h = _modulate(x_ref[...], sh_ref[0], sc_ref[0]).astype(BF16)
    r = lax.broadcasted_iota(jnp.int32, (MOE_STAGE, MOE_T), 0)
    dd = dd_ref[0]
    onehot = jnp.where((r == dd[0:1, :]) | (r == dd[1:2, :]), 1.0, 0.0).astype(BF16)
    stage_scr[slot] = jnp.dot(onehot, h, preferred_element_type=F32)

    def seg_copy(blk, e, c):
        src0 = pl.multiple_of(lo_ref[blk * N_EXPERTS + e] * MOE_ALIGN + c * MOE_CH, MOE_ALIGN)
        dst0 = pl.multiple_of(pos_ref[blk * N_EXPERTS + e] * MOE_ALIGN + c * MOE_CH, MOE_ALIGN)
        return pltpu.make_async_copy(stage_scr.at[blk % 2, pl.ds(src0, MOE_CH)], xs_ref.at[pl.ds(dst0, MOE_CH)],
                                     sem.at[blk % 2, e, c])

    def for_segments(blk, fn):
        for e in range(N_EXPERTS):
            for c in range(2):
                @pl.when(c < nch_ref[blk * N_EXPERTS + e])
                def _():
                    fn(seg_copy(blk, e, c))

    @pl.when(b > 0)
    def _():
        for_segments(b - 1, lambda cp: cp.wait())

    for_segments(b, lambda cp: cp.start())

    @pl.when(b == nb - 1)
    def _():
        for_segments(b, lambda cp: cp.wait())


def moe_dispatch(x2, mod, plan, seq_len):
    n_tokens, D = x2.shape
    per_seq = seq_len // MOE_T
    nb = plan["nb"]
    vec = lambda: pl.BlockSpec((1, 1, D), lambda b, *_: (b // per_seq, 0, 0))
    grid_spec = pltpu.PrefetchScalarGridSpec(
        num_scalar_prefetch=3,
        grid=(nb,),
        in_specs=[
            pl.BlockSpec((MOE_T, D), lambda b, *_: (b, 0)),
            vec(), vec(),
            pl.BlockSpec((1, 2, MOE_T), lambda b, *_: (b, 0, 0)),
            pl.BlockSpec(memory_space=pltpu.MemorySpace.HBM),
        ],
        out_specs=pl.BlockSpec(memory_space=pltpu.MemorySpace.HBM),
        scratch_shapes=[pltpu.VMEM((2, MOE_STAGE, D), F32), pltpu.SemaphoreType.DMA((2, N_EXPERTS, 2))],
    )
    return pl.pallas_call(
        functools.partial(_dispatch_body, nb=nb),
        grid_spec=grid_spec,
        out_shape=jax.ShapeDtypeStruct((plan["m_pad"], D), F32),
        input_output_aliases={7: 0},
        compiler_params=_cparams(1),
        name="moe_dispatch",
    )(plan["lo"], plan["pos"], plan["nch"], x2, mod[0], mod[1], plan["dd"], jnp.zeros((plan["m_pad"], D), F32))


def _gffn_body(te_ref, nu_ref, *refs, has_prev):
    it = iter(refs)
    xs_ref = next(it)
    yp_ref = next(it) if has_prev else None
    wg_ref, wu_ref, wd_ref, o_ref, wg_scr, wu_scr, wd_scr = (next(it) for _ in range(7))
    t = pl.program_id(0)
    e_here = te_ref[t]
    e_prev = te_ref[jnp.maximum(t - 1, 0)]

    @pl.when(t < nu_ref[0])
    def _():
        @pl.when((t == 0) | (e_here != e_prev))
        def _():
            wg_scr[...] = wg_ref[0].astype(BF16)
            wu_scr[...] = wu_ref[0].astype(BF16)
            wd_scr[...] = wd_ref[0].astype(BF16)

        h = xs_ref[...].astype(BF16)
        a = jnp.dot(h, wg_scr[...], preferred_element_type=F32)
        b = jnp.dot(h, wu_scr[...], preferred_element_type=F32)
        mid = (a * _sigmoid(a) * b).astype(BF16)
        y = jnp.dot(mid, wd_scr[...], preferred_element_type=F32)
        o_ref[...] = (yp_ref[...] + y) if has_prev else y

    @pl.when(t >= nu_ref[0])
    def _():
        o_ref[...] = jnp.zeros_like(o_ref)


def moe_grouped_ffn(xs, w_gu3, w_down3, wl, plan, f, y_prev=None):
    m_pad, D = xs.shape
    nf = FFN_DIM // MOE_TF
    has_prev = y_prev is not None
    row = lambda t, te, nu: (jnp.minimum(t, nu[0] - 1), 0)
    tile = lambda: pl.BlockSpec((MOE_TM, D), row)
    in_specs = [tile()] + ([tile()] if has_prev else [])
    in_specs += [
        pl.BlockSpec((1, D, MOE_TF), lambda t, te, nu: (wl + te[t], 0, f)),
        pl.BlockSpec((1, D, MOE_TF), lambda t, te, nu: (wl + te[t], 0, nf + f)),
        pl.BlockSpec((1, MOE_TF, D), lambda t, te, nu: (wl + te[t], f, 0), pipeline_mode=pl.Buffered(1)),
    ]
    grid_spec = pltpu.PrefetchScalarGridSpec(
        num_scalar_prefetch=2,
        grid=(plan["n_tiles"],),
        in_specs=in_specs,
        out_specs=pl.BlockSpec((MOE_TM, D), lambda t, te, nu: (t, 0)),
        scratch_shapes=[pltpu.VMEM((D, MOE_TF), BF16), pltpu.VMEM((D, MOE_TF), BF16), pltpu.VMEM((MOE_TF, D), BF16)],
    )
    args = [xs] + ([y_prev] if has_prev else []) + [w_gu3, w_gu3, w_down3]
    return pl.pallas_call(
        functools.partial(_gffn_body, has_prev=has_prev),
        grid_spec=grid_spec,
        out_shape=jax.ShapeDtypeStruct((m_pad, D), F32),
        compiler_params=_cparams(1),
        name=f"moe_grouped_ffn{f}",
    )(plan["tile_e"], plan["n_used"], *args)


def _combine_body(pos_ref, nch_ref, two_ref, x_ref, gate_ref, cm_ref, *refs):
    y_refs = refs[:N_EXPERTS]
    o_ref, acc_scr, yhi_scr, ylo_scr = refs[N_EXPERTS:]
    del pos_ref, nch_ref
    b = pl.program_id(0)
    c = pl.program_id(1)

    def gathered():
        for e in range(N_EXPERTS):
            y = y_refs[e][...]
            y_hi = y.astype(BF16)
            yhi_scr[e * MOE_CH:(e + 1) * MOE_CH, :] = y_hi
            ylo_scr[e * MOE_CH:(e + 1) * MOE_CH, :] = (y - y_hi.astype(F32)).astype(BF16)
        cm = cm_ref[...]
        lane = lax.broadcasted_iota(jnp.int32, (MOE_T, N_EXPERTS * MOE_CH), 1).astype(F32)
        picks = []
        for k in range(2):
            want = jnp.where(c == 0, cm[:, k:k + 1], cm[:, 2 + k:3 + k])
            picks.append(jnp.where(lane == want, 1.0, 0.0).astype(BF16))
        onehot = jnp.concatenate(picks, axis=0)
        rows = (jnp.dot(onehot, yhi_scr[...], preferred_element_type=F32)
                + jnp.dot(onehot, ylo_scr[...], preferred_element_type=F32))
        return cm[:, 4:5] * rows[:MOE_T] + cm[:, 5:6] * rows[MOE_T:]

    @pl.when(c == 0)
    def _():
        acc_scr[...] = gathered()

    @pl.when(c == 1)
    def _():
        @pl.when(two_ref[b] > 0)
        def _():
            acc_scr[...] += gathered()

        o_ref[...] = x_ref[...] + gate_ref[0] * acc_scr[...]


def moe_combine(x2, gate, y, plan, seq_len):
    n_tokens, D = x2.shape
    per_seq = seq_len // MOE_T

    def window(e):
        def idx(b, c, pos, nch, two):
            s = b * N_EXPERTS + e
            return ((pos[s] + jnp.where(c < nch[s], c, 0) * (MOE_CH // MOE_ALIGN)) * MOE_ALIGN, 0)
        return pl.BlockSpec((pl.Element(MOE_CH), pl.Element(D)), idx)

    blk = lambda w: pl.BlockSpec((MOE_T, w), lambda b, c, *_: (b, 0))
    grid_spec = pltpu.PrefetchScalarGridSpec(
        num_scalar_prefetch=3,
        grid=(plan["nb"], 2),
        in_specs=[blk(D), pl.BlockSpec((1, 1, D), lambda b, c, *_: (b // per_seq, 0, 0)), blk(LANES)]
        + [window(e) for e in range(N_EXPERTS)],
        out_specs=blk(D),
        scratch_shapes=[pltpu.VMEM((MOE_T, D), F32), pltpu.VMEM((N_EXPERTS * MOE_CH, D), BF16),
                        pltpu.VMEM((N_EXPERTS * MOE_CH, D), BF16)],
    )
    return pl.pallas_call(
        _combine_body,
        grid_spec=grid_spec,
        out_shape=jax.ShapeDtypeStruct((n_tokens, D), F32),
        compiler_params=_cparams(2),
        name="moe_combine",
    )(plan["pos"], plan["nch"], plan["two"], x2, gate, plan["cmeta"], *([y] * N_EXPERTS))


def moe_sparse(x3, mod, gate, gates, w_gu3, w_down3, wl):
    B, L, D = x3.shape
    n_tokens = B * L
    assert L % MOE_T == 0 and FFN_DIM % MOE_TF == 0
    x2 = x3.reshape(n_tokens, D)
    gates2 = gates.reshape(n_tokens, LANES)
    plan = _moe_plan(gates2, n_tokens)
    xs = moe_dispatch(x2, mod, plan, L)
    y = None
    for f in range(FFN_DIM // MOE_TF):
        y = moe_grouped_ffn(xs, w_gu3, w_down3, wl, plan, f, y_prev=y)
    return moe_combine(x2, gate, y, plan, L).reshape(B, L, D)


RET_CHUNKS_PER_STEP = 4


def _ret_body(*refs, zero_init, nc, chunk, per_step):
    it = iter(refs)
    q_ref, k_ref, v_ref, g_ref = next(it), next(it), next(it), next(it)
    s0_ref = None if zero_init else next(it)
    inner_ref, qd_ref, kd_ref, cd_ref = next(it), next(it), next(it), next(it)
    o_ref, sout_ref, s_scr = next(it), next(it), next(it)
    c = pl.program_id(2)

    @pl.when(c == 0)
    def _():
        if zero_init:
            s_scr[...] = jnp.zeros_like(s_scr)
        else:
            s_scr[...] = s0_ref[0, 0, 0]

    s = s_scr[...]
    for ci in range(per_step):
        rows = slice(ci * chunk, (ci + 1) * chunk)
        q = q_ref[0, rows, :]
        k = k_ref[0, rows, :]
        v = v_ref[0, rows, :]
        att = lax.dot_general(q, k, (((1,), (1,)), ((), ())), preferred_element_type=F32) * inner_ref[0]
        inner = jnp.dot(att.astype(BF16), v, preferred_element_type=F32)
        cross = jnp.dot(q, s.astype(BF16), preferred_element_type=F32) * qd_ref[0]
        kdt = (k.astype(F32) * kd_ref[0]).T.astype(BF16)
        s = s * cd_ref[0] + jnp.dot(kdt, v, preferred_element_type=F32)
        o = inner + cross
        on = o * lax.rsqrt(jnp.mean(o * o, axis=-1, keepdims=True) + EPS)
        g = g_ref[0, rows, :]
        o_ref[0, rows, :] = (g * _sigmoid(g) * on).astype(o_ref.dtype)
    s_scr[...] = s

    @pl.when(c == nc - 1)
    def _():
        sout_ref[0, 0] = s


def retention_scan(q, k, v, g, tables, chunk, *, s0=None, s0_layer=0):
    B, L, _ = q.shape
    per_step = math.gcd(L // chunk, RET_CHUNKS_PER_STEP)
    rows = per_step * chunk
    nc = L // rows
    inner, qd, kd, cd = tables
    zero_init = s0 is None
    in_specs = [
        pl.BlockSpec((1, rows, RET_DK), lambda h, b, c: (b, c, h)),
        pl.BlockSpec((1, rows, RET_DK), lambda h, b, c: (b, c, h)),
        pl.BlockSpec((1, rows, RET_DV), lambda h, b, c: (b, c, h)),
        pl.BlockSpec((1, rows, RET_DV), lambda h, b, c: (b, c, h)),
    ]
    args = [q, k, v, g]
    if not zero_init:
        in_specs.append(pl.BlockSpec((1, 1, 1, RET_DK, RET_DV), lambda h, b, c: (s0_layer, b, h, 0, 0)))
        args.append(s0)
    in_specs += [
        pl.BlockSpec((1, chunk, chunk), lambda h, b, c: (h, 0, 0)),
        pl.BlockSpec((1, chunk, RET_DV), lambda h, b, c: (h, 0, 0)),
        pl.BlockSpec((1, chunk, RET_DK), lambda h, b, c: (h, 0, 0)),
        pl.BlockSpec((1, 1, RET_DV), lambda h, b, c: (h, 0, 0)),
    ]
    args += [inner, qd, kd, cd]
    return pl.pallas_call(
        functools.partial(_ret_body, zero_init=zero_init, nc=nc, chunk=chunk, per_step=per_step),
        grid=(RET_HEADS, B, nc),
        in_specs=in_specs,
        out_specs=[
            pl.BlockSpec((1, rows, RET_DV), lambda h, b, c: (b, c, h)),
            pl.BlockSpec((1, 1, RET_DK, RET_DV), lambda h, b, c: (b, h, 0, 0)),
        ],
        out_shape=[
            jax.ShapeDtypeStruct((B, L, RET_V), BF16),
            jax.ShapeDtypeStruct((B, RET_HEADS, RET_DK, RET_DV), F32),
        ],
        scratch_shapes=[pltpu.VMEM((RET_DK, RET_DV), F32)],
        compiler_params=_cparams(3),
        name="retention_scan",
    )(*args)


def retention_tables(n_real, n_pad):
    log_gamma = jnp.log1p(-(2.0 ** (-5.0 - jnp.arange(RET_HEADS, dtype=F32))))
    idx = jnp.arange(n_pad, dtype=F32)
    valid = idx < n_real
    diff = idx[:, None] - idx[None, :]
    ok = (diff >= 0) & valid[:, None] & valid[None, :]
    inner = jnp.where(ok[None], jnp.exp(log_gamma[:, None, None] * jnp.maximum(diff, 0.0)[None]), 0.0)
    qd = jnp.exp(log_gamma[:, None] * (idx[None, :] + 1.0))
    kd = jnp.where(valid[None, :], jnp.exp(log_gamma[:, None] * (n_real - 1.0 - idx[None, :])), 0.0)
    cd = jnp.exp(log_gamma * n_real)
    return (inner,
            jnp.broadcast_to(qd[:, :, None], (RET_HEADS, n_pad, RET_DV)),
            jnp.broadcast_to(kd[:, :, None], (RET_HEADS, n_pad, RET_DK)),
            jnp.broadcast_to(cd[:, None, None], (RET_HEADS, 1, RET_DV)))


def rotary_tables(pos, reps):
    inv_freq = ROPE_BASE ** (-jnp.arange(0, RET_DK, 2, dtype=F32) / RET_DK)
    ang = pos.astype(F32)[:, None] * inv_freq[None, :]
    cos = jnp.repeat(jnp.cos(ang), 2, axis=1)
    sin = jnp.sin(ang)
    sin_signed = jnp.stack([-sin, sin], axis=-1).reshape(ang.shape[0], RET_DK)
    return jnp.tile(cos, (1, reps)), jnp.tile(sin_signed, (1, reps))


def _pool_body(x_ref, xp_ref, buf_ref, sh_ref, sc_ref, gate_ref, w_ref, cs_ref, o_ref, tail_ref, ext_scr,
               *, tm, pos0, has_prev):
    m = pl.program_id(1)
    sh, sc = sh_ref[0], sc_ref[0]
    h = _modulate(x_ref[0], sh, sc)

    @pl.when(m == 0)
    def _():
        ext_scr[0:POOL_HALO, :] = buf_ref[0]

    if has_prev:

        @pl.when(m > 0)
        def _():
            ext_scr[0:POOL_HALO, :] = _modulate(xp_ref[0], sh, sc)

    ext_scr[POOL_HALO:POOL_HALO + tm, :] = h
    tail_ref[0] = ext_scr[tm:tm + POOL_HALO, :]
    row = lax.broadcasted_iota(jnp.int32, (tm, 1), 0)
    pos1 = (pos0 + m * tm + row + 1).astype(F32)
    rows = max(tm, 16)
    ys = []
    for gi, w in enumerate(POOL_WINDOWS):
        c0, c1 = gi * POOL_GW, (gi + 1) * POOL_GW
        win = ext_scr[POOL_HALO:POOL_HALO + tm, c0:c1]
        for j in range(1, w):
            win = win + ext_scr[POOL_HALO - j:POOL_HALO - j + tm, c0:c1]
        d = win / jnp.minimum(jnp.float32(w), pos1) - h[:, c0:c1]
        if rows != tm:
            d = jnp.concatenate([d, jnp.zeros((rows - tm, POOL_GW), F32)], axis=0)
        y = jnp.dot(d.astype(BF16), w_ref[0, gi].astype(BF16), preferred_element_type=F32)
        ys.append(y[0:tm])
    y = jnp.concatenate(ys, axis=1) * cs_ref[...]
    o_ref[0] = x_ref[0] + gate_ref[0] * y


def pool_layer(x3, buf16, mod, gate, pool_w, pool_scale, wl, pos0, *, tm):
    B, L, D = x3.shape
    assert L % tm == 0 and (L == tm or tm % POOL_HALO == 0)
    has_prev = L > tm
    ph = POOL_HALO if has_prev else min(L, POOL_HALO)
    per = tm // POOL_HALO if has_prev else 1
    vec = lambda: pl.BlockSpec((1, 1, D), lambda b, m: (b, 0, 0))
    return pl.pallas_call(
        functools.partial(_pool_body, tm=tm, pos0=pos0, has_prev=has_prev),
        grid=(B, L // tm),
        in_specs=[
            pl.BlockSpec((1, tm, D), lambda b, m: (b, m, 0)),
            pl.BlockSpec((1, ph, D), lambda b, m: (b, jnp.maximum(m * per - 1, 0), 0)),
            pl.BlockSpec((1, POOL_HALO, D), lambda b, m: (b, 0, 0)),
            vec(), vec(), vec(),
            pl.BlockSpec((1,) + pool_w.shape[1:], lambda b, m: (wl, 0, 0, 0)),
            pl.BlockSpec((1, D), lambda b, m: (wl, 0)),
        ],
        out_specs=[
            pl.BlockSpec((1, tm, D), lambda b, m: (b, m, 0)),
            pl.BlockSpec((1, POOL_HALO, D), lambda b, m: (b, 0, 0)),
        ],
        out_shape=[
            jax.ShapeDtypeStruct((B, L, D), F32),
            jax.ShapeDtypeStruct((B, POOL_HALO, D), F32),
        ],
        scratch_shapes=[pltpu.VMEM((tm + POOL_HALO, D), F32)],
        compiler_params=_cparams(2),
        name="pool_layer",
    )(x3, x3, buf16, mod[0], mod[1], gate, pool_w, pool_scale)


def _split3(x):
    p0 = x.astype(BF16)
    r1 = x - p0.astype(F32)
    p1 = r1.astype(BF16)
    p2 = (r1 - p1.astype(F32)).astype(BF16)
    return p0, p1, p2


def _lane_cumsum(x, tri):
    p0, p1, p2 = _split3(x)
    dot = lambda p: jnp.dot(p, tri, preferred_element_type=F32)
    return (dot(p0) + dot(p1)) + dot(p2)


def _upper_tri(t):
    r = lax.broadcasted_iota(jnp.int32, (t, t), 0)
    c = lax.broadcasted_iota(jnp.int32, (t, t), 1)
    return jnp.where(r <= c, 1.0, 0.0).astype(BF16)


BIAS_PIECES = 3


def _fbias_body(lf_ref, o_ref, carry_scr, *, tc):
    @pl.when(pl.program_id(1) == 0)
    def _():
        carry_scr[...] = jnp.zeros_like(carry_scr)

    r = lax.broadcasted_iota(jnp.int32, (tc, tc), 0)
    c = lax.broadcasted_iota(jnp.int32, (tc, tc), 1)
    tril = jnp.where(c <= r, 1.0, 0.0).astype(BF16)
    p0, p1, p2 = _split3(lf_ref[0])
    dot = lambda p: jnp.dot(tril, p, preferred_element_type=F32)
    f = carry_scr[...] + ((dot(p0) + dot(p1)) + dot(p2))
    carry_scr[...] = f[tc - 1:tc, :]
    head = lax.broadcasted_iota(jnp.int32, (FOX_HEADS, LANES), 0)
    lane = lax.broadcasted_iota(jnp.int32, (FOX_HEADS, LANES), 1)
    out = None
    for p, piece in enumerate(_split3(-f)):
        place = jnp.where(lane == BIAS_PIECES * head + p, 1.0, 0.0).astype(BF16)
        term = jnp.dot(piece, place, preferred_element_type=F32)
        out = term if out is None else out + term
    o_ref[0] = out.astype(BF16)


def fox_bias_features(lf, *, tc=512):
    B, L, H = lf.shape
    tc = min(tc, L)
    return pl.pallas_call(
        functools.partial(_fbias_body, tc=tc),
        grid=(B, L // tc),
        in_specs=[pl.BlockSpec((1, tc, H), lambda b, c: (b, c, 0))],
        out_specs=pl.BlockSpec((1, tc, LANES), lambda b, c: (b, c, 0)),
        out_shape=jax.ShapeDtypeStruct((B, L, LANES), BF16),
        scratch_shapes=[pltpu.VMEM((1, H), F32)],
        compiler_params=_cparams(2),
        name="fox_bias_features",
    )(lf)


def _flash_body(qt_ref, k_ref, fb_ref, vt_ref, o_ref, *, tq, tk):
    hp = pl.program_id(1)
    qi = pl.program_id(2)
    pair = 2 * FOX_DH
    row = lax.broadcasted_iota(jnp.int32, (pair, tq), 0)
    qt = qt_ref[0]
    qaug = []
    for i in range(2):
        q_head = jnp.where(row // FOX_DH == i, qt, jnp.zeros_like(qt))
        pick = jnp.where(row // BIAS_PIECES == 2 * hp + i, 1.0, 0.0).astype(BF16)
        qaug.append(jnp.concatenate([q_head, pick], axis=0))
    key_i = lax.broadcasted_iota(jnp.int32, (tk, tq), 0)
    qry_i = lax.broadcasted_iota(jnp.int32, (tk, tq), 1)
    per_q = tq // tk

    def step(j, carry, diag):
        k0 = pl.multiple_of(j * tk, tk)
        kaug = jnp.concatenate([k_ref[0, pl.ds(k0, tk), :], fb_ref[0, pl.ds(k0, tk), :]], axis=1)
        new = []
        for i in range(2):
            m_old, l_old, acc = carry[i]
            st = jnp.dot(kaug, qaug[i], preferred_element_type=F32)
            if diag is not None:
                st = jnp.where(key_i + diag * tk <= qry_i, st, NEG_BIG)
            m_new = jnp.maximum(m_old, jnp.max(st, axis=0, keepdims=True))
            alpha = jnp.exp(m_old - m_new)
            p = jnp.exp(st - m_new)
            l_new = alpha * l_old + jnp.sum(p, axis=0, keepdims=True)
            vt = vt_ref[0, i * FOX_DH:(i + 1) * FOX_DH, pl.ds(k0, tk)].astype(BF16)
            acc = alpha * acc + jnp.dot(vt, p.astype(BF16), preferred_element_type=F32)
            new.append((m_new, l_new, acc))
        return tuple(new)

    init = tuple((jnp.full((1, tq), NEG_BIG, F32), jnp.zeros((1, tq), F32), jnp.zeros((FOX_DH, tq), F32))
                 for _ in range(2))
    carry = lax.fori_loop(0, qi * per_q, lambda j, c: step(j, c, None), init)
    for d in range(per_q):
        carry = step(qi * per_q + d, carry, d)
    o_ref[0] = jnp.concatenate([(acc / l).T for _, l, acc in carry], axis=1).astype(o_ref.dtype)


def fox_flash(qt, k, fb, vt, *, tq=1024, tk=1024):
    B, L, D = k.shape
    tq = min(tq, L)
    tk = min(tk, tq)
    assert L % tq == 0 and tq % tk == 0
    pair = 2 * FOX_DH
    return pl.pallas_call(
        functools.partial(_flash_body, tq=tq, tk=tk),
        grid=(B, FOX_HEADS // 2, L // tq),
        in_specs=[
            pl.BlockSpec((1, pair, tq), lambda b, hp, qi: (b, hp, qi)),
            pl.BlockSpec((1, L, pair), lambda b, hp, qi: (b, 0, hp)),
            pl.BlockSpec((1, L, LANES), lambda b, hp, qi: (b, 0, 0)),
            pl.BlockSpec((1, pair, L), lambda b, hp, qi: (b, hp, 0)),
        ],
        out_specs=pl.BlockSpec((1, tq, pair), lambda b, hp, qi: (b, qi, hp)),
        out_shape=jax.ShapeDtypeStruct((B, L, D), BF16),
        compiler_params=_cparams(3),
        name="fox_flash",
    )(qt, k, fb, vt)


MAX_DECODE_PAGES = 16


def _decode_body(pt_ref, q_ref, kn_ref, vn_ref, lfn_ref, *refs, n_steps, lq, pages):
    kc = refs[0:pages]
    vc = refs[pages:2 * pages]
    lc = refs[2 * pages:3 * pages]
    o_ref, qbd_scr, m_scr, l_scr, acc_scr, carry_scr = refs[3 * pages:]
    b = pl.program_id(0)
    st = pl.program_id(1)
    rows = lq * FOX_HEADS
    last = (((1,), (1,)), ((), ()))
    tri = _upper_tri(PAGE_SIZE)

    @pl.when(st == 0)
    def _():
        head = lax.broadcasted_iota(jnp.int32, (FOX_HEADS, D_MODEL), 0)
        lane_head = lax.broadcasted_iota(jnp.int32, (FOX_HEADS, D_MODEL), 1) // FOX_DH
        blocks = [jnp.where(head == lane_head, jnp.broadcast_to(q_ref[0, t:t + 1, :], (FOX_HEADS, D_MODEL)), 0.0)
                  for t in range(lq)]
        qbd_scr[...] = jnp.concatenate(blocks, axis=0).astype(BF16)
        m_scr[...] = jnp.full_like(m_scr, NEG_BIG)
        l_scr[...] = jnp.zeros_like(l_scr)
        acc_scr[...] = jnp.zeros_like(acc_scr)
        carry_scr[...] = jnp.zeros_like(carry_scr)

    def absorb(s, v_mat, v_is_t):
        m_old = m_scr[...]
        m_new = jnp.maximum(m_old, jnp.max(s, axis=1, keepdims=True))
        alpha = jnp.exp(m_old - m_new)
        p = jnp.exp(s - m_new)
        l_scr[...] = alpha * l_scr[...] + jnp.sum(p, axis=1, keepdims=True)
        if v_is_t:
            pv = lax.dot_general(p.astype(BF16), v_mat, last, preferred_element_type=F32)
        else:
            pv = jnp.dot(p.astype(BF16), v_mat, preferred_element_type=F32)
        acc_scr[...] = alpha * acc_scr[...] + pv
        m_scr[...] = m_new

    @pl.when(st < n_steps)
    def _():
        kt = jnp.concatenate([kc[i][0].reshape(D_MODEL, PAGE_SIZE).astype(BF16) for i in range(pages)], axis=1)
        vt = jnp.concatenate([vc[i][0].reshape(D_MODEL, PAGE_SIZE).astype(BF16) for i in range(pages)], axis=1)
        within = _lane_cumsum(jnp.concatenate([lc[i][0] for i in range(pages)], axis=0), tri)
        f = carry_scr[...]
        biases = []
        for i in range(pages):
            f_page = f + within[i * FOX_HEADS:(i + 1) * FOX_HEADS, :]
            biases.append(jnp.tile(f_page, (lq, 1)))
            f = jnp.broadcast_to(f_page[:, PAGE_SIZE - 1:PAGE_SIZE], f_page.shape)
        carry_scr[...] = f
        s = jnp.dot(qbd_scr[...], kt, preferred_element_type=F32) - jnp.concatenate(biases, axis=1)
        absorb(s, vt, True)

    @pl.when(st == n_steps)
    def _():
        pad = jnp.zeros((PAGE_SIZE - lq, D_MODEL), F32)
        kn = jnp.concatenate([kn_ref[0], pad], axis=0).astype(BF16)
        vn = jnp.concatenate([vn_ref[0], pad], axis=0).astype(BF16)
        n_tok = lfn_ref.shape[1]
        tok = lax.broadcasted_iota(jnp.int32, (n_tok, PAGE_SIZE), 0)
        key = lax.broadcasted_iota(jnp.int32, (n_tok, PAGE_SIZE), 1)
        sel = jnp.where((tok // lq == b) & (tok % lq <= key) & (key < lq), 1.0, 0.0).astype(BF16)
        p0, p1, p2 = _split3(lfn_ref[...])
        dot = lambda p: jnp.dot(p, sel, preferred_element_type=F32)
        f = carry_scr[...] + ((dot(p0) + dot(p1)) + dot(p2))
        s = lax.dot_general(qbd_scr[...], kn, last, preferred_element_type=F32) - jnp.tile(f, (lq, 1))
        rq = lax.broadcasted_iota(jnp.int32, (rows, PAGE_SIZE), 0) // FOX_HEADS
        kk = lax.broadcasted_iota(jnp.int32, (rows, PAGE_SIZE), 1)
        s = jnp.where(kk <= rq, s, NEG_BIG)
        absorb(s, vn, False)
        o = acc_scr[...] / l_scr[...]
        head = lax.broadcasted_iota(jnp.int32, (FOX_HEADS, D_MODEL), 0)
        lane_head = lax.broadcasted_iota(jnp.int32, (FOX_HEADS, D_MODEL), 1) // FOX_DH
        outs = [jnp.sum(jnp.where(head == lane_head, o[t * FOX_HEADS:(t + 1) * FOX_HEADS, :], 0.0), axis=0,
                        keepdims=True) for t in range(lq)]
        o_ref[0] = jnp.concatenate(outs, axis=0).astype(o_ref.dtype)


def fox_decode(q, k_new, v_new, lft_new, cache_kt, cache_vt, cache_lt, page_table):
    B, lq, D = q.shape
    n_pages = page_table.shape[1]
    pages = math.gcd(n_pages, MAX_DECODE_PAGES)
    n_steps = n_pages // pages
    rows = lq * FOX_HEADS

    def page_idx(i):
        return lambda b, s, pt: (pt[b * n_pages + jnp.minimum(s, n_steps - 1) * pages + i], 0, 0, 0)

    def page_idx3(i):
        return lambda b, s, pt: (pt[b * n_pages + jnp.minimum(s, n_steps - 1) * pages + i], 0, 0)

    seq = lambda: pl.BlockSpec((1, lq, D), lambda b, s, pt: (b, 0, 0))
    in_specs = [seq(), seq(), seq(), pl.BlockSpec(lft_new.shape, lambda b, s, pt: (0, 0))]
    in_specs += [pl.BlockSpec((1, FOX_HEADS, FOX_DH, PAGE_SIZE), page_idx(i)) for i in range(pages)]
    in_specs += [pl.BlockSpec((1, FOX_HEADS, FOX_DH, PAGE_SIZE), page_idx(i)) for i in range(pages)]
    in_specs += [pl.BlockSpec((1, FOX_HEADS, PAGE_SIZE), page_idx3(i)) for i in range(pages)]
    grid_spec = pltpu.PrefetchScalarGridSpec(
        num_scalar_prefetch=1,
        grid=(B, n_steps + 1),
        in_specs=in_specs,
        out_specs=pl.BlockSpec((1, lq, D), lambda b, s, pt: (b, 0, 0)),
        scratch_shapes=[
            pltpu.VMEM((rows, D), BF16),
            pltpu.VMEM((rows, 1), F32),
            pltpu.VMEM((rows, 1), F32),
            pltpu.VMEM((rows, D), F32),
            pltpu.VMEM((FOX_HEADS, PAGE_SIZE), F32),
        ],
    )
    return pl.pallas_call(
        functools.partial(_decode_body, n_steps=n_steps, lq=lq, pages=pages),
        grid_spec=grid_spec,
        out_shape=jax.ShapeDtypeStruct((B, lq, D), F32),
        compiler_params=_cparams(2),
        name="fox_decode",
    )(page_table.reshape(-1), q, k_new, v_new, lft_new,
      *([cache_kt] * pages), *([cache_vt] * pages), *([cache_lt] * pages))


def _final_body(x_ref, g_ref, o_ref):
    xf = x_ref[0]
    o_ref[0] = xf * lax.rsqrt(jnp.mean(xf * xf, axis=-1, keepdims=True) + EPS) * g_ref[...]


def final_norm(x3, final_g, *, tm=None):
    bx, L, D = x3.shape
    tm = tm or min(L, 1024)
    return pl.pallas_call(
        _final_body,
        grid=(bx, L // tm),
        in_specs=[pl.BlockSpec((1, tm, D), lambda b, m: (b, m, 0)), pl.BlockSpec((1, D), lambda b, m: (0, 0))],
        out_specs=pl.BlockSpec((1, tm, D), lambda b, m: (b, m, 0)),
        out_shape=jax.ShapeDtypeStruct((bx, L, D), F32),
        compiler_params=_cparams(2),
        name="final_norm",
    )(x3, final_g.reshape(1, D))


def _trunk(x, mods, ret_state, pool_state, fox_past, pos0, params):
    (ret_w_in, ret_w_out, pool_w, pool_scale, fox_wt, fox_b_f, fox_w_out, ffn_w_gu, ffn_w_down,
     w_rt_pad, b_rt_pad, moe_w_gu, moe_w_down, final_g) = params
    B, L, D = x.shape
    decode = fox_past is not None
    if decode:
        x3 = x.reshape(1, B * L, D)
        expand = lambda v: jnp.repeat(v, L, axis=0)[None]
    else:
        x3 = x
        expand = lambda v: v[:, None, :]
    n_rows = x3.shape[1]
    pos = pos0 + jnp.arange(L)
    cos, sin = rotary_tables(pos, RET_HEADS)
    if decode:
        cos, sin = jnp.tile(cos, (B, 1)), jnp.tile(sin, (B, 1))
    chunk = min(L, 256)
    chunk_pad = max(chunk, PAGE_SIZE)
    tables = retention_tables(chunk, chunk_pad)
    ret_new, extras = [], {}
    for i in range(DEPTH):
        sh_a, sc_a, g_a, sh_f, sc_f, g_f = [expand(v) for v in jnp.split(mods[i], 6, axis=-1)]
        kind, j = i % N_MIXERS, i // N_MIXERS
        if kind == 0:
            q, k, v, g = ret_proj(x3, (sh_a, sc_a), ret_w_in, j, (cos, sin))
            if decode:
                padr = lambda t: jnp.pad(t.reshape(B, L, -1), ((0, 0), (0, chunk_pad - L), (0, 0)))
                o, s = retention_scan(padr(q), padr(k), padr(v), padr(g), tables, chunk_pad, s0=ret_state,
                                      s0_layer=j)
                o = o[:, :L].reshape(1, n_rows, RET_V)
            else:
                o, s = retention_scan(q, k, v, g, tables, chunk_pad)
            ret_new.append(s)
            x3 = mm(o, ret_w_out, j, n0=0, n_out=D, out_dtype=F32, epi="res", res=(x3, g_a), name="ret_out")
        elif kind == 1:
            vecs = [v[:, None, :] for v in jnp.split(mods[i], 6, axis=-1)[:3]]
            if decode:
                buf16 = jnp.pad(pool_state[j], ((0, 0), (1, 0), (0, 0)))
                tm = L
            else:
                buf16 = jnp.zeros((B, POOL_HALO, D), F32)
                tm = min(L, 512)
            xn, tail = pool_layer(x3.reshape(B, L, D), buf16, (vecs[0], vecs[1]), vecs[2], pool_w, pool_scale, j,
                                  pos0, tm=tm)
            x3 = xn.reshape(x3.shape)
            extras["pool"] = tail[:, 1:, :]
        else:
            fproj = functools.partial(mm, x3, fox_wt, j, mod=(sh_a, sc_a), w_t=True)
            if decode:
                ck, cv, cl, pt = fox_past
                q = fproj(n0=0, n_out=D, out_dtype=F32, scale=FOX_DH ** -0.5, name="fox_q")
                lft = fproj(n0=3 * D, n_out=FOX_HEADS, out_dtype=F32, out_t=True, epi="logsig",
                            bias=fox_b_f[j].reshape(FOX_HEADS, 1), name="fox_logft")
                k = fproj(n0=D, n_out=D, out_dtype=F32, name="fox_k")
                v = fproj(n0=2 * D, n_out=D, out_dtype=F32, name="fox_v")
                o = fox_decode(q.reshape(B, L, D), k.reshape(B, L, D), v.reshape(B, L, D), lft[0],
                               jnp.transpose(ck[j], (0, 2, 3, 1)), jnp.transpose(cv[j], (0, 2, 3, 1)),
                               jnp.transpose(cl[j], (0, 2, 1)), pt)
                o = o.reshape(1, n_rows, D)
                extras["k"] = k.reshape(B, L, FOX_HEADS, FOX_DH)
                extras["v"] = v.reshape(B, L, FOX_HEADS, FOX_DH)
                extras["l"] = jnp.transpose(lft[0].reshape(FOX_HEADS, B, L), (1, 2, 0))
            else:
                qt = fproj(n0=0, n_out=D, out_dtype=BF16, out_t=True, scale=FOX_DH ** -0.5, name="fox_qt")
                kb = fproj(n0=D, n_out=D, out_dtype=BF16, name="fox_kb")
                kt = fproj(n0=D, n_out=D, out_dtype=F32, out_t=True, name="fox_kt")
                vt = fproj(n0=2 * D, n_out=D, out_dtype=F32, out_t=True, name="fox_vt")
                lf = fproj(n0=3 * D, n_out=FOX_HEADS, out_dtype=F32, epi="logsig",
                           bias=fox_b_f[j].reshape(1, FOX_HEADS), name="fox_logf")
                o = fox_flash(qt, kb, fox_bias_features(lf), vt)
                unt = lambda t: jnp.transpose(t.reshape(B, FOX_HEADS, FOX_DH, L), (0, 3, 1, 2))
                extras["k"], extras["v"] = unt(kt), unt(vt)
                extras["l"] = lf
            x3 = mm(o, fox_w_out, j, n0=0, n_out=D, out_dtype=F32, epi="res", res=(x3, g_a), name="fox_out")
        ml = i // 2
        if i % 2 == 0:
            x3 = ffn(x3, (sh_f, sc_f), g_f, ffn_w_gu, ffn_w_down, ml, name="ffn_dense")
        else:
            gates = router(x3, (sh_f, sc_f), w_rt_pad, b_rt_pad, ml)
            if decode:
                x3 = ffn(x3, (sh_f, sc_f), g_f, moe_w_gu, moe_w_down, ml * N_EXPERTS, gates=gates, name="ffn_moe")
            else:
                x3 = moe_sparse(x3, (sh_f, sc_f), g_f, gates, moe_w_gu, moe_w_down, ml * N_EXPERTS)
    out = final_norm(x3, final_g).reshape(B, L, D)
    return (out, jnp.stack(ret_new), extras["pool"][None], extras["k"][None], extras["v"][None], extras["l"][None])


def kernel(x_prompt, x_sample, state_ret, state_pool, cache_fox_k, cache_fox_v, cache_fox_logf, page_table,
           c_prompt, c_sample, ada_w, ada_b, ret_w_in, ret_w_out, pool_w, pool_scale, fox_w_in, fox_b_f, fox_w_out,
           ffn_w_gu, ffn_w_down, moe_w_router, moe_b_router, moe_w_gu, moe_w_down, final_g):
    bp, bs = x_prompt.shape[0], x_sample.shape[0]
    rows = -(-(bp + bs) // 8) * 8
    c_all = jnp.concatenate([c_prompt, c_sample, jnp.zeros((rows - bp - bs, D_MODEL), F32)], axis=0)
    mods = ada_mods(c_all, ada_w, ada_b)
    n_moe = moe_w_router.shape[0]
    params = (
        ret_w_in, ret_w_out, pool_w, pool_scale,
        jnp.swapaxes(fox_w_in, 1, 2),
        fox_b_f, fox_w_out, ffn_w_gu, ffn_w_down,
        jnp.pad(jnp.swapaxes(moe_w_router, 1, 2), ((0, 0), (0, LANES - N_EXPERTS), (0, 0))),
        jnp.pad(moe_b_router, ((0, 0), (0, LANES - N_EXPERTS))).reshape(n_moe, 1, LANES),
        moe_w_gu.reshape((n_moe * N_EXPERTS,) + moe_w_gu.shape[2:]),
        moe_w_down.reshape((n_moe * N_EXPERTS,) + moe_w_down.shape[2:]),
        final_g,
    )
    y_p, ret_p, pool_p, k_p, v_p, l_p = _trunk(x_prompt, mods[:, :bp], None, None, None, 0, params)
    n_past = page_table.shape[1] * PAGE_SIZE
    y_s, ret_s, pool_s, k_s, v_s, l_s = _trunk(
        x_sample, mods[:, bp:bp + bs], state_ret, state_pool,
        (cache_fox_k, cache_fox_v, cache_fox_logf, page_table), n_past, params)
    return (y_p, y_s, ret_p, ret_s, pool_p, pool_s, k_p, k_s, v_p, v_s, l_p, l_s)
```

```python
import functools
import math

import jax
import jax.numpy as jnp
from jax import lax
from jax.experimental import pallas as pl
from jax.experimental.pallas import tpu as pltpu

F32 = jnp.float32
BF16 = jnp.bfloat16

D_MODEL = 1024
DEPTH = 4
PAGE_SIZE = 128
N_MIXERS = 3
RET_HEADS = 4
RET_DK = D_MODEL // RET_HEADS
RET_DV = 2 * D_MODEL // RET_HEADS
RET_QK = RET_HEADS * RET_DK
RET_V = RET_HEADS * RET_DV
ROPE_BASE = 10000.0
POOL_WINDOWS = (2, 4, 8, 16)
POOL_GW = D_MODEL // len(POOL_WINDOWS)
POOL_BUF = max(POOL_WINDOWS) - 1
POOL_HALO = POOL_BUF + 1
FOX_HEADS = 16
FOX_DH = D_MODEL // FOX_HEADS
FFN_DIM = 2816
N_EXPERTS = 8
EPS = 1e-6
NEG_BIG = -1e30

V7X_VMEM_BYTES = 64 * 1024 * 1024
VMEM_LIMIT = V7X_VMEM_BYTES - 8 * 1024 * 1024
LANES = 128
FFN_TF = 256


def _cparams(n_axes):
    return pltpu.CompilerParams(dimension_semantics=("arbitrary",) * n_axes, vmem_limit_bytes=VMEM_LIMIT)


def _sigmoid(x):
    return 1.0 / (1.0 + jnp.exp(-x))


def _modulate(x, shift, scale):
    xf = x.astype(F32)
    ms = jnp.mean(xf * xf, axis=-1, keepdims=True)
    return (xf * lax.rsqrt(ms + EPS)) * (1.0 + scale) + shift


def _row_spec(arr, tm, width, col_fn):
    if arr.shape[1] == 1:
        return pl.BlockSpec((1, 1, width), lambda b, m, *r: (b, 0, col_fn(*r)))
    return pl.BlockSpec((1, tm, width), lambda b, m, *r: (b, m, col_fn(*r)))


def _ada_body(c_ref, w_ref, b_ref, o_ref):
    c = c_ref[...]
    cond = c * _sigmoid(c)
    o_ref[0] = jnp.dot(cond.astype(BF16), w_ref[0].astype(BF16), preferred_element_type=F32) + b_ref[0]


def ada_mods(c_all, ada_w, ada_b):
    rows = c_all.shape[0]
    n_out = ada_w.shape[2]
    tn = 1024
    return pl.pallas_call(
        _ada_body,
        grid=(DEPTH, n_out // tn),
        in_specs=[
            pl.BlockSpec((rows, D_MODEL), lambda i, n: (0, 0)),
            pl.BlockSpec((1, D_MODEL, tn), lambda i, n: (i, 0, n)),
            pl.BlockSpec((1, 1, tn), lambda i, n: (i, 0, n)),
        ],
        out_specs=pl.BlockSpec((1, rows, tn), lambda i, n: (i, 0, n)),
        out_shape=jax.ShapeDtypeStruct((DEPTH, rows, n_out), F32),
        compiler_params=_cparams(2),
        name="ada_mods",
    )(c_all, ada_w, ada_b.reshape(DEPTH, 1, n_out))


def _mm_body(*refs, has_mod, epi, w_t, out_t, scale):
    it = iter(refs)
    x_ref = next(it)
    if has_mod:
        sh_ref, sc_ref = next(it), next(it)
    w_ref = next(it)
    if epi == "rot":
        cos_ref, sin_ref = next(it), next(it)
    elif epi == "res":
        res_ref, gate_ref = next(it), next(it)
    elif epi == "logsig":
        b_ref = next(it)
    o_ref = next(it)
    n = pl.program_id(2)
    if has_mod:
        h_scr = next(it)

        @pl.when(n == 0)
        def _():
            h_scr[...] = _modulate(x_ref[0], sh_ref[0], sc_ref[0]).astype(BF16)

        lhs = h_scr[...]
    else:
        lhs = x_ref[0].astype(BF16)
    w = w_ref[0].astype(BF16)
    last = (((1,), (1,)), ((), ()))
    if not w_t:
        acc = jnp.dot(lhs, w, preferred_element_type=F32)
    elif not out_t:
        acc = lax.dot_general(lhs, w, last, preferred_element_type=F32)
    else:
        acc = lax.dot_general(w, lhs, last, preferred_element_type=F32)
    if epi == "rot":
        acc = _rotary(acc, cos_ref[...], sin_ref[...])
    elif epi == "res":
        acc = res_ref[0] + gate_ref[0] * acc
    elif epi == "logsig":
        z = acc + b_ref[...]
        acc = jnp.minimum(z, 0.0) - jnp.log1p(jnp.exp(-jnp.abs(z)))
    if scale != 1.0:
        acc = acc * scale
    o_ref[0] = acc.astype(o_ref.dtype)


def mm(x3, w3, wl, *, n0, n_out, out_dtype, mod=None, epi="plain", w_t=False, out_t=False, scale=1.0,
       rot=None, res=None, bias=None, tm=None, tn=None, name="mm"):
    bx, L, K = x3.shape
    tm = tm or min(L, 512 if epi == "rot" else 1024)
    tn = tn or min(n_out, 512 if K > 1024 else 1024)
    assert L % tm == 0 and n_out % tn == 0 and n0 % tn == 0
    nb0 = n0 // tn
    has_mod = mod is not None
    in_specs = [pl.BlockSpec((1, tm, K), lambda b, m, n: (b, m, 0))]
    args = [x3]
    if has_mod:
        for a in mod:
            in_specs.append(_row_spec(a, tm, K, lambda n: 0))
            args.append(a)
    if w_t:
        in_specs.append(pl.BlockSpec((1, tn, K), lambda b, m, n: (wl, nb0 + n, 0)))
    else:
        in_specs.append(pl.BlockSpec((1, K, tn), lambda b, m, n: (wl, 0, nb0 + n)))
    args.append(w3)
    if epi == "rot":
        for a in rot:
            in_specs.append(pl.BlockSpec((tm, tn), lambda b, m, n: (m, n)))
            args.append(a)
    elif epi == "res":
        in_specs.append(pl.BlockSpec((1, tm, tn), lambda b, m, n: (b, m, n)))
        in_specs.append(_row_spec(res[1], tm, tn, lambda n: n))
        args.extend(res)
    elif epi == "logsig":
        in_specs.append(pl.BlockSpec(bias.shape, lambda b, m, n: (0, 0)))
        args.append(bias)
    if out_t:
        out_spec = pl.BlockSpec((1, tn, tm), lambda b, m, n: (b, n, m))
        out_shape = jax.ShapeDtypeStruct((bx, n_out, L), out_dtype)
    else:
        out_spec = pl.BlockSpec((1, tm, tn), lambda b, m, n: (b, m, n))
        out_shape = jax.ShapeDtypeStruct((bx, L, n_out), out_dtype)
    return pl.pallas_call(
        functools.partial(_mm_body, has_mod=has_mod, epi=epi, w_t=w_t, out_t=out_t, scale=scale),
        grid=(bx, L // tm, n_out // tn),
        in_specs=in_specs,
        out_specs=out_spec,
        out_shape=out_shape,
        scratch_shapes=[pltpu.VMEM((tm, K), BF16)] if has_mod else [],
        compiler_params=_cparams(3),
        name=name,
    )(*args)


def _rotary(acc, cos, sin_signed):
    width = acc.shape[1]
    lane = lax.broadcasted_iota(jnp.int32, acc.shape, 1)
    partner = jnp.where(lane % 2 == 0, pltpu.roll(acc, width - 1, axis=1), pltpu.roll(acc, 1, axis=1))
    return acc * cos + partner * sin_signed


def _ffn_body(*refs, moe, nf, ne):
    it = iter(refs)
    x_ref, sh_ref, sc_ref, gate_ref = next(it), next(it), next(it), next(it)
    gw_ref = next(it) if moe else None
    wg_ref, wu_ref, wd_ref, o_ref, h_scr, acc_scr = next(it), next(it), next(it), next(it), next(it), next(it)
    tot_scr = next(it) if moe else None
    e = pl.program_id(2)
    f = pl.program_id(3)

    @pl.when((e == 0) & (f == 0))
    def _():
        h_scr[...] = _modulate(x_ref[0], sh_ref[0], sc_ref[0]).astype(BF16)

    h = h_scr[...]
    a = jnp.dot(h, wg_ref[0].astype(BF16), preferred_element_type=F32)
    b = jnp.dot(h, wu_ref[0].astype(BF16), preferred_element_type=F32)
    mid = (a * _sigmoid(a) * b).astype(BF16)
    y = jnp.dot(mid, wd_ref[0].astype(BF16), preferred_element_type=F32)

    @pl.when(f == 0)
    def _():
        acc_scr[...] = y

    @pl.when(f > 0)
    def _():
        acc_scr[...] += y

    if not moe:

        @pl.when(f == nf - 1)
        def _():
            o_ref[0] = x_ref[0] + gate_ref[0] * acc_scr[...]

    else:

        @pl.when(f == nf - 1)
        def _():
            gw = gw_ref[0]
            lane = lax.broadcasted_iota(jnp.int32, gw.shape, 1)
            col = jnp.sum(jnp.where(lane == e, gw, 0.0), axis=1, keepdims=True)
            contrib = col * acc_scr[...]

            @pl.when(e == 0)
            def _():
                tot_scr[...] = contrib

            @pl.when(e > 0)
            def _():
                tot_scr[...] += contrib

            @pl.when(e == ne - 1)
            def _():
                o_ref[0] = x_ref[0] + gate_ref[0] * tot_scr[...]


def ffn(x3, mod, gate, w_gu3, w_down3, wl, *, gates=None, tm=None, name="ffn"):
    bx, L, D = x3.shape
    moe = gates is not None
    ne = N_EXPERTS if moe else 1
    tm = tm or min(L, 1024)
    tf = FFN_TF
    nf = FFN_DIM // tf
    assert L % tm == 0 and FFN_DIM % tf == 0
    in_specs = [pl.BlockSpec((1, tm, D), lambda b, m, e, f: (b, m, 0))]
    args = [x3]
    for a in (*mod, gate):
        in_specs.append(_row_spec(a, tm, D, lambda e, f: 0))
        args.append(a)
    if moe:
        in_specs.append(pl.BlockSpec((1, tm, LANES), lambda b, m, e, f: (b, m, 0)))
        args.append(gates)
    in_specs += [
        pl.BlockSpec((1, D, tf), lambda b, m, e, f: (wl + e, 0, f)),
        pl.BlockSpec((1, D, tf), lambda b, m, e, f: (wl + e, 0, nf + f)),
        pl.BlockSpec((1, tf, D), lambda b, m, e, f: (wl + e, f, 0)),
    ]
    args += [w_gu3, w_gu3, w_down3]
    scratch = [pltpu.VMEM((tm, D), BF16), pltpu.VMEM((tm, D), F32)]
    if moe:
        scratch.append(pltpu.VMEM((tm, D), F32))
    return pl.pallas_call(
        functools.partial(_ffn_body, moe=moe, nf=nf, ne=ne),
        grid=(bx, L // tm, ne, nf),
        in_specs=in_specs,
        out_specs=pl.BlockSpec((1, tm, D), lambda b, m, e, f: (b, m, 0)),
        out_shape=jax.ShapeDtypeStruct((bx, L, D), F32),
        scratch_shapes=scratch,
        compiler_params=_cparams(4),
        name=name,
    )(*args)


def _router_body(x_ref, sh_ref, sc_ref, w_ref, b_ref, o_ref):
    h = _modulate(x_ref[0], sh_ref[0], sc_ref[0]).astype(BF16)
    logits = lax.dot_general(h, w_ref[0].astype(BF16), (((1,), (1,)), ((), ())), preferred_element_type=F32)
    logits = logits + b_ref[0]
    lane = lax.broadcasted_iota(jnp.int32, logits.shape, 1).astype(F32)
    lg = jnp.where(lane < N_EXPERTS, logits, -jnp.inf)
    m1 = jnp.max(lg, axis=1, keepdims=True)
    i1 = jnp.min(jnp.where(lg == m1, lane, float(LANES)), axis=1, keepdims=True)
    lg2 = jnp.where(lane == i1, -jnp.inf, lg)
    m2 = jnp.max(lg2, axis=1, keepdims=True)
    i2 = jnp.min(jnp.where(lg2 == m2, lane, float(LANES)), axis=1, keepdims=True)
    e2 = jnp.exp(m2 - m1)
    den = 1.0 + e2
    o_ref[0] = jnp.where(lane == i1, 1.0 / den, 0.0) + jnp.where(lane == i2, e2 / den, 0.0)


def router(x3, mod, w_rt_pad, b_pad, wl, *, tm=None):
    bx, L, D = x3.shape
    tm = tm or min(L, 1024)
    in_specs = [pl.BlockSpec((1, tm, D), lambda b, m: (b, m, 0))]
    args = [x3]
    for a in mod:
        in_specs.append(_row_spec(a, tm, D, lambda: 0))
        args.append(a)
    in_specs += [
        pl.BlockSpec((1, LANES, D), lambda b, m: (wl, 0, 0)),
        pl.BlockSpec((1, 1, LANES), lambda b, m: (wl, 0, 0)),
    ]
    args += [w_rt_pad, b_pad]
    return pl.pallas_call(
        _router_body,
        grid=(bx, L // tm),
        in_specs=in_specs,
        out_specs=pl.BlockSpec((1, tm, LANES), lambda b, m: (b, m, 0)),
        out_shape=jax.ShapeDtypeStruct((bx, L, LANES), F32),
        compiler_params=_cparams(2),
        name="router",
    )(*args)


MOE_T = 256
MOE_CH = 128
MOE_TM = 256
MOE_TF = FFN_DIM // 2
MOE_ALIGN = 8
MOE_STAGE = -(-(2 * MOE_T + N_EXPERTS * (MOE_ALIGN - 1) + MOE_CH) // 8) * 8


def _moe_plan(gates2, n_tokens):
    nb = n_tokens // MOE_T
    routed = (gates2[:, :N_EXPERTS] > 0).reshape(nb, MOE_T, N_EXPERTS)
    ri = routed.astype(jnp.int32)
    rank = jnp.cumsum(ri, axis=1) - ri
    cnt = jnp.sum(ri, axis=1)
    cnt_al = -(-cnt // MOE_ALIGN) * MOE_ALIGN
    lo = jnp.cumsum(cnt_al, axis=1) - cnt_al
    total = jnp.sum(cnt_al, axis=0)
    region = -(-(total + MOE_CH) // MOE_TM) * MOE_TM
    ends = jnp.cumsum(region)
    off = ends - region
    pos = off[None, :] + jnp.cumsum(cnt_al, axis=0) - cnt_al
    nch = -(-cnt_al // MOE_CH)
    m_pad = -(-(2 * n_tokens + nb * N_EXPERTS * (MOE_ALIGN - 1) + N_EXPERTS * (MOE_CH + MOE_TM)) // MOE_TM) * MOE_TM
    n_tiles = m_pad // MOE_TM
    tile_start = jnp.arange(n_tiles, dtype=jnp.int32) * MOE_TM
    tile_e = jnp.minimum(jnp.sum(tile_start[:, None] >= ends[None, :], axis=1), N_EXPERTS - 1).astype(jnp.int32)
    n_used = (ends[-1] // MOE_TM).astype(jnp.int32).reshape(1)
    dest = jnp.where(routed, lo[:, None, :] + rank, -1)
    d_hi = jnp.max(dest, axis=2)
    d_lo = jnp.min(jnp.where(routed, dest, MOE_STAGE), axis=2)
    dd = jnp.stack([d_hi, d_lo], axis=1).astype(jnp.int32)
    eidx = jnp.arange(N_EXPERTS, dtype=jnp.int32)
    e_a = jnp.min(jnp.where(routed, eidx, N_EXPERTS), axis=2)
    e_b = jnp.max(jnp.where(routed, eidx, -1), axis=2)
    g3 = gates2[:, :N_EXPERTS].reshape(nb, MOE_T, N_EXPERTS)
    at = lambda a, e: jnp.sum(jnp.where(eidx == e[..., None], a, 0), axis=2)
    r_a, r_b = at(rank, e_a), at(rank, e_b)
    w_a, w_b = at(g3, e_a), jnp.where(e_b != e_a, at(g3, e_b), 0.0)
    col = lambda e, r, c: jnp.where(r // MOE_CH == c, e * MOE_CH + r % MOE_CH, -1).astype(F32)
    cmeta = jnp.stack([col(e_a, r_a, 0), col(e_b, r_b, 0), col(e_a, r_a, 1), col(e_b, r_b, 1), w_a, w_b], axis=-1)
    cmeta = jnp.pad(cmeta.reshape(n_tokens, 6), ((0, 0), (0, LANES - 6)))
    two = (jnp.max(nch, axis=1) > 1).astype(jnp.int32)
    flat = lambda a: a.reshape(-1).astype(jnp.int32)
    return dict(lo=flat(lo // MOE_ALIGN), pos=flat(pos // MOE_ALIGN), nch=flat(nch), tile_e=tile_e, n_used=n_used,
                dd=dd, cmeta=cmeta, two=two, m_pad=m_pad, n_tiles=n_tiles, nb=nb)


def _dispatch_body(lo_ref, pos_ref, nch_ref, x_ref, sh_ref, sc_ref, dd_ref, xs_in_ref, xs_ref, stage_scr, sem,
                   *, nb):
    del xs_in_ref
    b = pl.program_id(0)
    slot = b % 2
    h = _modulate(x_ref[...], sh_ref[0], sc_ref[0]).astype(BF16)
    r = lax.broadcasted_iota(jnp.int32, (MOE_STAGE, MOE_T), 0)
    dd = dd_ref[0]
    onehot = jnp.where((r == dd[0:1, :]) | (r == dd[1:2, :]), 1.0, 0.0).astype(BF16)
    stage_scr[slot] = jnp.dot(onehot, h, preferred_element_type=F32)

    def seg_copy(blk, e, c):
        src0 = pl.multiple_of(lo_ref[blk * N_EXPERTS + e] * MOE_ALIGN + c * MOE_CH, MOE_ALIGN)
        dst0 = pl.multiple_of(pos_ref[blk * N_EXPERTS + e] * MOE_ALIGN + c * MOE_CH, MOE_ALIGN)
        return pltpu.make_async_copy(stage_scr.at[blk % 2, pl.ds(src0, MOE_CH)], xs_ref.at[pl.ds(dst0, MOE_CH)],
                                     sem.at[blk % 2, e, c])

    def for_segments(blk, fn):
        for e in range(N_EXPERTS):
            for c in range(2):
                @pl.when(c < nch_ref[blk * N_EXPERTS + e])
                def _():
                    fn(seg_copy(blk, e, c))

    @pl.when(b > 0)
    def _():
        for_segments(b - 1, lambda cp: cp.wait())

    for_segments(b, lambda cp: cp.start())

    @pl.when(b == nb - 1)
    def _():
        for_segments(b, lambda cp: cp.wait())


def moe_dispatch(x2, mod, plan, seq_len):
    n_tokens, D = x2.shape
    per_seq = seq_len // MOE_T
    nb = plan["nb"]
    vec = lambda: pl.BlockSpec((1, 1, D), lambda b, *_: (b // per_seq, 0, 0))
    grid_spec = pltpu.PrefetchScalarGridSpec(
        num_scalar_prefetch=3,
        grid=(nb,),
        in_specs=[
            pl.BlockSpec((MOE_T, D), lambda b, *_: (b, 0)),
            vec(), vec(),
            pl.BlockSpec((1, 2, MOE_T), lambda b, *_: (b, 0, 0)),
            pl.BlockSpec(memory_space=pltpu.MemorySpace.HBM),
        ],
        out_specs=pl.BlockSpec(memory_space=pltpu.MemorySpace.HBM),
        scratch_shapes=[pltpu.VMEM((2, MOE_STAGE, D), F32), pltpu.SemaphoreType.DMA((2, N_EXPERTS, 2))],
    )
    return pl.pallas_call(
        functools.partial(_dispatch_body, nb=nb),
        grid_spec=grid_spec,
        out_shape=jax.ShapeDtypeStruct((plan["m_pad"], D), F32),
        input_output_aliases={7: 0},
        compiler_params=_cparams(1),
        name="moe_dispatch",
    )(plan["lo"], plan["pos"], plan["nch"], x2, mod[0], mod[1], plan["dd"], jnp.zeros((plan["m_pad"], D), F32))


def _gffn_body(te_ref, nu_ref, *refs, has_prev):
    it = iter(refs)
    xs_ref = next(it)
    yp_ref = next(it) if has_prev else None
    wg_ref, wu_ref, wd_ref, o_ref, wg_scr, wu_scr, wd_scr = (next(it) for _ in range(7))
    t = pl.program_id(0)
    e_here = te_ref[t]
    e_prev = te_ref[jnp.maximum(t - 1, 0)]

    @pl.when(t < nu_ref[0])
    def _():
        @pl.when((t == 0) | (e_here != e_prev))
        def _():
            wg_scr[...] = wg_ref[0].astype(BF16)
            wu_scr[...] = wu_ref[0].astype(BF16)
            wd_scr[...] = wd_ref[0].astype(BF16)

        h = xs_ref[...].astype(BF16)
        a = jnp.dot(h, wg_scr[...], preferred_element_type=F32)
        b = jnp.dot(h, wu_scr[...], preferred_element_type=F32)
        mid = (a * _sigmoid(a) * b).astype(BF16)
        y = jnp.dot(mid, wd_scr[...], preferred_element_type=F32)
        o_ref[...] = (yp_ref[...] + y) if has_prev else y

    @pl.when(t >= nu_ref[0])
    def _():
        o_ref[...] = jnp.zeros_like(o_ref)


def moe_grouped_ffn(xs, w_gu3, w_down3, wl, plan, f, y_prev=None):
    m_pad, D = xs.shape
    nf = FFN_DIM // MOE_TF
    has_prev = y_prev is not None
    row = lambda t, te, nu: (jnp.minimum(t, nu[0] - 1), 0)
    tile = lambda: pl.BlockSpec((MOE_TM, D), row)
    in_specs = [tile()] + ([tile()] if has_prev else [])
    in_specs += [
        pl.BlockSpec((1, D, MOE_TF), lambda t, te, nu: (wl + te[t], 0, f)),
        pl.BlockSpec((1, D, MOE_TF), lambda t, te, nu: (wl + te[t], 0, nf + f)),
        pl.BlockSpec((1, MOE_TF, D), lambda t, te, nu: (wl + te[t], f, 0), pipeline_mode=pl.Buffered(1)),
    ]
    grid_spec = pltpu.PrefetchScalarGridSpec(
        num_scalar_prefetch=2,
        grid=(plan["n_tiles"],),
        in_specs=in_specs,
        out_specs=pl.BlockSpec((MOE_TM, D), lambda t, te, nu: (t, 0)),
        scratch_shapes=[pltpu.VMEM((D, MOE_TF), BF16), pltpu.VMEM((D, MOE_TF), BF16), pltpu.VMEM((MOE_TF, D), BF16)],
    )
    args = [xs] + ([y_prev] if has_prev else []) + [w_gu3, w_gu3, w_down3]
    return pl.pallas_call(
        functools.partial(_gffn_body, has_prev=has_prev),
        grid_spec=grid_spec,
        out_shape=jax.ShapeDtypeStruct((m_pad, D), F32),
        compiler_params=_cparams(1),
        name=f"moe_grouped_ffn{f}",
    )(plan["tile_e"], plan["n_used"], *args)


def _combine_body(pos_ref, nch_ref, two_ref, x_ref, gate_ref, cm_ref, *refs):
    y_refs = refs[:N_EXPERTS]
    y_hbm, o_ref, yhi_scr, ylo_scr, over_scr, sem = refs[N_EXPERTS:]
    b = pl.program_id(0)

    def gathered(window, chunk):
        for e in range(N_EXPERTS):
            y = window(e)
            y_hi = y.astype(BF16)
            yhi_scr[e * MOE_CH:(e + 1) * MOE_CH, :] = y_hi
            ylo_scr[e * MOE_CH:(e + 1) * MOE_CH, :] = (y - y_hi.astype(F32)).astype(BF16)
        cm = cm_ref[...]
        lane = lax.broadcasted_iota(jnp.int32, (MOE_T, N_EXPERTS * MOE_CH), 1).astype(F32)
        picks = [jnp.where(lane == cm[:, 2 * chunk + k:2 * chunk + k + 1], 1.0, 0.0).astype(BF16) for k in range(2)]
        onehot = jnp.concatenate(picks, axis=0)
        rows = (jnp.dot(onehot, yhi_scr[...], preferred_element_type=F32)
                + jnp.dot(onehot, ylo_scr[...], preferred_element_type=F32))
        return cm[:, 4:5] * rows[:MOE_T] + cm[:, 5:6] * rows[MOE_T:]

    o_ref[...] = x_ref[...] + gate_ref[0] * gathered(lambda e: y_refs[e][...], 0)

    @pl.when(two_ref[b] > 0)
    def _():
        for e in range(N_EXPERTS):
            s = b * N_EXPERTS + e

            @pl.when(nch_ref[s] > 1)
            def _():
                start = pl.multiple_of(pos_ref[s] * MOE_ALIGN + MOE_CH, MOE_ALIGN)
                cp = pltpu.make_async_copy(y_hbm.at[pl.ds(start, MOE_CH)], over_scr.at[e], sem.at[e])
                cp.start()
                cp.wait()

            @pl.when(nch_ref[s] <= 1)
            def _():
                over_scr[e] = jnp.zeros((MOE_CH, over_scr.shape[2]), F32)

        o_ref[...] += gate_ref[0] * gathered(lambda e: over_scr[e], 1)


def moe_combine(x2, gate, y, plan, seq_len):
    n_tokens, D = x2.shape
    per_seq = seq_len // MOE_T

    def window(e):
        return pl.BlockSpec((pl.Element(MOE_CH), pl.Element(D)),
                            lambda b, pos, nch, two: (pos[b * N_EXPERTS + e] * MOE_ALIGN, 0))

    blk = lambda w: pl.BlockSpec((MOE_T, w), lambda b, *_: (b, 0))
    grid_spec = pltpu.PrefetchScalarGridSpec(
        num_scalar_prefetch=3,
        grid=(plan["nb"],),
        in_specs=[blk(D), pl.BlockSpec((1, 1, D), lambda b, *_: (b // per_seq, 0, 0)), blk(LANES)]
        + [window(e) for e in range(N_EXPERTS)] + [pl.BlockSpec(memory_space=pltpu.MemorySpace.HBM)],
        out_specs=blk(D),
        scratch_shapes=[pltpu.VMEM((N_EXPERTS * MOE_CH, D), BF16), pltpu.VMEM((N_EXPERTS * MOE_CH, D), BF16),
                        pltpu.VMEM((N_EXPERTS, MOE_CH, D), F32), pltpu.SemaphoreType.DMA((N_EXPERTS,))],
    )
    return pl.pallas_call(
        _combine_body,
        grid_spec=grid_spec,
        out_shape=jax.ShapeDtypeStruct((n_tokens, D), F32),
        compiler_params=_cparams(1),
        name="moe_combine",
    )(plan["pos"], plan["nch"], plan["two"], x2, gate, plan["cmeta"], *([y] * (N_EXPERTS + 1)))


def moe_sparse(x3, mod, gate, gates, w_gu3, w_down3, wl):
    B, L, D = x3.shape
    n_tokens = B * L
    assert L % MOE_T == 0 and FFN_DIM % MOE_TF == 0
    x2 = x3.reshape(n_tokens, D)
    gates2 = gates.reshape(n_tokens, LANES)
    plan = _moe_plan(gates2, n_tokens)
    xs = moe_dispatch(x2, mod, plan, L)
    y = None
    for f in range(FFN_DIM // MOE_TF):
        y = moe_grouped_ffn(xs, w_gu3, w_down3, wl, plan, f, y_prev=y)
    return moe_combine(x2, gate, y, plan, L).reshape(B, L, D)


RET_CHUNKS_PER_STEP = 4


def _ret_body(*refs, zero_init, nc, chunk, per_step, heads):
    it = iter(refs)
    q_ref, k_ref, v_ref, g_ref = next(it), next(it), next(it), next(it)
    s0_ref = None if zero_init else next(it)
    inner_ref, qd_ref, kd_ref, cd_ref = next(it), next(it), next(it), next(it)
    o_ref, sout_ref, s_scr = next(it), next(it), next(it)
    c = pl.program_id(2)

    @pl.when(c == 0)
    def _():
        if zero_init:
            s_scr[...] = jnp.zeros_like(s_scr)
        else:
            s_scr[...] = s0_ref[0, 0]

    for hi in range(heads):
        kcols = slice(hi * RET_DK, (hi + 1) * RET_DK)
        vcols = slice(hi * RET_DV, (hi + 1) * RET_DV)
        s = s_scr[hi]
        for ci in range(per_step):
            rows = slice(ci * chunk, (ci + 1) * chunk)
            q = q_ref[0, rows, kcols]
            k = k_ref[0, rows, kcols]
            v = v_ref[0, rows, vcols]
            att = lax.dot_general(q, k, (((1,), (1,)), ((), ())), preferred_element_type=F32) * inner_ref[hi]
            inner = jnp.dot(att.astype(BF16), v, preferred_element_type=F32)
            cross = jnp.dot(q, s.astype(BF16), preferred_element_type=F32) * qd_ref[hi]
            kdt = (k.astype(F32) * kd_ref[hi]).T.astype(BF16)
            s = s * cd_ref[hi] + jnp.dot(kdt, v, preferred_element_type=F32)
            o = inner + cross
            on = o * lax.rsqrt(jnp.mean(o * o, axis=-1, keepdims=True) + EPS)
            g = g_ref[0, rows, vcols]
            o_ref[0, rows, vcols] = (g * _sigmoid(g) * on).astype(o_ref.dtype)
        s_scr[hi] = s

    @pl.when(c == nc - 1)
    def _():
        sout_ref[0] = s_scr[...]


def retention_scan(q, k, v, g, tables, chunk, *, s0=None, s0_layer=0, heads=1):
    B, L, _ = q.shape
    per_step = math.gcd(L // chunk, RET_CHUNKS_PER_STEP)
    rows = per_step * chunk
    nc = L // rows
    inner, qd, kd, cd = tables
    zero_init = s0 is None
    in_specs = [
        pl.BlockSpec((1, rows, heads * RET_DK), lambda h, b, c: (b, c, h)),
        pl.BlockSpec((1, rows, heads * RET_DK), lambda h, b, c: (b, c, h)),
        pl.BlockSpec((1, rows, heads * RET_DV), lambda h, b, c: (b, c, h)),
        pl.BlockSpec((1, rows, heads * RET_DV), lambda h, b, c: (b, c, h)),
    ]
    args = [q, k, v, g]
    if not zero_init:
        in_specs.append(pl.BlockSpec((1, 1, heads, RET_DK, RET_DV), lambda h, b, c: (s0_layer, b, h, 0, 0)))
        args.append(s0)
    in_specs += [
        pl.BlockSpec((heads, chunk, chunk), lambda h, b, c: (h, 0, 0)),
        pl.BlockSpec((heads, chunk, RET_DV), lambda h, b, c: (h, 0, 0)),
        pl.BlockSpec((heads, chunk, RET_DK), lambda h, b, c: (h, 0, 0)),
        pl.BlockSpec((heads, 1, RET_DV), lambda h, b, c: (h, 0, 0)),
    ]
    args += [inner, qd, kd, cd]
    return pl.pallas_call(
        functools.partial(_ret_body, zero_init=zero_init, nc=nc, chunk=chunk, per_step=per_step, heads=heads),
        grid=(RET_HEADS // heads, B, nc),
        in_specs=in_specs,
        out_specs=[
            pl.BlockSpec((1, rows, heads * RET_DV), lambda h, b, c: (b, c, h)),
            pl.BlockSpec((1, heads, RET_DK, RET_DV), lambda h, b, c: (b, h, 0, 0)),
        ],
        out_shape=[
            jax.ShapeDtypeStruct((B, L, RET_V), BF16),
            jax.ShapeDtypeStruct((B, RET_HEADS, RET_DK, RET_DV), F32),
        ],
        scratch_shapes=[pltpu.VMEM((heads, RET_DK, RET_DV), F32)],
        compiler_params=_cparams(3),
        name="retention_scan",
    )(*args)


def retention_tables(n_real, n_pad):
    log_gamma = jnp.log1p(-(2.0 ** (-5.0 - jnp.arange(RET_HEADS, dtype=F32))))
    idx = jnp.arange(n_pad, dtype=F32)
    valid = idx < n_real
    diff = idx[:, None] - idx[None, :]
    ok = (diff >= 0) & valid[:, None] & valid[None, :]
    inner = jnp.where(ok[None], jnp.exp(log_gamma[:, None, None] * jnp.maximum(diff, 0.0)[None]), 0.0)
    qd = jnp.exp(log_gamma[:, None] * (idx[None, :] + 1.0))
    kd = jnp.where(valid[None, :], jnp.exp(log_gamma[:, None] * (n_real - 1.0 - idx[None, :])), 0.0)
    cd = jnp.exp(log_gamma * n_real)
    return (inner,
            jnp.broadcast_to(qd[:, :, None], (RET_HEADS, n_pad, RET_DV)),
            jnp.broadcast_to(kd[:, :, None], (RET_HEADS, n_pad, RET_DK)),
            jnp.broadcast_to(cd[:, None, None], (RET_HEADS, 1, RET_DV)))


def rotary_tables(pos, reps):
    inv_freq = ROPE_BASE ** (-jnp.arange(0, RET_DK, 2, dtype=F32) / RET_DK)
    ang = pos.astype(F32)[:, None] * inv_freq[None, :]
    cos = jnp.repeat(jnp.cos(ang), 2, axis=1)
    sin = jnp.sin(ang)
    sin_signed = jnp.stack([-sin, sin], axis=-1).reshape(ang.shape[0], RET_DK)
    return jnp.tile(cos, (1, reps)), jnp.tile(sin_signed, (1, reps))


def _pool_body(x_ref, xp_ref, buf_ref, sh_ref, sc_ref, gate_ref, w_ref, cs_ref, o_ref, tail_ref, ext_scr,
               *, tm, pos0, has_prev):
    m = pl.program_id(1)
    sh, sc = sh_ref[0], sc_ref[0]
    h = _modulate(x_ref[0], sh, sc)

    @pl.when(m == 0)
    def _():
        ext_scr[0:POOL_HALO, :] = buf_ref[0]

    if has_prev:

        @pl.when(m > 0)
        def _():
            ext_scr[0:POOL_HALO, :] = _modulate(xp_ref[0], sh, sc)

    ext_scr[POOL_HALO:POOL_HALO + tm, :] = h
    tail_ref[0] = ext_scr[tm:tm + POOL_HALO, :]
    row = lax.broadcasted_iota(jnp.int32, (tm, 1), 0)
    pos1 = (pos0 + m * tm + row + 1).astype(F32)
    rows = max(tm, 16)
    ys = []
    for gi, w in enumerate(POOL_WINDOWS):
        c0, c1 = gi * POOL_GW, (gi + 1) * POOL_GW
        win = ext_scr[POOL_HALO:POOL_HALO + tm, c0:c1]
        for j in range(1, w):
            win = win + ext_scr[POOL_HALO - j:POOL_HALO - j + tm, c0:c1]
        d = win / jnp.minimum(jnp.float32(w), pos1) - h[:, c0:c1]
        if rows != tm:
            d = jnp.concatenate([d, jnp.zeros((rows - tm, POOL_GW), F32)], axis=0)
        y = jnp.dot(d.astype(BF16), w_ref[0, gi].astype(BF16), preferred_element_type=F32)
        ys.append(y[0:tm])
    y = jnp.concatenate(ys, axis=1) * cs_ref[...]
    o_ref[0] = x_ref[0] + gate_ref[0] * y


def pool_layer(x3, buf16, mod, gate, pool_w, pool_scale, wl, pos0, *, tm):
    B, L, D = x3.shape
    assert L % tm == 0 and (L == tm or tm % POOL_HALO == 0)
    has_prev = L > tm
    ph = POOL_HALO if has_prev else min(L, POOL_HALO)
    per = tm // POOL_HALO if has_prev else 1
    vec = lambda: pl.BlockSpec((1, 1, D), lambda b, m: (b, 0, 0))
    return pl.pallas_call(
        functools.partial(_pool_body, tm=tm, pos0=pos0, has_prev=has_prev),
        grid=(B, L // tm),
        in_specs=[
            pl.BlockSpec((1, tm, D), lambda b, m: (b, m, 0)),
            pl.BlockSpec((1, ph, D), lambda b, m: (b, jnp.maximum(m * per - 1, 0), 0)),
            pl.BlockSpec((1, POOL_HALO, D), lambda b, m: (b, 0, 0)),
            vec(), vec(), vec(),
            pl.BlockSpec((1,) + pool_w.shape[1:], lambda b, m: (wl, 0, 0, 0)),
            pl.BlockSpec((1, D), lambda b, m: (wl, 0)),
        ],
        out_specs=[
            pl.BlockSpec((1, tm, D), lambda b, m: (b, m, 0)),
            pl.BlockSpec((1, POOL_HALO, D), lambda b, m: (b, 0, 0)),
        ],
        out_shape=[
            jax.ShapeDtypeStruct((B, L, D), F32),
            jax.ShapeDtypeStruct((B, POOL_HALO, D), F32),
        ],
        scratch_shapes=[pltpu.VMEM((tm + POOL_HALO, D), F32)],
        compiler_params=_cparams(2),
        name="pool_layer",
    )(x3, x3, buf16, mod[0], mod[1], gate, pool_w, pool_scale)


def _split3(x):
    p0 = x.astype(BF16)
    r1 = x - p0.astype(F32)
    p1 = r1.astype(BF16)
    p2 = (r1 - p1.astype(F32)).astype(BF16)
    return p0, p1, p2


def _lane_cumsum(x, tri):
    p0, p1, p2 = _split3(x)
    dot = lambda p: jnp.dot(p, tri, preferred_element_type=F32)
    return (dot(p0) + dot(p1)) + dot(p2)


def _upper_tri(t):
    r = lax.broadcasted_iota(jnp.int32, (t, t), 0)
    c = lax.broadcasted_iota(jnp.int32, (t, t), 1)
    return jnp.where(r <= c, 1.0, 0.0).astype(BF16)


BIAS_PIECES = 3


def _fbias_body(lf_ref, o_ref, carry_scr, *, tc):
    @pl.when(pl.program_id(1) == 0)
    def _():
        carry_scr[...] = jnp.zeros_like(carry_scr)

    r = lax.broadcasted_iota(jnp.int32, (tc, tc), 0)
    c = lax.broadcasted_iota(jnp.int32, (tc, tc), 1)
    tril = jnp.where(c <= r, 1.0, 0.0).astype(BF16)
    p0, p1, p2 = _split3(lf_ref[0])
    dot = lambda p: jnp.dot(tril, p, preferred_element_type=F32)
    f = carry_scr[...] + ((dot(p0) + dot(p1)) + dot(p2))
    carry_scr[...] = f[tc - 1:tc, :]
    head = lax.broadcasted_iota(jnp.int32, (FOX_HEADS, LANES), 0)
    lane = lax.broadcasted_iota(jnp.int32, (FOX_HEADS, LANES), 1)
    out = None
    for p, piece in enumerate(_split3(-f)):
        place = jnp.where(lane == BIAS_PIECES * head + p, 1.0, 0.0).astype(BF16)
        term = jnp.dot(piece, place, preferred_element_type=F32)
        out = term if out is None else out + term
    o_ref[0] = out.astype(BF16)


def fox_bias_features(lf, *, tc=512):
    B, L, H = lf.shape
    tc = min(tc, L)
    return pl.pallas_call(
        functools.partial(_fbias_body, tc=tc),
        grid=(B, L // tc),
        in_specs=[pl.BlockSpec((1, tc, H), lambda b, c: (b, c, 0))],
        out_specs=pl.BlockSpec((1, tc, LANES), lambda b, c: (b, c, 0)),
        out_shape=jax.ShapeDtypeStruct((B, L, LANES), BF16),
        scratch_shapes=[pltpu.VMEM((1, H), F32)],
        compiler_params=_cparams(2),
        name="fox_bias_features",
    )(lf)


def _flash_body(qt_ref, k_ref, fb_ref, vt_ref, o_ref, *, tq, tk):
    hp = pl.program_id(1)
    qi = pl.program_id(2)
    pair = 2 * FOX_DH
    row = lax.broadcasted_iota(jnp.int32, (pair, tq), 0)
    qt = qt_ref[0]
    qaug = []
    for i in range(2):
        q_head = jnp.where(row // FOX_DH == i, qt, jnp.zeros_like(qt))
        pick = jnp.where(row // BIAS_PIECES == 2 * hp + i, 1.0, 0.0).astype(BF16)
        qaug.append(jnp.concatenate([q_head, pick], axis=0))
    key_i = lax.broadcasted_iota(jnp.int32, (tk, tq), 0)
    qry_i = lax.broadcasted_iota(jnp.int32, (tk, tq), 1)
    per_q = tq // tk

    def step(j, carry, diag):
        k0 = pl.multiple_of(j * tk, tk)
        kaug = jnp.concatenate([k_ref[0, pl.ds(k0, tk), :], fb_ref[0, pl.ds(k0, tk), :]], axis=1)
        new = []
        for i in range(2):
            m_old, l_old, acc = carry[i]
            st = jnp.dot(kaug, qaug[i], preferred_element_type=F32)
            if diag is not None:
                st = jnp.where(key_i + diag * tk <= qry_i, st, NEG_BIG)
            m_new = jnp.maximum(m_old, jnp.max(st, axis=0, keepdims=True))
            alpha = jnp.exp(m_old - m_new)
            p = jnp.exp(st - m_new)
            l_new = alpha * l_old + jnp.sum(p, axis=0, keepdims=True)
            vt = vt_ref[0, i * FOX_DH:(i + 1) * FOX_DH, pl.ds(k0, tk)].astype(BF16)
            acc = alpha * acc + jnp.dot(vt, p.astype(BF16), preferred_element_type=F32)
            new.append((m_new, l_new, acc))
        return tuple(new)

    init = tuple((jnp.full((1, tq), NEG_BIG, F32), jnp.zeros((1, tq), F32), jnp.zeros((FOX_DH, tq), F32))
                 for _ in range(2))
    carry = lax.fori_loop(0, qi * per_q, lambda j, c: step(j, c, None), init)
    for d in range(per_q):
        carry = step(qi * per_q + d, carry, d)
    o_ref[0] = jnp.concatenate([(acc / l).T for _, l, acc in carry], axis=1).astype(o_ref.dtype)


def fox_flash(qt, k, fb, vt, *, tq=1024, tk=1024):
    B, L, D = k.shape
    tq = min(tq, L)
    tk = min(tk, tq)
    assert L % tq == 0 and tq % tk == 0
    pair = 2 * FOX_DH
    return pl.pallas_call(
        functools.partial(_flash_body, tq=tq, tk=tk),
        grid=(B, FOX_HEADS // 2, L // tq),
        in_specs=[
            pl.BlockSpec((1, pair, tq), lambda b, hp, qi: (b, hp, qi)),
            pl.BlockSpec((1, L, pair), lambda b, hp, qi: (b, 0, hp)),
            pl.BlockSpec((1, L, LANES), lambda b, hp, qi: (b, 0, 0)),
            pl.BlockSpec((1, pair, L), lambda b, hp, qi: (b, hp, 0)),
        ],
        out_specs=pl.BlockSpec((1, tq, pair), lambda b, hp, qi: (b, qi, hp)),
        out_shape=jax.ShapeDtypeStruct((B, L, D), BF16),
        compiler_params=_cparams(3),
        name="fox_flash",
    )(qt, k, fb, vt)


MAX_DECODE_PAGES = 16


def _decode_body(pt_ref, q_ref, kn_ref, vn_ref, lfn_ref, *refs, n_steps, lq, pages):
    kc = refs[0:pages]
    vc = refs[pages:2 * pages]
    lc = refs[2 * pages:3 * pages]
    o_ref, qbd_scr, m_scr, l_scr, acc_scr, carry_scr = refs[3 * pages:]
    b = pl.program_id(0)
    st = pl.program_id(1)
    rows = lq * FOX_HEADS
    last = (((1,), (1,)), ((), ()))
    tri = _upper_tri(PAGE_SIZE)

    @pl.when(st == 0)
    def _():
        head = lax.broadcasted_iota(jnp.int32, (FOX_HEADS, D_MODEL), 0)
        lane_head = lax.broadcasted_iota(jnp.int32, (FOX_HEADS, D_MODEL), 1) // FOX_DH
        blocks = [jnp.where(head == lane_head, jnp.broadcast_to(q_ref[0, t:t + 1, :], (FOX_HEADS, D_MODEL)), 0.0)
                  for t in range(lq)]
        qbd_scr[...] = jnp.concatenate(blocks, axis=0).astype(BF16)
        m_scr[...] = jnp.full_like(m_scr, NEG_BIG)
        l_scr[...] = jnp.zeros_like(l_scr)
        acc_scr[...] = jnp.zeros_like(acc_scr)
        carry_scr[...] = jnp.zeros_like(carry_scr)

    def absorb(s, v_mat, v_is_t):
        m_old = m_scr[...]
        m_new = jnp.maximum(m_old, jnp.max(s, axis=1, keepdims=True))
        alpha = jnp.exp(m_old - m_new)
        p = jnp.exp(s - m_new)
        l_scr[...] = alpha * l_scr[...] + jnp.sum(p, axis=1, keepdims=True)
        if v_is_t:
            pv = lax.dot_general(p.astype(BF16), v_mat, last, preferred_element_type=F32)
        else:
            pv = jnp.dot(p.astype(BF16), v_mat, preferred_element_type=F32)
        acc_scr[...] = alpha * acc_scr[...] + pv
        m_scr[...] = m_new

    @pl.when(st < n_steps)
    def _():
        kt = jnp.concatenate([kc[i][0].reshape(D_MODEL, PAGE_SIZE).astype(BF16) for i in range(pages)], axis=1)
        vt = jnp.concatenate([vc[i][0].reshape(D_MODEL, PAGE_SIZE).astype(BF16) for i in range(pages)], axis=1)
        within = _lane_cumsum(jnp.concatenate([lc[i][0] for i in range(pages)], axis=0), tri)
        f = carry_scr[...]
        biases = []
        for i in range(pages):
            f_page = f + within[i * FOX_HEADS:(i + 1) * FOX_HEADS, :]
            biases.append(jnp.tile(f_page, (lq, 1)))
            f = jnp.broadcast_to(f_page[:, PAGE_SIZE - 1:PAGE_SIZE], f_page.shape)
        carry_scr[...] = f
        s = jnp.dot(qbd_scr[...], kt, preferred_element_type=F32) - jnp.concatenate(biases, axis=1)
        absorb(s, vt, True)

    @pl.when(st == n_steps)
    def _():
        pad = jnp.zeros((PAGE_SIZE - lq, D_MODEL), F32)
        kn = jnp.concatenate([kn_ref[0], pad], axis=0).astype(BF16)
        vn = jnp.concatenate([vn_ref[0], pad], axis=0).astype(BF16)
        n_tok = lfn_ref.shape[1]
        tok = lax.broadcasted_iota(jnp.int32, (n_tok, PAGE_SIZE), 0)
        key = lax.broadcasted_iota(jnp.int32, (n_tok, PAGE_SIZE), 1)
        sel = jnp.where((tok // lq == b) & (tok % lq <= key) & (key < lq), 1.0, 0.0).astype(BF16)
        p0, p1, p2 = _split3(lfn_ref[...])
        dot = lambda p: jnp.dot(p, sel, preferred_element_type=F32)
        f = carry_scr[...] + ((dot(p0) + dot(p1)) + dot(p2))
        s = lax.dot_general(qbd_scr[...], kn, last, preferred_element_type=F32) - jnp.tile(f, (lq, 1))
        rq = lax.broadcasted_iota(jnp.int32, (rows, PAGE_SIZE), 0) // FOX_HEADS
        kk = lax.broadcasted_iota(jnp.int32, (rows, PAGE_SIZE), 1)
        s = jnp.where(kk <= rq, s, NEG_BIG)
        absorb(s, vn, False)
        o = acc_scr[...] / l_scr[...]
        head = lax.broadcasted_iota(jnp.int32, (FOX_HEADS, D_MODEL), 0)
        lane_head = lax.broadcasted_iota(jnp.int32, (FOX_HEADS, D_MODEL), 1) // FOX_DH
        outs = [jnp.sum(jnp.where(head == lane_head, o[t * FOX_HEADS:(t + 1) * FOX_HEADS, :], 0.0), axis=0,
                        keepdims=True) for t in range(lq)]
        o_ref[0] = jnp.concatenate(outs, axis=0).astype(o_ref.dtype)


def fox_decode(q, k_new, v_new, lft_new, cache_kt, cache_vt, cache_lt, page_table):
    B, lq, D = q.shape
    n_pages = page_table.shape[1]
    pages = math.gcd(n_pages, MAX_DECODE_PAGES)
    n_steps = n_pages // pages
    rows = lq * FOX_HEADS

    def page_idx(i):
        return lambda b, s, pt: (pt[b * n_pages + jnp.minimum(s, n_steps - 1) * pages + i], 0, 0, 0)

    def page_idx3(i):
        return lambda b, s, pt: (pt[b * n_pages + jnp.minimum(s, n_steps - 1) * pages + i], 0, 0)

    seq = lambda: pl.BlockSpec((1, lq, D), lambda b, s, pt: (b, 0, 0))
    in_specs = [seq(), seq(), seq(), pl.BlockSpec(lft_new.shape, lambda b, s, pt: (0, 0))]
    in_specs += [pl.BlockSpec((1, FOX_HEADS, FOX_DH, PAGE_SIZE), page_idx(i)) for i in range(pages)]
    in_specs += [pl.BlockSpec((1, FOX_HEADS, FOX_DH, PAGE_SIZE), page_idx(i)) for i in range(pages)]
    in_specs += [pl.BlockSpec((1, FOX_HEADS, PAGE_SIZE), page_idx3(i)) for i in range(pages)]
    grid_spec = pltpu.PrefetchScalarGridSpec(
        num_scalar_prefetch=1,
        grid=(B, n_steps + 1),
        in_specs=in_specs,
        out_specs=pl.BlockSpec((1, lq, D), lambda b, s, pt: (b, 0, 0)),
        scratch_shapes=[
            pltpu.VMEM((rows, D), BF16),
            pltpu.VMEM((rows, 1), F32),
            pltpu.VMEM((rows, 1), F32),
            pltpu.VMEM((rows, D), F32),
            pltpu.VMEM((FOX_HEADS, PAGE_SIZE), F32),
        ],
    )
    return pl.pallas_call(
        functools.partial(_decode_body, n_steps=n_steps, lq=lq, pages=pages),
        grid_spec=grid_spec,
        out_shape=jax.ShapeDtypeStruct((B, lq, D), F32),
        compiler_params=_cparams(2),
        name="fox_decode",
    )(page_table.reshape(-1), q, k_new, v_new, lft_new,
      *([cache_kt] * pages), *([cache_vt] * pages), *([cache_lt] * pages))


def _final_body(x_ref, g_ref, o_ref):
    xf = x_ref[0]
    o_ref[0] = xf * lax.rsqrt(jnp.mean(xf * xf, axis=-1, keepdims=True) + EPS) * g_ref[...]


def final_norm(x3, final_g, *, tm=None):
    bx, L, D = x3.shape
    tm = tm or min(L, 1024)
    return pl.pallas_call(
        _final_body,
        grid=(bx, L // tm),
        in_specs=[pl.BlockSpec((1, tm, D), lambda b, m: (b, m, 0)), pl.BlockSpec((1, D), lambda b, m: (0, 0))],
        out_specs=pl.BlockSpec((1, tm, D), lambda b, m: (b, m, 0)),
        out_shape=jax.ShapeDtypeStruct((bx, L, D), F32),
        compiler_params=_cparams(2),
        name="final_norm",
    )(x3, final_g.reshape(1, D))


def _trunk(x, mods, ret_state, pool_state, fox_past, pos0, params):
    (ret_w_in, ret_w_out, pool_w, pool_scale, fox_wt, fox_b_f, fox_w_out, ffn_w_gu, ffn_w_down,
     w_rt_pad, b_rt_pad, moe_w_gu, moe_w_down, final_g) = params
    B, L, D = x.shape
    decode = fox_past is not None
    if decode:
        x3 = x.reshape(1, B * L, D)
        expand = lambda v: jnp.repeat(v, L, axis=0)[None]
    else:
        x3 = x
        expand = lambda v: v[:, None, :]
    n_rows = x3.shape[1]
    pos = pos0 + jnp.arange(L)
    cos, sin = rotary_tables(pos, RET_HEADS)
    if decode:
        cos, sin = jnp.tile(cos, (B, 1)), jnp.tile(sin, (B, 1))
    chunk = min(L, 256)
    chunk_pad = max(chunk, PAGE_SIZE)
    tables = retention_tables(chunk, chunk_pad)
    ret_new, extras = [], {}
    for i in range(DEPTH):
        sh_a, sc_a, g_a, sh_f, sc_f, g_f = [expand(v) for v in jnp.split(mods[i], 6, axis=-1)]
        kind, j = i % N_MIXERS, i // N_MIXERS
        if kind == 0:
            proj = functools.partial(mm, x3, ret_w_in, j, mod=(sh_a, sc_a))
            q = proj(n0=0, n_out=RET_QK, out_dtype=BF16, epi="rot", rot=(cos, sin), name="ret_q")
            k = proj(n0=RET_QK, n_out=RET_QK, out_dtype=BF16, epi="rot", rot=(cos, sin), scale=RET_DK ** -0.5,
                     name="ret_k")
            v = proj(n0=2 * RET_QK, n_out=RET_V, out_dtype=BF16, name="ret_v")
            g = proj(n0=2 * RET_QK + RET_V, n_out=RET_V, out_dtype=F32, name="ret_g")
            if decode:
                padr = lambda t: jnp.pad(t.reshape(B, L, -1), ((0, 0), (0, chunk_pad - L), (0, 0)))
                o, s = retention_scan(padr(q), padr(k), padr(v), padr(g), tables, chunk_pad, s0=ret_state,
                                      s0_layer=j, heads=RET_HEADS)
                o = o[:, :L].reshape(1, n_rows, RET_V)
            else:
                o, s = retention_scan(q, k, v, g, tables, chunk_pad)
            ret_new.append(s)
            x3 = mm(o, ret_w_out, j, n0=0, n_out=D, out_dtype=F32, epi="res", res=(x3, g_a), name="ret_out")
        elif kind == 1:
            vecs = [v[:, None, :] for v in jnp.split(mods[i], 6, axis=-1)[:3]]
            if decode:
                buf16 = jnp.pad(pool_state[j], ((0, 0), (1, 0), (0, 0)))
                tm = L
            else:
                buf16 = jnp.zeros((B, POOL_HALO, D), F32)
                tm = min(L, 512)
            xn, tail = pool_layer(x3.reshape(B, L, D), buf16, (vecs[0], vecs[1]), vecs[2], pool_w, pool_scale, j,
                                  pos0, tm=tm)
            x3 = xn.reshape(x3.shape)
            extras["pool"] = tail[:, 1:, :]
        else:
            fproj = functools.partial(mm, x3, fox_wt, j, mod=(sh_a, sc_a), w_t=True)
            if decode:
                ck, cv, cl, pt = fox_past
                q = fproj(n0=0, n_out=D, out_dtype=F32, scale=FOX_DH ** -0.5, name="fox_q")
                lft = fproj(n0=3 * D, n_out=FOX_HEADS, out_dtype=F32, out_t=True, epi="logsig",
                            bias=fox_b_f[j].reshape(FOX_HEADS, 1), name="fox_logft")
                k = fproj(n0=D, n_out=D, out_dtype=F32, name="fox_k")
                v = fproj(n0=2 * D, n_out=D, out_dtype=F32, name="fox_v")
                o = fox_decode(q.reshape(B, L, D), k.reshape(B, L, D), v.reshape(B, L, D), lft[0],
                               jnp.transpose(ck[j], (0, 2, 3, 1)), jnp.transpose(cv[j], (0, 2, 3, 1)),
                               jnp.transpose(cl[j], (0, 2, 1)), pt)
                o = o.reshape(1, n_rows, D)
                extras["k"] = k.reshape(B, L, FOX_HEADS, FOX_DH)
                extras["v"] = v.reshape(B, L, FOX_HEADS, FOX_DH)
                extras["l"] = jnp.transpose(lft[0].reshape(FOX_HEADS, B, L), (1, 2, 0))
            else:
                qt = fproj(n0=0, n_out=D, out_dtype=BF16, out_t=True, scale=FOX_DH ** -0.5, name="fox_qt")
                kb = fproj(n0=D, n_out=D, out_dtype=BF16, name="fox_kb")
                kt = fproj(n0=D, n_out=D, out_dtype=F32, out_t=True, name="fox_kt")
                vt = fproj(n0=2 * D, n_out=D, out_dtype=F32, out_t=True, name="fox_vt")
                lf = fproj(n0=3 * D, n_out=FOX_HEADS, out_dtype=F32, epi="logsig",
                           bias=fox_b_f[j].reshape(1, FOX_HEADS), name="fox_logf")
                o = fox_flash(qt, kb, fox_bias_features(lf), vt)
                unt = lambda t: jnp.transpose(t.reshape(B, FOX_HEADS, FOX_DH, L), (0, 3, 1, 2))
                extras["k"], extras["v"] = unt(kt), unt(vt)
                extras["l"] = lf
            x3 = mm(o, fox_w_out, j, n0=0, n_out=D, out_dtype=F32, epi="res", res=(x3, g_a), name="fox_out")
        ml = i // 2
        if i % 2 == 0:
            x3 = ffn(x3, (sh_f, sc_f), g_f, ffn_w_gu, ffn_w_down, ml, name="ffn_dense")
        else:
            gates = router(x3, (sh_f, sc_f), w_rt_pad, b_rt_pad, ml)
            if decode:
                x3 = ffn(x3, (sh_f, sc_f), g_f, moe_w_gu, moe_w_down, ml * N_EXPERTS, gates=gates, name="ffn_moe")
            else:
                x3 = moe_sparse(x3, (sh_f, sc_f), g_f, gates, moe_w_gu, moe_w_down, ml * N_EXPERTS)
    out = final_norm(x3, final_g).reshape(B, L, D)
    return (out, jnp.stack(ret_new), extras["pool"][None], extras["k"][None], extras["v"][None], extras["l"][None])


def kernel(x_prompt, x_sample, state_ret, state_pool, cache_fox_k, cache_fox_v, cache_fox_logf, page_table,
           c_prompt, c_sample, ada_w, ada_b, ret_w_in, ret_w_out, pool_w, pool_scale, fox_w_in, fox_b_f, fox_w_out,
           ffn_w_gu, ffn_w_down, moe_w_router, moe_b_router, moe_w_gu, moe_w_down, final_g):
    bp, bs = x_prompt.shape[0], x_sample.shape[0]
    rows = -(-(bp + bs) // 8) * 8
    c_all = jnp.concatenate([c_prompt, c_sample, jnp.zeros((rows - bp - bs, D_MODEL), F32)], axis=0)
    mods = ada_mods(c_all, ada_w, ada_b)
    n_moe = moe_w_router.shape[0]
    params = (
        ret_w_in, ret_w_out, pool_w, pool_scale,
        jnp.swapaxes(fox_w_in, 1, 2),
        fox_b_f, fox_w_out, ffn_w_gu, ffn_w_down,
        jnp.pad(jnp.swapaxes(moe_w_router, 1, 2), ((0, 0), (0, LANES - N_EXPERTS), (0, 0))),
        jnp.pad(moe_b_router, ((0, 0), (0, LANES - N_EXPERTS))).reshape(n_moe, 1, LANES),
        moe_w_gu.reshape((n_moe * N_EXPERTS,) + moe_w_gu.shape[2:]),
        moe_w_down.reshape((n_moe * N_EXPERTS,) + moe_w_down.shape[2:]),
        final_g,
    )
    y_p, ret_p, pool_p, k_p, v_p, l_p = _trunk(x_prompt, mods[:, :bp], None, None, None, 0, params)
    n_past = page_table.shape[1] * PAGE_SIZE
    y_s, ret_s, pool_s, k_s, v_s, l_s = _trunk(
        x_sample, mods[:, bp:bp + bs], state_ret, state_pool,
        (cache_fox_k, cache_fox_v, cache_fox_logf, page_table), n_past, params)
    return (y_p, y_s, ret_p, ret_s, pool_p, pool_s, k_p, k_s, v_p, v_s, l_p, l_s)
```

```python
import functools
import math

import jax
import jax.numpy as jnp
from jax import lax
from jax.experimental import pallas as pl
from jax.experimental.pallas import tpu as pltpu

F32 = jnp.float32
BF16 = jnp.bfloat16

D_MODEL = 1024
DEPTH = 4
PAGE_SIZE = 128
N_MIXERS = 3
RET_HEADS = 4
RET_DK = D_MODEL // RET_HEADS
RET_DV = 2 * D_MODEL // RET_HEADS
RET_QK = RET_HEADS * RET_DK
RET_V = RET_HEADS * RET_DV
ROPE_BASE = 10000.0
POOL_WINDOWS = (2, 4, 8, 16)
POOL_GW = D_MODEL // len(POOL_WINDOWS)
POOL_BUF = max(POOL_WINDOWS) - 1
POOL_HALO = POOL_BUF + 1
FOX_HEADS = 16
FOX_DH = D_MODEL // FOX_HEADS
FFN_DIM = 2816
N_EXPERTS = 8
EPS = 1e-6
NEG_BIG = -1e30

V7X_VMEM_BYTES = 64 * 1024 * 1024
VMEM_LIMIT = V7X_VMEM_BYTES - 8 * 1024 * 1024
LANES = 128
FFN_TF = 256


def _cparams(n_axes):
    return pltpu.CompilerParams(dimension_semantics=("arbitrary",) * n_axes, vmem_limit_bytes=VMEM_LIMIT)


def _sigmoid(x):
    return 1.0 / (1.0 + jnp.exp(-x))


def _modulate(x, shift, scale):
    xf = x.astype(F32)
    ms = jnp.mean(xf * xf, axis=-1, keepdims=True)
    return (xf * lax.rsqrt(ms + EPS)) * (1.0 + scale) + shift


def _row_spec(arr, tm, width, col_fn):
    if arr.shape[1] == 1:
        return pl.BlockSpec((1, 1, width), lambda b, m, *r: (b, 0, col_fn(*r)))
    return pl.BlockSpec((1, tm, width), lambda b, m, *r: (b, m, col_fn(*r)))


def _ada_body(c_ref, w_ref, b_ref, o_ref):
    c = c_ref[...]
    cond = c * _sigmoid(c)
    o_ref[0] = jnp.dot(cond.astype(BF16), w_ref[0].astype(BF16), preferred_element_type=F32) + b_ref[0]


def ada_mods(c_all, ada_w, ada_b):
    rows = c_all.shape[0]
    n_out = ada_w.shape[2]
    tn = 1024
    return pl.pallas_call(
        _ada_body,
        grid=(DEPTH, n_out // tn),
        in_specs=[
            pl.BlockSpec((rows, D_MODEL), lambda i, n: (0, 0)),
            pl.BlockSpec((1, D_MODEL, tn), lambda i, n: (i, 0, n)),
            pl.BlockSpec((1, 1, tn), lambda i, n: (i, 0, n)),
        ],
        out_specs=pl.BlockSpec((1, rows, tn), lambda i, n: (i, 0, n)),
        out_shape=jax.ShapeDtypeStruct((DEPTH, rows, n_out), F32),
        compiler_params=_cparams(2),
        name="ada_mods",
    )(c_all, ada_w, ada_b.reshape(DEPTH, 1, n_out))


def _mm_body(*refs, has_mod, epi, w_t, out_t, scale):
    it = iter(refs)
    x_ref = next(it)
    if has_mod:
        sh_ref, sc_ref = next(it), next(it)
    w_ref = next(it)
    if epi == "rot":
        cos_ref, sin_ref = next(it), next(it)
    elif epi == "res":
        res_ref, gate_ref = next(it), next(it)
    elif epi == "logsig":
        b_ref = next(it)
    o_ref = next(it)
    n = pl.program_id(2)
    if has_mod:
        h_scr = next(it)

        @pl.when(n == 0)
        def _():
            h_scr[...] = _modulate(x_ref[0], sh_ref[0], sc_ref[0]).astype(BF16)

        lhs = h_scr[...]
    else:
        lhs = x_ref[0].astype(BF16)
    w = w_ref[0].astype(BF16)
    last = (((1,), (1,)), ((), ()))
    if not w_t:
        acc = jnp.dot(lhs, w, preferred_element_type=F32)
    elif not out_t:
        acc = lax.dot_general(lhs, w, last, preferred_element_type=F32)
    else:
        acc = lax.dot_general(w, lhs, last, preferred_element_type=F32)
    if epi == "rot":
        acc = _rotary(acc, cos_ref[...], sin_ref[...])
    elif epi == "res":
        acc = res_ref[0] + gate_ref[0] * acc
    elif epi == "logsig":
        z = acc + b_ref[...]
        acc = jnp.minimum(z, 0.0) - jnp.log1p(jnp.exp(-jnp.abs(z)))
    if scale != 1.0:
        acc = acc * scale
    o_ref[0] = acc.astype(o_ref.dtype)


def mm(x3, w3, wl, *, n0, n_out, out_dtype, mod=None, epi="plain", w_t=False, out_t=False, scale=1.0,
       rot=None, res=None, bias=None, tm=None, tn=None, name="mm"):
    bx, L, K = x3.shape
    tm = tm or min(L, 512 if epi == "rot" else 1024)
    tn = tn or min(n_out, 512 if K > 1024 else 1024)
    assert L % tm == 0 and n_out % tn == 0 and n0 % tn == 0
    nb0 = n0 // tn
    has_mod = mod is not None
    in_specs = [pl.BlockSpec((1, tm, K), lambda b, m, n: (b, m, 0))]
    args = [x3]
    if has_mod:
        for a in mod:
            in_specs.append(_row_spec(a, tm, K, lambda n: 0))
            args.append(a)
    if w_t:
        in_specs.append(pl.BlockSpec((1, tn, K), lambda b, m, n: (wl, nb0 + n, 0)))
    else:
        in_specs.append(pl.BlockSpec((1, K, tn), lambda b, m, n: (wl, 0, nb0 + n)))
    args.append(w3)
    if epi == "rot":
        for a in rot:
            in_specs.append(pl.BlockSpec((tm, tn), lambda b, m, n: (m, n)))
            args.append(a)
    elif epi == "res":
        in_specs.append(pl.BlockSpec((1, tm, tn), lambda b, m, n: (b, m, n)))
        in_specs.append(_row_spec(res[1], tm, tn, lambda n: n))
        args.extend(res)
    elif epi == "logsig":
        in_specs.append(pl.BlockSpec(bias.shape, lambda b, m, n: (0, 0)))
        args.append(bias)
    if out_t:
        out_spec = pl.BlockSpec((1, tn, tm), lambda b, m, n: (b, n, m))
        out_shape = jax.ShapeDtypeStruct((bx, n_out, L), out_dtype)
    else:
        out_spec = pl.BlockSpec((1, tm, tn), lambda b, m, n: (b, m, n))
        out_shape = jax.ShapeDtypeStruct((bx, L, n_out), out_dtype)
    return pl.pallas_call(
        functools.partial(_mm_body, has_mod=has_mod, epi=epi, w_t=w_t, out_t=out_t, scale=scale),
        grid=(bx, L // tm, n_out // tn),
        in_specs=in_specs,
        out_specs=out_spec,
        out_shape=out_shape,
        scratch_shapes=[pltpu.VMEM((tm, K), BF16)] if has_mod else [],
        compiler_params=_cparams(3),
        name=name,
    )(*args)


def _rotary(acc, cos, sin_signed):
    width = acc.shape[1]
    lane = lax.broadcasted_iota(jnp.int32, acc.shape, 1)
    partner = jnp.where(lane % 2 == 0, pltpu.roll(acc, width - 1, axis=1), pltpu.roll(acc, 1, axis=1))
    return acc * cos + partner * sin_signed


def _ffn_body(*refs, moe, nf, ne):
    it = iter(refs)
    x_ref, sh_ref, sc_ref, gate_ref = next(it), next(it), next(it), next(it)
    gw_ref = next(it) if moe else None
    wg_ref, wu_ref, wd_ref, o_ref, h_scr, acc_scr = next(it), next(it), next(it), next(it), next(it), next(it)
    tot_scr = next(it) if moe else None
    e = pl.program_id(2)
    f = pl.program_id(3)

    @pl.when((e == 0) & (f == 0))
    def _():
        h_scr[...] = _modulate(x_ref[0], sh_ref[0], sc_ref[0]).astype(BF16)

    @pl.when(f == 0)
    def _():
        acc_scr[...] = jnp.zeros_like(acc_scr)

    if moe:

        @pl.when((e == 0) & (f == 0))
        def _():
            tot_scr[...] = jnp.zeros_like(tot_scr)

    h = h_scr[...]
    a = jnp.dot(h, wg_ref[0].astype(BF16), preferred_element_type=F32)
    b = jnp.dot(h, wu_ref[0].astype(BF16), preferred_element_type=F32)
    mid = (a * _sigmoid(a) * b).astype(BF16)
    acc_scr[...] += jnp.dot(mid, wd_ref[0].astype(BF16), preferred_element_type=F32)

    if not moe:

        @pl.when(f == nf - 1)
        def _():
            o_ref[0] = x_ref[0] + gate_ref[0] * acc_scr[...]

    else:

        @pl.when(f == nf - 1)
        def _():
            gw = gw_ref[0]
            lane = lax.broadcasted_iota(jnp.int32, gw.shape, 1)
            col = jnp.sum(jnp.where(lane == e, gw, 0.0), axis=1, keepdims=True)
            tot_scr[...] += col * acc_scr[...]

            @pl.when(e == ne - 1)
            def _():
                o_ref[0] = x_ref[0] + gate_ref[0] * tot_scr[...]


def ffn(x3, mod, gate, w_gu3, w_down3, wl, *, gates=None, tm=None, name="ffn"):
    bx, L, D = x3.shape
    moe = gates is not None
    ne = N_EXPERTS if moe else 1
    tm = tm or min(L, 1024)
    tf = FFN_TF if tm > 256 else MOE_TF
    nf = FFN_DIM // tf
    assert L % tm == 0 and FFN_DIM % tf == 0
    in_specs = [pl.BlockSpec((1, tm, D), lambda b, m, e, f: (b, m, 0))]
    args = [x3]
    for a in (*mod, gate):
        in_specs.append(_row_spec(a, tm, D, lambda e, f: 0))
        args.append(a)
    if moe:
        in_specs.append(pl.BlockSpec((1, tm, LANES), lambda b, m, e, f: (b, m, 0)))
        args.append(gates)
    in_specs += [
        pl.BlockSpec((1, D, tf), lambda b, m, e, f: (wl + e, 0, f)),
        pl.BlockSpec((1, D, tf), lambda b, m, e, f: (wl + e, 0, nf + f)),
        pl.BlockSpec((1, tf, D), lambda b, m, e, f: (wl + e, f, 0)),
    ]
    args += [w_gu3, w_gu3, w_down3]
    scratch = [pltpu.VMEM((tm, D), BF16), pltpu.VMEM((tm, D), F32)]
    if moe:
        scratch.append(pltpu.VMEM((tm, D), F32))
    return pl.pallas_call(
        functools.partial(_ffn_body, moe=moe, nf=nf, ne=ne),
        grid=(bx, L // tm, ne, nf),
        in_specs=in_specs,
        out_specs=pl.BlockSpec((1, tm, D), lambda b, m, e, f: (b, m, 0)),
        out_shape=jax.ShapeDtypeStruct((bx, L, D), F32),
        scratch_shapes=scratch,
        compiler_params=_cparams(4),
        name=name,
    )(*args)


def _router_body(x_ref, sh_ref, sc_ref, w_ref, b_ref, o_ref):
    h = _modulate(x_ref[0], sh_ref[0], sc_ref[0]).astype(BF16)
    logits = lax.dot_general(h, w_ref[0].astype(BF16), (((1,), (1,)), ((), ())), preferred_element_type=F32)
    logits = logits + b_ref[0]
    lane = lax.broadcasted_iota(jnp.int32, logits.shape, 1).astype(F32)
    lg = jnp.where(lane < N_EXPERTS, logits, -jnp.inf)
    m1 = jnp.max(lg, axis=1, keepdims=True)
    i1 = jnp.min(jnp.where(lg == m1, lane, float(LANES)), axis=1, keepdims=True)
    lg2 = jnp.where(lane == i1, -jnp.inf, lg)
    m2 = jnp.max(lg2, axis=1, keepdims=True)
    i2 = jnp.min(jnp.where(lg2 == m2, lane, float(LANES)), axis=1, keepdims=True)
    e2 = jnp.exp(m2 - m1)
    den = 1.0 + e2
    o_ref[0] = jnp.where(lane == i1, 1.0 / den, 0.0) + jnp.where(lane == i2, e2 / den, 0.0)


def router(x3, mod, w_rt_pad, b_pad, wl, *, tm=None):
    bx, L, D = x3.shape
    tm = tm or min(L, 1024)
    in_specs = [pl.BlockSpec((1, tm, D), lambda b, m: (b, m, 0))]
    args = [x3]
    for a in mod:
        in_specs.append(_row_spec(a, tm, D, lambda: 0))
        args.append(a)
    in_specs += [
        pl.BlockSpec((1, LANES, D), lambda b, m: (wl, 0, 0)),
        pl.BlockSpec((1, 1, LANES), lambda b, m: (wl, 0, 0)),
    ]
    args += [w_rt_pad, b_pad]
    return pl.pallas_call(
        _router_body,
        grid=(bx, L // tm),
        in_specs=in_specs,
        out_specs=pl.BlockSpec((1, tm, LANES), lambda b, m: (b, m, 0)),
        out_shape=jax.ShapeDtypeStruct((bx, L, LANES), F32),
        compiler_params=_cparams(2),
        name="router",
    )(*args)


MOE_T = 256
MOE_CH = 128
MOE_TM = 256
MOE_TF = FFN_DIM // 2
MOE_ALIGN = 8
MOE_STAGE = -(-(2 * MOE_T + N_EXPERTS * (MOE_ALIGN - 1) + MOE_CH) // 8) * 8


def _moe_plan(gates2, n_tokens):
    nb = n_tokens // MOE_T
    routed = (gates2[:, :N_EXPERTS] > 0).reshape(nb, MOE_T, N_EXPERTS)
    ri = routed.astype(jnp.int32)
    rank = jnp.cumsum(ri, axis=1) - ri
    cnt = jnp.sum(ri, axis=1)
    cnt_al = -(-cnt // MOE_ALIGN) * MOE_ALIGN
    lo = jnp.cumsum(cnt_al, axis=1) - cnt_al
    total = jnp.sum(cnt_al, axis=0)
    region = -(-(total + MOE_CH) // MOE_TM) * MOE_TM
    ends = jnp.cumsum(region)
    off = ends - region
    pos = off[None, :] + jnp.cumsum(cnt_al, axis=0) - cnt_al
    nch = -(-cnt_al // MOE_CH)
    m_pad = -(-(2 * n_tokens + nb * N_EXPERTS * (MOE_ALIGN - 1) + N_EXPERTS * (MOE_CH + MOE_TM)) // MOE_TM) * MOE_TM
    n_tiles = m_pad // MOE_TM
    tile_start = jnp.arange(n_tiles, dtype=jnp.int32) * MOE_TM
    tile_e = jnp.minimum(jnp.sum(tile_start[:, None] >= ends[None, :], axis=1), N_EXPERTS - 1).astype(jnp.int32)
    n_used = (ends[-1] // MOE_TM).astype(jnp.int32).reshape(1)
    dest = jnp.where(routed, lo[:, None, :] + rank, -1)
    d_hi = jnp.max(dest, axis=2)
    d_lo = jnp.min(jnp.where(routed, dest, MOE_STAGE), axis=2)
    dd = jnp.stack([d_hi, d_lo], axis=1).astype(jnp.int32)
    eidx = jnp.arange(N_EXPERTS, dtype=jnp.int32)
    e_a = jnp.min(jnp.where(routed, eidx, N_EXPERTS), axis=2)
    e_b = jnp.max(jnp.where(routed, eidx, -1), axis=2)
    g3 = gates2[:, :N_EXPERTS].reshape(nb, MOE_T, N_EXPERTS)
    at = lambda a, e: jnp.sum(jnp.where(eidx == e[..., None], a, 0), axis=2)
    r_a, r_b = at(rank, e_a), at(rank, e_b)
    w_a, w_b = at(g3, e_a), jnp.where(e_b != e_a, at(g3, e_b), 0.0)
    col = lambda e, r, c: jnp.where(r // MOE_CH == c, e * MOE_CH + r % MOE_CH, -1).astype(F32)
    cmeta = jnp.stack([col(e_a, r_a, 0), col(e_b, r_b, 0), col(e_a, r_a, 1), col(e_b, r_b, 1), w_a, w_b], axis=-1)
    cmeta = jnp.pad(cmeta.reshape(n_tokens, 6), ((0, 0), (0, LANES - 6)))
    two = (jnp.max(nch, axis=1) > 1).astype(jnp.int32)
    flat = lambda a: a.reshape(-1).astype(jnp.int32)
    return dict(lo=flat(lo // MOE_ALIGN), pos=flat(pos // MOE_ALIGN), nch=flat(nch), tile_e=tile_e, n_used=n_used,
                dd=dd, cmeta=cmeta, two=two, m_pad=m_pad, n_tiles=n_tiles, nb=nb)


def _dispatch_body(lo_ref, pos_ref, nch_ref, x_ref, sh_ref, sc_ref, dd_ref, xs_in_ref, xs_ref, stage_scr, sem,
                   *, nb):
    del xs_in_ref
    b = pl.program_id(0)
    slot = b % 2
    h = _modulate(x_ref[...], sh_ref[0], sc_ref[0]).astype(BF16)
    r = lax.broadcasted_iota(jnp.int32, (MOE_STAGE, MOE_T), 0)
    dd = dd_ref[0]
    onehot = jnp.where((r == dd[0:1, :]) | (r == dd[1:2, :]), 1.0, 0.0).astype(BF16)
    stage_scr[slot] = jnp.dot(onehot, h, preferred_element_type=F32)

    def seg_copy(blk, e, c):
        src0 = pl.multiple_of(lo_ref[blk * N_EXPERTS + e] * MOE_ALIGN + c * MOE_CH, MOE_ALIGN)
        dst0 = pl.multiple_of(pos_ref[blk * N_EXPERTS + e] * MOE_ALIGN + c * MOE_CH, MOE_ALIGN)
        return pltpu.make_async_copy(stage_scr.at[blk % 2, pl.ds(src0, MOE_CH)], xs_ref.at[pl.ds(dst0, MOE_CH)],
                                     sem.at[blk % 2, e, c])

    def for_segments(blk, fn):
        for e in range(N_EXPERTS):
            for c in range(2):
                @pl.when(c < nch_ref[blk * N_EXPERTS + e])
                def _():
                    fn(seg_copy(blk, e, c))

    @pl.when(b > 0)
    def _():
        for_segments(b - 1, lambda cp: cp.wait())

    for_segments(b, lambda cp: cp.start())

    @pl.when(b == nb - 1)
    def _():
        for_segments(b, lambda cp: cp.wait())


def moe_dispatch(x2, mod, plan, seq_len):
    n_tokens, D = x2.shape
    per_seq = seq_len // MOE_T
    nb = plan["nb"]
    vec = lambda: pl.BlockSpec((1, 1, D), lambda b, *_: (b // per_seq, 0, 0))
    grid_spec = pltpu.PrefetchScalarGridSpec(
        num_scalar_prefetch=3,
        grid=(nb,),
        in_specs=[
            pl.BlockSpec((MOE_T, D), lambda b, *_: (b, 0)),
            vec(), vec(),
            pl.BlockSpec((1, 2, MOE_T), lambda b, *_: (b, 0, 0)),
            pl.BlockSpec(memory_space=pltpu.MemorySpace.HBM),
        ],
        out_specs=pl.BlockSpec(memory_space=pltpu.MemorySpace.HBM),
        scratch_shapes=[pltpu.VMEM((2, MOE_STAGE, D), F32), pltpu.SemaphoreType.DMA((2, N_EXPERTS, 2))],
    )
    return pl.pallas_call(
        functools.partial(_dispatch_body, nb=nb),
        grid_spec=grid_spec,
        out_shape=jax.ShapeDtypeStruct((plan["m_pad"], D), F32),
        input_output_aliases={7: 0},
        compiler_params=_cparams(1),
        name="moe_dispatch",
    )(plan["lo"], plan["pos"], plan["nch"], x2, mod[0], mod[1], plan["dd"], jnp.zeros((plan["m_pad"], D), F32))


def _gffn_body(te_ref, nu_ref, *refs, has_prev):
    it = iter(refs)
    xs_ref = next(it)
    yp_ref = next(it) if has_prev else None
    wg_ref, wu_ref, wd_ref, o_ref, wg_scr, wu_scr, wd_scr = (next(it) for _ in range(7))
    t = pl.program_id(0)
    e_here = te_ref[t]
    e_prev = te_ref[jnp.maximum(t - 1, 0)]

    @pl.when(t < nu_ref[0])
    def _():
        @pl.when((t == 0) | (e_here != e_prev))
        def _():
            wg_scr[...] = wg_ref[0].astype(BF16)
            wu_scr[...] = wu_ref[0].astype(BF16)
            wd_scr[...] = wd_ref[0].astype(BF16)

        h = xs_ref[...].astype(BF16)
        a = jnp.dot(h, wg_scr[...], preferred_element_type=F32)
        b = jnp.dot(h, wu_scr[...], preferred_element_type=F32)
        mid = (a * _sigmoid(a) * b).astype(BF16)
        y = jnp.dot(mid, wd_scr[...], preferred_element_type=F32)
        o_ref[...] = (yp_ref[...] + y) if has_prev else y

    @pl.when(t >= nu_ref[0])
    def _():
        o_ref[...] = jnp.zeros_like(o_ref)


def moe_grouped_ffn(xs, w_gu3, w_down3, wl, plan, f, y_prev=None):
    m_pad, D = xs.shape
    nf = FFN_DIM // MOE_TF
    has_prev = y_prev is not None
    row = lambda t, te, nu: (jnp.minimum(t, nu[0] - 1), 0)
    tile = lambda: pl.BlockSpec((MOE_TM, D), row)
    in_specs = [tile()] + ([tile()] if has_prev else [])
    in_specs += [
        pl.BlockSpec((1, D, MOE_TF), lambda t, te, nu: (wl + te[t], 0, f)),
        pl.BlockSpec((1, D, MOE_TF), lambda t, te, nu: (wl + te[t], 0, nf + f)),
        pl.BlockSpec((1, MOE_TF, D), lambda t, te, nu: (wl + te[t], f, 0), pipeline_mode=pl.Buffered(1)),
    ]
    grid_spec = pltpu.PrefetchScalarGridSpec(
        num_scalar_prefetch=2,
        grid=(plan["n_tiles"],),
        in_specs=in_specs,
        out_specs=pl.BlockSpec((MOE_TM, D), lambda t, te, nu: (t, 0)),
        scratch_shapes=[pltpu.VMEM((D, MOE_TF), BF16), pltpu.VMEM((D, MOE_TF), BF16), pltpu.VMEM((MOE_TF, D), BF16)],
    )
    args = [xs] + ([y_prev] if has_prev else []) + [w_gu3, w_gu3, w_down3]
    return pl.pallas_call(
        functools.partial(_gffn_body, has_prev=has_prev),
        grid_spec=grid_spec,
        out_shape=jax.ShapeDtypeStruct((m_pad, D), F32),
        compiler_params=_cparams(1),
        name=f"moe_grouped_ffn{f}",
    )(plan["tile_e"], plan["n_used"], *args)


def _combine_body(pos_ref, nch_ref, two_ref, x_ref, gate_ref, cm_ref, *refs):
    y_refs = refs[:N_EXPERTS]
    y_hbm, o_ref, yhi_scr, ylo_scr, over_scr, sem = refs[N_EXPERTS:]
    b = pl.program_id(0)

    def gathered(window, chunk):
        for e in range(N_EXPERTS):
            y = window(e)
            y_hi = y.astype(BF16)
            yhi_scr[e * MOE_CH:(e + 1) * MOE_CH, :] = y_hi
            ylo_scr[e * MOE_CH:(e + 1) * MOE_CH, :] = (y - y_hi.astype(F32)).astype(BF16)
        cm = cm_ref[...]
        lane = lax.broadcasted_iota(jnp.int32, (MOE_T, N_EXPERTS * MOE_CH), 1).astype(F32)
        picks = [jnp.where(lane == cm[:, 2 * chunk + k:2 * chunk + k + 1], 1.0, 0.0).astype(BF16) for k in range(2)]
        onehot = jnp.concatenate(picks, axis=0)
        rows = (jnp.dot(onehot, yhi_scr[...], preferred_element_type=F32)
                + jnp.dot(onehot, ylo_scr[...], preferred_element_type=F32))
        return cm[:, 4:5] * rows[:MOE_T] + cm[:, 5:6] * rows[MOE_T:]

    o_ref[...] = x_ref[...] + gate_ref[0] * gathered(lambda e: y_refs[e][...], 0)

    @pl.when(two_ref[b] > 0)
    def _():
        for e in range(N_EXPERTS):
            s = b * N_EXPERTS + e

            @pl.when(nch_ref[s] > 1)
            def _():
                start = pl.multiple_of(pos_ref[s] * MOE_ALIGN + MOE_CH, MOE_ALIGN)
                cp = pltpu.make_async_copy(y_hbm.at[pl.ds(start, MOE_CH)], over_scr.at[e], sem.at[e])
                cp.start()
                cp.wait()

            @pl.when(nch_ref[s] <= 1)
            def _():
                over_scr[e] = jnp.zeros((MOE_CH, over_scr.shape[2]), F32)

        o_ref[...] += gate_ref[0] * gathered(lambda e: over_scr[e], 1)


def moe_combine(x2, gate, y, plan, seq_len):
    n_tokens, D = x2.shape
    per_seq = seq_len // MOE_T

    def window(e):
        return pl.BlockSpec((pl.Element(MOE_CH), pl.Element(D)),
                            lambda b, pos, nch, two: (pos[b * N_EXPERTS + e] * MOE_ALIGN, 0))

    blk = lambda w: pl.BlockSpec((MOE_T, w), lambda b, *_: (b, 0))
    grid_spec = pltpu.PrefetchScalarGridSpec(
        num_scalar_prefetch=3,
        grid=(plan["nb"],),
        in_specs=[blk(D), pl.BlockSpec((1, 1, D), lambda b, *_: (b // per_seq, 0, 0)), blk(LANES)]
        + [window(e) for e in range(N_EXPERTS)] + [pl.BlockSpec(memory_space=pltpu.MemorySpace.HBM)],
        out_specs=blk(D),
        scratch_shapes=[pltpu.VMEM((N_EXPERTS * MOE_CH, D), BF16), pltpu.VMEM((N_EXPERTS * MOE_CH, D), BF16),
                        pltpu.VMEM((N_EXPERTS, MOE_CH, D), F32), pltpu.SemaphoreType.DMA((N_EXPERTS,))],
    )
    return pl.pallas_call(
        _combine_body,
        grid_spec=grid_spec,
        out_shape=jax.ShapeDtypeStruct((n_tokens, D), F32),
        compiler_params=_cparams(1),
        name="moe_combine",
    )(plan["pos"], plan["nch"], plan["two"], x2, gate, plan["cmeta"], *([y] * (N_EXPERTS + 1)))


def moe_sparse(x3, mod, gate, gates, w_gu3, w_down3, wl):
    B, L, D = x3.shape
    n_tokens = B * L
    assert L % MOE_T == 0 and FFN_DIM % MOE_TF == 0
    x2 = x3.reshape(n_tokens, D)
    gates2 = gates.reshape(n_tokens, LANES)
    plan = _moe_plan(gates2, n_tokens)
    xs = moe_dispatch(x2, mod, plan, L)
    y = None
    for f in range(FFN_DIM // MOE_TF):
        y = moe_grouped_ffn(xs, w_gu3, w_down3, wl, plan, f, y_prev=y)
    return moe_combine(x2, gate, y, plan, L).reshape(B, L, D)


RET_CHUNKS_PER_STEP = 4


def _ret_body(*refs, zero_init, nc, chunk, per_step, heads):
    it = iter(refs)
    q_ref, k_ref, v_ref, g_ref = next(it), next(it), next(it), next(it)
    s0_ref = None if zero_init else next(it)
    inner_ref, qd_ref, kd_ref, cd_ref = next(it), next(it), next(it), next(it)
    o_ref, sout_ref, s_scr = next(it), next(it), next(it)
    c = pl.program_id(2)

    @pl.when(c == 0)
    def _():
        if zero_init:
            s_scr[...] = jnp.zeros_like(s_scr)
        else:
            s_scr[...] = s0_ref[0, 0]

    for hi in range(heads):
        kcols = slice(hi * RET_DK, (hi + 1) * RET_DK)
        vcols = slice(hi * RET_DV, (hi + 1) * RET_DV)
        s = s_scr[hi]
        for ci in range(per_step):
            rows = slice(ci * chunk, (ci + 1) * chunk)
            q = q_ref[0, rows, kcols]
            k = k_ref[0, rows, kcols]
            v = v_ref[0, rows, vcols]
            att = lax.dot_general(q, k, (((1,), (1,)), ((), ())), preferred_element_type=F32) * inner_ref[hi]
            inner = jnp.dot(att.astype(BF16), v, preferred_element_type=F32)
            cross = jnp.dot(q, s.astype(BF16), preferred_element_type=F32) * qd_ref[hi]
            kdt = (k.astype(F32) * kd_ref[hi]).T.astype(BF16)
            s = s * cd_ref[hi] + jnp.dot(kdt, v, preferred_element_type=F32)
            o = inner + cross
            on = o * lax.rsqrt(jnp.mean(o * o, axis=-1, keepdims=True) + EPS)
            g = g_ref[0, rows, vcols]
            o_ref[0, rows, vcols] = (g * _sigmoid(g) * on).astype(o_ref.dtype)
        s_scr[hi] = s

    @pl.when(c == nc - 1)
    def _():
        sout_ref[0] = s_scr[...]


def retention_scan(q, k, v, g, tables, chunk, *, s0=None, s0_layer=0, heads=1):
    B, L, _ = q.shape
    per_step = math.gcd(L // chunk, RET_CHUNKS_PER_STEP)
    rows = per_step * chunk
    nc = L // rows
    inner, qd, kd, cd = tables
    zero_init = s0 is None
    in_specs = [
        pl.BlockSpec((1, rows, heads * RET_DK), lambda h, b, c: (b, c, h)),
        pl.BlockSpec((1, rows, heads * RET_DK), lambda h, b, c: (b, c, h)),
        pl.BlockSpec((1, rows, heads * RET_DV), lambda h, b, c: (b, c, h)),
        pl.BlockSpec((1, rows, heads * RET_DV), lambda h, b, c: (b, c, h)),
    ]
    args = [q, k, v, g]
    if not zero_init:
        in_specs.append(pl.BlockSpec((1, 1, heads, RET_DK, RET_DV), lambda h, b, c: (s0_layer, b, h, 0, 0)))
        args.append(s0)
    in_specs += [
        pl.BlockSpec((heads, chunk, chunk), lambda h, b, c: (h, 0, 0)),
        pl.BlockSpec((heads, chunk, RET_DV), lambda h, b, c: (h, 0, 0)),
        pl.BlockSpec((heads, chunk, RET_DK), lambda h, b, c: (h, 0, 0)),
        pl.BlockSpec((heads, 1, RET_DV), lambda h, b, c: (h, 0, 0)),
    ]
    args += [inner, qd, kd, cd]
    return pl.pallas_call(
        functools.partial(_ret_body, zero_init=zero_init, nc=nc, chunk=chunk, per_step=per_step, heads=heads),
        grid=(RET_HEADS // heads, B, nc),
        in_specs=in_specs,
        out_specs=[
            pl.BlockSpec((1, rows, heads * RET_DV), lambda h, b, c: (b, c, h)),
            pl.BlockSpec((1, heads, RET_DK, RET_DV), lambda h, b, c: (b, h, 0, 0)),
        ],
        out_shape=[
            jax.ShapeDtypeStruct((B, L, RET_V), BF16),
            jax.ShapeDtypeStruct((B, RET_HEADS, RET_DK, RET_DV), F32),
        ],
        scratch_shapes=[pltpu.VMEM((heads, RET_DK, RET_DV), F32)],
        compiler_params=_cparams(3),
        name="retention_scan",
    )(*args)


def retention_tables(n_real, n_pad):
    log_gamma = jnp.log1p(-(2.0 ** (-5.0 - jnp.arange(RET_HEADS, dtype=F32))))
    idx = jnp.arange(n_pad, dtype=F32)
    valid = idx < n_real
    diff = idx[:, None] - idx[None, :]
    ok = (diff >= 0) & valid[:, None] & valid[None, :]
    inner = jnp.where(ok[None], jnp.exp(log_gamma[:, None, None] * jnp.maximum(diff, 0.0)[None]), 0.0)
    qd = jnp.exp(log_gamma[:, None] * (idx[None, :] + 1.0))
    kd = jnp.where(valid[None, :], jnp.exp(log_gamma[:, None] * (n_real - 1.0 - idx[None, :])), 0.0)
    cd = jnp.exp(log_gamma * n_real)
    return (inner,
            jnp.broadcast_to(qd[:, :, None], (RET_HEADS, n_pad, RET_DV)),
            jnp.broadcast_to(kd[:, :, None], (RET_HEADS, n_pad, RET_DK)),
            jnp.broadcast_to(cd[:, None, None], (RET_HEADS, 1, RET_DV)))


def rotary_tables(pos, reps):
    inv_freq = ROPE_BASE ** (-jnp.arange(0, RET_DK, 2, dtype=F32) / RET_DK)
    ang = pos.astype(F32)[:, None] * inv_freq[None, :]
    cos = jnp.repeat(jnp.cos(ang), 2, axis=1)
    sin = jnp.sin(ang)
    sin_signed = jnp.stack([-sin, sin], axis=-1).reshape(ang.shape[0], RET_DK)
    return jnp.tile(cos, (1, reps)), jnp.tile(sin_signed, (1, reps))


def _pool_body(x_ref, xp_ref, buf_ref, sh_ref, sc_ref, gate_ref, w_ref, cs_ref, o_ref, tail_ref, ext_scr,
               *, tm, pos0, has_prev):
    m = pl.program_id(1)
    sh, sc = sh_ref[0], sc_ref[0]
    h = _modulate(x_ref[0], sh, sc)

    @pl.when(m == 0)
    def _():
        ext_scr[0:POOL_HALO, :] = buf_ref[0]

    if has_prev:

        @pl.when(m > 0)
        def _():
            ext_scr[0:POOL_HALO, :] = _modulate(xp_ref[0], sh, sc)

    ext_scr[POOL_HALO:POOL_HALO + tm, :] = h
    tail_ref[0] = ext_scr[tm:tm + POOL_HALO, :]
    row = lax.broadcasted_iota(jnp.int32, (tm, 1), 0)
    pos1 = (pos0 + m * tm + row + 1).astype(F32)
    rows = max(tm, 16)
    ys = []
    for gi, w in enumerate(POOL_WINDOWS):
        c0, c1 = gi * POOL_GW, (gi + 1) * POOL_GW
        win = ext_scr[POOL_HALO:POOL_HALO + tm, c0:c1]
        for j in range(1, w):
            win = win + ext_scr[POOL_HALO - j:POOL_HALO - j + tm, c0:c1]
        d = win / jnp.minimum(jnp.float32(w), pos1) - h[:, c0:c1]
        if rows != tm:
            d = jnp.concatenate([d, jnp.zeros((rows - tm, POOL_GW), F32)], axis=0)
        y = jnp.dot(d.astype(BF16), w_ref[0, gi].astype(BF16), preferred_element_type=F32)
        ys.append(y[0:tm])
    y = jnp.concatenate(ys, axis=1) * cs_ref[...]
    o_ref[0] = x_ref[0] + gate_ref[0] * y


def pool_layer(x3, buf16, mod, gate, pool_w, pool_scale, wl, pos0, *, tm):
    B, L, D = x3.shape
    assert L % tm == 0 and (L == tm or tm % POOL_HALO == 0)
    has_prev = L > tm
    ph = POOL_HALO if has_prev else min(L, POOL_HALO)
    per = tm // POOL_HALO if has_prev else 1
    vec = lambda: pl.BlockSpec((1, 1, D), lambda b, m: (b, 0, 0))
    return pl.pallas_call(
        functools.partial(_pool_body, tm=tm, pos0=pos0, has_prev=has_prev),
        grid=(B, L // tm),
        in_specs=[
            pl.BlockSpec((1, tm, D), lambda b, m: (b, m, 0)),
            pl.BlockSpec((1, ph, D), lambda b, m: (b, jnp.maximum(m * per - 1, 0), 0)),
            pl.BlockSpec((1, POOL_HALO, D), lambda b, m: (b, 0, 0)),
            vec(), vec(), vec(),
            pl.BlockSpec((1,) + pool_w.shape[1:], lambda b, m: (wl, 0, 0, 0)),
            pl.BlockSpec((1, D), lambda b, m: (wl, 0)),
        ],
        out_specs=[
            pl.BlockSpec((1, tm, D), lambda b, m: (b, m, 0)),
            pl.BlockSpec((1, POOL_HALO, D), lambda b, m: (b, 0, 0)),
        ],
        out_shape=[
            jax.ShapeDtypeStruct((B, L, D), F32),
            jax.ShapeDtypeStruct((B, POOL_HALO, D), F32),
        ],
        scratch_shapes=[pltpu.VMEM((tm + POOL_HALO, D), F32)],
        compiler_params=_cparams(2),
        name="pool_layer",
    )(x3, x3, buf16, mod[0], mod[1], gate, pool_w, pool_scale)


def _split3(x):
    p0 = x.astype(BF16)
    r1 = x - p0.astype(F32)
    p1 = r1.astype(BF16)
    p2 = (r1 - p1.astype(F32)).astype(BF16)
    return p0, p1, p2


def _lane_cumsum(x, tri):
    p0, p1, p2 = _split3(x)
    dot = lambda p: jnp.dot(p, tri, preferred_element_type=F32)
    return (dot(p0) + dot(p1)) + dot(p2)


def _upper_tri(t):
    r = lax.broadcasted_iota(jnp.int32, (t, t), 0)
    c = lax.broadcasted_iota(jnp.int32, (t, t), 1)
    return jnp.where(r <= c, 1.0, 0.0).astype(BF16)


BIAS_PIECES = 3


def _fbias_body(lf_ref, o_ref, carry_scr, *, tc):
    @pl.when(pl.program_id(1) == 0)
    def _():
        carry_scr[...] = jnp.zeros_like(carry_scr)

    r = lax.broadcasted_iota(jnp.int32, (tc, tc), 0)
    c = lax.broadcasted_iota(jnp.int32, (tc, tc), 1)
    tril = jnp.where(c <= r, 1.0, 0.0).astype(BF16)
    p0, p1, p2 = _split3(lf_ref[0])
    dot = lambda p: jnp.dot(tril, p, preferred_element_type=F32)
    f = carry_scr[...] + ((dot(p0) + dot(p1)) + dot(p2))
    carry_scr[...] = f[tc - 1:tc, :]
    head = lax.broadcasted_iota(jnp.int32, (FOX_HEADS, LANES), 0)
    lane = lax.broadcasted_iota(jnp.int32, (FOX_HEADS, LANES), 1)
    out = None
    for p, piece in enumerate(_split3(-f)):
        place = jnp.where(lane == BIAS_PIECES * head + p, 1.0, 0.0).astype(BF16)
        term = jnp.dot(piece, place, preferred_element_type=F32)
        out = term if out is None else out + term
    o_ref[0] = out.astype(BF16)


def fox_bias_features(lf, *, tc=512):
    B, L, H = lf.shape
    tc = min(tc, L)
    return pl.pallas_call(
        functools.partial(_fbias_body, tc=tc),
        grid=(B, L // tc),
        in_specs=[pl.BlockSpec((1, tc, H), lambda b, c: (b, c, 0))],
        out_specs=pl.BlockSpec((1, tc, LANES), lambda b, c: (b, c, 0)),
        out_shape=jax.ShapeDtypeStruct((B, L, LANES), BF16),
        scratch_shapes=[pltpu.VMEM((1, H), F32)],
        compiler_params=_cparams(2),
        name="fox_bias_features",
    )(lf)


DEN_ROWS = 16


def _flash_body(qt_ref, k_ref, fb_ref, vt_ref, o_ref, *, tq, tk):
    hp = pl.program_id(1)
    qi = pl.program_id(2)
    pair = 2 * FOX_DH
    row = lax.broadcasted_iota(jnp.int32, (pair, tq), 0)
    qt = qt_ref[0]
    qaug = []
    for i in range(2):
        q_head = jnp.where(row // FOX_DH == i, qt, jnp.zeros_like(qt))
        pick = jnp.where(row // BIAS_PIECES == 2 * hp + i, 1.0, 0.0).astype(BF16)
        qaug.append(jnp.concatenate([q_head, pick], axis=0))
    key_i = lax.broadcasted_iota(jnp.int32, (tk, tq), 0)
    qry_i = lax.broadcasted_iota(jnp.int32, (tk, tq), 1)
    per_q = tq // tk

    def step(j, carry, diag):
        k0 = pl.multiple_of(j * tk, tk)
        kaug = jnp.concatenate([k_ref[0, pl.ds(k0, tk), :], fb_ref[0, pl.ds(k0, tk), :]], axis=1)
        ones = jnp.ones((DEN_ROWS, tk), BF16)
        new = []
        for i in range(2):
            m_old, acc = carry[i]
            st = jnp.dot(kaug, qaug[i], preferred_element_type=F32)
            if diag is not None:
                st = jnp.where(key_i + diag * tk <= qry_i, st, NEG_BIG)
            m_new = jnp.maximum(m_old, jnp.max(st, axis=0, keepdims=True))
            alpha = jnp.exp(m_old - m_new)
            p = jnp.exp(st - m_new).astype(BF16)
            vt = jnp.concatenate([vt_ref[0, i * FOX_DH:(i + 1) * FOX_DH, pl.ds(k0, tk)].astype(BF16), ones], axis=0)
            acc = alpha * acc + jnp.dot(vt, p, preferred_element_type=F32)
            new.append((m_new, acc))
        return tuple(new)

    init = tuple((jnp.full((1, tq), NEG_BIG, F32), jnp.zeros((FOX_DH + DEN_ROWS, tq), F32)) for _ in range(2))
    carry = lax.fori_loop(0, qi * per_q, lambda j, c: step(j, c, None), init)
    for d in range(per_q):
        carry = step(qi * per_q + d, carry, d)
    o_ref[0] = jnp.concatenate([(acc[:FOX_DH] / acc[FOX_DH:FOX_DH + 1]).T for _, acc in carry],
                               axis=1).astype(o_ref.dtype)


def fox_flash(qt, k, fb, vt, *, tq=1024, tk=1024):
    B, L, D = k.shape
    tq = min(tq, L)
    tk = min(tk, tq)
    assert L % tq == 0 and tq % tk == 0
    pair = 2 * FOX_DH
    return pl.pallas_call(
        functools.partial(_flash_body, tq=tq, tk=tk),
        grid=(B, FOX_HEADS // 2, L // tq),
        in_specs=[
            pl.BlockSpec((1, pair, tq), lambda b, hp, qi: (b, hp, qi)),
            pl.BlockSpec((1, L, pair), lambda b, hp, qi: (b, 0, hp)),
            pl.BlockSpec((1, L, LANES), lambda b, hp, qi: (b, 0, 0)),
            pl.BlockSpec((1, pair, L), lambda b, hp, qi: (b, hp, 0)),
        ],
        out_specs=pl.BlockSpec((1, tq, pair), lambda b, hp, qi: (b, qi, hp)),
        out_shape=jax.ShapeDtypeStruct((B, L, D), BF16),
        compiler_params=_cparams(3),
        name="fox_flash",
    )(qt, k, fb, vt)


MAX_DECODE_PAGES = 16


def _decode_body(pt_ref, q_ref, kn_ref, vn_ref, lfn_ref, *refs, n_steps, lq, pages):
    kc = refs[0:pages]
    vc = refs[pages:2 * pages]
    lc = refs[2 * pages:3 * pages]
    o_ref, qbd_scr, m_scr, l_scr, acc_scr, carry_scr = refs[3 * pages:]
    b = pl.program_id(0)
    st = pl.program_id(1)
    rows = lq * FOX_HEADS
    last = (((1,), (1,)), ((), ()))
    tri = _upper_tri(PAGE_SIZE)

    @pl.when(st == 0)
    def _():
        head = lax.broadcasted_iota(jnp.int32, (FOX_HEADS, D_MODEL), 0)
        lane_head = lax.broadcasted_iota(jnp.int32, (FOX_HEADS, D_MODEL), 1) // FOX_DH
        blocks = [jnp.where(head == lane_head, jnp.broadcast_to(q_ref[0, t:t + 1, :], (FOX_HEADS, D_MODEL)), 0.0)
                  for t in range(lq)]
        qbd_scr[...] = jnp.concatenate(blocks, axis=0).astype(BF16)
        m_scr[...] = jnp.full_like(m_scr, NEG_BIG)
        l_scr[...] = jnp.zeros_like(l_scr)
        acc_scr[...] = jnp.zeros_like(acc_scr)
        carry_scr[...] = jnp.zeros_like(carry_scr)

    def absorb(s, v_mat, v_is_t):
        m_old = m_scr[...]
        m_new = jnp.maximum(m_old, jnp.max(s, axis=1, keepdims=True))
        alpha = jnp.exp(m_old - m_new)
        p = jnp.exp(s - m_new)
        l_scr[...] = alpha * l_scr[...] + jnp.sum(p, axis=1, keepdims=True)
        if v_is_t:
            pv = lax.dot_general(p.astype(BF16), v_mat, last, preferred_element_type=F32)
        else:
            pv = jnp.dot(p.astype(BF16), v_mat, preferred_element_type=F32)
        acc_scr[...] = alpha * acc_scr[...] + pv
        m_scr[...] = m_new

    @pl.when(st < n_steps)
    def _():
        kt = jnp.concatenate([kc[i][0].reshape(D_MODEL, PAGE_SIZE).astype(BF16) for i in range(pages)], axis=1)
        vt = jnp.concatenate([vc[i][0].reshape(D_MODEL, PAGE_SIZE).astype(BF16) for i in range(pages)], axis=1)
        within = _lane_cumsum(jnp.concatenate([lc[i][0] for i in range(pages)], axis=0), tri)
        f = carry_scr[...]
        biases = []
        for i in range(pages):
            f_page = f + within[i * FOX_HEADS:(i + 1) * FOX_HEADS, :]
            biases.append(jnp.tile(f_page, (lq, 1)))
            f = jnp.broadcast_to(f_page[:, PAGE_SIZE - 1:PAGE_SIZE], f_page.shape)
        carry_scr[...] = f
        s = jnp.dot(qbd_scr[...], kt, preferred_element_type=F32) - jnp.concatenate(biases, axis=1)
        absorb(s, vt, True)

    @pl.when(st == n_steps)
    def _():
        pad = jnp.zeros((PAGE_SIZE - lq, D_MODEL), F32)
        kn = jnp.concatenate([kn_ref[0], pad], axis=0).astype(BF16)
        vn = jnp.concatenate([vn_ref[0], pad], axis=0).astype(BF16)
        n_tok = lfn_ref.shape[1]
        tok = lax.broadcasted_iota(jnp.int32, (n_tok, PAGE_SIZE), 0)
        key = lax.broadcasted_iota(jnp.int32, (n_tok, PAGE_SIZE), 1)
        sel = jnp.where((tok // lq == b) & (tok % lq <= key) & (key < lq), 1.0, 0.0).astype(BF16)
        p0, p1, p2 = _split3(lfn_ref[...])
        dot = lambda p: jnp.dot(p, sel, preferred_element_type=F32)
        f = carry_scr[...] + ((dot(p0) + dot(p1)) + dot(p2))
        s = lax.dot_general(qbd_scr[...], kn, last, preferred_element_type=F32) - jnp.tile(f, (lq, 1))
        rq = lax.broadcasted_iota(jnp.int32, (rows, PAGE_SIZE), 0) // FOX_HEADS
        kk = lax.broadcasted_iota(jnp.int32, (rows, PAGE_SIZE), 1)
        s = jnp.where(kk <= rq, s, NEG_BIG)
        absorb(s, vn, False)
        o = acc_scr[...] / l_scr[...]
        head = lax.broadcasted_iota(jnp.int32, (FOX_HEADS, D_MODEL), 0)
        lane_head = lax.broadcasted_iota(jnp.int32, (FOX_HEADS, D_MODEL), 1) // FOX_DH
        outs = [jnp.sum(jnp.where(head == lane_head, o[t * FOX_HEADS:(t + 1) * FOX_HEADS, :], 0.0), axis=0,
                        keepdims=True) for t in range(lq)]
        o_ref[0] = jnp.concatenate(outs, axis=0).astype(o_ref.dtype)


def fox_decode(q, k_new, v_new, lft_new, cache_kt, cache_vt, cache_lt, page_table):
    B, lq, D = q.shape
    n_pages = page_table.shape[1]
    pages = math.gcd(n_pages, MAX_DECODE_PAGES)
    n_steps = n_pages // pages
    rows = lq * FOX_HEADS

    def page_idx(i):
        return lambda b, s, pt: (pt[b * n_pages + jnp.minimum(s, n_steps - 1) * pages + i], 0, 0, 0)

    def page_idx3(i):
        return lambda b, s, pt: (pt[b * n_pages + jnp.minimum(s, n_steps - 1) * pages + i], 0, 0)

    seq = lambda: pl.BlockSpec((1, lq, D), lambda b, s, pt: (b, 0, 0))
    in_specs = [seq(), seq(), seq(), pl.BlockSpec(lft_new.shape, lambda b, s, pt: (0, 0))]
    in_specs += [pl.BlockSpec((1, FOX_HEADS, FOX_DH, PAGE_SIZE), page_idx(i)) for i in range(pages)]
    in_specs += [pl.BlockSpec((1, FOX_HEADS, FOX_DH, PAGE_SIZE), page_idx(i)) for i in range(pages)]
    in_specs += [pl.BlockSpec((1, FOX_HEADS, PAGE_SIZE), page_idx3(i)) for i in range(pages)]
    grid_spec = pltpu.PrefetchScalarGridSpec(
        num_scalar_prefetch=1,
        grid=(B, n_steps + 1),
        in_specs=in_specs,
        out_specs=pl.BlockSpec((1, lq, D), lambda b, s, pt: (b, 0, 0)),
        scratch_shapes=[
            pltpu.VMEM((rows, D), BF16),
            pltpu.VMEM((rows, 1), F32),
            pltpu.VMEM((rows, 1), F32),
            pltpu.VMEM((rows, D), F32),
            pltpu.VMEM((FOX_HEADS, PAGE_SIZE), F32),
        ],
    )
    return pl.pallas_call(
        functools.partial(_decode_body, n_steps=n_steps, lq=lq, pages=pages),
        grid_spec=grid_spec,
        out_shape=jax.ShapeDtypeStruct((B, lq, D), F32),
        compiler_params=_cparams(2),
        name="fox_decode",
    )(page_table.reshape(-1), q, k_new, v_new, lft_new,
      *([cache_kt] * pages), *([cache_vt] * pages), *([cache_lt] * pages))


def _final_body(x_ref, g_ref, o_ref):
    xf = x_ref[0]
    o_ref[0] = xf * lax.rsqrt(jnp.mean(xf * xf, axis=-1, keepdims=True) + EPS) * g_ref[...]


def final_norm(x3, final_g, *, tm=None):
    bx, L, D = x3.shape
    tm = tm or min(L, 1024)
    return pl.pallas_call(
        _final_body,
        grid=(bx, L // tm),
        in_specs=[pl.BlockSpec((1, tm, D), lambda b, m: (b, m, 0)), pl.BlockSpec((1, D), lambda b, m: (0, 0))],
        out_specs=pl.BlockSpec((1, tm, D), lambda b, m: (b, m, 0)),
        out_shape=jax.ShapeDtypeStruct((bx, L, D), F32),
        compiler_params=_cparams(2),
        name="final_norm",
    )(x3, final_g.reshape(1, D))


def _trunk(x, mods, ret_state, pool_state, fox_past, pos0, params):
    (ret_w_in, ret_w_out, pool_w, pool_scale, fox_wt, fox_b_f, fox_w_out, ffn_w_gu, ffn_w_down,
     w_rt_pad, b_rt_pad, moe_w_gu, moe_w_down, final_g) = params
    B, L, D = x.shape
    decode = fox_past is not None
    if decode:
        x3 = x.reshape(1, B * L, D)
        expand = lambda v: jnp.repeat(v, L, axis=0)[None]
    else:
        x3 = x
        expand = lambda v: v[:, None, :]
    n_rows = x3.shape[1]
    pos = pos0 + jnp.arange(L)
    cos, sin = rotary_tables(pos, RET_HEADS)
    if decode:
        cos, sin = jnp.tile(cos, (B, 1)), jnp.tile(sin, (B, 1))
    chunk = min(L, 256)
    chunk_pad = max(chunk, PAGE_SIZE)
    tables = retention_tables(chunk, chunk_pad)
    ret_new, extras = [], {}
    for i in range(DEPTH):
        sh_a, sc_a, g_a, sh_f, sc_f, g_f = [expand(v) for v in jnp.split(mods[i], 6, axis=-1)]
        kind, j = i % N_MIXERS, i // N_MIXERS
        if kind == 0:
            proj = functools.partial(mm, x3, ret_w_in, j, mod=(sh_a, sc_a))
            q = proj(n0=0, n_out=RET_QK, out_dtype=BF16, epi="rot", rot=(cos, sin), name="ret_q")
            k = proj(n0=RET_QK, n_out=RET_QK, out_dtype=BF16, epi="rot", rot=(cos, sin), scale=RET_DK ** -0.5,
                     name="ret_k")
            v = proj(n0=2 * RET_QK, n_out=RET_V, out_dtype=BF16, name="ret_v")
            g = proj(n0=2 * RET_QK + RET_V, n_out=RET_V, out_dtype=F32, name="ret_g")
            if decode:
                padr = lambda t: jnp.pad(t.reshape(B, L, -1), ((0, 0), (0, chunk_pad - L), (0, 0)))
                o, s = retention_scan(padr(q), padr(k), padr(v), padr(g), tables, chunk_pad, s0=ret_state,
                                      s0_layer=j, heads=RET_HEADS)
                o = o[:, :L].reshape(1, n_rows, RET_V)
            else:
                o, s = retention_scan(q, k, v, g, tables, chunk_pad)
            ret_new.append(s)
            x3 = mm(o, ret_w_out, j, n0=0, n_out=D, out_dtype=F32, epi="res", res=(x3, g_a), name="ret_out")
        elif kind == 1:
            vecs = [v[:, None, :] for v in jnp.split(mods[i], 6, axis=-1)[:3]]
            if decode:
                buf16 = jnp.pad(pool_state[j], ((0, 0), (1, 0), (0, 0)))
                tm = L
            else:
                buf16 = jnp.zeros((B, POOL_HALO, D), F32)
                tm = min(L, 512)
            xn, tail = pool_layer(x3.reshape(B, L, D), buf16, (vecs[0], vecs[1]), vecs[2], pool_w, pool_scale, j,
                                  pos0, tm=tm)
            x3 = xn.reshape(x3.shape)
            extras["pool"] = tail[:, 1:, :]
        else:
            fproj = functools.partial(mm, x3, fox_wt, j, mod=(sh_a, sc_a), w_t=True)
            if decode:
                ck, cv, cl, pt = fox_past
                q = fproj(n0=0, n_out=D, out_dtype=F32, scale=FOX_DH ** -0.5, name="fox_q")
                lft = fproj(n0=3 * D, n_out=FOX_HEADS, out_dtype=F32, out_t=True, epi="logsig",
                            bias=fox_b_f[j].reshape(FOX_HEADS, 1), name="fox_logft")
                k = fproj(n0=D, n_out=D, out_dtype=F32, name="fox_k")
                v = fproj(n0=2 * D, n_out=D, out_dtype=F32, name="fox_v")
                o = fox_decode(q.reshape(B, L, D), k.reshape(B, L, D), v.reshape(B, L, D), lft[0],
                               jnp.transpose(ck[j], (0, 2, 3, 1)), jnp.transpose(cv[j], (0, 2, 3, 1)),
                               jnp.transpose(cl[j], (0, 2, 1)), pt)
                o = o.reshape(1, n_rows, D)
                extras["k"] = k.reshape(B, L, FOX_HEADS, FOX_DH)
                extras["v"] = v.reshape(B, L, FOX_HEADS, FOX_DH)
                extras["l"] = jnp.transpose(lft[0].reshape(FOX_HEADS, B, L), (1, 2, 0))
            else:
                qt = fproj(n0=0, n_out=D, out_dtype=BF16, out_t=True, scale=FOX_DH ** -0.5, name="fox_qt")
                kb = fproj(n0=D, n_out=D, out_dtype=BF16, name="fox_kb")
                kt = fproj(n0=D, n_out=D, out_dtype=F32, out_t=True, name="fox_kt")
                vt = fproj(n0=2 * D, n_out=D, out_dtype=F32, out_t=True, name="fox_vt")
                lf = fproj(n0=3 * D, n_out=FOX_HEADS, out_dtype=F32, epi="logsig",
                           bias=fox_b_f[j].reshape(1, FOX_HEADS), name="fox_logf")
                o = fox_flash(qt, kb, fox_bias_features(lf), vt)
                unt = lambda t: jnp.transpose(t.reshape(B, FOX_HEADS, FOX_DH, L), (0, 3, 1, 2))
                extras["k"], extras["v"] = unt(kt), unt(vt)
                extras["l"] = lf
            x3 = mm(o, fox_w_out, j, n0=0, n_out=D, out_dtype=F32, epi="res", res=(x3, g_a), name="fox_out")
        ml = i // 2
        if i % 2 == 0:
            x3 = ffn(x3, (sh_f, sc_f), g_f, ffn_w_gu, ffn_w_down, ml, name="ffn_dense")
        else:
            gates = router(x3, (sh_f, sc_f), w_rt_pad, b_rt_pad, ml)
            if decode:
                x3 = ffn(x3, (sh_f, sc_f), g_f, moe_w_gu, moe_w_down, ml * N_EXPERTS, gates=gates, name="ffn_moe")
            else:
                x3 = moe_sparse(x3, (sh_f, sc_f), g_f, gates, moe_w_gu, moe_w_down, ml * N_EXPERTS)
    out = final_norm(x3, final_g).reshape(B, L, D)
    return (out, jnp.stack(ret_new), extras["pool"][None], extras["k"][None], extras["v"][None], extras["l"][None])


def kernel(x_prompt, x_sample, state_ret, state_pool, cache_fox_k, cache_fox_v, cache_fox_logf, page_table,
           c_prompt, c_sample, ada_w, ada_b, ret_w_in, ret_w_out, pool_w, pool_scale, fox_w_in, fox_b_f, fox_w_out,
           ffn_w_gu, ffn_w_down, moe_w_router, moe_b_router, moe_w_gu, moe_w_down, final_g):
    bp, bs = x_prompt.shape[0], x_sample.shape[0]
    rows = -(-(bp + bs) // 8) * 8
    c_all = jnp.concatenate([c_prompt, c_sample, jnp.zeros((rows - bp - bs, D_MODEL), F32)], axis=0)
    mods = ada_mods(c_all, ada_w, ada_b)
    n_moe = moe_w_router.shape[0]
    params = (
        ret_w_in, ret_w_out, pool_w, pool_scale,
        jnp.swapaxes(fox_w_in, 1, 2),
        fox_b_f, fox_w_out, ffn_w_gu, ffn_w_down,
        jnp.pad(jnp.swapaxes(moe_w_router, 1, 2), ((0, 0), (0, LANES - N_EXPERTS), (0, 0))),
        jnp.pad(moe_b_router, ((0, 0), (0, LANES - N_EXPERTS))).reshape(n_moe, 1, LANES),
        moe_w_gu.reshape((n_moe * N_EXPERTS,) + moe_w_gu.shape[2:]),
        moe_w_down.reshape((n_moe * N_EXPERTS,) + moe_w_down.shape[2:]),
        final_g,
    )
    y_p, ret_p, pool_p, k_p, v_p, l_p = _trunk(x_prompt, mods[:, :bp], None, None, None, 0, params)
    n_past = page_table.shape[1] * PAGE_SIZE
    y_s, ret_s, pool_s, k_s, v_s, l_s = _trunk(
        x_sample, mods[:, bp:bp + bs], state_ret, state_pool,
        (cache_fox_k, cache_fox_v, cache_fox_logf, page_table), n_past, params)
    return (y_p, y_s, ret_p, ret_s, pool_p, pool_s, k_p, k_s, v_p, v_s, l_p, l_s)
```

```python
import functools
import math

import jax
import jax.numpy as jnp
from jax import lax
from jax.experimental import pallas as pl
from jax.experimental.pallas import tpu as pltpu

F32 = jnp.float32
BF16 = jnp.bfloat16

D_MODEL = 1024
DEPTH = 4
PAGE_SIZE = 128
N_MIXERS = 3
RET_HEADS = 4
RET_DK = D_MODEL // RET_HEADS
RET_DV = 2 * D_MODEL // RET_HEADS
RET_QK = RET_HEADS * RET_DK
RET_V = RET_HEADS * RET_DV
ROPE_BASE = 10000.0
POOL_WINDOWS = (2, 4, 8, 16)
POOL_GW = D_MODEL // len(POOL_WINDOWS)
POOL_BUF = max(POOL_WINDOWS) - 1
POOL_HALO = POOL_BUF + 1
FOX_HEADS = 16
FOX_DH = D_MODEL // FOX_HEADS
FFN_DIM = 2816
N_EXPERTS = 8
EPS = 1e-6
NEG_BIG = -1e30

V7X_VMEM_BYTES = 64 * 1024 * 1024
VMEM_LIMIT = V7X_VMEM_BYTES - 8 * 1024 * 1024
LANES = 128
FFN_TF = 256


def _cparams(n_axes):
    return pltpu.CompilerParams(dimension_semantics=("arbitrary",) * n_axes, vmem_limit_bytes=VMEM_LIMIT)


def _sigmoid(x):
    return 1.0 / (1.0 + jnp.exp(-x))


def _modulate(x, shift, scale):
    xf = x.astype(F32)
    ms = jnp.mean(xf * xf, axis=-1, keepdims=True)
    return (xf * lax.rsqrt(ms + EPS)) * (1.0 + scale) + shift


def _row_spec(arr, tm, width, col_fn):
    if arr.shape[1] == 1:
        return pl.BlockSpec((1, 1, width), lambda b, m, *r: (b, 0, col_fn(*r)))
    return pl.BlockSpec((1, tm, width), lambda b, m, *r: (b, m, col_fn(*r)))


def _ada_body(c_ref, w_ref, b_ref, o_ref):
    c = c_ref[...]
    cond = c * _sigmoid(c)
    o_ref[0] = jnp.dot(cond.astype(BF16), w_ref[0].astype(BF16), preferred_element_type=F32) + b_ref[0]


def ada_mods(c_all, ada_w, ada_b):
    rows = c_all.shape[0]
    n_out = ada_w.shape[2]
    tn = 1024
    return pl.pallas_call(
        _ada_body,
        grid=(DEPTH, n_out // tn),
        in_specs=[
            pl.BlockSpec((rows, D_MODEL), lambda i, n: (0, 0)),
            pl.BlockSpec((1, D_MODEL, tn), lambda i, n: (i, 0, n)),
            pl.BlockSpec((1, 1, tn), lambda i, n: (i, 0, n)),
        ],
        out_specs=pl.BlockSpec((1, rows, tn), lambda i, n: (i, 0, n)),
        out_shape=jax.ShapeDtypeStruct((DEPTH, rows, n_out), F32),
        compiler_params=_cparams(2),
        name="ada_mods",
    )(c_all, ada_w, ada_b.reshape(DEPTH, 1, n_out))


def _mm_body(*refs, has_mod, epi, w_t, out_t, scale):
    it = iter(refs)
    x_ref = next(it)
    if has_mod:
        sh_ref, sc_ref = next(it), next(it)
    w_ref = next(it)
    if epi == "rot":
        cos_ref, sin_ref = next(it), next(it)
    elif epi == "res":
        res_ref, gate_ref = next(it), next(it)
    elif epi == "logsig":
        b_ref = next(it)
    o_ref = next(it)
    n = pl.program_id(2)
    if has_mod:
        h_scr = next(it)

        @pl.when(n == 0)
        def _():
            h_scr[...] = _modulate(x_ref[0], sh_ref[0], sc_ref[0]).astype(BF16)

        lhs = h_scr[...]
    else:
        lhs = x_ref[0].astype(BF16)
    w = w_ref[0].astype(BF16)
    last = (((1,), (1,)), ((), ()))
    if not w_t:
        acc = jnp.dot(lhs, w, preferred_element_type=F32)
    elif not out_t:
        acc = lax.dot_general(lhs, w, last, preferred_element_type=F32)
    else:
        acc = lax.dot_general(w, lhs, last, preferred_element_type=F32)
    if epi == "rot":
        acc = _rotary(acc, cos_ref[...], sin_ref[...])
    elif epi == "res":
        acc = res_ref[0] + gate_ref[0] * acc
    elif epi == "logsig":
        z = acc + b_ref[...]
        acc = jnp.minimum(z, 0.0) - jnp.log1p(jnp.exp(-jnp.abs(z)))
    if scale != 1.0:
        acc = acc * scale
    o_ref[0] = acc.astype(o_ref.dtype)


def mm(x3, w3, wl, *, n0, n_out, out_dtype, mod=None, epi="plain", w_t=False, out_t=False, scale=1.0,
       rot=None, res=None, bias=None, tm=None, tn=None, name="mm"):
    bx, L, K = x3.shape
    tm = tm or min(L, 512 if epi == "rot" else 1024)
    tn = tn or min(n_out, 512 if K > 1024 else 1024)
    assert L % tm == 0 and n_out % tn == 0 and n0 % tn == 0
    nb0 = n0 // tn
    has_mod = mod is not None
    in_specs = [pl.BlockSpec((1, tm, K), lambda b, m, n: (b, m, 0))]
    args = [x3]
    if has_mod:
        for a in mod:
            in_specs.append(_row_spec(a, tm, K, lambda n: 0))
            args.append(a)
    if w_t:
        in_specs.append(pl.BlockSpec((1, tn, K), lambda b, m, n: (wl, nb0 + n, 0)))
    else:
        in_specs.append(pl.BlockSpec((1, K, tn), lambda b, m, n: (wl, 0, nb0 + n)))
    args.append(w3)
    if epi == "rot":
        for a in rot:
            in_specs.append(pl.BlockSpec((tm, tn), lambda b, m, n: (m, n)))
            args.append(a)
    elif epi == "res":
        in_specs.append(pl.BlockSpec((1, tm, tn), lambda b, m, n: (b, m, n)))
        in_specs.append(_row_spec(res[1], tm, tn, lambda n: n))
        args.extend(res)
    elif epi == "logsig":
        in_specs.append(pl.BlockSpec(bias.shape, lambda b, m, n: (0, 0)))
        args.append(bias)
    if out_t:
        out_spec = pl.BlockSpec((1, tn, tm), lambda b, m, n: (b, n, m))
        out_shape = jax.ShapeDtypeStruct((bx, n_out, L), out_dtype)
    else:
        out_spec = pl.BlockSpec((1, tm, tn), lambda b, m, n: (b, m, n))
        out_shape = jax.ShapeDtypeStruct((bx, L, n_out), out_dtype)
    return pl.pallas_call(
        functools.partial(_mm_body, has_mod=has_mod, epi=epi, w_t=w_t, out_t=out_t, scale=scale),
        grid=(bx, L // tm, n_out // tn),
        in_specs=in_specs,
        out_specs=out_spec,
        out_shape=out_shape,
        scratch_shapes=[pltpu.VMEM((tm, K), BF16)] if has_mod else [],
        compiler_params=_cparams(3),
        name=name,
    )(*args)


def _rotary(acc, cos, sin_signed):
    width = acc.shape[1]
    lane = lax.broadcasted_iota(jnp.int32, acc.shape, 1)
    partner = jnp.where(lane % 2 == 0, pltpu.roll(acc, width - 1, axis=1), pltpu.roll(acc, 1, axis=1))
    return acc * cos + partner * sin_signed


def _ffn_body(*refs, moe, nf, ne):
    it = iter(refs)
    x_ref, sh_ref, sc_ref, gate_ref = next(it), next(it), next(it), next(it)
    gw_ref = next(it) if moe else None
    wg_ref, wu_ref, wd_ref, o_ref, h_scr, acc_scr = next(it), next(it), next(it), next(it), next(it), next(it)
    tot_scr = next(it) if moe else None
    e = pl.program_id(2)
    f = pl.program_id(3)

    @pl.when((e == 0) & (f == 0))
    def _():
        h_scr[...] = _modulate(x_ref[0], sh_ref[0], sc_ref[0]).astype(BF16)

    @pl.when(f == 0)
    def _():
        acc_scr[...] = jnp.zeros_like(acc_scr)

    if moe:

        @pl.when((e == 0) & (f == 0))
        def _():
            tot_scr[...] = jnp.zeros_like(tot_scr)

    h = h_scr[...]
    a = jnp.dot(h, wg_ref[0].astype(BF16), preferred_element_type=F32)
    b = jnp.dot(h, wu_ref[0].astype(BF16), preferred_element_type=F32)
    mid = (a * _sigmoid(a) * b).astype(BF16)
    acc_scr[...] += jnp.dot(mid, wd_ref[0].astype(BF16), preferred_element_type=F32)

    if not moe:

        @pl.when(f == nf - 1)
        def _():
            o_ref[0] = x_ref[0] + gate_ref[0] * acc_scr[...]

    else:

        @pl.when(f == nf - 1)
        def _():
            gw = gw_ref[0]
            lane = lax.broadcasted_iota(jnp.int32, gw.shape, 1)
            col = jnp.sum(jnp.where(lane == e, gw, 0.0), axis=1, keepdims=True)
            tot_scr[...] += col * acc_scr[...]

            @pl.when(e == ne - 1)
            def _():
                o_ref[0] = x_ref[0] + gate_ref[0] * tot_scr[...]


def ffn(x3, mod, gate, w_gu3, w_down3, wl, *, gates=None, tm=None, name="ffn"):
    bx, L, D = x3.shape
    moe = gates is not None
    ne = N_EXPERTS if moe else 1
    tm = tm or min(L, 1024 if moe else 2048)
    tf = FFN_TF if tm > 256 else MOE_TF
    nf = FFN_DIM // tf
    assert L % tm == 0 and FFN_DIM % tf == 0
    in_specs = [pl.BlockSpec((1, tm, D), lambda b, m, e, f: (b, m, 0))]
    args = [x3]
    for a in (*mod, gate):
        in_specs.append(_row_spec(a, tm, D, lambda e, f: 0))
        args.append(a)
    if moe:
        in_specs.append(pl.BlockSpec((1, tm, LANES), lambda b, m, e, f: (b, m, 0)))
        args.append(gates)
    in_specs += [
        pl.BlockSpec((1, D, tf), lambda b, m, e, f: (wl + e, 0, f)),
        pl.BlockSpec((1, D, tf), lambda b, m, e, f: (wl + e, 0, nf + f)),
        pl.BlockSpec((1, tf, D), lambda b, m, e, f: (wl + e, f, 0)),
    ]
    args += [w_gu3, w_gu3, w_down3]
    scratch = [pltpu.VMEM((tm, D), BF16), pltpu.VMEM((tm, D), F32)]
    if moe:
        scratch.append(pltpu.VMEM((tm, D), F32))
    return pl.pallas_call(
        functools.partial(_ffn_body, moe=moe, nf=nf, ne=ne),
        grid=(bx, L // tm, ne, nf),
        in_specs=in_specs,
        out_specs=pl.BlockSpec((1, tm, D), lambda b, m, e, f: (b, m, 0)),
        out_shape=jax.ShapeDtypeStruct((bx, L, D), F32),
        scratch_shapes=scratch,
        compiler_params=_cparams(4),
        name=name,
    )(*args)


def _router_body(x_ref, sh_ref, sc_ref, w_ref, b_ref, o_ref):
    h = _modulate(x_ref[0], sh_ref[0], sc_ref[0]).astype(BF16)
    logits = lax.dot_general(h, w_ref[0].astype(BF16), (((1,), (1,)), ((), ())), preferred_element_type=F32)
    logits = logits + b_ref[0]
    lane = lax.broadcasted_iota(jnp.int32, logits.shape, 1).astype(F32)
    lg = jnp.where(lane < N_EXPERTS, logits, -jnp.inf)
    m1 = jnp.max(lg, axis=1, keepdims=True)
    i1 = jnp.min(jnp.where(lg == m1, lane, float(LANES)), axis=1, keepdims=True)
    lg2 = jnp.where(lane == i1, -jnp.inf, lg)
    m2 = jnp.max(lg2, axis=1, keepdims=True)
    i2 = jnp.min(jnp.where(lg2 == m2, lane, float(LANES)), axis=1, keepdims=True)
    e2 = jnp.exp(m2 - m1)
    den = 1.0 + e2
    o_ref[0] = jnp.where(lane == i1, 1.0 / den, 0.0) + jnp.where(lane == i2, e2 / den, 0.0)


def router(x3, mod, w_rt_pad, b_pad, wl, *, tm=None):
    bx, L, D = x3.shape
    tm = tm or min(L, 1024)
    in_specs = [pl.BlockSpec((1, tm, D), lambda b, m: (b, m, 0))]
    args = [x3]
    for a in mod:
        in_specs.append(_row_spec(a, tm, D, lambda: 0))
        args.append(a)
    in_specs += [
        pl.BlockSpec((1, LANES, D), lambda b, m: (wl, 0, 0)),
        pl.BlockSpec((1, 1, LANES), lambda b, m: (wl, 0, 0)),
    ]
    args += [w_rt_pad, b_pad]
    return pl.pallas_call(
        _router_body,
        grid=(bx, L // tm),
        in_specs=in_specs,
        out_specs=pl.BlockSpec((1, tm, LANES), lambda b, m: (b, m, 0)),
        out_shape=jax.ShapeDtypeStruct((bx, L, LANES), F32),
        compiler_params=_cparams(2),
        name="router",
    )(*args)


MOE_T = 256
MOE_CH = 128
MOE_TM = 256
MOE_TF = FFN_DIM // 2
MOE_ALIGN = 8
MOE_STAGE = -(-(2 * MOE_T + N_EXPERTS * (MOE_ALIGN - 1) + MOE_CH) // 8) * 8


def _moe_plan(gates2, n_tokens):
    nb = n_tokens // MOE_T
    routed = (gates2[:, :N_EXPERTS] > 0).reshape(nb, MOE_T, N_EXPERTS)
    ri = routed.astype(jnp.int32)
    t_i = jnp.arange(MOE_T)
    before = (t_i[None, :] < t_i[:, None]).astype(F32)
    rank = jnp.einsum("ts,bse->bte", before, routed.astype(F32)).astype(jnp.int32)
    cnt = jnp.sum(ri, axis=1)
    cnt_al = -(-cnt // MOE_ALIGN) * MOE_ALIGN
    lo = jnp.cumsum(cnt_al, axis=1) - cnt_al
    total = jnp.sum(cnt_al, axis=0)
    region = -(-(total + MOE_CH) // MOE_TM) * MOE_TM
    ends = jnp.cumsum(region)
    off = ends - region
    pos = off[None, :] + jnp.cumsum(cnt_al, axis=0) - cnt_al
    nch = -(-cnt_al // MOE_CH)
    m_pad = -(-(2 * n_tokens + nb * N_EXPERTS * (MOE_ALIGN - 1) + N_EXPERTS * (MOE_CH + MOE_TM)) // MOE_TM) * MOE_TM
    n_tiles = m_pad // MOE_TM
    tile_start = jnp.arange(n_tiles, dtype=jnp.int32) * MOE_TM
    tile_e = jnp.minimum(jnp.sum(tile_start[:, None] >= ends[None, :], axis=1), N_EXPERTS - 1).astype(jnp.int32)
    n_used = (ends[-1] // MOE_TM).astype(jnp.int32).reshape(1)
    dest = jnp.where(routed, lo[:, None, :] + rank, -1)
    d_hi = jnp.max(dest, axis=2)
    d_lo = jnp.min(jnp.where(routed, dest, MOE_STAGE), axis=2)
    dd = jnp.stack([d_hi, d_lo], axis=1).astype(jnp.int32)
    eidx = jnp.arange(N_EXPERTS, dtype=jnp.int32)
    e_a = jnp.min(jnp.where(routed, eidx, N_EXPERTS), axis=2)
    e_b = jnp.max(jnp.where(routed, eidx, -1), axis=2)
    g3 = gates2[:, :N_EXPERTS].reshape(nb, MOE_T, N_EXPERTS)
    at = lambda a, e: jnp.sum(jnp.where(eidx == e[..., None], a, 0), axis=2)
    r_a, r_b = at(rank, e_a), at(rank, e_b)
    w_a, w_b = at(g3, e_a), jnp.where(e_b != e_a, at(g3, e_b), 0.0)
    col = lambda e, r, c: jnp.where(r // MOE_CH == c, e * MOE_CH + r % MOE_CH, -1).astype(F32)
    cmeta = jnp.stack([col(e_a, r_a, 0), col(e_b, r_b, 0), col(e_a, r_a, 1), col(e_b, r_b, 1), w_a, w_b], axis=-1)
    cmeta = jnp.pad(cmeta.reshape(n_tokens, 6), ((0, 0), (0, LANES - 6)))
    two = (jnp.max(nch, axis=1) > 1).astype(jnp.int32)
    flat = lambda a: a.reshape(-1).astype(jnp.int32)
    return dict(lo=flat(lo // MOE_ALIGN), pos=flat(pos // MOE_ALIGN), nch=flat(nch), tile_e=tile_e, n_used=n_used,
                dd=dd, cmeta=cmeta, two=two, m_pad=m_pad, n_tiles=n_tiles, nb=nb)


def _dispatch_body(lo_ref, pos_ref, nch_ref, x_ref, sh_ref, sc_ref, dd_ref, xs_in_ref, xs_ref, stage_scr, sem,
                   *, nb):
    del xs_in_ref
    b = pl.program_id(0)
    slot = b % 2
    h = _modulate(x_ref[...], sh_ref[0], sc_ref[0]).astype(BF16)
    r = lax.broadcasted_iota(jnp.int32, (MOE_STAGE, MOE_T), 0)
    dd = dd_ref[0]
    onehot = jnp.where((r == dd[0:1, :]) | (r == dd[1:2, :]), 1.0, 0.0).astype(BF16)
    stage_scr[slot] = jnp.dot(onehot, h, preferred_element_type=F32)

    def seg_copy(blk, e, c):
        src0 = pl.multiple_of(lo_ref[blk * N_EXPERTS + e] * MOE_ALIGN + c * MOE_CH, MOE_ALIGN)
        dst0 = pl.multiple_of(pos_ref[blk * N_EXPERTS + e] * MOE_ALIGN + c * MOE_CH, MOE_ALIGN)
        return pltpu.make_async_copy(stage_scr.at[blk % 2, pl.ds(src0, MOE_CH)], xs_ref.at[pl.ds(dst0, MOE_CH)],
                                     sem.at[blk % 2, e, c])

    def for_segments(blk, fn):
        for e in range(N_EXPERTS):
            for c in range(2):
                @pl.when(c < nch_ref[blk * N_EXPERTS + e])
                def _():
                    fn(seg_copy(blk, e, c))

    @pl.when(b > 0)
    def _():
        for_segments(b - 1, lambda cp: cp.wait())

    for_segments(b, lambda cp: cp.start())

    @pl.when(b == nb - 1)
    def _():
        for_segments(b, lambda cp: cp.wait())


def moe_dispatch(x2, mod, plan, seq_len, buf=None):
    n_tokens, D = x2.shape
    if buf is None:
        buf = jnp.zeros((plan["m_pad"], D), F32)
    per_seq = seq_len // MOE_T
    nb = plan["nb"]
    vec = lambda: pl.BlockSpec((1, 1, D), lambda b, *_: (b // per_seq, 0, 0))
    grid_spec = pltpu.PrefetchScalarGridSpec(
        num_scalar_prefetch=3,
        grid=(nb,),
        in_specs=[
            pl.BlockSpec((MOE_T, D), lambda b, *_: (b, 0)),
            vec(), vec(),
            pl.BlockSpec((1, 2, MOE_T), lambda b, *_: (b, 0, 0)),
            pl.BlockSpec(memory_space=pltpu.MemorySpace.HBM),
        ],
        out_specs=pl.BlockSpec(memory_space=pltpu.MemorySpace.HBM),
        scratch_shapes=[pltpu.VMEM((2, MOE_STAGE, D), F32), pltpu.SemaphoreType.DMA((2, N_EXPERTS, 2))],
    )
    return pl.pallas_call(
        functools.partial(_dispatch_body, nb=nb),
        grid_spec=grid_spec,
        out_shape=jax.ShapeDtypeStruct((plan["m_pad"], D), F32),
        input_output_aliases={7: 0},
        compiler_params=_cparams(1),
        name="moe_dispatch",
    )(plan["lo"], plan["pos"], plan["nch"], x2, mod[0], mod[1], plan["dd"], buf)


def _gffn_body(te_ref, nu_ref, *refs, has_prev):
    it = iter(refs)
    xs_ref = next(it)
    yp_ref = next(it) if has_prev else None
    wg_ref, wu_ref, wd_ref, o_ref, wg_scr, wu_scr, wd_scr = (next(it) for _ in range(7))
    t = pl.program_id(0)
    e_here = te_ref[t]
    e_prev = te_ref[jnp.maximum(t - 1, 0)]

    @pl.when(t < nu_ref[0])
    def _():
        @pl.when((t == 0) | (e_here != e_prev))
        def _():
            wg_scr[...] = wg_ref[0].astype(BF16)
            wu_scr[...] = wu_ref[0].astype(BF16)
            wd_scr[...] = wd_ref[0].astype(BF16)

        h = xs_ref[...].astype(BF16)
        a = jnp.dot(h, wg_scr[...], preferred_element_type=F32)
        b = jnp.dot(h, wu_scr[...], preferred_element_type=F32)
        mid = (a * _sigmoid(a) * b).astype(BF16)
        y = jnp.dot(mid, wd_scr[...], preferred_element_type=F32)
        o_ref[...] = (yp_ref[...] + y) if has_prev else y

    @pl.when(t >= nu_ref[0])
    def _():
        o_ref[...] = jnp.zeros_like(o_ref)


def moe_grouped_ffn(xs, w_gu3, w_down3, wl, plan, f, y_prev=None):
    m_pad, D = xs.shape
    nf = FFN_DIM // MOE_TF
    has_prev = y_prev is not None
    row = lambda t, te, nu: (jnp.minimum(t, nu[0] - 1), 0)
    tile = lambda: pl.BlockSpec((MOE_TM, D), row)
    in_specs = [tile()] + ([tile()] if has_prev else [])
    in_specs += [
        pl.BlockSpec((1, D, MOE_TF), lambda t, te, nu: (wl + te[t], 0, f)),
        pl.BlockSpec((1, D, MOE_TF), lambda t, te, nu: (wl + te[t], 0, nf + f)),
        pl.BlockSpec((1, MOE_TF, D), lambda t, te, nu: (wl + te[t], f, 0), pipeline_mode=pl.Buffered(1)),
    ]
    grid_spec = pltpu.PrefetchScalarGridSpec(
        num_scalar_prefetch=2,
        grid=(plan["n_tiles"],),
        in_specs=in_specs,
        out_specs=pl.BlockSpec((MOE_TM, D), lambda t, te, nu: (t, 0)),
        scratch_shapes=[pltpu.VMEM((D, MOE_TF), BF16), pltpu.VMEM((D, MOE_TF), BF16), pltpu.VMEM((MOE_TF, D), BF16)],
    )
    args = [xs] + ([y_prev] if has_prev else []) + [w_gu3, w_gu3, w_down3]
    return pl.pallas_call(
        functools.partial(_gffn_body, has_prev=has_prev),
        grid_spec=grid_spec,
        out_shape=jax.ShapeDtypeStruct((m_pad, D), F32),
        compiler_params=_cparams(1),
        name=f"moe_grouped_ffn{f}",
    )(plan["tile_e"], plan["n_used"], *args)


def _combine_body(pos_ref, nch_ref, two_ref, x_ref, gate_ref, cm_ref, *refs, final):
    y_refs = refs[:N_EXPERTS]
    refs = refs[N_EXPERTS:]
    y_hbm, fg_ref = (refs[0], refs[1]) if final else (refs[0], None)
    o_ref, yhi_scr, ylo_scr, over_scr, sem = refs[2 if final else 1:]
    b = pl.program_id(0)

    def gathered(window, chunk):
        for e in range(N_EXPERTS):
            y = window(e)
            y_hi = y.astype(BF16)
            yhi_scr[e * MOE_CH:(e + 1) * MOE_CH, :] = y_hi
            ylo_scr[e * MOE_CH:(e + 1) * MOE_CH, :] = (y - y_hi.astype(F32)).astype(BF16)
        cm = cm_ref[...]
        lane = lax.broadcasted_iota(jnp.int32, (MOE_T, N_EXPERTS * MOE_CH), 1).astype(F32)
        picks = [jnp.where(lane == cm[:, 2 * chunk + k:2 * chunk + k + 1], 1.0, 0.0).astype(BF16) for k in range(2)]
        onehot = jnp.concatenate(picks, axis=0)
        rows = (jnp.dot(onehot, yhi_scr[...], preferred_element_type=F32)
                + jnp.dot(onehot, ylo_scr[...], preferred_element_type=F32))
        return cm[:, 4:5] * rows[:MOE_T] + cm[:, 5:6] * rows[MOE_T:]

    o_ref[...] = x_ref[...] + gate_ref[0] * gathered(lambda e: y_refs[e][...], 0)

    @pl.when(two_ref[b] > 0)
    def _():
        for e in range(N_EXPERTS):
            s = b * N_EXPERTS + e

            @pl.when(nch_ref[s] > 1)
            def _():
                start = pl.multiple_of(pos_ref[s] * MOE_ALIGN + MOE_CH, MOE_ALIGN)
                cp = pltpu.make_async_copy(y_hbm.at[pl.ds(start, MOE_CH)], over_scr.at[e], sem.at[e])
                cp.start()
                cp.wait()

            @pl.when(nch_ref[s] <= 1)
            def _():
                over_scr[e] = jnp.zeros((MOE_CH, over_scr.shape[2]), F32)

        o_ref[...] += gate_ref[0] * gathered(lambda e: over_scr[e], 1)

    if final:
        xo = o_ref[...]
        o_ref[...] = xo * lax.rsqrt(jnp.mean(xo * xo, axis=-1, keepdims=True) + EPS) * fg_ref[...]


def moe_combine(x2, gate, y, plan, seq_len, final_g=None):
    n_tokens, D = x2.shape
    per_seq = seq_len // MOE_T
    final = final_g is not None
    tail_specs = [pl.BlockSpec(memory_space=pltpu.MemorySpace.HBM)]
    tail_args = [y]
    if final:
        tail_specs.append(pl.BlockSpec((1, D), lambda b, *_: (0, 0)))
        tail_args.append(final_g.reshape(1, D))

    def window(e):
        return pl.BlockSpec((pl.Element(MOE_CH), pl.Element(D)),
                            lambda b, pos, nch, two: (pos[b * N_EXPERTS + e] * MOE_ALIGN, 0))

    blk = lambda w: pl.BlockSpec((MOE_T, w), lambda b, *_: (b, 0))
    grid_spec = pltpu.PrefetchScalarGridSpec(
        num_scalar_prefetch=3,
        grid=(plan["nb"],),
        in_specs=[blk(D), pl.BlockSpec((1, 1, D), lambda b, *_: (b // per_seq, 0, 0)), blk(LANES)]
        + [window(e) for e in range(N_EXPERTS)] + tail_specs,
        out_specs=blk(D),
        scratch_shapes=[pltpu.VMEM((N_EXPERTS * MOE_CH, D), BF16), pltpu.VMEM((N_EXPERTS * MOE_CH, D), BF16),
                        pltpu.VMEM((N_EXPERTS, MOE_CH, D), F32), pltpu.SemaphoreType.DMA((N_EXPERTS,))],
    )
    return pl.pallas_call(
        functools.partial(_combine_body, final=final),
        grid_spec=grid_spec,
        out_shape=jax.ShapeDtypeStruct((n_tokens, D), F32),
        compiler_params=_cparams(1),
        name="moe_combine",
    )(plan["pos"], plan["nch"], plan["two"], x2, gate, plan["cmeta"], *([y] * N_EXPERTS), *tail_args)


def moe_sparse(x3, mod, gate, gates, w_gu3, w_down3, wl, final_g=None, buf=None):
    B, L, D = x3.shape
    n_tokens = B * L
    assert L % MOE_T == 0 and FFN_DIM % MOE_TF == 0
    x2 = x3.reshape(n_tokens, D)
    gates2 = gates.reshape(n_tokens, LANES)
    plan = _moe_plan(gates2, n_tokens)
    xs = moe_dispatch(x2, mod, plan, L, buf)
    y = None
    for f in range(FFN_DIM // MOE_TF):
        y = moe_grouped_ffn(xs, w_gu3, w_down3, wl, plan, f, y_prev=y)
    return moe_combine(x2, gate, y, plan, L, final_g).reshape(B, L, D), xs


RET_CHUNKS_PER_STEP = 4


def _ret_body(*refs, zero_init, nc, chunk, per_step, heads):
    it = iter(refs)
    q_ref, k_ref, v_ref, g_ref = next(it), next(it), next(it), next(it)
    s0_ref = None if zero_init else next(it)
    inner_ref, qd_ref, kd_ref, cd_ref = next(it), next(it), next(it), next(it)
    o_ref, sout_ref, s_scr = next(it), next(it), next(it)
    c = pl.program_id(2)

    @pl.when(c == 0)
    def _():
        if zero_init:
            s_scr[...] = jnp.zeros_like(s_scr)
        else:
            s_scr[...] = s0_ref[0, 0]

    for hi in range(heads):
        kcols = slice(hi * RET_DK, (hi + 1) * RET_DK)
        vcols = slice(hi * RET_DV, (hi + 1) * RET_DV)
        s = s_scr[hi]
        for ci in range(per_step):
            rows = slice(ci * chunk, (ci + 1) * chunk)
            q = q_ref[0, rows, kcols]
            k = k_ref[0, rows, kcols]
            v = v_ref[0, rows, vcols]
            att = lax.dot_general(q, k, (((1,), (1,)), ((), ())), preferred_element_type=F32) * inner_ref[hi]
            inner = jnp.dot(att.astype(BF16), v, preferred_element_type=F32)
            cross = jnp.dot(q, s.astype(BF16), preferred_element_type=F32) * qd_ref[hi]
            kdt = (k.astype(F32) * kd_ref[hi]).T.astype(BF16)
            s = s * cd_ref[hi] + jnp.dot(kdt, v, preferred_element_type=F32)
            o = inner + cross
            on = o * lax.rsqrt(jnp.mean(o * o, axis=-1, keepdims=True) + EPS)
            g = g_ref[0, rows, vcols]
            o_ref[0, rows, vcols] = (g * _sigmoid(g) * on).astype(o_ref.dtype)
        s_scr[hi] = s

    @pl.when(c == nc - 1)
    def _():
        sout_ref[0] = s_scr[...]


def retention_scan(q, k, v, g, tables, chunk, *, s0=None, s0_layer=0, heads=1):
    B, L, _ = q.shape
    per_step = math.gcd(L // chunk, RET_CHUNKS_PER_STEP)
    rows = per_step * chunk
    nc = L // rows
    inner, qd, kd, cd = tables
    zero_init = s0 is None
    in_specs = [
        pl.BlockSpec((1, rows, heads * RET_DK), lambda h, b, c: (b, c, h)),
        pl.BlockSpec((1, rows, heads * RET_DK), lambda h, b, c: (b, c, h)),
        pl.BlockSpec((1, rows, heads * RET_DV), lambda h, b, c: (b, c, h)),
        pl.BlockSpec((1, rows, heads * RET_DV), lambda h, b, c: (b, c, h)),
    ]
    args = [q, k, v, g]
    if not zero_init:
        in_specs.append(pl.BlockSpec((1, 1, heads, RET_DK, RET_DV), lambda h, b, c: (s0_layer, b, h, 0, 0)))
        args.append(s0)
    in_specs += [
        pl.BlockSpec((heads, chunk, chunk), lambda h, b, c: (h, 0, 0)),
        pl.BlockSpec((heads, chunk, RET_DV), lambda h, b, c: (h, 0, 0)),
        pl.BlockSpec((heads, chunk, RET_DK), lambda h, b, c: (h, 0, 0)),
        pl.BlockSpec((heads, 1, RET_DV), lambda h, b, c: (h, 0, 0)),
    ]
    args += [inner, qd, kd, cd]
    return pl.pallas_call(
        functools.partial(_ret_body, zero_init=zero_init, nc=nc, chunk=chunk, per_step=per_step, heads=heads),
        grid=(RET_HEADS // heads, B, nc),
        in_specs=in_specs,
        out_specs=[
            pl.BlockSpec((1, rows, heads * RET_DV), lambda h, b, c: (b, c, h)),
            pl.BlockSpec((1, heads, RET_DK, RET_DV), lambda h, b, c: (b, h, 0, 0)),
        ],
        out_shape=[
            jax.ShapeDtypeStruct((B, L, RET_V), BF16),
            jax.ShapeDtypeStruct((B, RET_HEADS, RET_DK, RET_DV), F32),
        ],
        scratch_shapes=[pltpu.VMEM((heads, RET_DK, RET_DV), F32)],
        compiler_params=_cparams(3),
        name="retention_scan",
    )(*args)


def retention_tables(n_real, n_pad):
    log_gamma = jnp.log1p(-(2.0 ** (-5.0 - jnp.arange(RET_HEADS, dtype=F32))))
    idx = jnp.arange(n_pad, dtype=F32)
    valid = idx < n_real
    diff = idx[:, None] - idx[None, :]
    ok = (diff >= 0) & valid[:, None] & valid[None, :]
    inner = jnp.where(ok[None], jnp.exp(log_gamma[:, None, None] * jnp.maximum(diff, 0.0)[None]), 0.0)
    qd = jnp.exp(log_gamma[:, None] * (idx[None, :] + 1.0))
    kd = jnp.where(valid[None, :], jnp.exp(log_gamma[:, None] * (n_real - 1.0 - idx[None, :])), 0.0)
    cd = jnp.exp(log_gamma * n_real)
    return (inner,
            jnp.broadcast_to(qd[:, :, None], (RET_HEADS, n_pad, RET_DV)),
            jnp.broadcast_to(kd[:, :, None], (RET_HEADS, n_pad, RET_DK)),
            jnp.broadcast_to(cd[:, None, None], (RET_HEADS, 1, RET_DV)))


def rotary_tables(pos, reps):
    inv_freq = ROPE_BASE ** (-jnp.arange(0, RET_DK, 2, dtype=F32) / RET_DK)
    ang = pos.astype(F32)[:, None] * inv_freq[None, :]
    cos = jnp.repeat(jnp.cos(ang), 2, axis=1)
    sin = jnp.sin(ang)
    sin_signed = jnp.stack([-sin, sin], axis=-1).reshape(ang.shape[0], RET_DK)
    return jnp.tile(cos, (1, reps)), jnp.tile(sin_signed, (1, reps))


def _pool_body(x_ref, xp_ref, buf_ref, sh_ref, sc_ref, gate_ref, w_ref, cs_ref, o_ref, tail_ref, ext_scr,
               *, tm, pos0, has_prev):
    m = pl.program_id(1)
    sh, sc = sh_ref[0], sc_ref[0]
    h = _modulate(x_ref[0], sh, sc)

    @pl.when(m == 0)
    def _():
        ext_scr[0:POOL_HALO, :] = buf_ref[0]

    if has_prev:

        @pl.when(m > 0)
        def _():
            ext_scr[0:POOL_HALO, :] = _modulate(xp_ref[0], sh, sc)

    ext_scr[POOL_HALO:POOL_HALO + tm, :] = h
    tail_ref[0] = ext_scr[tm:tm + POOL_HALO, :]
    row = lax.broadcasted_iota(jnp.int32, (tm, 1), 0)
    pos1 = (pos0 + m * tm + row + 1).astype(F32)
    rows = max(tm, 16)
    ys = []
    for gi, w in enumerate(POOL_WINDOWS):
        c0, c1 = gi * POOL_GW, (gi + 1) * POOL_GW
        win = ext_scr[POOL_HALO:POOL_HALO + tm, c0:c1]
        for j in range(1, w):
            win = win + ext_scr[POOL_HALO - j:POOL_HALO - j + tm, c0:c1]
        d = win / jnp.minimum(jnp.float32(w), pos1) - h[:, c0:c1]
        if rows != tm:
            d = jnp.concatenate([d, jnp.zeros((rows - tm, POOL_GW), F32)], axis=0)
        y = jnp.dot(d.astype(BF16), w_ref[0, gi].astype(BF16), preferred_element_type=F32)
        ys.append(y[0:tm])
    y = jnp.concatenate(ys, axis=1) * cs_ref[...]
    o_ref[0] = x_ref[0] + gate_ref[0] * y


def pool_layer(x3, buf16, mod, gate, pool_w, pool_scale, wl, pos0, *, tm):
    B, L, D = x3.shape
    assert L % tm == 0 and (L == tm or tm % POOL_HALO == 0)
    has_prev = L > tm
    ph = POOL_HALO if has_prev else min(L, POOL_HALO)
    per = tm // POOL_HALO if has_prev else 1
    vec = lambda: pl.BlockSpec((1, 1, D), lambda b, m: (b, 0, 0))
    return pl.pallas_call(
        functools.partial(_pool_body, tm=tm, pos0=pos0, has_prev=has_prev),
        grid=(B, L // tm),
        in_specs=[
            pl.BlockSpec((1, tm, D), lambda b, m: (b, m, 0)),
            pl.BlockSpec((1, ph, D), lambda b, m: (b, jnp.maximum(m * per - 1, 0), 0)),
            pl.BlockSpec((1, POOL_HALO, D), lambda b, m: (b, 0, 0)),
            vec(), vec(), vec(),
            pl.BlockSpec((1,) + pool_w.shape[1:], lambda b, m: (wl, 0, 0, 0)),
            pl.BlockSpec((1, D), lambda b, m: (wl, 0)),
        ],
        out_specs=[
            pl.BlockSpec((1, tm, D), lambda b, m: (b, m, 0)),
            pl.BlockSpec((1, POOL_HALO, D), lambda b, m: (b, 0, 0)),
        ],
        out_shape=[
            jax.ShapeDtypeStruct((B, L, D), F32),
            jax.ShapeDtypeStruct((B, POOL_HALO, D), F32),
        ],
        scratch_shapes=[pltpu.VMEM((tm + POOL_HALO, D), F32)],
        compiler_params=_cparams(2),
        name="pool_layer",
    )(x3, x3, buf16, mod[0], mod[1], gate, pool_w, pool_scale)


def _split3(x):
    p0 = x.astype(BF16)
    r1 = x - p0.astype(F32)
    p1 = r1.astype(BF16)
    p2 = (r1 - p1.astype(F32)).astype(BF16)
    return p0, p1, p2


def _lane_cumsum(x, tri):
    p0, p1, p2 = _split3(x)
    dot = lambda p: jnp.dot(p, tri, preferred_element_type=F32)
    return (dot(p0) + dot(p1)) + dot(p2)


def _upper_tri(t):
    r = lax.broadcasted_iota(jnp.int32, (t, t), 0)
    c = lax.broadcasted_iota(jnp.int32, (t, t), 1)
    return jnp.where(r <= c, 1.0, 0.0).astype(BF16)


BIAS_PIECES = 3


def _fbias_body(lf_ref, o_ref, carry_scr, *, tc):
    @pl.when(pl.program_id(1) == 0)
    def _():
        carry_scr[...] = jnp.zeros_like(carry_scr)

    r = lax.broadcasted_iota(jnp.int32, (tc, tc), 0)
    c = lax.broadcasted_iota(jnp.int32, (tc, tc), 1)
    tril = jnp.where(c <= r, 1.0, 0.0).astype(BF16)
    p0, p1, p2 = _split3(lf_ref[0])
    dot = lambda p: jnp.dot(tril, p, preferred_element_type=F32)
    f = carry_scr[...] + ((dot(p0) + dot(p1)) + dot(p2))
    carry_scr[...] = f[tc - 1:tc, :]
    head = lax.broadcasted_iota(jnp.int32, (FOX_HEADS, LANES), 0)
    lane = lax.broadcasted_iota(jnp.int32, (FOX_HEADS, LANES), 1)
    out = None
    for p, piece in enumerate(_split3(-f)):
        place = jnp.where(lane == BIAS_PIECES * head + p, 1.0, 0.0).astype(BF16)
        term = jnp.dot(piece, place, preferred_element_type=F32)
        out = term if out is None else out + term
    o_ref[0] = out.astype(BF16)


def fox_bias_features(lf, *, tc=512):
    B, L, H = lf.shape
    tc = min(tc, L)
    return pl.pallas_call(
        functools.partial(_fbias_body, tc=tc),
        grid=(B, L // tc),
        in_specs=[pl.BlockSpec((1, tc, H), lambda b, c: (b, c, 0))],
        out_specs=pl.BlockSpec((1, tc, LANES), lambda b, c: (b, c, 0)),
        out_shape=jax.ShapeDtypeStruct((B, L, LANES), BF16),
        scratch_shapes=[pltpu.VMEM((1, H), F32)],
        compiler_params=_cparams(2),
        name="fox_bias_features",
    )(lf)


DEN_ROWS = 16


def _flash_body(qt_ref, k_ref, fb_ref, vt_ref, o_ref, *, tq, tk):
    hp = pl.program_id(1)
    qi = pl.program_id(2)
    pair = 2 * FOX_DH
    row = lax.broadcasted_iota(jnp.int32, (pair, tq), 0)
    qt = qt_ref[0]
    qaug = []
    for i in range(2):
        q_head = jnp.where(row // FOX_DH == i, qt, jnp.zeros_like(qt))
        pick = jnp.where(row // BIAS_PIECES == 2 * hp + i, 1.0, 0.0).astype(BF16)
        qaug.append(jnp.concatenate([q_head, pick], axis=0))
    key_i = lax.broadcasted_iota(jnp.int32, (tk, tq), 0)
    qry_i = lax.broadcasted_iota(jnp.int32, (tk, tq), 1)
    per_q = tq // tk

    def step(j, carry, diag):
        k0 = pl.multiple_of(j * tk, tk)
        kaug = jnp.concatenate([k_ref[0, pl.ds(k0, tk), :], fb_ref[0, pl.ds(k0, tk), :]], axis=1)
        ones = jnp.ones((DEN_ROWS, tk), BF16)
        new = []
        for i in range(2):
            m_old, acc = carry[i]
            st = jnp.dot(kaug, qaug[i], preferred_element_type=F32)
            if diag is not None:
                st = jnp.where(key_i + diag * tk <= qry_i, st, NEG_BIG)
            m_new = jnp.maximum(m_old, jnp.max(st, axis=0, keepdims=True))
            alpha = jnp.exp(m_old - m_new)
            p = jnp.exp(st - m_new).astype(BF16)
            vt = jnp.concatenate([vt_ref[0, i * FOX_DH:(i + 1) * FOX_DH, pl.ds(k0, tk)].astype(BF16), ones], axis=0)
            acc = alpha * acc + jnp.dot(vt, p, preferred_element_type=F32)
            new.append((m_new, acc))
        return tuple(new)

    init = tuple((jnp.full((1, tq), NEG_BIG, F32), jnp.zeros((FOX_DH + DEN_ROWS, tq), F32)) for _ in range(2))
    carry = lax.fori_loop(0, qi * per_q, lambda j, c: step(j, c, None), init)
    for d in range(per_q):
        carry = step(qi * per_q + d, carry, d)
    o_ref[0] = jnp.concatenate([(acc[:FOX_DH] / acc[FOX_DH:FOX_DH + 1]).T for _, acc in carry],
                               axis=1).astype(o_ref.dtype)


def fox_flash(qt, k, fb, vt, *, tq=1024, tk=1024):
    B, L, D = k.shape
    tq = min(tq, L)
    tk = min(tk, tq)
    assert L % tq == 0 and tq % tk == 0
    pair = 2 * FOX_DH
    return pl.pallas_call(
        functools.partial(_flash_body, tq=tq, tk=tk),
        grid=(B, FOX_HEADS // 2, L // tq),
        in_specs=[
            pl.BlockSpec((1, pair, tq), lambda b, hp, qi: (b, hp, qi)),
            pl.BlockSpec((1, L, pair), lambda b, hp, qi: (b, 0, hp)),
            pl.BlockSpec((1, L, LANES), lambda b, hp, qi: (b, 0, 0)),
            pl.BlockSpec((1, pair, L), lambda b, hp, qi: (b, hp, 0)),
        ],
        out_specs=pl.BlockSpec((1, tq, pair), lambda b, hp, qi: (b, qi, hp)),
        out_shape=jax.ShapeDtypeStruct((B, L, D), BF16),
        compiler_params=_cparams(3),
        name="fox_flash",
    )(qt, k, fb, vt)


MAX_DECODE_PAGES = 16


def _decode_body(pt_ref, q_ref, kn_ref, vn_ref, lfn_ref, *refs, n_steps, lq, pages):
    kc = refs[0:pages]
    vc = refs[pages:2 * pages]
    lc = refs[2 * pages:3 * pages]
    o_ref, qbd_scr, m_scr, l_scr, acc_scr, carry_scr = refs[3 * pages:]
    b = pl.program_id(0)
    st = pl.program_id(1)
    rows = lq * FOX_HEADS
    last = (((1,), (1,)), ((), ()))
    tri = _upper_tri(PAGE_SIZE)

    @pl.when(st == 0)
    def _():
        head = lax.broadcasted_iota(jnp.int32, (FOX_HEADS, D_MODEL), 0)
        lane_head = lax.broadcasted_iota(jnp.int32, (FOX_HEADS, D_MODEL), 1) // FOX_DH
        blocks = [jnp.where(head == lane_head, jnp.broadcast_to(q_ref[0, t:t + 1, :], (FOX_HEADS, D_MODEL)), 0.0)
                  for t in range(lq)]
        qbd_scr[...] = jnp.concatenate(blocks, axis=0).astype(BF16)
        m_scr[...] = jnp.full_like(m_scr, NEG_BIG)
        l_scr[...] = jnp.zeros_like(l_scr)
        acc_scr[...] = jnp.zeros_like(acc_scr)
        carry_scr[...] = jnp.zeros_like(carry_scr)

    def absorb(s, v_mat, v_is_t):
        m_old = m_scr[...]
        m_new = jnp.maximum(m_old, jnp.max(s, axis=1, keepdims=True))
        alpha = jnp.exp(m_old - m_new)
        p = jnp.exp(s - m_new)
        l_scr[...] = alpha * l_scr[...] + jnp.sum(p, axis=1, keepdims=True)
        if v_is_t:
            pv = lax.dot_general(p.astype(BF16), v_mat, last, preferred_element_type=F32)
        else:
            pv = jnp.dot(p.astype(BF16), v_mat, preferred_element_type=F32)
        acc_scr[...] = alpha * acc_scr[...] + pv
        m_scr[...] = m_new

    @pl.when(st < n_steps)
    def _():
        kt = jnp.concatenate([kc[i][0].reshape(D_MODEL, PAGE_SIZE).astype(BF16) for i in range(pages)], axis=1)
        vt = jnp.concatenate([vc[i][0].reshape(D_MODEL, PAGE_SIZE).astype(BF16) for i in range(pages)], axis=1)
        within = _lane_cumsum(jnp.concatenate([lc[i][0] for i in range(pages)], axis=0), tri)
        f = carry_scr[...]
        biases = []
        for i in range(pages):
            f_page = f + within[i * FOX_HEADS:(i + 1) * FOX_HEADS, :]
            biases.append(jnp.tile(f_page, (lq, 1)))
            f = jnp.broadcast_to(f_page[:, PAGE_SIZE - 1:PAGE_SIZE], f_page.shape)
        carry_scr[...] = f
        s = jnp.dot(qbd_scr[...], kt, preferred_element_type=F32) - jnp.concatenate(biases, axis=1)
        absorb(s, vt, True)

    @pl.when(st == n_steps)
    def _():
        pad = jnp.zeros((PAGE_SIZE - lq, D_MODEL), F32)
        kn = jnp.concatenate([kn_ref[0], pad], axis=0).astype(BF16)
        vn = jnp.concatenate([vn_ref[0], pad], axis=0).astype(BF16)
        n_tok = lfn_ref.shape[1]
        tok = lax.broadcasted_iota(jnp.int32, (n_tok, PAGE_SIZE), 0)
        key = lax.broadcasted_iota(jnp.int32, (n_tok, PAGE_SIZE), 1)
        sel = jnp.where((tok // lq == b) & (tok % lq <= key) & (key < lq), 1.0, 0.0).astype(BF16)
        p0, p1, p2 = _split3(lfn_ref[...])
        dot = lambda p: jnp.dot(p, sel, preferred_element_type=F32)
        f = carry_scr[...] + ((dot(p0) + dot(p1)) + dot(p2))
        s = lax.dot_general(qbd_scr[...], kn, last, preferred_element_type=F32) - jnp.tile(f, (lq, 1))
        rq = lax.broadcasted_iota(jnp.int32, (rows, PAGE_SIZE), 0) // FOX_HEADS
        kk = lax.broadcasted_iota(jnp.int32, (rows, PAGE_SIZE), 1)
        s = jnp.where(kk <= rq, s, NEG_BIG)
        absorb(s, vn, False)
        o = acc_scr[...] / l_scr[...]
        head = lax.broadcasted_iota(jnp.int32, (FOX_HEADS, D_MODEL), 0)
        lane_head = lax.broadcasted_iota(jnp.int32, (FOX_HEADS, D_MODEL), 1) // FOX_DH
        outs = [jnp.sum(jnp.where(head == lane_head, o[t * FOX_HEADS:(t + 1) * FOX_HEADS, :], 0.0), axis=0,
                        keepdims=True) for t in range(lq)]
        o_ref[0] = jnp.concatenate(outs, axis=0).astype(o_ref.dtype)


def fox_decode(q, k_new, v_new, lft_new, cache_kt, cache_vt, cache_lt, page_table):
    B, lq, D = q.shape
    n_pages = page_table.shape[1]
    pages = math.gcd(n_pages, MAX_DECODE_PAGES)
    n_steps = n_pages // pages
    rows = lq * FOX_HEADS

    def page_idx(i):
        return lambda b, s, pt: (pt[b * n_pages + jnp.minimum(s, n_steps - 1) * pages + i], 0, 0, 0)

    def page_idx3(i):
        return lambda b, s, pt: (pt[b * n_pages + jnp.minimum(s, n_steps - 1) * pages + i], 0, 0)

    seq = lambda: pl.BlockSpec((1, lq, D), lambda b, s, pt: (b, 0, 0))
    in_specs = [seq(), seq(), seq(), pl.BlockSpec(lft_new.shape, lambda b, s, pt: (0, 0))]
    in_specs += [pl.BlockSpec((1, FOX_HEADS, FOX_DH, PAGE_SIZE), page_idx(i)) for i in range(pages)]
    in_specs += [pl.BlockSpec((1, FOX_HEADS, FOX_DH, PAGE_SIZE), page_idx(i)) for i in range(pages)]
    in_specs += [pl.BlockSpec((1, FOX_HEADS, PAGE_SIZE), page_idx3(i)) for i in range(pages)]
    grid_spec = pltpu.PrefetchScalarGridSpec(
        num_scalar_prefetch=1,
        grid=(B, n_steps + 1),
        in_specs=in_specs,
        out_specs=pl.BlockSpec((1, lq, D), lambda b, s, pt: (b, 0, 0)),
        scratch_shapes=[
            pltpu.VMEM((rows, D), BF16),
            pltpu.VMEM((rows, 1), F32),
            pltpu.VMEM((rows, 1), F32),
            pltpu.VMEM((rows, D), F32),
            pltpu.VMEM((FOX_HEADS, PAGE_SIZE), F32),
        ],
    )
    return pl.pallas_call(
        functools.partial(_decode_body, n_steps=n_steps, lq=lq, pages=pages),
        grid_spec=grid_spec,
        out_shape=jax.ShapeDtypeStruct((B, lq, D), F32),
        compiler_params=_cparams(2),
        name="fox_decode",
    )(page_table.reshape(-1), q, k_new, v_new, lft_new,
      *([cache_kt] * pages), *([cache_vt] * pages), *([cache_lt] * pages))


def _final_body(x_ref, g_ref, o_ref):
    xf = x_ref[0]
    o_ref[0] = xf * lax.rsqrt(jnp.mean(xf * xf, axis=-1, keepdims=True) + EPS) * g_ref[...]


def final_norm(x3, final_g, *, tm=None):
    bx, L, D = x3.shape
    tm = tm or min(L, 1024)
    return pl.pallas_call(
        _final_body,
        grid=(bx, L // tm),
        in_specs=[pl.BlockSpec((1, tm, D), lambda b, m: (b, m, 0)), pl.BlockSpec((1, D), lambda b, m: (0, 0))],
        out_specs=pl.BlockSpec((1, tm, D), lambda b, m: (b, m, 0)),
        out_shape=jax.ShapeDtypeStruct((bx, L, D), F32),
        compiler_params=_cparams(2),
        name="final_norm",
    )(x3, final_g.reshape(1, D))


def _trunk(x, mods, ret_state, pool_state, fox_past, pos0, params):
    (ret_w_in, ret_w_out, pool_w, pool_scale, fox_wt, fox_b_f, fox_w_out, ffn_w_gu, ffn_w_down,
     w_rt_pad, b_rt_pad, moe_w_gu, moe_w_down, final_g) = params
    B, L, D = x.shape
    decode = fox_past is not None
    if decode:
        x3 = x.reshape(1, B * L, D)
        expand = lambda v: jnp.repeat(v, L, axis=0)[None]
    else:
        x3 = x
        expand = lambda v: v[:, None, :]
    n_rows = x3.shape[1]
    pos = pos0 + jnp.arange(L)
    cos, sin = rotary_tables(pos, RET_HEADS)
    if decode:
        cos, sin = jnp.tile(cos, (B, 1)), jnp.tile(sin, (B, 1))
    chunk = min(L, 256)
    chunk_pad = max(chunk, PAGE_SIZE)
    tables = retention_tables(chunk, chunk_pad)
    ret_new, extras = [], {}
    moe_buf = None
    for i in range(DEPTH):
        sh_a, sc_a, g_a, sh_f, sc_f, g_f = [expand(v) for v in jnp.split(mods[i], 6, axis=-1)]
        kind, j = i % N_MIXERS, i // N_MIXERS
        if kind == 0:
            proj = functools.partial(mm, x3, ret_w_in, j, mod=(sh_a, sc_a))
            q = proj(n0=0, n_out=RET_QK, out_dtype=BF16, epi="rot", rot=(cos, sin), name="ret_q")
            k = proj(n0=RET_QK, n_out=RET_QK, out_dtype=BF16, epi="rot", rot=(cos, sin), scale=RET_DK ** -0.5,
                     name="ret_k")
            v = proj(n0=2 * RET_QK, n_out=RET_V, out_dtype=BF16, name="ret_v")
            g = proj(n0=2 * RET_QK + RET_V, n_out=RET_V, out_dtype=F32, name="ret_g")
            if decode:
                padr = lambda t: jnp.pad(t.reshape(B, L, -1), ((0, 0), (0, chunk_pad - L), (0, 0)))
                o, s = retention_scan(padr(q), padr(k), padr(v), padr(g), tables, chunk_pad, s0=ret_state,
                                      s0_layer=j, heads=RET_HEADS)
                o = o[:, :L].reshape(1, n_rows, RET_V)
            else:
                o, s = retention_scan(q, k, v, g, tables, chunk_pad)
            ret_new.append(s)
            x3 = mm(o, ret_w_out, j, n0=0, n_out=D, out_dtype=F32, epi="res", res=(x3, g_a), name="ret_out")
        elif kind == 1:
            vecs = [v[:, None, :] for v in jnp.split(mods[i], 6, axis=-1)[:3]]
            if decode:
                buf16 = jnp.pad(pool_state[j], ((0, 0), (1, 0), (0, 0)))
                tm = L
            else:
                buf16 = jnp.zeros((B, POOL_HALO, D), F32)
                tm = min(L, 512)
            xn, tail = pool_layer(x3.reshape(B, L, D), buf16, (vecs[0], vecs[1]), vecs[2], pool_w, pool_scale, j,
                                  pos0, tm=tm)
            x3 = xn.reshape(x3.shape)
            extras["pool"] = tail[:, 1:, :]
        else:
            fproj = functools.partial(mm, x3, fox_wt, j, mod=(sh_a, sc_a), w_t=True)
            if decode:
                ck, cv, cl, pt = fox_past
                q = fproj(n0=0, n_out=D, out_dtype=F32, scale=FOX_DH ** -0.5, name="fox_q")
                lft = fproj(n0=3 * D, n_out=FOX_HEADS, out_dtype=F32, out_t=True, epi="logsig",
                            bias=fox_b_f[j].reshape(FOX_HEADS, 1), name="fox_logft")
                k = fproj(n0=D, n_out=D, out_dtype=F32, name="fox_k")
                v = fproj(n0=2 * D, n_out=D, out_dtype=F32, name="fox_v")
                o = fox_decode(q.reshape(B, L, D), k.reshape(B, L, D), v.reshape(B, L, D), lft[0],
                               jnp.transpose(ck[j], (0, 2, 3, 1)), jnp.transpose(cv[j], (0, 2, 3, 1)),
                               jnp.transpose(cl[j], (0, 2, 1)), pt)
                o = o.reshape(1, n_rows, D)
                extras["k"] = k.reshape(B, L, FOX_HEADS, FOX_DH)
                extras["v"] = v.reshape(B, L, FOX_HEADS, FOX_DH)
                extras["l"] = jnp.transpose(lft[0].reshape(FOX_HEADS, B, L), (1, 2, 0))
            else:
                qt = fproj(n0=0, n_out=D, out_dtype=BF16, out_t=True, scale=FOX_DH ** -0.5, name="fox_qt")
                kb = fproj(n0=D, n_out=D, out_dtype=BF16, name="fox_kb")
                kt = fproj(n0=D, n_out=D, out_dtype=F32, out_t=True, name="fox_kt")
                vt = fproj(n0=2 * D, n_out=D, out_dtype=F32, out_t=True, name="fox_vt")
                lf = fproj(n0=3 * D, n_out=FOX_HEADS, out_dtype=F32, epi="logsig",
                           bias=fox_b_f[j].reshape(1, FOX_HEADS), name="fox_logf")
                o = fox_flash(qt, kb, fox_bias_features(lf), vt)
                unt = lambda t: jnp.transpose(t.reshape(B, FOX_HEADS, FOX_DH, L), (0, 3, 1, 2))
                extras["k"], extras["v"] = unt(kt), unt(vt)
                extras["l"] = lf
            x3 = mm(o, fox_w_out, j, n0=0, n_out=D, out_dtype=F32, epi="res", res=(x3, g_a), name="fox_out")
        ml = i // 2
        if i % 2 == 0:
            x3 = ffn(x3, (sh_f, sc_f), g_f, ffn_w_gu, ffn_w_down, ml, name="ffn_dense")
        else:
            gates = router(x3, (sh_f, sc_f), w_rt_pad, b_rt_pad, ml)
            if decode:
                x3 = ffn(x3, (sh_f, sc_f), g_f, moe_w_gu, moe_w_down, ml * N_EXPERTS, gates=gates, name="ffn_moe")
            else:
                closing = final_g if i == DEPTH - 1 else None
                x3, moe_buf = moe_sparse(x3, (sh_f, sc_f), g_f, gates, moe_w_gu, moe_w_down, ml * N_EXPERTS, closing,
                                         moe_buf)
    fused_final = (not decode) and (DEPTH - 1) % 2 == 1
    out = (x3 if fused_final else final_norm(x3, final_g)).reshape(B, L, D)
    return (out, jnp.stack(ret_new), extras["pool"][None], extras["k"][None], extras["v"][None], extras["l"][None])


def kernel(x_prompt, x_sample, state_ret, state_pool, cache_fox_k, cache_fox_v, cache_fox_logf, page_table,
           c_prompt, c_sample, ada_w, ada_b, ret_w_in, ret_w_out, pool_w, pool_scale, fox_w_in, fox_b_f, fox_w_out,
           ffn_w_gu, ffn_w_down, moe_w_router, moe_b_router, moe_w_gu, moe_w_down, final_g):
    bp, bs = x_prompt.shape[0], x_sample.shape[0]
    rows = -(-(bp + bs) // 8) * 8
    c_all = jnp.concatenate([c_prompt, c_sample, jnp.zeros((rows - bp - bs, D_MODEL), F32)], axis=0)
    mods = ada_mods(c_all, ada_w, ada_b)
    n_moe = moe_w_router.shape[0]
    params = (
        ret_w_in, ret_w_out, pool_w, pool_scale,
        jnp.swapaxes(fox_w_in, 1, 2),
        fox_b_f, fox_w_out, ffn_w_gu, ffn_w_down,
        jnp.pad(jnp.swapaxes(moe_w_router, 1, 2), ((0, 0), (0, LANES - N_EXPERTS), (0, 0))),
        jnp.pad(moe_b_router, ((0, 0), (0, LANES - N_EXPERTS))).reshape(n_moe, 1, LANES),
        moe_w_gu.reshape((n_moe * N_EXPERTS,) + moe_w_gu.shape[2:]),
        moe_w_down.reshape((n_moe * N_EXPERTS,) + moe_w_down.shape[2:]),
        final_g,
    )
    y_p, ret_p, pool_p, k_p, v_p, l_p = _trunk(x_prompt, mods[:, :bp], None, None, None, 0, params)
    n_past = page_table.shape[1] * PAGE_SIZE
    y_s, ret_s, pool_s, k_s, v_s, l_s = _trunk(
        x_sample, mods[:, bp:bp + bs], state_ret, state_pool,
        (cache_fox_k, cache_fox_v, cache_fox_logf, page_table), n_past, params)
    return (y_p, y_s, ret_p, ret_s, pool_p, pool_s, k_p, k_s, v_p, v_s, l_p, l_s)
```

```python
import functools
import math

import jax
import jax.numpy as jnp
from jax import lax
from jax.experimental import pallas as pl
from jax.experimental.pallas import tpu as pltpu

F32 = jnp.float32
BF16 = jnp.bfloat16

D_MODEL = 1024
DEPTH = 4
PAGE_SIZE = 128
N_MIXERS = 3
RET_HEADS = 4
RET_DK = D_MODEL // RET_HEADS
RET_DV = 2 * D_MODEL // RET_HEADS
RET_QK = RET_HEADS * RET_DK
RET_V = RET_HEADS * RET_DV
ROPE_BASE = 10000.0
POOL_WINDOWS = (2, 4, 8, 16)
POOL_GW = D_MODEL // len(POOL_WINDOWS)
POOL_BUF = max(POOL_WINDOWS) - 1
POOL_HALO = POOL_BUF + 1
FOX_HEADS = 16
FOX_DH = D_MODEL // FOX_HEADS
FFN_DIM = 2816
N_EXPERTS = 8
EPS = 1e-6
NEG_BIG = -1e30

V7X_VMEM_BYTES = 64 * 1024 * 1024
VMEM_LIMIT = V7X_VMEM_BYTES - 8 * 1024 * 1024
LANES = 128
FFN_TF = 256


def _cparams(n_axes):
    return pltpu.CompilerParams(dimension_semantics=("arbitrary",) * n_axes, vmem_limit_bytes=VMEM_LIMIT)


def _sigmoid(x):
    return 1.0 / (1.0 + jnp.exp(-x))


def _modulate(x, shift, scale):
    xf = x.astype(F32)
    ms = jnp.mean(xf * xf, axis=-1, keepdims=True)
    return (xf * lax.rsqrt(ms + EPS)) * (1.0 + scale) + shift


def _row_spec(arr, tm, width, col_fn):
    if arr.shape[1] == 1:
        return pl.BlockSpec((1, 1, width), lambda b, m, *r: (b, 0, col_fn(*r)))
    return pl.BlockSpec((1, tm, width), lambda b, m, *r: (b, m, col_fn(*r)))


def _ada_body(c_ref, w_ref, b_ref, o_ref):
    c = c_ref[...]
    cond = c * _sigmoid(c)
    o_ref[0] = jnp.dot(cond.astype(BF16), w_ref[0].astype(BF16), preferred_element_type=F32) + b_ref[0]


def ada_mods(c_all, ada_w, ada_b):
    rows = c_all.shape[0]
    n_out = ada_w.shape[2]
    tn = 1024
    return pl.pallas_call(
        _ada_body,
        grid=(DEPTH, n_out // tn),
        in_specs=[
            pl.BlockSpec((rows, D_MODEL), lambda i, n: (0, 0)),
            pl.BlockSpec((1, D_MODEL, tn), lambda i, n: (i, 0, n)),
            pl.BlockSpec((1, 1, tn), lambda i, n: (i, 0, n)),
        ],
        out_specs=pl.BlockSpec((1, rows, tn), lambda i, n: (i, 0, n)),
        out_shape=jax.ShapeDtypeStruct((DEPTH, rows, n_out), F32),
        compiler_params=_cparams(2),
        name="ada_mods",
    )(c_all, ada_w, ada_b.reshape(DEPTH, 1, n_out))


def _mm_body(*refs, has_mod, epi, w_t, out_t, scale):
    it = iter(refs)
    x_ref = next(it)
    if has_mod:
        sh_ref, sc_ref = next(it), next(it)
    w_ref = next(it)
    if epi == "rot":
        cos_ref, sin_ref = next(it), next(it)
    elif epi == "res":
        res_ref, gate_ref = next(it), next(it)
    elif epi == "logsig":
        b_ref = next(it)
    o_ref = next(it)
    n = pl.program_id(2)
    if has_mod:
        h_scr = next(it)

        @pl.when(n == 0)
        def _():
            h_scr[...] = _modulate(x_ref[0], sh_ref[0], sc_ref[0]).astype(BF16)

        lhs = h_scr[...]
    else:
        lhs = x_ref[0].astype(BF16)
    w = w_ref[0].astype(BF16)
    last = (((1,), (1,)), ((), ()))
    if not w_t:
        acc = jnp.dot(lhs, w, preferred_element_type=F32)
    elif not out_t:
        acc = lax.dot_general(lhs, w, last, preferred_element_type=F32)
    else:
        acc = lax.dot_general(w, lhs, last, preferred_element_type=F32)
    if epi == "rot":
        acc = _rotary(acc, cos_ref[...], sin_ref[...])
    elif epi == "res":
        acc = res_ref[0] + gate_ref[0] * acc
    elif epi == "logsig":
        z = acc + b_ref[...]
        acc = jnp.minimum(z, 0.0) - jnp.log1p(jnp.exp(-jnp.abs(z)))
    if scale != 1.0:
        acc = acc * scale
    o_ref[0] = acc.astype(o_ref.dtype)


def mm(x3, w3, wl, *, n0, n_out, out_dtype, mod=None, epi="plain", w_t=False, out_t=False, scale=1.0,
       rot=None, res=None, bias=None, tm=None, tn=None, name="mm"):
    bx, L, K = x3.shape
    tm = tm or min(L, 1024 if (epi in ("rot", "res") or K > 1024) else 2048)
    tn = tn or min(n_out, 512 if K > 1024 else 1024)
    assert L % tm == 0 and n_out % tn == 0 and n0 % tn == 0
    nb0 = n0 // tn
    has_mod = mod is not None
    in_specs = [pl.BlockSpec((1, tm, K), lambda b, m, n: (b, m, 0))]
    args = [x3]
    if has_mod:
        for a in mod:
            in_specs.append(_row_spec(a, tm, K, lambda n: 0))
            args.append(a)
    if w_t:
        in_specs.append(pl.BlockSpec((1, tn, K), lambda b, m, n: (wl, nb0 + n, 0)))
    else:
        in_specs.append(pl.BlockSpec((1, K, tn), lambda b, m, n: (wl, 0, nb0 + n)))
    args.append(w3)
    if epi == "rot":
        for a in rot:
            in_specs.append(pl.BlockSpec((tm, tn), lambda b, m, n: (m, n)))
            args.append(a)
    elif epi == "res":
        in_specs.append(pl.BlockSpec((1, tm, tn), lambda b, m, n: (b, m, n)))
        in_specs.append(_row_spec(res[1], tm, tn, lambda n: n))
        args.extend(res)
    elif epi == "logsig":
        in_specs.append(pl.BlockSpec(bias.shape, lambda b, m, n: (0, 0)))
        args.append(bias)
    if out_t:
        out_spec = pl.BlockSpec((1, tn, tm), lambda b, m, n: (b, n, m))
        out_shape = jax.ShapeDtypeStruct((bx, n_out, L), out_dtype)
    else:
        out_spec = pl.BlockSpec((1, tm, tn), lambda b, m, n: (b, m, n))
        out_shape = jax.ShapeDtypeStruct((bx, L, n_out), out_dtype)
    return pl.pallas_call(
        functools.partial(_mm_body, has_mod=has_mod, epi=epi, w_t=w_t, out_t=out_t, scale=scale),
        grid=(bx, L // tm, n_out // tn),
        in_specs=in_specs,
        out_specs=out_spec,
        out_shape=out_shape,
        scratch_shapes=[pltpu.VMEM((tm, K), BF16)] if has_mod else [],
        compiler_params=_cparams(3),
        name=name,
    )(*args)


def _rotary(acc, cos, sin_signed):
    width = acc.shape[1]
    lane = lax.broadcasted_iota(jnp.int32, acc.shape, 1)
    partner = jnp.where(lane % 2 == 0, pltpu.roll(acc, width - 1, axis=1), pltpu.roll(acc, 1, axis=1))
    return acc * cos + partner * sin_signed


def _ffn_body(*refs, moe, nf, ne):
    it = iter(refs)
    x_ref, sh_ref, sc_ref, gate_ref = next(it), next(it), next(it), next(it)
    gw_ref = next(it) if moe else None
    wg_ref, wu_ref, wd_ref, o_ref, h_scr, acc_scr = next(it), next(it), next(it), next(it), next(it), next(it)
    tot_scr = next(it) if moe else None
    e = pl.program_id(2)
    f = pl.program_id(3)

    @pl.when((e == 0) & (f == 0))
    def _():
        h_scr[...] = _modulate(x_ref[0], sh_ref[0], sc_ref[0]).astype(BF16)

    @pl.when(f == 0)
    def _():
        acc_scr[...] = jnp.zeros_like(acc_scr)

    if moe:

        @pl.when((e == 0) & (f == 0))
        def _():
            tot_scr[...] = jnp.zeros_like(tot_scr)

    h = h_scr[...]
    a = jnp.dot(h, wg_ref[0].astype(BF16), preferred_element_type=F32)
    b = jnp.dot(h, wu_ref[0].astype(BF16), preferred_element_type=F32)
    mid = (a * _sigmoid(a) * b).astype(BF16)
    acc_scr[...] += jnp.dot(mid, wd_ref[0].astype(BF16), preferred_element_type=F32)

    if not moe:

        @pl.when(f == nf - 1)
        def _():
            o_ref[0] = x_ref[0] + gate_ref[0] * acc_scr[...]

    else:

        @pl.when(f == nf - 1)
        def _():
            gw = gw_ref[0]
            lane = lax.broadcasted_iota(jnp.int32, gw.shape, 1)
            col = jnp.sum(jnp.where(lane == e, gw, 0.0), axis=1, keepdims=True)
            tot_scr[...] += col * acc_scr[...]

            @pl.when(e == ne - 1)
            def _():
                o_ref[0] = x_ref[0] + gate_ref[0] * tot_scr[...]


def ffn(x3, mod, gate, w_gu3, w_down3, wl, *, gates=None, tm=None, name="ffn"):
    bx, L, D = x3.shape
    moe = gates is not None
    ne = N_EXPERTS if moe else 1
    tm = tm or min(L, 1024 if moe else 2048)
    tf = FFN_TF if tm > 256 else MOE_TF
    nf = FFN_DIM // tf
    assert L % tm == 0 and FFN_DIM % tf == 0
    in_specs = [pl.BlockSpec((1, tm, D), lambda b, m, e, f: (b, m, 0))]
    args = [x3]
    for a in (*mod, gate):
        in_specs.append(_row_spec(a, tm, D, lambda e, f: 0))
        args.append(a)
    if moe:
        in_specs.append(pl.BlockSpec((1, tm, LANES), lambda b, m, e, f: (b, m, 0)))
        args.append(gates)
    in_specs += [
        pl.BlockSpec((1, D, tf), lambda b, m, e, f: (wl + e, 0, f)),
        pl.BlockSpec((1, D, tf), lambda b, m, e, f: (wl + e, 0, nf + f)),
        pl.BlockSpec((1, tf, D), lambda b, m, e, f: (wl + e, f, 0)),
    ]
    args += [w_gu3, w_gu3, w_down3]
    scratch = [pltpu.VMEM((tm, D), BF16), pltpu.VMEM((tm, D), F32)]
    if moe:
        scratch.append(pltpu.VMEM((tm, D), F32))
    return pl.pallas_call(
        functools.partial(_ffn_body, moe=moe, nf=nf, ne=ne),
        grid=(bx, L // tm, ne, nf),
        in_specs=in_specs,
        out_specs=pl.BlockSpec((1, tm, D), lambda b, m, e, f: (b, m, 0)),
        out_shape=jax.ShapeDtypeStruct((bx, L, D), F32),
        scratch_shapes=scratch,
        compiler_params=_cparams(4),
        name=name,
    )(*args)


def _router_body(x_ref, sh_ref, sc_ref, w_ref, b_ref, o_ref):
    h = _modulate(x_ref[0], sh_ref[0], sc_ref[0]).astype(BF16)
    logits = lax.dot_general(h, w_ref[0].astype(BF16), (((1,), (1,)), ((), ())), preferred_element_type=F32)
    logits = logits + b_ref[0]
    lane = lax.broadcasted_iota(jnp.int32, logits.shape, 1).astype(F32)
    lg = jnp.where(lane < N_EXPERTS, logits, -jnp.inf)
    m1 = jnp.max(lg, axis=1, keepdims=True)
    i1 = jnp.min(jnp.where(lg == m1, lane, float(LANES)), axis=1, keepdims=True)
    lg2 = jnp.where(lane == i1, -jnp.inf, lg)
    m2 = jnp.max(lg2, axis=1, keepdims=True)
    i2 = jnp.min(jnp.where(lg2 == m2, lane, float(LANES)), axis=1, keepdims=True)
    e2 = jnp.exp(m2 - m1)
    den = 1.0 + e2
    o_ref[0] = jnp.where(lane == i1, 1.0 / den, 0.0) + jnp.where(lane == i2, e2 / den, 0.0)


def router(x3, mod, w_rt_pad, b_pad, wl, *, tm=None):
    bx, L, D = x3.shape
    tm = tm or min(L, 1024)
    in_specs = [pl.BlockSpec((1, tm, D), lambda b, m: (b, m, 0))]
    args = [x3]
    for a in mod:
        in_specs.append(_row_spec(a, tm, D, lambda: 0))
        args.append(a)
    in_specs += [
        pl.BlockSpec((1, LANES, D), lambda b, m: (wl, 0, 0)),
        pl.BlockSpec((1, 1, LANES), lambda b, m: (wl, 0, 0)),
    ]
    args += [w_rt_pad, b_pad]
    return pl.pallas_call(
        _router_body,
        grid=(bx, L // tm),
        in_specs=in_specs,
        out_specs=pl.BlockSpec((1, tm, LANES), lambda b, m: (b, m, 0)),
        out_shape=jax.ShapeDtypeStruct((bx, L, LANES), F32),
        compiler_params=_cparams(2),
        name="router",
    )(*args)


MOE_T = 256
MOE_CH = 128
MOE_TM = 256
MOE_TF = FFN_DIM // 2
MOE_ALIGN = 8
MOE_STAGE = -(-(2 * MOE_T + N_EXPERTS * (MOE_ALIGN - 1) + MOE_CH) // 8) * 8


def _moe_plan(gates2, n_tokens):
    nb = n_tokens // MOE_T
    routed = (gates2[:, :N_EXPERTS] > 0).reshape(nb, MOE_T, N_EXPERTS)
    ri = routed.astype(jnp.int32)
    t_i = jnp.arange(MOE_T)
    before = (t_i[None, :] < t_i[:, None]).astype(F32)
    rank = jnp.einsum("ts,bse->bte", before, routed.astype(F32)).astype(jnp.int32)
    cnt = jnp.sum(ri, axis=1)
    cnt_al = -(-cnt // MOE_ALIGN) * MOE_ALIGN
    lo = jnp.cumsum(cnt_al, axis=1) - cnt_al
    total = jnp.sum(cnt_al, axis=0)
    region = -(-(total + MOE_CH) // MOE_TM) * MOE_TM
    ends = jnp.cumsum(region)
    off = ends - region
    pos = off[None, :] + jnp.cumsum(cnt_al, axis=0) - cnt_al
    nch = -(-cnt_al // MOE_CH)
    m_pad = -(-(2 * n_tokens + nb * N_EXPERTS * (MOE_ALIGN - 1) + N_EXPERTS * (MOE_CH + MOE_TM)) // MOE_TM) * MOE_TM
    n_tiles = m_pad // MOE_TM
    tile_start = jnp.arange(n_tiles, dtype=jnp.int32) * MOE_TM
    tile_e = jnp.minimum(jnp.sum(tile_start[:, None] >= ends[None, :], axis=1), N_EXPERTS - 1).astype(jnp.int32)
    n_used = (ends[-1] // MOE_TM).astype(jnp.int32).reshape(1)
    dest = jnp.where(routed, lo[:, None, :] + rank, -1)
    d_hi = jnp.max(dest, axis=2)
    d_lo = jnp.min(jnp.where(routed, dest, MOE_STAGE), axis=2)
    dd = jnp.stack([d_hi, d_lo], axis=1).astype(jnp.int32)
    eidx = jnp.arange(N_EXPERTS, dtype=jnp.int32)
    e_a = jnp.min(jnp.where(routed, eidx, N_EXPERTS), axis=2)
    e_b = jnp.max(jnp.where(routed, eidx, -1), axis=2)
    g3 = gates2[:, :N_EXPERTS].reshape(nb, MOE_T, N_EXPERTS)
    at = lambda a, e: jnp.sum(jnp.where(eidx == e[..., None], a, 0), axis=2)
    r_a, r_b = at(rank, e_a), at(rank, e_b)
    w_a, w_b = at(g3, e_a), jnp.where(e_b != e_a, at(g3, e_b), 0.0)
    col = lambda e, r, c: jnp.where(r // MOE_CH == c, e * MOE_CH + r % MOE_CH, -1).astype(F32)
    cmeta = jnp.stack([col(e_a, r_a, 0), col(e_b, r_b, 0), col(e_a, r_a, 1), col(e_b, r_b, 1), w_a, w_b], axis=-1)
    cmeta = jnp.pad(cmeta.reshape(n_tokens, 6), ((0, 0), (0, LANES - 6)))
    two = (jnp.max(nch, axis=1) > 1).astype(jnp.int32)
    flat = lambda a: a.reshape(-1).astype(jnp.int32)
    return dict(lo=flat(lo // MOE_ALIGN), pos=flat(pos // MOE_ALIGN), nch=flat(nch), tile_e=tile_e, n_used=n_used,
                dd=dd, cmeta=cmeta, two=two, m_pad=m_pad, n_tiles=n_tiles, nb=nb)


def _dispatch_body(lo_ref, pos_ref, nch_ref, x_ref, sh_ref, sc_ref, dd_ref, xs_in_ref, xs_ref, stage_scr, sem,
                   *, nb):
    del xs_in_ref
    b = pl.program_id(0)
    slot = b % 2
    h = _modulate(x_ref[...], sh_ref[0], sc_ref[0]).astype(BF16)
    r = lax.broadcasted_iota(jnp.int32, (MOE_STAGE, MOE_T), 0)
    dd = dd_ref[0]
    onehot = jnp.where((r == dd[0:1, :]) | (r == dd[1:2, :]), 1.0, 0.0).astype(BF16)
    stage_scr[slot] = jnp.dot(onehot, h, preferred_element_type=F32)

    def seg_copy(blk, e, c):
        src0 = pl.multiple_of(lo_ref[blk * N_EXPERTS + e] * MOE_ALIGN + c * MOE_CH, MOE_ALIGN)
        dst0 = pl.multiple_of(pos_ref[blk * N_EXPERTS + e] * MOE_ALIGN + c * MOE_CH, MOE_ALIGN)
        return pltpu.make_async_copy(stage_scr.at[blk % 2, pl.ds(src0, MOE_CH)], xs_ref.at[pl.ds(dst0, MOE_CH)],
                                     sem.at[blk % 2, e, c])

    def for_segments(blk, fn):
        for e in range(N_EXPERTS):
            for c in range(2):
                @pl.when(c < nch_ref[blk * N_EXPERTS + e])
                def _():
                    fn(seg_copy(blk, e, c))

    @pl.when(b > 0)
    def _():
        for_segments(b - 1, lambda cp: cp.wait())

    for_segments(b, lambda cp: cp.start())

    @pl.when(b == nb - 1)
    def _():
        for_segments(b, lambda cp: cp.wait())


def moe_dispatch(x2, mod, plan, seq_len, buf=None):
    n_tokens, D = x2.shape
    if buf is None:
        buf = jnp.zeros((plan["m_pad"], D), F32)
    per_seq = seq_len // MOE_T
    nb = plan["nb"]
    vec = lambda: pl.BlockSpec((1, 1, D), lambda b, *_: (b // per_seq, 0, 0))
    grid_spec = pltpu.PrefetchScalarGridSpec(
        num_scalar_prefetch=3,
        grid=(nb,),
        in_specs=[
            pl.BlockSpec((MOE_T, D), lambda b, *_: (b, 0)),
            vec(), vec(),
            pl.BlockSpec((1, 2, MOE_T), lambda b, *_: (b, 0, 0)),
            pl.BlockSpec(memory_space=pltpu.MemorySpace.HBM),
        ],
        out_specs=pl.BlockSpec(memory_space=pltpu.MemorySpace.HBM),
        scratch_shapes=[pltpu.VMEM((2, MOE_STAGE, D), F32), pltpu.SemaphoreType.DMA((2, N_EXPERTS, 2))],
    )
    return pl.pallas_call(
        functools.partial(_dispatch_body, nb=nb),
        grid_spec=grid_spec,
        out_shape=jax.ShapeDtypeStruct((plan["m_pad"], D), F32),
        input_output_aliases={7: 0},
        compiler_params=_cparams(1),
        name="moe_dispatch",
    )(plan["lo"], plan["pos"], plan["nch"], x2, mod[0], mod[1], plan["dd"], buf)


def _gffn_body(te_ref, nu_ref, *refs, has_prev):
    it = iter(refs)
    xs_ref = next(it)
    yp_ref = next(it) if has_prev else None
    wg_ref, wu_ref, wd_ref, o_ref, wg_scr, wu_scr, wd_scr = (next(it) for _ in range(7))
    t = pl.program_id(0)
    e_here = te_ref[t]
    e_prev = te_ref[jnp.maximum(t - 1, 0)]

    @pl.when(t < nu_ref[0])
    def _():
        @pl.when((t == 0) | (e_here != e_prev))
        def _():
            wg_scr[...] = wg_ref[0].astype(BF16)
            wu_scr[...] = wu_ref[0].astype(BF16)
            wd_scr[...] = wd_ref[0].astype(BF16)

        h = xs_ref[...].astype(BF16)
        a = jnp.dot(h, wg_scr[...], preferred_element_type=F32)
        b = jnp.dot(h, wu_scr[...], preferred_element_type=F32)
        mid = (a * _sigmoid(a) * b).astype(BF16)
        y = jnp.dot(mid, wd_scr[...], preferred_element_type=F32)
        o_ref[...] = (yp_ref[...] + y) if has_prev else y

    @pl.when(t >= nu_ref[0])
    def _():
        o_ref[...] = jnp.zeros_like(o_ref)


def moe_grouped_ffn(xs, w_gu3, w_down3, wl, plan, f, y_prev=None):
    m_pad, D = xs.shape
    nf = FFN_DIM // MOE_TF
    has_prev = y_prev is not None
    row = lambda t, te, nu: (jnp.minimum(t, nu[0] - 1), 0)
    tile = lambda: pl.BlockSpec((MOE_TM, D), row)
    in_specs = [tile()] + ([tile()] if has_prev else [])
    in_specs += [
        pl.BlockSpec((1, D, MOE_TF), lambda t, te, nu: (wl + te[t], 0, f)),
        pl.BlockSpec((1, D, MOE_TF), lambda t, te, nu: (wl + te[t], 0, nf + f)),
        pl.BlockSpec((1, MOE_TF, D), lambda t, te, nu: (wl + te[t], f, 0), pipeline_mode=pl.Buffered(1)),
    ]
    grid_spec = pltpu.PrefetchScalarGridSpec(
        num_scalar_prefetch=2,
        grid=(plan["n_tiles"],),
        in_specs=in_specs,
        out_specs=pl.BlockSpec((MOE_TM, D), lambda t, te, nu: (t, 0)),
        scratch_shapes=[pltpu.VMEM((D, MOE_TF), BF16), pltpu.VMEM((D, MOE_TF), BF16), pltpu.VMEM((MOE_TF, D), BF16)],
    )
    args = [xs] + ([y_prev] if has_prev else []) + [w_gu3, w_gu3, w_down3]
    return pl.pallas_call(
        functools.partial(_gffn_body, has_prev=has_prev),
        grid_spec=grid_spec,
        out_shape=jax.ShapeDtypeStruct((m_pad, D), F32),
        compiler_params=_cparams(1),
        name=f"moe_grouped_ffn{f}",
    )(plan["tile_e"], plan["n_used"], *args)


def _combine_body(pos_ref, nch_ref, two_ref, x_ref, gate_ref, cm_ref, *refs, final):
    y_refs = refs[:N_EXPERTS]
    refs = refs[N_EXPERTS:]
    y_hbm, fg_ref = (refs[0], refs[1]) if final else (refs[0], None)
    o_ref, yhi_scr, ylo_scr, over_scr, sem = refs[2 if final else 1:]
    b = pl.program_id(0)

    def gathered(window, chunk):
        for e in range(N_EXPERTS):
            y = window(e)
            y_hi = y.astype(BF16)
            yhi_scr[e * MOE_CH:(e + 1) * MOE_CH, :] = y_hi
            ylo_scr[e * MOE_CH:(e + 1) * MOE_CH, :] = (y - y_hi.astype(F32)).astype(BF16)
        cm = cm_ref[...]
        lane = lax.broadcasted_iota(jnp.int32, (MOE_T, N_EXPERTS * MOE_CH), 1).astype(F32)
        picks = [jnp.where(lane == cm[:, 2 * chunk + k:2 * chunk + k + 1], 1.0, 0.0).astype(BF16) for k in range(2)]
        onehot = jnp.concatenate(picks, axis=0)
        rows = (jnp.dot(onehot, yhi_scr[...], preferred_element_type=F32)
                + jnp.dot(onehot, ylo_scr[...], preferred_element_type=F32))
        return cm[:, 4:5] * rows[:MOE_T] + cm[:, 5:6] * rows[MOE_T:]

    o_ref[...] = x_ref[...] + gate_ref[0] * gathered(lambda e: y_refs[e][...], 0)

    @pl.when(two_ref[b] > 0)
    def _():
        for e in range(N_EXPERTS):
            s = b * N_EXPERTS + e

            @pl.when(nch_ref[s] > 1)
            def _():
                start = pl.multiple_of(pos_ref[s] * MOE_ALIGN + MOE_CH, MOE_ALIGN)
                cp = pltpu.make_async_copy(y_hbm.at[pl.ds(start, MOE_CH)], over_scr.at[e], sem.at[e])
                cp.start()
                cp.wait()

            @pl.when(nch_ref[s] <= 1)
            def _():
                over_scr[e] = jnp.zeros((MOE_CH, over_scr.shape[2]), F32)

        o_ref[...] += gate_ref[0] * gathered(lambda e: over_scr[e], 1)

    if final:
        xo = o_ref[...]
        o_ref[...] = xo * lax.rsqrt(jnp.mean(xo * xo, axis=-1, keepdims=True) + EPS) * fg_ref[...]


def moe_combine(x2, gate, y, plan, seq_len, final_g=None):
    n_tokens, D = x2.shape
    per_seq = seq_len // MOE_T
    final = final_g is not None
    tail_specs = [pl.BlockSpec(memory_space=pltpu.MemorySpace.HBM)]
    tail_args = [y]
    if final:
        tail_specs.append(pl.BlockSpec((1, D), lambda b, *_: (0, 0)))
        tail_args.append(final_g.reshape(1, D))

    def window(e):
        return pl.BlockSpec((pl.Element(MOE_CH), pl.Element(D)),
                            lambda b, pos, nch, two: (pos[b * N_EXPERTS + e] * MOE_ALIGN, 0))

    blk = lambda w: pl.BlockSpec((MOE_T, w), lambda b, *_: (b, 0))
    grid_spec = pltpu.PrefetchScalarGridSpec(
        num_scalar_prefetch=3,
        grid=(plan["nb"],),
        in_specs=[blk(D), pl.BlockSpec((1, 1, D), lambda b, *_: (b // per_seq, 0, 0)), blk(LANES)]
        + [window(e) for e in range(N_EXPERTS)] + tail_specs,
        out_specs=blk(D),
        scratch_shapes=[pltpu.VMEM((N_EXPERTS * MOE_CH, D), BF16), pltpu.VMEM((N_EXPERTS * MOE_CH, D), BF16),
                        pltpu.VMEM((N_EXPERTS, MOE_CH, D), F32), pltpu.SemaphoreType.DMA((N_EXPERTS,))],
    )
    return pl.pallas_call(
        functools.partial(_combine_body, final=final),
        grid_spec=grid_spec,
        out_shape=jax.ShapeDtypeStruct((n_tokens, D), F32),
        compiler_params=_cparams(1),
        name="moe_combine",
    )(plan["pos"], plan["nch"], plan["two"], x2, gate, plan["cmeta"], *([y] * N_EXPERTS), *tail_args)


def moe_sparse(x3, mod, gate, gates, w_gu3, w_down3, wl, final_g=None, buf=None):
    B, L, D = x3.shape
    n_tokens = B * L
    assert L % MOE_T == 0 and FFN_DIM % MOE_TF == 0
    x2 = x3.reshape(n_tokens, D)
    gates2 = gates.reshape(n_tokens, LANES)
    plan = _moe_plan(gates2, n_tokens)
    xs = moe_dispatch(x2, mod, plan, L, buf)
    y = None
    for f in range(FFN_DIM // MOE_TF):
        y = moe_grouped_ffn(xs, w_gu3, w_down3, wl, plan, f, y_prev=y)
    return moe_combine(x2, gate, y, plan, L, final_g).reshape(B, L, D), xs


RET_CHUNKS_PER_STEP = 4


def _ret_body(*refs, zero_init, nc, chunk, per_step, heads):
    it = iter(refs)
    q_ref, k_ref, v_ref, g_ref = next(it), next(it), next(it), next(it)
    s0_ref = None if zero_init else next(it)
    inner_ref, qd_ref, kd_ref, cd_ref = next(it), next(it), next(it), next(it)
    o_ref, sout_ref, s_scr = next(it), next(it), next(it)
    c = pl.program_id(2)

    @pl.when(c == 0)
    def _():
        if zero_init:
            s_scr[...] = jnp.zeros_like(s_scr)
        else:
            s_scr[...] = s0_ref[0, 0]

    for hi in range(heads):
        kcols = slice(hi * RET_DK, (hi + 1) * RET_DK)
        vcols = slice(hi * RET_DV, (hi + 1) * RET_DV)
        s = s_scr[hi]
        for ci in range(per_step):
            rows = slice(ci * chunk, (ci + 1) * chunk)
            q = q_ref[0, rows, kcols]
            k = k_ref[0, rows, kcols]
            v = v_ref[0, rows, vcols]
            att = lax.dot_general(q, k, (((1,), (1,)), ((), ())), preferred_element_type=F32) * inner_ref[hi]
            inner = jnp.dot(att.astype(BF16), v, preferred_element_type=F32)
            cross = jnp.dot(q, s.astype(BF16), preferred_element_type=F32) * qd_ref[hi]
            kdt = (k.astype(F32) * kd_ref[hi]).T.astype(BF16)
            s = s * cd_ref[hi] + jnp.dot(kdt, v, preferred_element_type=F32)
            o = inner + cross
            on = o * lax.rsqrt(jnp.mean(o * o, axis=-1, keepdims=True) + EPS)
            g = g_ref[0, rows, vcols]
            o_ref[0, rows, vcols] = (g * _sigmoid(g) * on).astype(o_ref.dtype)
        s_scr[hi] = s

    @pl.when(c == nc - 1)
    def _():
        sout_ref[0] = s_scr[...]


def retention_scan(q, k, v, g, tables, chunk, *, s0=None, s0_layer=0, heads=1):
    B, L, _ = q.shape
    per_step = math.gcd(L // chunk, RET_CHUNKS_PER_STEP)
    rows = per_step * chunk
    nc = L // rows
    inner, qd, kd, cd = tables
    zero_init = s0 is None
    in_specs = [
        pl.BlockSpec((1, rows, heads * RET_DK), lambda h, b, c: (b, c, h)),
        pl.BlockSpec((1, rows, heads * RET_DK), lambda h, b, c: (b, c, h)),
        pl.BlockSpec((1, rows, heads * RET_DV), lambda h, b, c: (b, c, h)),
        pl.BlockSpec((1, rows, heads * RET_DV), lambda h, b, c: (b, c, h)),
    ]
    args = [q, k, v, g]
    if not zero_init:
        in_specs.append(pl.BlockSpec((1, 1, heads, RET_DK, RET_DV), lambda h, b, c: (s0_layer, b, h, 0, 0)))
        args.append(s0)
    in_specs += [
        pl.BlockSpec((heads, chunk, chunk), lambda h, b, c: (h, 0, 0)),
        pl.BlockSpec((heads, chunk, RET_DV), lambda h, b, c: (h, 0, 0)),
        pl.BlockSpec((heads, chunk, RET_DK), lambda h, b, c: (h, 0, 0)),
        pl.BlockSpec((heads, 1, RET_DV), lambda h, b, c: (h, 0, 0)),
    ]
    args += [inner, qd, kd, cd]
    return pl.pallas_call(
        functools.partial(_ret_body, zero_init=zero_init, nc=nc, chunk=chunk, per_step=per_step, heads=heads),
        grid=(RET_HEADS // heads, B, nc),
        in_specs=in_specs,
        out_specs=[
            pl.BlockSpec((1, rows, heads * RET_DV), lambda h, b, c: (b, c, h)),
            pl.BlockSpec((1, heads, RET_DK, RET_DV), lambda h, b, c: (b, h, 0, 0)),
        ],
        out_shape=[
            jax.ShapeDtypeStruct((B, L, RET_V), BF16),
            jax.ShapeDtypeStruct((B, RET_HEADS, RET_DK, RET_DV), F32),
        ],
        scratch_shapes=[pltpu.VMEM((heads, RET_DK, RET_DV), F32)],
        compiler_params=_cparams(3),
        name="retention_scan",
    )(*args)


def retention_tables(n_real, n_pad):
    log_gamma = jnp.log1p(-(2.0 ** (-5.0 - jnp.arange(RET_HEADS, dtype=F32))))
    idx = jnp.arange(n_pad, dtype=F32)
    valid = idx < n_real
    diff = idx[:, None] - idx[None, :]
    ok = (diff >= 0) & valid[:, None] & valid[None, :]
    inner = jnp.where(ok[None], jnp.exp(log_gamma[:, None, None] * jnp.maximum(diff, 0.0)[None]), 0.0)
    qd = jnp.exp(log_gamma[:, None] * (idx[None, :] + 1.0))
    kd = jnp.where(valid[None, :], jnp.exp(log_gamma[:, None] * (n_real - 1.0 - idx[None, :])), 0.0)
    cd = jnp.exp(log_gamma * n_real)
    return (inner,
            jnp.broadcast_to(qd[:, :, None], (RET_HEADS, n_pad, RET_DV)),
            jnp.broadcast_to(kd[:, :, None], (RET_HEADS, n_pad, RET_DK)),
            jnp.broadcast_to(cd[:, None, None], (RET_HEADS, 1, RET_DV)))


def rotary_tables(pos, reps):
    inv_freq = ROPE_BASE ** (-jnp.arange(0, RET_DK, 2, dtype=F32) / RET_DK)
    ang = pos.astype(F32)[:, None] * inv_freq[None, :]
    cos = jnp.repeat(jnp.cos(ang), 2, axis=1)
    sin = jnp.sin(ang)
    sin_signed = jnp.stack([-sin, sin], axis=-1).reshape(ang.shape[0], RET_DK)
    return jnp.tile(cos, (1, reps)), jnp.tile(sin_signed, (1, reps))


def _pool_body(x_ref, xp_ref, buf_ref, sh_ref, sc_ref, gate_ref, w_ref, cs_ref, o_ref, tail_ref, ext_scr,
               *, tm, pos0, has_prev):
    m = pl.program_id(1)
    sh, sc = sh_ref[0], sc_ref[0]
    h = _modulate(x_ref[0], sh, sc)

    @pl.when(m == 0)
    def _():
        ext_scr[0:POOL_HALO, :] = buf_ref[0]

    if has_prev:

        @pl.when(m > 0)
        def _():
            ext_scr[0:POOL_HALO, :] = _modulate(xp_ref[0], sh, sc)

    ext_scr[POOL_HALO:POOL_HALO + tm, :] = h
    tail_ref[0] = ext_scr[tm:tm + POOL_HALO, :]
    row = lax.broadcasted_iota(jnp.int32, (tm, 1), 0)
    pos1 = (pos0 + m * tm + row + 1).astype(F32)
    rows = max(tm, 16)
    ys = []
    for gi, w in enumerate(POOL_WINDOWS):
        c0, c1 = gi * POOL_GW, (gi + 1) * POOL_GW
        win = ext_scr[POOL_HALO:POOL_HALO + tm, c0:c1]
        for j in range(1, w):
            win = win + ext_scr[POOL_HALO - j:POOL_HALO - j + tm, c0:c1]
        d = win / jnp.minimum(jnp.float32(w), pos1) - h[:, c0:c1]
        if rows != tm:
            d = jnp.concatenate([d, jnp.zeros((rows - tm, POOL_GW), F32)], axis=0)
        y = jnp.dot(d.astype(BF16), w_ref[0, gi].astype(BF16), preferred_element_type=F32)
        ys.append(y[0:tm])
    y = jnp.concatenate(ys, axis=1) * cs_ref[...]
    o_ref[0] = x_ref[0] + gate_ref[0] * y


def pool_layer(x3, buf16, mod, gate, pool_w, pool_scale, wl, pos0, *, tm):
    B, L, D = x3.shape
    assert L % tm == 0 and (L == tm or tm % POOL_HALO == 0)
    has_prev = L > tm
    ph = POOL_HALO if has_prev else min(L, POOL_HALO)
    per = tm // POOL_HALO if has_prev else 1
    vec = lambda: pl.BlockSpec((1, 1, D), lambda b, m: (b, 0, 0))
    return pl.pallas_call(
        functools.partial(_pool_body, tm=tm, pos0=pos0, has_prev=has_prev),
        grid=(B, L // tm),
        in_specs=[
            pl.BlockSpec((1, tm, D), lambda b, m: (b, m, 0)),
            pl.BlockSpec((1, ph, D), lambda b, m: (b, jnp.maximum(m * per - 1, 0), 0)),
            pl.BlockSpec((1, POOL_HALO, D), lambda b, m: (b, 0, 0)),
            vec(), vec(), vec(),
            pl.BlockSpec((1,) + pool_w.shape[1:], lambda b, m: (wl, 0, 0, 0)),
            pl.BlockSpec((1, D), lambda b, m: (wl, 0)),
        ],
        out_specs=[
            pl.BlockSpec((1, tm, D), lambda b, m: (b, m, 0)),
            pl.BlockSpec((1, POOL_HALO, D), lambda b, m: (b, 0, 0)),
        ],
        out_shape=[
            jax.ShapeDtypeStruct((B, L, D), F32),
            jax.ShapeDtypeStruct((B, POOL_HALO, D), F32),
        ],
        scratch_shapes=[pltpu.VMEM((tm + POOL_HALO, D), F32)],
        compiler_params=_cparams(2),
        name="pool_layer",
    )(x3, x3, buf16, mod[0], mod[1], gate, pool_w, pool_scale)


def _split3(x):
    p0 = x.astype(BF16)
    r1 = x - p0.astype(F32)
    p1 = r1.astype(BF16)
    p2 = (r1 - p1.astype(F32)).astype(BF16)
    return p0, p1, p2


def _lane_cumsum(x, tri):
    p0, p1, p2 = _split3(x)
    dot = lambda p: jnp.dot(p, tri, preferred_element_type=F32)
    return (dot(p0) + dot(p1)) + dot(p2)


def _upper_tri(t):
    r = lax.broadcasted_iota(jnp.int32, (t, t), 0)
    c = lax.broadcasted_iota(jnp.int32, (t, t), 1)
    return jnp.where(r <= c, 1.0, 0.0).astype(BF16)


BIAS_PIECES = 3


def _fbias_body(lf_ref, o_ref, carry_scr, *, tc):
    @pl.when(pl.program_id(1) == 0)
    def _():
        carry_scr[...] = jnp.zeros_like(carry_scr)

    r = lax.broadcasted_iota(jnp.int32, (tc, tc), 0)
    c = lax.broadcasted_iota(jnp.int32, (tc, tc), 1)
    tril = jnp.where(c <= r, 1.0, 0.0).astype(BF16)
    p0, p1, p2 = _split3(lf_ref[0])
    dot = lambda p: jnp.dot(tril, p, preferred_element_type=F32)
    f = carry_scr[...] + ((dot(p0) + dot(p1)) + dot(p2))
    carry_scr[...] = f[tc - 1:tc, :]
    head = lax.broadcasted_iota(jnp.int32, (FOX_HEADS, LANES), 0)
    lane = lax.broadcasted_iota(jnp.int32, (FOX_HEADS, LANES), 1)
    out = None
    for p, piece in enumerate(_split3(-f)):
        place = jnp.where(lane == BIAS_PIECES * head + p, 1.0, 0.0).astype(BF16)
        term = jnp.dot(piece, place, preferred_element_type=F32)
        out = term if out is None else out + term
    o_ref[0] = out.astype(BF16)


def fox_bias_features(lf, *, tc=512):
    B, L, H = lf.shape
    tc = min(tc, L)
    return pl.pallas_call(
        functools.partial(_fbias_body, tc=tc),
        grid=(B, L // tc),
        in_specs=[pl.BlockSpec((1, tc, H), lambda b, c: (b, c, 0))],
        out_specs=pl.BlockSpec((1, tc, LANES), lambda b, c: (b, c, 0)),
        out_shape=jax.ShapeDtypeStruct((B, L, LANES), BF16),
        scratch_shapes=[pltpu.VMEM((1, H), F32)],
        compiler_params=_cparams(2),
        name="fox_bias_features",
    )(lf)


DEN_ROWS = 16


def _flash_body(qt_ref, k_ref, fb_ref, vt_ref, o_ref, *, tq, tk):
    hp = pl.program_id(1)
    qi = pl.program_id(2)
    pair = 2 * FOX_DH
    row = lax.broadcasted_iota(jnp.int32, (pair, tq), 0)
    qt = qt_ref[0]
    qaug = []
    for i in range(2):
        q_head = jnp.where(row // FOX_DH == i, qt, jnp.zeros_like(qt))
        pick = jnp.where(row // BIAS_PIECES == 2 * hp + i, 1.0, 0.0).astype(BF16)
        qaug.append(jnp.concatenate([q_head, pick], axis=0))
    key_i = lax.broadcasted_iota(jnp.int32, (tk, tq), 0)
    qry_i = lax.broadcasted_iota(jnp.int32, (tk, tq), 1)
    per_q = tq // tk

    def step(j, carry, diag):
        k0 = pl.multiple_of(j * tk, tk)
        kaug = jnp.concatenate([k_ref[0, pl.ds(k0, tk), :], fb_ref[0, pl.ds(k0, tk), :]], axis=1)
        ones = jnp.ones((DEN_ROWS, tk), BF16)
        new = []
        for i in range(2):
            m_old, acc = carry[i]
            st = jnp.dot(kaug, qaug[i], preferred_element_type=F32)
            if diag is not None:
                st = jnp.where(key_i + diag * tk <= qry_i, st, NEG_BIG)
            m_new = jnp.maximum(m_old, jnp.max(st, axis=0, keepdims=True))
            alpha = jnp.exp(m_old - m_new)
            p = jnp.exp(st - m_new).astype(BF16)
            vt = jnp.concatenate([vt_ref[0, i * FOX_DH:(i + 1) * FOX_DH, pl.ds(k0, tk)].astype(BF16), ones], axis=0)
            acc = alpha * acc + jnp.dot(vt, p, preferred_element_type=F32)
            new.append((m_new, acc))
        return tuple(new)

    init = tuple((jnp.full((1, tq), NEG_BIG, F32), jnp.zeros((FOX_DH + DEN_ROWS, tq), F32)) for _ in range(2))
    carry = lax.fori_loop(0, qi * per_q, lambda j, c: step(j, c, None), init)
    for d in range(per_q):
        carry = step(qi * per_q + d, carry, d)
    o_ref[0] = jnp.concatenate([(acc[:FOX_DH] / acc[FOX_DH:FOX_DH + 1]).T for _, acc in carry],
                               axis=1).astype(o_ref.dtype)


def fox_flash(qt, k, fb, vt, *, tq=1024, tk=1024):
    B, L, D = k.shape
    tq = min(tq, L)
    tk = min(tk, tq)
    assert L % tq == 0 and tq % tk == 0
    pair = 2 * FOX_DH
    return pl.pallas_call(
        functools.partial(_flash_body, tq=tq, tk=tk),
        grid=(B, FOX_HEADS // 2, L // tq),
        in_specs=[
            pl.BlockSpec((1, pair, tq), lambda b, hp, qi: (b, hp, qi)),
            pl.BlockSpec((1, L, pair), lambda b, hp, qi: (b, 0, hp)),
            pl.BlockSpec((1, L, LANES), lambda b, hp, qi: (b, 0, 0)),
            pl.BlockSpec((1, pair, L), lambda b, hp, qi: (b, hp, 0)),
        ],
        out_specs=pl.BlockSpec((1, tq, pair), lambda b, hp, qi: (b, qi, hp)),
        out_shape=jax.ShapeDtypeStruct((B, L, D), BF16),
        compiler_params=_cparams(3),
        name="fox_flash",
    )(qt, k, fb, vt)


MAX_DECODE_PAGES = 16


def _decode_body(pt_ref, q_ref, kn_ref, vn_ref, lfn_ref, *refs, n_steps, lq, pages):
    kc = refs[0:pages]
    vc = refs[pages:2 * pages]
    lc = refs[2 * pages:3 * pages]
    o_ref, qbd_scr, m_scr, l_scr, acc_scr, carry_scr = refs[3 * pages:]
    b = pl.program_id(0)
    st = pl.program_id(1)
    rows = lq * FOX_HEADS
    last = (((1,), (1,)), ((), ()))
    tri = _upper_tri(PAGE_SIZE)

    @pl.when(st == 0)
    def _():
        head = lax.broadcasted_iota(jnp.int32, (FOX_HEADS, D_MODEL), 0)
        lane_head = lax.broadcasted_iota(jnp.int32, (FOX_HEADS, D_MODEL), 1) // FOX_DH
        blocks = [jnp.where(head == lane_head, jnp.broadcast_to(q_ref[0, t:t + 1, :], (FOX_HEADS, D_MODEL)), 0.0)
                  for t in range(lq)]
        qbd_scr[...] = jnp.concatenate(blocks, axis=0).astype(BF16)
        m_scr[...] = jnp.full_like(m_scr, NEG_BIG)
        l_scr[...] = jnp.zeros_like(l_scr)
        acc_scr[...] = jnp.zeros_like(acc_scr)
        carry_scr[...] = jnp.zeros_like(carry_scr)

    def absorb(s, v_mat, v_is_t):
        m_old = m_scr[...]
        m_new = jnp.maximum(m_old, jnp.max(s, axis=1, keepdims=True))
        alpha = jnp.exp(m_old - m_new)
        p = jnp.exp(s - m_new)
        l_scr[...] = alpha * l_scr[...] + jnp.sum(p, axis=1, keepdims=True)
        if v_is_t:
            pv = lax.dot_general(p.astype(BF16), v_mat, last, preferred_element_type=F32)
        else:
            pv = jnp.dot(p.astype(BF16), v_mat, preferred_element_type=F32)
        acc_scr[...] = alpha * acc_scr[...] + pv
        m_scr[...] = m_new

    @pl.when(st < n_steps)
    def _():
        kt = jnp.concatenate([kc[i][0].reshape(D_MODEL, PAGE_SIZE).astype(BF16) for i in range(pages)], axis=1)
        vt = jnp.concatenate([vc[i][0].reshape(D_MODEL, PAGE_SIZE).astype(BF16) for i in range(pages)], axis=1)
        within = _lane_cumsum(jnp.concatenate([lc[i][0] for i in range(pages)], axis=0), tri)
        f = carry_scr[...]
        biases = []
        for i in range(pages):
            f_page = f + within[i * FOX_HEADS:(i + 1) * FOX_HEADS, :]
            biases.append(jnp.tile(f_page, (lq, 1)))
            f = jnp.broadcast_to(f_page[:, PAGE_SIZE - 1:PAGE_SIZE], f_page.shape)
        carry_scr[...] = f
        s = jnp.dot(qbd_scr[...], kt, preferred_element_type=F32) - jnp.concatenate(biases, axis=1)
        absorb(s, vt, True)

    @pl.when(st == n_steps)
    def _():
        pad = jnp.zeros((PAGE_SIZE - lq, D_MODEL), F32)
        kn = jnp.concatenate([kn_ref[0], pad], axis=0).astype(BF16)
        vn = jnp.concatenate([vn_ref[0], pad], axis=0).astype(BF16)
        n_tok = lfn_ref.shape[1]
        tok = lax.broadcasted_iota(jnp.int32, (n_tok, PAGE_SIZE), 0)
        key = lax.broadcasted_iota(jnp.int32, (n_tok, PAGE_SIZE), 1)
        sel = jnp.where((tok // lq == b) & (tok % lq <= key) & (key < lq), 1.0, 0.0).astype(BF16)
        p0, p1, p2 = _split3(lfn_ref[...])
        dot = lambda p: jnp.dot(p, sel, preferred_element_type=F32)
        f = carry_scr[...] + ((dot(p0) + dot(p1)) + dot(p2))
        s = lax.dot_general(qbd_scr[...], kn, last, preferred_element_type=F32) - jnp.tile(f, (lq, 1))
        rq = lax.broadcasted_iota(jnp.int32, (rows, PAGE_SIZE), 0) // FOX_HEADS
        kk = lax.broadcasted_iota(jnp.int32, (rows, PAGE_SIZE), 1)
        s = jnp.where(kk <= rq, s, NEG_BIG)
        absorb(s, vn, False)
        o = acc_scr[...] / l_scr[...]
        head = lax.broadcasted_iota(jnp.int32, (FOX_HEADS, D_MODEL), 0)
        lane_head = lax.broadcasted_iota(jnp.int32, (FOX_HEADS, D_MODEL), 1) // FOX_DH
        outs = [jnp.sum(jnp.where(head == lane_head, o[t * FOX_HEADS:(t + 1) * FOX_HEADS, :], 0.0), axis=0,
                        keepdims=True) for t in range(lq)]
        o_ref[0] = jnp.concatenate(outs, axis=0).astype(o_ref.dtype)


def fox_decode(q, k_new, v_new, lft_new, cache_kt, cache_vt, cache_lt, page_table):
    B, lq, D = q.shape
    n_pages = page_table.shape[1]
    pages = math.gcd(n_pages, MAX_DECODE_PAGES)
    n_steps = n_pages // pages
    rows = lq * FOX_HEADS

    def page_idx(i):
        return lambda b, s, pt: (pt[b * n_pages + jnp.minimum(s, n_steps - 1) * pages + i], 0, 0, 0)

    def page_idx3(i):
        return lambda b, s, pt: (pt[b * n_pages + jnp.minimum(s, n_steps - 1) * pages + i], 0, 0)

    seq = lambda: pl.BlockSpec((1, lq, D), lambda b, s, pt: (b, 0, 0))
    in_specs = [seq(), seq(), seq(), pl.BlockSpec(lft_new.shape, lambda b, s, pt: (0, 0))]
    in_specs += [pl.BlockSpec((1, FOX_HEADS, FOX_DH, PAGE_SIZE), page_idx(i)) for i in range(pages)]
    in_specs += [pl.BlockSpec((1, FOX_HEADS, FOX_DH, PAGE_SIZE), page_idx(i)) for i in range(pages)]
    in_specs += [pl.BlockSpec((1, FOX_HEADS, PAGE_SIZE), page_idx3(i)) for i in range(pages)]
    grid_spec = pltpu.PrefetchScalarGridSpec(
        num_scalar_prefetch=1,
        grid=(B, n_steps + 1),
        in_specs=in_specs,
        out_specs=pl.BlockSpec((1, lq, D), lambda b, s, pt: (b, 0, 0)),
        scratch_shapes=[
            pltpu.VMEM((rows, D), BF16),
            pltpu.VMEM((rows, 1), F32),
            pltpu.VMEM((rows, 1), F32),
            pltpu.VMEM((rows, D), F32),
            pltpu.VMEM((FOX_HEADS, PAGE_SIZE), F32),
        ],
    )
    return pl.pallas_call(
        functools.partial(_decode_body, n_steps=n_steps, lq=lq, pages=pages),
        grid_spec=grid_spec,
        out_shape=jax.ShapeDtypeStruct((B, lq, D), F32),
        compiler_params=_cparams(2),
        name="fox_decode",
    )(page_table.reshape(-1), q, k_new, v_new, lft_new,
      *([cache_kt] * pages), *([cache_vt] * pages), *([cache_lt] * pages))


def _final_body(x_ref, g_ref, o_ref):
    xf = x_ref[0]
    o_ref[0] = xf * lax.rsqrt(jnp.mean(xf * xf, axis=-1, keepdims=True) + EPS) * g_ref[...]


def final_norm(x3, final_g, *, tm=None):
    bx, L, D = x3.shape
    tm = tm or min(L, 1024)
    return pl.pallas_call(
        _final_body,
        grid=(bx, L // tm),
        in_specs=[pl.BlockSpec((1, tm, D), lambda b, m: (b, m, 0)), pl.BlockSpec((1, D), lambda b, m: (0, 0))],
        out_specs=pl.BlockSpec((1, tm, D), lambda b, m: (b, m, 0)),
        out_shape=jax.ShapeDtypeStruct((bx, L, D), F32),
        compiler_params=_cparams(2),
        name="final_norm",
    )(x3, final_g.reshape(1, D))


def _trunk(x, mods, ret_state, pool_state, fox_past, pos0, params):
    (ret_w_in, ret_w_out, pool_w, pool_scale, fox_wt, fox_b_f, fox_w_out, ffn_w_gu, ffn_w_down,
     w_rt_pad, b_rt_pad, moe_w_gu, moe_w_down, final_g) = params
    B, L, D = x.shape
    decode = fox_past is not None
    if decode:
        x3 = x.reshape(1, B * L, D)
        expand = lambda v: jnp.repeat(v, L, axis=0)[None]
    else:
        x3 = x
        expand = lambda v: v[:, None, :]
    n_rows = x3.shape[1]
    pos = pos0 + jnp.arange(L)
    cos, sin = rotary_tables(pos, RET_HEADS)
    if decode:
        cos, sin = jnp.tile(cos, (B, 1)), jnp.tile(sin, (B, 1))
    chunk = min(L, 256)
    chunk_pad = max(chunk, PAGE_SIZE)
    tables = retention_tables(chunk, chunk_pad)
    ret_new, extras = [], {}
    moe_buf = None
    for i in range(DEPTH):
        sh_a, sc_a, g_a, sh_f, sc_f, g_f = [expand(v) for v in jnp.split(mods[i], 6, axis=-1)]
        kind, j = i % N_MIXERS, i // N_MIXERS
        if kind == 0:
            proj = functools.partial(mm, x3, ret_w_in, j, mod=(sh_a, sc_a))
            q = proj(n0=0, n_out=RET_QK, out_dtype=BF16, epi="rot", rot=(cos, sin), name="ret_q")
            k = proj(n0=RET_QK, n_out=RET_QK, out_dtype=BF16, epi="rot", rot=(cos, sin), scale=RET_DK ** -0.5,
                     name="ret_k")
            v = proj(n0=2 * RET_QK, n_out=RET_V, out_dtype=BF16, name="ret_v")
            g = proj(n0=2 * RET_QK + RET_V, n_out=RET_V, out_dtype=F32, name="ret_g")
            if decode:
                padr = lambda t: jnp.pad(t.reshape(B, L, -1), ((0, 0), (0, chunk_pad - L), (0, 0)))
                o, s = retention_scan(padr(q), padr(k), padr(v), padr(g), tables, chunk_pad, s0=ret_state,
                                      s0_layer=j, heads=RET_HEADS)
                o = o[:, :L].reshape(1, n_rows, RET_V)
            else:
                o, s = retention_scan(q, k, v, g, tables, chunk_pad)
            ret_new.append(s)
            x3 = mm(o, ret_w_out, j, n0=0, n_out=D, out_dtype=F32, epi="res", res=(x3, g_a), name="ret_out")
        elif kind == 1:
            vecs = [v[:, None, :] for v in jnp.split(mods[i], 6, axis=-1)[:3]]
            if decode:
                buf16 = jnp.pad(pool_state[j], ((0, 0), (1, 0), (0, 0)))
                tm = L
            else:
                buf16 = jnp.zeros((B, POOL_HALO, D), F32)
                tm = min(L, 512)
            xn, tail = pool_layer(x3.reshape(B, L, D), buf16, (vecs[0], vecs[1]), vecs[2], pool_w, pool_scale, j,
                                  pos0, tm=tm)
            x3 = xn.reshape(x3.shape)
            extras["pool"] = tail[:, 1:, :]
        else:
            fproj = functools.partial(mm, x3, fox_wt, j, mod=(sh_a, sc_a), w_t=True)
            if decode:
                ck, cv, cl, pt = fox_past
                q = fproj(n0=0, n_out=D, out_dtype=F32, scale=FOX_DH ** -0.5, name="fox_q")
                lft = fproj(n0=3 * D, n_out=FOX_HEADS, out_dtype=F32, out_t=True, epi="logsig",
                            bias=fox_b_f[j].reshape(FOX_HEADS, 1), name="fox_logft")
                k = fproj(n0=D, n_out=D, out_dtype=F32, name="fox_k")
                v = fproj(n0=2 * D, n_out=D, out_dtype=F32, name="fox_v")
                o = fox_decode(q.reshape(B, L, D), k.reshape(B, L, D), v.reshape(B, L, D), lft[0],
                               jnp.transpose(ck[j], (0, 2, 3, 1)), jnp.transpose(cv[j], (0, 2, 3, 1)),
                               jnp.transpose(cl[j], (0, 2, 1)), pt)
                o = o.reshape(1, n_rows, D)
                extras["k"] = k.reshape(B, L, FOX_HEADS, FOX_DH)
                extras["v"] = v.reshape(B, L, FOX_HEADS, FOX_DH)
                extras["l"] = jnp.transpose(lft[0].reshape(FOX_HEADS, B, L), (1, 2, 0))
            else:
                qt = fproj(n0=0, n_out=D, out_dtype=BF16, out_t=True, scale=FOX_DH ** -0.5, name="fox_qt")
                kb = fproj(n0=D, n_out=D, out_dtype=BF16, name="fox_kb")
                kt = fproj(n0=D, n_out=D, out_dtype=F32, out_t=True, name="fox_kt")
                vt = fproj(n0=2 * D, n_out=D, out_dtype=F32, out_t=True, name="fox_vt")
                lf = fproj(n0=3 * D, n_out=FOX_HEADS, out_dtype=F32, epi="logsig",
                           bias=fox_b_f[j].reshape(1, FOX_HEADS), name="fox_logf")
                o = fox_flash(qt, kb, fox_bias_features(lf), vt)
                unt = lambda t: jnp.transpose(t.reshape(B, FOX_HEADS, FOX_DH, L), (0, 3, 1, 2))
                extras["k"], extras["v"] = unt(kt), unt(vt)
                extras["l"] = lf
            x3 = mm(o, fox_w_out, j, n0=0, n_out=D, out_dtype=F32, epi="res", res=(x3, g_a), name="fox_out")
        ml = i // 2
        if i % 2 == 0:
            x3 = ffn(x3, (sh_f, sc_f), g_f, ffn_w_gu, ffn_w_down, ml, name="ffn_dense")
        else:
            gates = router(x3, (sh_f, sc_f), w_rt_pad, b_rt_pad, ml)
            if decode:
                x3 = ffn(x3, (sh_f, sc_f), g_f, moe_w_gu, moe_w_down, ml * N_EXPERTS, gates=gates, name="ffn_moe")
            else:
                closing = final_g if i == DEPTH - 1 else None
                x3, moe_buf = moe_sparse(x3, (sh_f, sc_f), g_f, gates, moe_w_gu, moe_w_down, ml * N_EXPERTS, closing,
                                         moe_buf)
    fused_final = (not decode) and (DEPTH - 1) % 2 == 1
    out = (x3 if fused_final else final_norm(x3, final_g)).reshape(B, L, D)
    return (out, jnp.stack(ret_new), extras["pool"][None], extras["k"][None], extras["v"][None], extras["l"][None])


def kernel(x_prompt, x_sample, state_ret, state_pool, cache_fox_k, cache_fox_v, cache_fox_logf, page_table,
           c_prompt, c_sample, ada_w, ada_b, ret_w_in, ret_w_out, pool_w, pool_scale, fox_w_in, fox_b_f, fox_w_out,
           ffn_w_gu, ffn_w_down, moe_w_router, moe_b_router, moe_w_gu, moe_w_down, final_g):
    bp, bs = x_prompt.shape[0], x_sample.shape[0]
    rows = -(-(bp + bs) // 8) * 8
    c_all = jnp.concatenate([c_prompt, c_sample, jnp.zeros((rows - bp - bs, D_MODEL), F32)], axis=0)
    mods = ada_mods(c_all, ada_w, ada_b)
    n_moe = moe_w_router.shape[0]
    params = (
        ret_w_in, ret_w_out, pool_w, pool_scale,
        jnp.swapaxes(fox_w_in, 1, 2),
        fox_b_f, fox_w_out, ffn_w_gu, ffn_w_down,
        jnp.pad(jnp.swapaxes(moe_w_router, 1, 2), ((0, 0), (0, LANES - N_EXPERTS), (0, 0))),
        jnp.pad(moe_b_router, ((0, 0), (0, LANES - N_EXPERTS))).reshape(n_moe, 1, LANES),
        moe_w_gu.reshape((n_moe * N_EXPERTS,) + moe_w_gu.shape[2:]),
        moe_w_down.reshape((n_moe * N_EXPERTS,) + moe_w_down.shape[2:]),
        final_g,
    )
    y_p, ret_p, pool_p, k_p, v_p, l_p = _trunk(x_prompt, mods[:, :bp], None, None, None, 0, params)
    n_past = page_table.shape[1] * PAGE_SIZE
    y_s, ret_s, pool_s, k_s, v_s, l_s = _trunk(
        x_sample, mods[:, bp:bp + bs], state_ret, state_pool,
        (cache_fox_k, cache_fox_v, cache_fox_logf, page_table), n_past, params)
    return (y_p, y_s, ret_p, ret_s, pool_p, pool_s, k_p, k_s, v_p, v_s, l_p, l_s)
```

```python
import functools
import math

import jax
import jax.numpy as jnp
from jax import lax
from jax.experimental import pallas as pl
from jax.experimental.pallas import tpu as pltpu

F32 = jnp.float32
BF16 = jnp.bfloat16

D_MODEL = 1024
DEPTH = 4
PAGE_SIZE = 128
N_MIXERS = 3
RET_HEADS = 4
RET_DK = D_MODEL // RET_HEADS
RET_DV = 2 * D_MODEL // RET_HEADS
RET_QK = RET_HEADS * RET_DK
RET_V = RET_HEADS * RET_DV
ROPE_BASE = 10000.0
POOL_WINDOWS = (2, 4, 8, 16)
POOL_GW = D_MODEL // len(POOL_WINDOWS)
POOL_BUF = max(POOL_WINDOWS) - 1
POOL_HALO = POOL_BUF + 1
FOX_HEADS = 16
FOX_DH = D_MODEL // FOX_HEADS
FFN_DIM = 2816
N_EXPERTS = 8
EPS = 1e-6
NEG_BIG = -1e30

V7X_VMEM_BYTES = 64 * 1024 * 1024
VMEM_LIMIT = V7X_VMEM_BYTES - 8 * 1024 * 1024
LANES = 128
FFN_TF = 256


def _cparams(n_axes):
    return pltpu.CompilerParams(dimension_semantics=("arbitrary",) * n_axes, vmem_limit_bytes=VMEM_LIMIT)


def _sigmoid(x):
    return 1.0 / (1.0 + jnp.exp(-x))


def _modulate(x, shift, scale):
    xf = x.astype(F32)
    ms = jnp.mean(xf * xf, axis=-1, keepdims=True)
    return (xf * lax.rsqrt(ms + EPS)) * (1.0 + scale) + shift


def _row_spec(arr, tm, width, col_fn):
    if arr.shape[1] == 1:
        return pl.BlockSpec((1, 1, width), lambda b, m, *r: (b, 0, col_fn(*r)))
    return pl.BlockSpec((1, tm, width), lambda b, m, *r: (b, m, col_fn(*r)))


def _ada_body(c_ref, w_ref, b_ref, o_ref):
    c = c_ref[...]
    cond = c * _sigmoid(c)
    o_ref[0] = jnp.dot(cond.astype(BF16), w_ref[0].astype(BF16), preferred_element_type=F32) + b_ref[0]


def ada_mods(c_all, ada_w, ada_b):
    rows = c_all.shape[0]
    n_out = ada_w.shape[2]
    tn = 1024
    return pl.pallas_call(
        _ada_body,
        grid=(DEPTH, n_out // tn),
        in_specs=[
            pl.BlockSpec((rows, D_MODEL), lambda i, n: (0, 0)),
            pl.BlockSpec((1, D_MODEL, tn), lambda i, n: (i, 0, n)),
            pl.BlockSpec((1, 1, tn), lambda i, n: (i, 0, n)),
        ],
        out_specs=pl.BlockSpec((1, rows, tn), lambda i, n: (i, 0, n)),
        out_shape=jax.ShapeDtypeStruct((DEPTH, rows, n_out), F32),
        compiler_params=_cparams(2),
        name="ada_mods",
    )(c_all, ada_w, ada_b.reshape(DEPTH, 1, n_out))


def _mm_body(*refs, has_mod, epi, w_t, out_t, scale):
    it = iter(refs)
    x_ref = next(it)
    if has_mod:
        sh_ref, sc_ref = next(it), next(it)
    w_ref = next(it)
    if epi == "rot":
        cos_ref, sin_ref = next(it), next(it)
    elif epi == "res":
        res_ref, gate_ref = next(it), next(it)
    elif epi == "logsig":
        b_ref = next(it)
    o_ref = next(it)
    n = pl.program_id(2)
    if has_mod:
        h_scr = next(it)

        @pl.when(n == 0)
        def _():
            h_scr[...] = _modulate(x_ref[0], sh_ref[0], sc_ref[0]).astype(BF16)

        lhs = h_scr[...]
    else:
        lhs = x_ref[0].astype(BF16)
    w = w_ref[0].astype(BF16)
    last = (((1,), (1,)), ((), ()))
    if not w_t:
        acc = jnp.dot(lhs, w, preferred_element_type=F32)
    elif not out_t:
        acc = lax.dot_general(lhs, w, last, preferred_element_type=F32)
    else:
        acc = lax.dot_general(w, lhs, last, preferred_element_type=F32)
    if epi == "rot":
        acc = _rotary(acc, cos_ref[...], sin_ref[...])
    elif epi == "res":
        acc = res_ref[0] + gate_ref[0] * acc
    elif epi == "logsig":
        z = acc + b_ref[...]
        acc = jnp.minimum(z, 0.0) - jnp.log1p(jnp.exp(-jnp.abs(z)))
    if scale != 1.0:
        acc = acc * scale
    o_ref[0] = acc.astype(o_ref.dtype)


def mm(x3, w3, wl, *, n0, n_out, out_dtype, mod=None, epi="plain", w_t=False, out_t=False, scale=1.0,
       rot=None, res=None, bias=None, tm=None, tn=None, name="mm"):
    bx, L, K = x3.shape
    tm = tm or min(L, 1024 if (epi in ("rot", "res") or K > 1024) else 2048)
    tn = tn or min(n_out, 512 if K > 1024 else 1024)
    assert L % tm == 0 and n_out % tn == 0 and n0 % tn == 0
    nb0 = n0 // tn
    has_mod = mod is not None
    in_specs = [pl.BlockSpec((1, tm, K), lambda b, m, n: (b, m, 0))]
    args = [x3]
    if has_mod:
        for a in mod:
            in_specs.append(_row_spec(a, tm, K, lambda n: 0))
            args.append(a)
    if w_t:
        in_specs.append(pl.BlockSpec((1, tn, K), lambda b, m, n: (wl, nb0 + n, 0)))
    else:
        in_specs.append(pl.BlockSpec((1, K, tn), lambda b, m, n: (wl, 0, nb0 + n)))
    args.append(w3)
    if epi == "rot":
        for a in rot:
            in_specs.append(pl.BlockSpec((tm, tn), lambda b, m, n: (m, n)))
            args.append(a)
    elif epi == "res":
        in_specs.append(pl.BlockSpec((1, tm, tn), lambda b, m, n: (b, m, n)))
        in_specs.append(_row_spec(res[1], tm, tn, lambda n: n))
        args.extend(res)
    elif epi == "logsig":
        in_specs.append(pl.BlockSpec(bias.shape, lambda b, m, n: (0, 0)))
        args.append(bias)
    if out_t:
        out_spec = pl.BlockSpec((1, tn, tm), lambda b, m, n: (b, n, m))
        out_shape = jax.ShapeDtypeStruct((bx, n_out, L), out_dtype)
    else:
        out_spec = pl.BlockSpec((1, tm, tn), lambda b, m, n: (b, m, n))
        out_shape = jax.ShapeDtypeStruct((bx, L, n_out), out_dtype)
    return pl.pallas_call(
        functools.partial(_mm_body, has_mod=has_mod, epi=epi, w_t=w_t, out_t=out_t, scale=scale),
        grid=(bx, L // tm, n_out // tn),
        in_specs=in_specs,
        out_specs=out_spec,
        out_shape=out_shape,
        scratch_shapes=[pltpu.VMEM((tm, K), BF16)] if has_mod else [],
        compiler_params=_cparams(3),
        name=name,
    )(*args)


def _rotary(acc, cos, sin_signed):
    width = acc.shape[1]
    lane = lax.broadcasted_iota(jnp.int32, acc.shape, 1)
    partner = jnp.where(lane % 2 == 0, pltpu.roll(acc, width - 1, axis=1), pltpu.roll(acc, 1, axis=1))
    return acc * cos + partner * sin_signed


def _ffn_body(*refs, moe, nf, ne):
    it = iter(refs)
    x_ref, sh_ref, sc_ref, gate_ref = next(it), next(it), next(it), next(it)
    gw_ref = next(it) if moe else None
    wg_ref, wu_ref, wd_ref, o_ref, h_scr, acc_scr = next(it), next(it), next(it), next(it), next(it), next(it)
    tot_scr = next(it) if moe else None
    e = pl.program_id(2)
    f = pl.program_id(3)

    @pl.when((e == 0) & (f == 0))
    def _():
        h_scr[...] = _modulate(x_ref[0], sh_ref[0], sc_ref[0]).astype(BF16)

    @pl.when(f == 0)
    def _():
        acc_scr[...] = jnp.zeros_like(acc_scr)

    if moe:

        @pl.when((e == 0) & (f == 0))
        def _():
            tot_scr[...] = jnp.zeros_like(tot_scr)

    h = h_scr[...]
    a = jnp.dot(h, wg_ref[0].astype(BF16), preferred_element_type=F32)
    b = jnp.dot(h, wu_ref[0].astype(BF16), preferred_element_type=F32)
    mid = (a * _sigmoid(a) * b).astype(BF16)
    acc_scr[...] += jnp.dot(mid, wd_ref[0].astype(BF16), preferred_element_type=F32)

    if not moe:

        @pl.when(f == nf - 1)
        def _():
            o_ref[0] = x_ref[0] + gate_ref[0] * acc_scr[...]

    else:

        @pl.when(f == nf - 1)
        def _():
            gw = gw_ref[0]
            lane = lax.broadcasted_iota(jnp.int32, gw.shape, 1)
            col = jnp.sum(jnp.where(lane == e, gw, 0.0), axis=1, keepdims=True)
            tot_scr[...] += col * acc_scr[...]

            @pl.when(e == ne - 1)
            def _():
                o_ref[0] = x_ref[0] + gate_ref[0] * tot_scr[...]


def ffn(x3, mod, gate, w_gu3, w_down3, wl, *, gates=None, tm=None, name="ffn"):
    bx, L, D = x3.shape
    moe = gates is not None
    ne = N_EXPERTS if moe else 1
    tm = tm or min(L, 1024 if moe else 2048)
    tf = FFN_TF if tm > 256 else MOE_TF
    nf = FFN_DIM // tf
    assert L % tm == 0 and FFN_DIM % tf == 0
    in_specs = [pl.BlockSpec((1, tm, D), lambda b, m, e, f: (b, m, 0))]
    args = [x3]
    for a in (*mod, gate):
        in_specs.append(_row_spec(a, tm, D, lambda e, f: 0))
        args.append(a)
    if moe:
        in_specs.append(pl.BlockSpec((1, tm, LANES), lambda b, m, e, f: (b, m, 0)))
        args.append(gates)
    in_specs += [
        pl.BlockSpec((1, D, tf), lambda b, m, e, f: (wl + e, 0, f)),
        pl.BlockSpec((1, D, tf), lambda b, m, e, f: (wl + e, 0, nf + f)),
        pl.BlockSpec((1, tf, D), lambda b, m, e, f: (wl + e, f, 0)),
    ]
    args += [w_gu3, w_gu3, w_down3]
    scratch = [pltpu.VMEM((tm, D), BF16), pltpu.VMEM((tm, D), F32)]
    if moe:
        scratch.append(pltpu.VMEM((tm, D), F32))
    return pl.pallas_call(
        functools.partial(_ffn_body, moe=moe, nf=nf, ne=ne),
        grid=(bx, L // tm, ne, nf),
        in_specs=in_specs,
        out_specs=pl.BlockSpec((1, tm, D), lambda b, m, e, f: (b, m, 0)),
        out_shape=jax.ShapeDtypeStruct((bx, L, D), F32),
        scratch_shapes=scratch,
        compiler_params=_cparams(4),
        name=name,
    )(*args)


def _router_body(x_ref, sh_ref, sc_ref, w_ref, b_ref, o_ref):
    h = _modulate(x_ref[0], sh_ref[0], sc_ref[0]).astype(BF16)
    logits = lax.dot_general(h, w_ref[0].astype(BF16), (((1,), (1,)), ((), ())), preferred_element_type=F32)
    logits = logits + b_ref[0]
    lane = lax.broadcasted_iota(jnp.int32, logits.shape, 1).astype(F32)
    lg = jnp.where(lane < N_EXPERTS, logits, -jnp.inf)
    m1 = jnp.max(lg, axis=1, keepdims=True)
    i1 = jnp.min(jnp.where(lg == m1, lane, float(LANES)), axis=1, keepdims=True)
    lg2 = jnp.where(lane == i1, -jnp.inf, lg)
    m2 = jnp.max(lg2, axis=1, keepdims=True)
    i2 = jnp.min(jnp.where(lg2 == m2, lane, float(LANES)), axis=1, keepdims=True)
    e2 = jnp.exp(m2 - m1)
    den = 1.0 + e2
    o_ref[0] = jnp.where(lane == i1, 1.0 / den, 0.0) + jnp.where(lane == i2, e2 / den, 0.0)


def router(x3, mod, w_rt_pad, b_pad, wl, *, tm=None):
    bx, L, D = x3.shape
    tm = tm or min(L, 1024)
    in_specs = [pl.BlockSpec((1, tm, D), lambda b, m: (b, m, 0))]
    args = [x3]
    for a in mod:
        in_specs.append(_row_spec(a, tm, D, lambda: 0))
        args.append(a)
    in_specs += [
        pl.BlockSpec((1, LANES, D), lambda b, m: (wl, 0, 0)),
        pl.BlockSpec((1, 1, LANES), lambda b, m: (wl, 0, 0)),
    ]
    args += [w_rt_pad, b_pad]
    return pl.pallas_call(
        _router_body,
        grid=(bx, L // tm),
        in_specs=in_specs,
        out_specs=pl.BlockSpec((1, tm, LANES), lambda b, m: (b, m, 0)),
        out_shape=jax.ShapeDtypeStruct((bx, L, LANES), F32),
        compiler_params=_cparams(2),
        name="router",
    )(*args)


MOE_T = 256
MOE_CH = 128
MOE_TM = 384
MOE_TF = FFN_DIM // 2
MOE_ALIGN = 8
MOE_STAGE = -(-(2 * MOE_T + N_EXPERTS * (MOE_ALIGN - 1) + MOE_CH) // 8) * 8


def _moe_plan(gates2, n_tokens):
    nb = n_tokens // MOE_T
    routed = (gates2[:, :N_EXPERTS] > 0).reshape(nb, MOE_T, N_EXPERTS)
    ri = routed.astype(jnp.int32)
    t_i = jnp.arange(MOE_T)
    before = (t_i[None, :] < t_i[:, None]).astype(F32)
    rank = jnp.einsum("ts,bse->bte", before, routed.astype(F32)).astype(jnp.int32)
    cnt = jnp.sum(ri, axis=1)
    cnt_al = -(-cnt // MOE_ALIGN) * MOE_ALIGN
    lo = jnp.cumsum(cnt_al, axis=1) - cnt_al
    total = jnp.sum(cnt_al, axis=0)
    region = -(-(total + MOE_CH) // MOE_TM) * MOE_TM
    ends = jnp.cumsum(region)
    off = ends - region
    pos = off[None, :] + jnp.cumsum(cnt_al, axis=0) - cnt_al
    nch = -(-cnt_al // MOE_CH)
    m_pad = -(-(2 * n_tokens + nb * N_EXPERTS * (MOE_ALIGN - 1) + N_EXPERTS * (MOE_CH + MOE_TM)) // MOE_TM) * MOE_TM
    n_tiles = m_pad // MOE_TM
    tile_start = jnp.arange(n_tiles, dtype=jnp.int32) * MOE_TM
    tile_e = jnp.minimum(jnp.sum(tile_start[:, None] >= ends[None, :], axis=1), N_EXPERTS - 1).astype(jnp.int32)
    n_used = (ends[-1] // MOE_TM).astype(jnp.int32).reshape(1)
    dest = jnp.where(routed, lo[:, None, :] + rank, -1)
    d_hi = jnp.max(dest, axis=2)
    d_lo = jnp.min(jnp.where(routed, dest, MOE_STAGE), axis=2)
    dd = jnp.stack([d_hi, d_lo], axis=1).astype(jnp.int32)
    eidx = jnp.arange(N_EXPERTS, dtype=jnp.int32)
    e_a = jnp.min(jnp.where(routed, eidx, N_EXPERTS), axis=2)
    e_b = jnp.max(jnp.where(routed, eidx, -1), axis=2)
    g3 = gates2[:, :N_EXPERTS].reshape(nb, MOE_T, N_EXPERTS)
    at = lambda a, e: jnp.sum(jnp.where(eidx == e[..., None], a, 0), axis=2)
    r_a, r_b = at(rank, e_a), at(rank, e_b)
    w_a, w_b = at(g3, e_a), jnp.where(e_b != e_a, at(g3, e_b), 0.0)
    col = lambda e, r, c: jnp.where(r // MOE_CH == c, e * MOE_CH + r % MOE_CH, -1).astype(F32)
    cmeta = jnp.stack([col(e_a, r_a, 0), col(e_b, r_b, 0), col(e_a, r_a, 1), col(e_b, r_b, 1), w_a, w_b], axis=-1)
    cmeta = jnp.pad(cmeta.reshape(n_tokens, 6), ((0, 0), (0, LANES - 6)))
    two = (jnp.max(nch, axis=1) > 1).astype(jnp.int32)
    flat = lambda a: a.reshape(-1).astype(jnp.int32)
    return dict(lo=flat(lo // MOE_ALIGN), pos=flat(pos // MOE_ALIGN), nch=flat(nch), tile_e=tile_e, n_used=n_used,
                dd=dd, cmeta=cmeta, two=two, m_pad=m_pad, n_tiles=n_tiles, nb=nb)


def _dispatch_body(lo_ref, pos_ref, nch_ref, x_ref, sh_ref, sc_ref, dd_ref, xs_in_ref, xs_ref, stage_scr, sem,
                   *, nb):
    del xs_in_ref
    b = pl.program_id(0)
    slot = b % 2
    h = _modulate(x_ref[...], sh_ref[0], sc_ref[0]).astype(BF16)
    r = lax.broadcasted_iota(jnp.int32, (MOE_STAGE, MOE_T), 0)
    dd = dd_ref[0]
    onehot = jnp.where((r == dd[0:1, :]) | (r == dd[1:2, :]), 1.0, 0.0).astype(BF16)
    stage_scr[slot] = jnp.dot(onehot, h, preferred_element_type=F32)

    def seg_copy(blk, e, c):
        src0 = pl.multiple_of(lo_ref[blk * N_EXPERTS + e] * MOE_ALIGN + c * MOE_CH, MOE_ALIGN)
        dst0 = pl.multiple_of(pos_ref[blk * N_EXPERTS + e] * MOE_ALIGN + c * MOE_CH, MOE_ALIGN)
        return pltpu.make_async_copy(stage_scr.at[blk % 2, pl.ds(src0, MOE_CH)], xs_ref.at[pl.ds(dst0, MOE_CH)],
                                     sem.at[blk % 2, e, c])

    def for_segments(blk, fn):
        for e in range(N_EXPERTS):
            for c in range(2):
                @pl.when(c < nch_ref[blk * N_EXPERTS + e])
                def _():
                    fn(seg_copy(blk, e, c))

    @pl.when(b > 0)
    def _():
        for_segments(b - 1, lambda cp: cp.wait())

    for_segments(b, lambda cp: cp.start())

    @pl.when(b == nb - 1)
    def _():
        for_segments(b, lambda cp: cp.wait())


def moe_dispatch(x2, mod, plan, seq_len, buf=None):
    n_tokens, D = x2.shape
    if buf is None:
        buf = jnp.zeros((plan["m_pad"], D), F32)
    per_seq = seq_len // MOE_T
    nb = plan["nb"]
    vec = lambda: pl.BlockSpec((1, 1, D), lambda b, *_: (b // per_seq, 0, 0))
    grid_spec = pltpu.PrefetchScalarGridSpec(
        num_scalar_prefetch=3,
        grid=(nb,),
        in_specs=[
            pl.BlockSpec((MOE_T, D), lambda b, *_: (b, 0)),
            vec(), vec(),
            pl.BlockSpec((1, 2, MOE_T), lambda b, *_: (b, 0, 0)),
            pl.BlockSpec(memory_space=pltpu.MemorySpace.HBM),
        ],
        out_specs=pl.BlockSpec(memory_space=pltpu.MemorySpace.HBM),
        scratch_shapes=[pltpu.VMEM((2, MOE_STAGE, D), F32), pltpu.SemaphoreType.DMA((2, N_EXPERTS, 2))],
    )
    return pl.pallas_call(
        functools.partial(_dispatch_body, nb=nb),
        grid_spec=grid_spec,
        out_shape=jax.ShapeDtypeStruct((plan["m_pad"], D), F32),
        input_output_aliases={7: 0},
        compiler_params=_cparams(1),
        name="moe_dispatch",
    )(plan["lo"], plan["pos"], plan["nch"], x2, mod[0], mod[1], plan["dd"], buf)


def _gffn_body(te_ref, nu_ref, *refs, has_prev):
    it = iter(refs)
    xs_ref = next(it)
    yp_ref = next(it) if has_prev else None
    wg_ref, wu_ref, wd_ref, o_ref, wg_scr, wu_scr, wd_scr = (next(it) for _ in range(7))
    t = pl.program_id(0)
    e_here = te_ref[t]
    e_prev = te_ref[jnp.maximum(t - 1, 0)]

    @pl.when(t < nu_ref[0])
    def _():
        @pl.when((t == 0) | (e_here != e_prev))
        def _():
            wg_scr[...] = wg_ref[0].astype(BF16)
            wu_scr[...] = wu_ref[0].astype(BF16)
            wd_scr[...] = wd_ref[0].astype(BF16)

        h = xs_ref[...].astype(BF16)
        a = jnp.dot(h, wg_scr[...], preferred_element_type=F32)
        b = jnp.dot(h, wu_scr[...], preferred_element_type=F32)
        mid = (a * _sigmoid(a) * b).astype(BF16)
        y = jnp.dot(mid, wd_scr[...], preferred_element_type=F32)
        o_ref[...] = (yp_ref[...] + y) if has_prev else y

    @pl.when(t >= nu_ref[0])
    def _():
        o_ref[...] = jnp.zeros_like(o_ref)


def moe_grouped_ffn(xs, w_gu3, w_down3, wl, plan, f, y_prev=None):
    m_pad, D = xs.shape
    nf = FFN_DIM // MOE_TF
    has_prev = y_prev is not None
    row = lambda t, te, nu: (jnp.minimum(t, nu[0] - 1), 0)
    tile = lambda: pl.BlockSpec((MOE_TM, D), row)
    in_specs = [tile()] + ([tile()] if has_prev else [])
    in_specs += [
        pl.BlockSpec((1, D, MOE_TF), lambda t, te, nu: (wl + te[t], 0, f)),
        pl.BlockSpec((1, D, MOE_TF), lambda t, te, nu: (wl + te[t], 0, nf + f)),
        pl.BlockSpec((1, MOE_TF, D), lambda t, te, nu: (wl + te[t], f, 0), pipeline_mode=pl.Buffered(1)),
    ]
    grid_spec = pltpu.PrefetchScalarGridSpec(
        num_scalar_prefetch=2,
        grid=(plan["n_tiles"],),
        in_specs=in_specs,
        out_specs=pl.BlockSpec((MOE_TM, D), lambda t, te, nu: (t, 0)),
        scratch_shapes=[pltpu.VMEM((D, MOE_TF), BF16), pltpu.VMEM((D, MOE_TF), BF16), pltpu.VMEM((MOE_TF, D), BF16)],
    )
    args = [xs] + ([y_prev] if has_prev else []) + [w_gu3, w_gu3, w_down3]
    return pl.pallas_call(
        functools.partial(_gffn_body, has_prev=has_prev),
        grid_spec=grid_spec,
        out_shape=jax.ShapeDtypeStruct((m_pad, D), F32),
        compiler_params=_cparams(1),
        name=f"moe_grouped_ffn{f}",
    )(plan["tile_e"], plan["n_used"], *args)


def _combine_body(pos_ref, nch_ref, two_ref, x_ref, gate_ref, cm_ref, *refs, final):
    y_refs = refs[:N_EXPERTS]
    refs = refs[N_EXPERTS:]
    y_hbm, fg_ref = (refs[0], refs[1]) if final else (refs[0], None)
    o_ref, yhi_scr, ylo_scr, over_scr, sem = refs[2 if final else 1:]
    b = pl.program_id(0)

    def gathered(window, chunk):
        for e in range(N_EXPERTS):
            y = window(e)
            y_hi = y.astype(BF16)
            yhi_scr[e * MOE_CH:(e + 1) * MOE_CH, :] = y_hi
            ylo_scr[e * MOE_CH:(e + 1) * MOE_CH, :] = (y - y_hi.astype(F32)).astype(BF16)
        cm = cm_ref[...]
        lane = lax.broadcasted_iota(jnp.int32, (MOE_T, N_EXPERTS * MOE_CH), 1).astype(F32)
        picks = [jnp.where(lane == cm[:, 2 * chunk + k:2 * chunk + k + 1], 1.0, 0.0).astype(BF16) for k in range(2)]
        onehot = jnp.concatenate(picks, axis=0)
        rows = (jnp.dot(onehot, yhi_scr[...], preferred_element_type=F32)
                + jnp.dot(onehot, ylo_scr[...], preferred_element_type=F32))
        return cm[:, 4:5] * rows[:MOE_T] + cm[:, 5:6] * rows[MOE_T:]

    o_ref[...] = x_ref[...] + gate_ref[0] * gathered(lambda e: y_refs[e][...], 0)

    @pl.when(two_ref[b] > 0)
    def _():
        for e in range(N_EXPERTS):
            s = b * N_EXPERTS + e

            @pl.when(nch_ref[s] > 1)
            def _():
                start = pl.multiple_of(pos_ref[s] * MOE_ALIGN + MOE_CH, MOE_ALIGN)
                cp = pltpu.make_async_copy(y_hbm.at[pl.ds(start, MOE_CH)], over_scr.at[e], sem.at[e])
                cp.start()
                cp.wait()

            @pl.when(nch_ref[s] <= 1)
            def _():
                over_scr[e] = jnp.zeros((MOE_CH, over_scr.shape[2]), F32)

        o_ref[...] += gate_ref[0] * gathered(lambda e: over_scr[e], 1)

    if final:
        xo = o_ref[...]
        o_ref[...] = xo * lax.rsqrt(jnp.mean(xo * xo, axis=-1, keepdims=True) + EPS) * fg_ref[...]


def moe_combine(x2, gate, y, plan, seq_len, final_g=None):
    n_tokens, D = x2.shape
    per_seq = seq_len // MOE_T
    final = final_g is not None
    tail_specs = [pl.BlockSpec(memory_space=pltpu.MemorySpace.HBM)]
    tail_args = [y]
    if final:
        tail_specs.append(pl.BlockSpec((1, D), lambda b, *_: (0, 0)))
        tail_args.append(final_g.reshape(1, D))

    def window(e):
        return pl.BlockSpec((pl.Element(MOE_CH), pl.Element(D)),
                            lambda b, pos, nch, two: (pos[b * N_EXPERTS + e] * MOE_ALIGN, 0))

    blk = lambda w: pl.BlockSpec((MOE_T, w), lambda b, *_: (b, 0))
    grid_spec = pltpu.PrefetchScalarGridSpec(
        num_scalar_prefetch=3,
        grid=(plan["nb"],),
        in_specs=[blk(D), pl.BlockSpec((1, 1, D), lambda b, *_: (b // per_seq, 0, 0)), blk(LANES)]
        + [window(e) for e in range(N_EXPERTS)] + tail_specs,
        out_specs=blk(D),
        scratch_shapes=[pltpu.VMEM((N_EXPERTS * MOE_CH, D), BF16), pltpu.VMEM((N_EXPERTS * MOE_CH, D), BF16),
                        pltpu.VMEM((N_EXPERTS, MOE_CH, D), F32), pltpu.SemaphoreType.DMA((N_EXPERTS,))],
    )
    return pl.pallas_call(
        functools.partial(_combine_body, final=final),
        grid_spec=grid_spec,
        out_shape=jax.ShapeDtypeStruct((n_tokens, D), F32),
        compiler_params=_cparams(1),
        name="moe_combine",
    )(plan["pos"], plan["nch"], plan["two"], x2, gate, plan["cmeta"], *([y] * N_EXPERTS), *tail_args)


def moe_sparse(x3, mod, gate, gates, w_gu3, w_down3, wl, final_g=None, buf=None):
    B, L, D = x3.shape
    n_tokens = B * L
    assert L % MOE_T == 0 and FFN_DIM % MOE_TF == 0
    x2 = x3.reshape(n_tokens, D)
    gates2 = gates.reshape(n_tokens, LANES)
    plan = _moe_plan(gates2, n_tokens)
    xs = moe_dispatch(x2, mod, plan, L, buf)
    y = None
    for f in range(FFN_DIM // MOE_TF):
        y = moe_grouped_ffn(xs, w_gu3, w_down3, wl, plan, f, y_prev=y)
    return moe_combine(x2, gate, y, plan, L, final_g).reshape(B, L, D), xs


RET_CHUNKS_PER_STEP = 4


def _ret_body(*refs, zero_init, nc, chunk, per_step, heads):
    it = iter(refs)
    q_ref, k_ref, v_ref, g_ref = next(it), next(it), next(it), next(it)
    s0_ref = None if zero_init else next(it)
    inner_ref, qd_ref, kd_ref, cd_ref = next(it), next(it), next(it), next(it)
    o_ref, sout_ref, s_scr = next(it), next(it), next(it)
    c = pl.program_id(2)

    @pl.when(c == 0)
    def _():
        if zero_init:
            s_scr[...] = jnp.zeros_like(s_scr)
        else:
            s_scr[...] = s0_ref[0, 0]

    for hi in range(heads):
        kcols = slice(hi * RET_DK, (hi + 1) * RET_DK)
        vcols = slice(hi * RET_DV, (hi + 1) * RET_DV)
        s = s_scr[hi]
        for ci in range(per_step):
            rows = slice(ci * chunk, (ci + 1) * chunk)
            q = q_ref[0, rows, kcols]
            k = k_ref[0, rows, kcols]
            v = v_ref[0, rows, vcols]
            att = lax.dot_general(q, k, (((1,), (1,)), ((), ())), preferred_element_type=F32) * inner_ref[hi]
            inner = jnp.dot(att.astype(BF16), v, preferred_element_type=F32)
            cross = jnp.dot(q, s.astype(BF16), preferred_element_type=F32) * qd_ref[hi]
            kdt = (k.astype(F32) * kd_ref[hi]).T.astype(BF16)
            s = s * cd_ref[hi] + jnp.dot(kdt, v, preferred_element_type=F32)
            o = inner + cross
            on = o * lax.rsqrt(jnp.mean(o * o, axis=-1, keepdims=True) + EPS)
            g = g_ref[0, rows, vcols]
            o_ref[0, rows, vcols] = (g * _sigmoid(g) * on).astype(o_ref.dtype)
        s_scr[hi] = s

    @pl.when(c == nc - 1)
    def _():
        sout_ref[0] = s_scr[...]


def retention_scan(q, k, v, g, tables, chunk, *, s0=None, s0_layer=0, heads=1):
    B, L, _ = q.shape
    per_step = math.gcd(L // chunk, RET_CHUNKS_PER_STEP)
    rows = per_step * chunk
    nc = L // rows
    inner, qd, kd, cd = tables
    zero_init = s0 is None
    in_specs = [
        pl.BlockSpec((1, rows, heads * RET_DK), lambda h, b, c: (b, c, h)),
        pl.BlockSpec((1, rows, heads * RET_DK), lambda h, b, c: (b, c, h)),
        pl.BlockSpec((1, rows, heads * RET_DV), lambda h, b, c: (b, c, h)),
        pl.BlockSpec((1, rows, heads * RET_DV), lambda h, b, c: (b, c, h)),
    ]
    args = [q, k, v, g]
    if not zero_init:
        in_specs.append(pl.BlockSpec((1, 1, heads, RET_DK, RET_DV), lambda h, b, c: (s0_layer, b, h, 0, 0)))
        args.append(s0)
    in_specs += [
        pl.BlockSpec((heads, chunk, chunk), lambda h, b, c: (h, 0, 0)),
        pl.BlockSpec((heads, chunk, RET_DV), lambda h, b, c: (h, 0, 0)),
        pl.BlockSpec((heads, chunk, RET_DK), lambda h, b, c: (h, 0, 0)),
        pl.BlockSpec((heads, 1, RET_DV), lambda h, b, c: (h, 0, 0)),
    ]
    args += [inner, qd, kd, cd]
    return pl.pallas_call(
        functools.partial(_ret_body, zero_init=zero_init, nc=nc, chunk=chunk, per_step=per_step, heads=heads),
        grid=(RET_HEADS // heads, B, nc),
        in_specs=in_specs,
        out_specs=[
            pl.BlockSpec((1, rows, heads * RET_DV), lambda h, b, c: (b, c, h)),
            pl.BlockSpec((1, heads, RET_DK, RET_DV), lambda h, b, c: (b, h, 0, 0)),
        ],
        out_shape=[
            jax.ShapeDtypeStruct((B, L, RET_V), BF16),
            jax.ShapeDtypeStruct((B, RET_HEADS, RET_DK, RET_DV), F32),
        ],
        scratch_shapes=[pltpu.VMEM((heads, RET_DK, RET_DV), F32)],
        compiler_params=_cparams(3),
        name="retention_scan",
    )(*args)


def retention_tables(n_real, n_pad):
    log_gamma = jnp.log1p(-(2.0 ** (-5.0 - jnp.arange(RET_HEADS, dtype=F32))))
    idx = jnp.arange(n_pad, dtype=F32)
    valid = idx < n_real
    diff = idx[:, None] - idx[None, :]
    ok = (diff >= 0) & valid[:, None] & valid[None, :]
    inner = jnp.where(ok[None], jnp.exp(log_gamma[:, None, None] * jnp.maximum(diff, 0.0)[None]), 0.0)
    qd = jnp.exp(log_gamma[:, None] * (idx[None, :] + 1.0))
    kd = jnp.where(valid[None, :], jnp.exp(log_gamma[:, None] * (n_real - 1.0 - idx[None, :])), 0.0)
    cd = jnp.exp(log_gamma * n_real)
    return (inner,
            jnp.broadcast_to(qd[:, :, None], (RET_HEADS, n_pad, RET_DV)),
            jnp.broadcast_to(kd[:, :, None], (RET_HEADS, n_pad, RET_DK)),
            jnp.broadcast_to(cd[:, None, None], (RET_HEADS, 1, RET_DV)))


def rotary_tables(pos, reps):
    inv_freq = ROPE_BASE ** (-jnp.arange(0, RET_DK, 2, dtype=F32) / RET_DK)
    ang = pos.astype(F32)[:, None] * inv_freq[None, :]
    cos = jnp.repeat(jnp.cos(ang), 2, axis=1)
    sin = jnp.sin(ang)
    sin_signed = jnp.stack([-sin, sin], axis=-1).reshape(ang.shape[0], RET_DK)
    return jnp.tile(cos, (1, reps)), jnp.tile(sin_signed, (1, reps))


def _pool_body(x_ref, xp_ref, buf_ref, sh_ref, sc_ref, gate_ref, w_ref, cs_ref, o_ref, tail_ref, ext_scr,
               *, tm, pos0, has_prev):
    m = pl.program_id(1)
    sh, sc = sh_ref[0], sc_ref[0]
    h = _modulate(x_ref[0], sh, sc)

    @pl.when(m == 0)
    def _():
        ext_scr[0:POOL_HALO, :] = buf_ref[0]

    if has_prev:

        @pl.when(m > 0)
        def _():
            ext_scr[0:POOL_HALO, :] = _modulate(xp_ref[0], sh, sc)

    ext_scr[POOL_HALO:POOL_HALO + tm, :] = h
    tail_ref[0] = ext_scr[tm:tm + POOL_HALO, :]
    row = lax.broadcasted_iota(jnp.int32, (tm, 1), 0)
    pos1 = (pos0 + m * tm + row + 1).astype(F32)
    rows = max(tm, 16)
    ys = []
    for gi, w in enumerate(POOL_WINDOWS):
        c0, c1 = gi * POOL_GW, (gi + 1) * POOL_GW
        win = ext_scr[POOL_HALO:POOL_HALO + tm, c0:c1]
        for j in range(1, w):
            win = win + ext_scr[POOL_HALO - j:POOL_HALO - j + tm, c0:c1]
        d = win / jnp.minimum(jnp.float32(w), pos1) - h[:, c0:c1]
        if rows != tm:
            d = jnp.concatenate([d, jnp.zeros((rows - tm, POOL_GW), F32)], axis=0)
        y = jnp.dot(d.astype(BF16), w_ref[0, gi].astype(BF16), preferred_element_type=F32)
        ys.append(y[0:tm])
    y = jnp.concatenate(ys, axis=1) * cs_ref[...]
    o_ref[0] = x_ref[0] + gate_ref[0] * y


def pool_layer(x3, buf16, mod, gate, pool_w, pool_scale, wl, pos0, *, tm):
    B, L, D = x3.shape
    assert L % tm == 0 and (L == tm or tm % POOL_HALO == 0)
    has_prev = L > tm
    ph = POOL_HALO if has_prev else min(L, POOL_HALO)
    per = tm // POOL_HALO if has_prev else 1
    vec = lambda: pl.BlockSpec((1, 1, D), lambda b, m: (b, 0, 0))
    return pl.pallas_call(
        functools.partial(_pool_body, tm=tm, pos0=pos0, has_prev=has_prev),
        grid=(B, L // tm),
        in_specs=[
            pl.BlockSpec((1, tm, D), lambda b, m: (b, m, 0)),
            pl.BlockSpec((1, ph, D), lambda b, m: (b, jnp.maximum(m * per - 1, 0), 0)),
            pl.BlockSpec((1, POOL_HALO, D), lambda b, m: (b, 0, 0)),
            vec(), vec(), vec(),
            pl.BlockSpec((1,) + pool_w.shape[1:], lambda b, m: (wl, 0, 0, 0)),
            pl.BlockSpec((1, D), lambda b, m: (wl, 0)),
        ],
        out_specs=[
            pl.BlockSpec((1, tm, D), lambda b, m: (b, m, 0)),
            pl.BlockSpec((1, POOL_HALO, D), lambda b, m: (b, 0, 0)),
        ],
        out_shape=[
            jax.ShapeDtypeStruct((B, L, D), F32),
            jax.ShapeDtypeStruct((B, POOL_HALO, D), F32),
        ],
        scratch_shapes=[pltpu.VMEM((tm + POOL_HALO, D), F32)],
        compiler_params=_cparams(2),
        name="pool_layer",
    )(x3, x3, buf16, mod[0], mod[1], gate, pool_w, pool_scale)


def _split3(x):
    p0 = x.astype(BF16)
    r1 = x - p0.astype(F32)
    p1 = r1.astype(BF16)
    p2 = (r1 - p1.astype(F32)).astype(BF16)
    return p0, p1, p2


def _lane_cumsum(x, tri):
    p0, p1, p2 = _split3(x)
    dot = lambda p: jnp.dot(p, tri, preferred_element_type=F32)
    return (dot(p0) + dot(p1)) + dot(p2)


def _upper_tri(t):
    r = lax.broadcasted_iota(jnp.int32, (t, t), 0)
    c = lax.broadcasted_iota(jnp.int32, (t, t), 1)
    return jnp.where(r <= c, 1.0, 0.0).astype(BF16)


BIAS_PIECES = 3


def _fbias_body(lf_ref, o_ref, carry_scr, *, tc):
    @pl.when(pl.program_id(1) == 0)
    def _():
        carry_scr[...] = jnp.zeros_like(carry_scr)

    r = lax.broadcasted_iota(jnp.int32, (tc, tc), 0)
    c = lax.broadcasted_iota(jnp.int32, (tc, tc), 1)
    tril = jnp.where(c <= r, 1.0, 0.0).astype(BF16)
    p0, p1, p2 = _split3(lf_ref[0])
    dot = lambda p: jnp.dot(tril, p, preferred_element_type=F32)
    f = carry_scr[...] + ((dot(p0) + dot(p1)) + dot(p2))
    carry_scr[...] = f[tc - 1:tc, :]
    head = lax.broadcasted_iota(jnp.int32, (FOX_HEADS, LANES), 0)
    lane = lax.broadcasted_iota(jnp.int32, (FOX_HEADS, LANES), 1)
    out = None
    for p, piece in enumerate(_split3(-f)):
        place = jnp.where(lane == BIAS_PIECES * head + p, 1.0, 0.0).astype(BF16)
        term = jnp.dot(piece, place, preferred_element_type=F32)
        out = term if out is None else out + term
    o_ref[0] = out.astype(BF16)


def fox_bias_features(lf, *, tc=512):
    B, L, H = lf.shape
    tc = min(tc, L)
    return pl.pallas_call(
        functools.partial(_fbias_body, tc=tc),
        grid=(B, L // tc),
        in_specs=[pl.BlockSpec((1, tc, H), lambda b, c: (b, c, 0))],
        out_specs=pl.BlockSpec((1, tc, LANES), lambda b, c: (b, c, 0)),
        out_shape=jax.ShapeDtypeStruct((B, L, LANES), BF16),
        scratch_shapes=[pltpu.VMEM((1, H), F32)],
        compiler_params=_cparams(2),
        name="fox_bias_features",
    )(lf)


DEN_ROWS = 16


def _flash_body(qt_ref, k_ref, fb_ref, vt_ref, o_ref, *, tq, tk):
    hp = pl.program_id(1)
    qi = pl.program_id(2)
    pair = 2 * FOX_DH
    row = lax.broadcasted_iota(jnp.int32, (pair, tq), 0)
    qt = qt_ref[0]
    qaug = []
    for i in range(2):
        q_head = jnp.where(row // FOX_DH == i, qt, jnp.zeros_like(qt))
        pick = jnp.where(row // BIAS_PIECES == 2 * hp + i, 1.0, 0.0).astype(BF16)
        qaug.append(jnp.concatenate([q_head, pick], axis=0))
    key_i = lax.broadcasted_iota(jnp.int32, (tk, tq), 0)
    qry_i = lax.broadcasted_iota(jnp.int32, (tk, tq), 1)
    per_q = tq // tk

    def step(j, carry, diag):
        k0 = pl.multiple_of(j * tk, tk)
        kaug = jnp.concatenate([k_ref[0, pl.ds(k0, tk), :], fb_ref[0, pl.ds(k0, tk), :]], axis=1)
        ones = jnp.ones((DEN_ROWS, tk), BF16)
        new = []
        for i in range(2):
            m_old, acc = carry[i]
            st = jnp.dot(kaug, qaug[i], preferred_element_type=F32)
            if diag is not None:
                st = jnp.where(key_i + diag * tk <= qry_i, st, NEG_BIG)
            m_new = jnp.maximum(m_old, jnp.max(st, axis=0, keepdims=True))
            alpha = jnp.exp(m_old - m_new)
            p = jnp.exp(st - m_new).astype(BF16)
            vt = jnp.concatenate([vt_ref[0, i * FOX_DH:(i + 1) * FOX_DH, pl.ds(k0, tk)].astype(BF16), ones], axis=0)
            acc = alpha * acc + jnp.dot(vt, p, preferred_element_type=F32)
            new.append((m_new, acc))
        return tuple(new)

    init = tuple((jnp.full((1, tq), NEG_BIG, F32), jnp.zeros((FOX_DH + DEN_ROWS, tq), F32)) for _ in range(2))
    carry = lax.fori_loop(0, qi * per_q, lambda j, c: step(j, c, None), init)
    for d in range(per_q):
        carry = step(qi * per_q + d, carry, d)
    o_ref[0] = jnp.concatenate([(acc[:FOX_DH] / acc[FOX_DH:FOX_DH + 1]).T for _, acc in carry],
                               axis=1).astype(o_ref.dtype)


def fox_flash(qt, k, fb, vt, *, tq=1024, tk=1024):
    B, L, D = k.shape
    tq = min(tq, L)
    tk = min(tk, tq)
    assert L % tq == 0 and tq % tk == 0
    pair = 2 * FOX_DH
    return pl.pallas_call(
        functools.partial(_flash_body, tq=tq, tk=tk),
        grid=(B, FOX_HEADS // 2, L // tq),
        in_specs=[
            pl.BlockSpec((1, pair, tq), lambda b, hp, qi: (b, hp, qi)),
            pl.BlockSpec((1, L, pair), lambda b, hp, qi: (b, 0, hp)),
            pl.BlockSpec((1, L, LANES), lambda b, hp, qi: (b, 0, 0)),
            pl.BlockSpec((1, pair, L), lambda b, hp, qi: (b, hp, 0)),
        ],
        out_specs=pl.BlockSpec((1, tq, pair), lambda b, hp, qi: (b, qi, hp)),
        out_shape=jax.ShapeDtypeStruct((B, L, D), BF16),
        compiler_params=_cparams(3),
        name="fox_flash",
    )(qt, k, fb, vt)


MAX_DECODE_PAGES = 16


def _decode_body(pt_ref, q_ref, kn_ref, vn_ref, lfn_ref, *refs, n_steps, lq, pages):
    kc = refs[0:pages]
    vc = refs[pages:2 * pages]
    lc = refs[2 * pages:3 * pages]
    o_ref, qbd_scr, m_scr, l_scr, acc_scr, carry_scr = refs[3 * pages:]
    b = pl.program_id(0)
    st = pl.program_id(1)
    rows = lq * FOX_HEADS
    last = (((1,), (1,)), ((), ()))
    tri = _upper_tri(PAGE_SIZE)

    @pl.when(st == 0)
    def _():
        head = lax.broadcasted_iota(jnp.int32, (FOX_HEADS, D_MODEL), 0)
        lane_head = lax.broadcasted_iota(jnp.int32, (FOX_HEADS, D_MODEL), 1) // FOX_DH
        blocks = [jnp.where(head == lane_head, jnp.broadcast_to(q_ref[0, t:t + 1, :], (FOX_HEADS, D_MODEL)), 0.0)
                  for t in range(lq)]
        qbd_scr[...] = jnp.concatenate(blocks, axis=0).astype(BF16)
        m_scr[...] = jnp.full_like(m_scr, NEG_BIG)
        l_scr[...] = jnp.zeros_like(l_scr)
        acc_scr[...] = jnp.zeros_like(acc_scr)
        carry_scr[...] = jnp.zeros_like(carry_scr)

    def absorb(s, v_mat, v_is_t):
        m_old = m_scr[...]
        m_new = jnp.maximum(m_old, jnp.max(s, axis=1, keepdims=True))
        alpha = jnp.exp(m_old - m_new)
        p = jnp.exp(s - m_new)
        l_scr[...] = alpha * l_scr[...] + jnp.sum(p, axis=1, keepdims=True)
        if v_is_t:
            pv = lax.dot_general(p.astype(BF16), v_mat, last, preferred_element_type=F32)
        else:
            pv = jnp.dot(p.astype(BF16), v_mat, preferred_element_type=F32)
        acc_scr[...] = alpha * acc_scr[...] + pv
        m_scr[...] = m_new

    @pl.when(st < n_steps)
    def _():
        kt = jnp.concatenate([kc[i][0].reshape(D_MODEL, PAGE_SIZE).astype(BF16) for i in range(pages)], axis=1)
        vt = jnp.concatenate([vc[i][0].reshape(D_MODEL, PAGE_SIZE).astype(BF16) for i in range(pages)], axis=1)
        within = _lane_cumsum(jnp.concatenate([lc[i][0] for i in range(pages)], axis=0), tri)
        f = carry_scr[...]
        biases = []
        for i in range(pages):
            f_page = f + within[i * FOX_HEADS:(i + 1) * FOX_HEADS, :]
            biases.append(jnp.tile(f_page, (lq, 1)))
            f = jnp.broadcast_to(f_page[:, PAGE_SIZE - 1:PAGE_SIZE], f_page.shape)
        carry_scr[...] = f
        s = jnp.dot(qbd_scr[...], kt, preferred_element_type=F32) - jnp.concatenate(biases, axis=1)
        absorb(s, vt, True)

    @pl.when(st == n_steps)
    def _():
        pad = jnp.zeros((PAGE_SIZE - lq, D_MODEL), F32)
        kn = jnp.concatenate([kn_ref[0], pad], axis=0).astype(BF16)
        vn = jnp.concatenate([vn_ref[0], pad], axis=0).astype(BF16)
        n_tok = lfn_ref.shape[1]
        tok = lax.broadcasted_iota(jnp.int32, (n_tok, PAGE_SIZE), 0)
        key = lax.broadcasted_iota(jnp.int32, (n_tok, PAGE_SIZE), 1)
        sel = jnp.where((tok // lq == b) & (tok % lq <= key) & (key < lq), 1.0, 0.0).astype(BF16)
        p0, p1, p2 = _split3(lfn_ref[...])
        dot = lambda p: jnp.dot(p, sel, preferred_element_type=F32)
        f = carry_scr[...] + ((dot(p0) + dot(p1)) + dot(p2))
        s = lax.dot_general(qbd_scr[...], kn, last, preferred_element_type=F32) - jnp.tile(f, (lq, 1))
        rq = lax.broadcasted_iota(jnp.int32, (rows, PAGE_SIZE), 0) // FOX_HEADS
        kk = lax.broadcasted_iota(jnp.int32, (rows, PAGE_SIZE), 1)
        s = jnp.where(kk <= rq, s, NEG_BIG)
        absorb(s, vn, False)
        o = acc_scr[...] / l_scr[...]
        head = lax.broadcasted_iota(jnp.int32, (FOX_HEADS, D_MODEL), 0)
        lane_head = lax.broadcasted_iota(jnp.int32, (FOX_HEADS, D_MODEL), 1) // FOX_DH
        outs = [jnp.sum(jnp.where(head == lane_head, o[t * FOX_HEADS:(t + 1) * FOX_HEADS, :], 0.0), axis=0,
                        keepdims=True) for t in range(lq)]
        o_ref[0] = jnp.concatenate(outs, axis=0).astype(o_ref.dtype)


def fox_decode(q, k_new, v_new, lft_new, cache_kt, cache_vt, cache_lt, page_table):
    B, lq, D = q.shape
    n_pages = page_table.shape[1]
    pages = math.gcd(n_pages, MAX_DECODE_PAGES)
    n_steps = n_pages // pages
    rows = lq * FOX_HEADS

    def page_idx(i):
        return lambda b, s, pt: (pt[b * n_pages + jnp.minimum(s, n_steps - 1) * pages + i], 0, 0, 0)

    def page_idx3(i):
        return lambda b, s, pt: (pt[b * n_pages + jnp.minimum(s, n_steps - 1) * pages + i], 0, 0)

    seq = lambda: pl.BlockSpec((1, lq, D), lambda b, s, pt: (b, 0, 0))
    in_specs = [seq(), seq(), seq(), pl.BlockSpec(lft_new.shape, lambda b, s, pt: (0, 0))]
    in_specs += [pl.BlockSpec((1, FOX_HEADS, FOX_DH, PAGE_SIZE), page_idx(i)) for i in range(pages)]
    in_specs += [pl.BlockSpec((1, FOX_HEADS, FOX_DH, PAGE_SIZE), page_idx(i)) for i in range(pages)]
    in_specs += [pl.BlockSpec((1, FOX_HEADS, PAGE_SIZE), page_idx3(i)) for i in range(pages)]
    grid_spec = pltpu.PrefetchScalarGridSpec(
        num_scalar_prefetch=1,
        grid=(B, n_steps + 1),
        in_specs=in_specs,
        out_specs=pl.BlockSpec((1, lq, D), lambda b, s, pt: (b, 0, 0)),
        scratch_shapes=[
            pltpu.VMEM((rows, D), BF16),
            pltpu.VMEM((rows, 1), F32),
            pltpu.VMEM((rows, 1), F32),
            pltpu.VMEM((rows, D), F32),
            pltpu.VMEM((FOX_HEADS, PAGE_SIZE), F32),
        ],
    )
    return pl.pallas_call(
        functools.partial(_decode_body, n_steps=n_steps, lq=lq, pages=pages),
        grid_spec=grid_spec,
        out_shape=jax.ShapeDtypeStruct((B, lq, D), F32),
        compiler_params=_cparams(2),
        name="fox_decode",
    )(page_table.reshape(-1), q, k_new, v_new, lft_new,
      *([cache_kt] * pages), *([cache_vt] * pages), *([cache_lt] * pages))


def _final_body(x_ref, g_ref, o_ref):
    xf = x_ref[0]
    o_ref[0] = xf * lax.rsqrt(jnp.mean(xf * xf, axis=-1, keepdims=True) + EPS) * g_ref[...]


def final_norm(x3, final_g, *, tm=None):
    bx, L, D = x3.shape
    tm = tm or min(L, 1024)
    return pl.pallas_call(
        _final_body,
        grid=(bx, L // tm),
        in_specs=[pl.BlockSpec((1, tm, D), lambda b, m: (b, m, 0)), pl.BlockSpec((1, D), lambda b, m: (0, 0))],
        out_specs=pl.BlockSpec((1, tm, D), lambda b, m: (b, m, 0)),
        out_shape=jax.ShapeDtypeStruct((bx, L, D), F32),
        compiler_params=_cparams(2),
        name="final_norm",
    )(x3, final_g.reshape(1, D))


def _trunk(x, mods, ret_state, pool_state, fox_past, pos0, params):
    (ret_w_in, ret_w_out, pool_w, pool_scale, fox_wt, fox_b_f, fox_w_out, ffn_w_gu, ffn_w_down,
     w_rt_pad, b_rt_pad, moe_w_gu, moe_w_down, final_g) = params
    B, L, D = x.shape
    decode = fox_past is not None
    if decode:
        x3 = x.reshape(1, B * L, D)
        expand = lambda v: jnp.repeat(v, L, axis=0)[None]
    else:
        x3 = x
        expand = lambda v: v[:, None, :]
    n_rows = x3.shape[1]
    pos = pos0 + jnp.arange(L)
    cos, sin = rotary_tables(pos, RET_HEADS)
    if decode:
        cos, sin = jnp.tile(cos, (B, 1)), jnp.tile(sin, (B, 1))
    chunk = min(L, 256)
    chunk_pad = max(chunk, PAGE_SIZE)
    tables = retention_tables(chunk, chunk_pad)
    ret_new, extras = [], {}
    moe_buf = None
    for i in range(DEPTH):
        sh_a, sc_a, g_a, sh_f, sc_f, g_f = [expand(v) for v in jnp.split(mods[i], 6, axis=-1)]
        kind, j = i % N_MIXERS, i // N_MIXERS
        if kind == 0:
            proj = functools.partial(mm, x3, ret_w_in, j, mod=(sh_a, sc_a))
            q = proj(n0=0, n_out=RET_QK, out_dtype=BF16, epi="rot", rot=(cos, sin), name="ret_q")
            k = proj(n0=RET_QK, n_out=RET_QK, out_dtype=BF16, epi="rot", rot=(cos, sin), scale=RET_DK ** -0.5,
                     name="ret_k")
            v = proj(n0=2 * RET_QK, n_out=RET_V, out_dtype=BF16, name="ret_v")
            g = proj(n0=2 * RET_QK + RET_V, n_out=RET_V, out_dtype=F32, name="ret_g")
            if decode:
                padr = lambda t: jnp.pad(t.reshape(B, L, -1), ((0, 0), (0, chunk_pad - L), (0, 0)))
                o, s = retention_scan(padr(q), padr(k), padr(v), padr(g), tables, chunk_pad, s0=ret_state,
                                      s0_layer=j, heads=RET_HEADS)
                o = o[:, :L].reshape(1, n_rows, RET_V)
            else:
                o, s = retention_scan(q, k, v, g, tables, chunk_pad)
            ret_new.append(s)
            x3 = mm(o, ret_w_out, j, n0=0, n_out=D, out_dtype=F32, epi="res", res=(x3, g_a), name="ret_out")
        elif kind == 1:
            vecs = [v[:, None, :] for v in jnp.split(mods[i], 6, axis=-1)[:3]]
            if decode:
                buf16 = jnp.pad(pool_state[j], ((0, 0), (1, 0), (0, 0)))
                tm = L
            else:
                buf16 = jnp.zeros((B, POOL_HALO, D), F32)
                tm = min(L, 512)
            xn, tail = pool_layer(x3.reshape(B, L, D), buf16, (vecs[0], vecs[1]), vecs[2], pool_w, pool_scale, j,
                                  pos0, tm=tm)
            x3 = xn.reshape(x3.shape)
            extras["pool"] = tail[:, 1:, :]
        else:
            fproj = functools.partial(mm, x3, fox_wt, j, mod=(sh_a, sc_a), w_t=True)
            if decode:
                ck, cv, cl, pt = fox_past
                q = fproj(n0=0, n_out=D, out_dtype=F32, scale=FOX_DH ** -0.5, name="fox_q")
                lft = fproj(n0=3 * D, n_out=FOX_HEADS, out_dtype=F32, out_t=True, epi="logsig",
                            bias=fox_b_f[j].reshape(FOX_HEADS, 1), name="fox_logft")
                k = fproj(n0=D, n_out=D, out_dtype=F32, name="fox_k")
                v = fproj(n0=2 * D, n_out=D, out_dtype=F32, name="fox_v")
                o = fox_decode(q.reshape(B, L, D), k.reshape(B, L, D), v.reshape(B, L, D), lft[0],
                               jnp.transpose(ck[j], (0, 2, 3, 1)), jnp.transpose(cv[j], (0, 2, 3, 1)),
                               jnp.transpose(cl[j], (0, 2, 1)), pt)
                o = o.reshape(1, n_rows, D)
                extras["k"] = k.reshape(B, L, FOX_HEADS, FOX_DH)
                extras["v"] = v.reshape(B, L, FOX_HEADS, FOX_DH)
                extras["l"] = jnp.transpose(lft[0].reshape(FOX_HEADS, B, L), (1, 2, 0))
            else:
                qt = fproj(n0=0, n_out=D, out_dtype=BF16, out_t=True, scale=FOX_DH ** -0.5, name="fox_qt")
                kb = fproj(n0=D, n_out=D, out_dtype=BF16, name="fox_kb")
                kt = fproj(n0=D, n_out=D, out_dtype=F32, out_t=True, name="fox_kt")
                vt = fproj(n0=2 * D, n_out=D, out_dtype=F32, out_t=True, name="fox_vt")
                lf = fproj(n0=3 * D, n_out=FOX_HEADS, out_dtype=F32, epi="logsig",
                           bias=fox_b_f[j].reshape(1, FOX_HEADS), name="fox_logf")
                o = fox_flash(qt, kb, fox_bias_features(lf), vt)
                unt = lambda t: jnp.transpose(t.reshape(B, FOX_HEADS, FOX_DH, L), (0, 3, 1, 2))
                extras["k"], extras["v"] = unt(kt), unt(vt)
                extras["l"] = lf
            x3 = mm(o, fox_w_out, j, n0=0, n_out=D, out_dtype=F32, epi="res", res=(x3, g_a), name="fox_out")
        ml = i // 2
        if i % 2 == 0:
            x3 = ffn(x3, (sh_f, sc_f), g_f, ffn_w_gu, ffn_w_down, ml, name="ffn_dense")
        else:
            gates = router(x3, (sh_f, sc_f), w_rt_pad, b_rt_pad, ml)
            if decode:
                x3 = ffn(x3, (sh_f, sc_f), g_f, moe_w_gu, moe_w_down, ml * N_EXPERTS, gates=gates, name="ffn_moe")
            else:
                closing = final_g if i == DEPTH - 1 else None
                x3, moe_buf = moe_sparse(x3, (sh_f, sc_f), g_f, gates, moe_w_gu, moe_w_down, ml * N_EXPERTS, closing,
                                         moe_buf)
    fused_final = (not decode) and (DEPTH - 1) % 2 == 1
    out = (x3 if fused_final else final_norm(x3, final_g)).reshape(B, L, D)
    return (out, jnp.stack(ret_new), extras["pool"][None], extras["k"][None], extras["v"][None], extras["l"][None])


def kernel(x_prompt, x_sample, state_ret, state_pool, cache_fox_k, cache_fox_v, cache_fox_logf, page_table,
           c_prompt, c_sample, ada_w, ada_b, ret_w_in, ret_w_out, pool_w, pool_scale, fox_w_in, fox_b_f, fox_w_out,
           ffn_w_gu, ffn_w_down, moe_w_router, moe_b_router, moe_w_gu, moe_w_down, final_g):
    bp, bs = x_prompt.shape[0], x_sample.shape[0]
    rows = -(-(bp + bs) // 8) * 8
    c_all = jnp.concatenate([c_prompt, c_sample, jnp.zeros((rows - bp - bs, D_MODEL), F32)], axis=0)
    mods = ada_mods(c_all, ada_w, ada_b)
    n_moe = moe_w_router.shape[0]
    params = (
        ret_w_in, ret_w_out, pool_w, pool_scale,
        jnp.swapaxes(fox_w_in, 1, 2),
        fox_b_f, fox_w_out, ffn_w_gu, ffn_w_down,
        jnp.pad(jnp.swapaxes(moe_w_router, 1, 2), ((0, 0), (0, LANES - N_EXPERTS), (0, 0))),
        jnp.pad(moe_b_router, ((0, 0), (0, LANES - N_EXPERTS))).reshape(n_moe, 1, LANES),
        moe_w_gu.reshape((n_moe * N_EXPERTS,) + moe_w_gu.shape[2:]),
        moe_w_down.reshape((n_moe * N_EXPERTS,) + moe_w_down.shape[2:]),
        final_g,
    )
    y_p, ret_p, pool_p, k_p, v_p, l_p = _trunk(x_prompt, mods[:, :bp], None, None, None, 0, params)
    n_past = page_table.shape[1] * PAGE_SIZE
    y_s, ret_s, pool_s, k_s, v_s, l_s = _trunk(
        x_sample, mods[:, bp:bp + bs], state_ret, state_pool,
        (cache_fox_k, cache_fox_v, cache_fox_logf, page_table), n_past, params)
    return (y_p, y_s, ret_p, ret_s, pool_p, pool_s, k_p, k_s, v_p, v_s, l_p, l_s)
```

```python
import functools
import math

import jax
import jax.numpy as jnp
from jax import lax
from jax.experimental import pallas as pl
from jax.experimental.pallas import tpu as pltpu

F32 = jnp.float32
BF16 = jnp.bfloat16

D_MODEL = 1024
DEPTH = 4
PAGE_SIZE = 128
N_MIXERS = 3
RET_HEADS = 4
RET_DK = D_MODEL // RET_HEADS
RET_DV = 2 * D_MODEL // RET_HEADS
RET_QK = RET_HEADS * RET_DK
RET_V = RET_HEADS * RET_DV
ROPE_BASE = 10000.0
POOL_WINDOWS = (2, 4, 8, 16)
POOL_GW = D_MODEL // len(POOL_WINDOWS)
POOL_BUF = max(POOL_WINDOWS) - 1
POOL_HALO = POOL_BUF + 1
FOX_HEADS = 16
FOX_DH = D_MODEL // FOX_HEADS
FFN_DIM = 2816
N_EXPERTS = 8
EPS = 1e-6
NEG_BIG = -1e30

V7X_VMEM_BYTES = 64 * 1024 * 1024
VMEM_LIMIT = V7X_VMEM_BYTES - 8 * 1024 * 1024
LANES = 128
FFN_TF = 256


def _cparams(n_axes):
    return pltpu.CompilerParams(dimension_semantics=("arbitrary",) * n_axes, vmem_limit_bytes=VMEM_LIMIT)


def _sigmoid(x):
    return 1.0 / (1.0 + jnp.exp(-x))


def _modulate(x, shift, scale):
    xf = x.astype(F32)
    ms = jnp.mean(xf * xf, axis=-1, keepdims=True)
    return (xf * lax.rsqrt(ms + EPS)) * (1.0 + scale) + shift


def _row_spec(arr, tm, width, col_fn):
    if arr.shape[1] == 1:
        return pl.BlockSpec((1, 1, width), lambda b, m, *r: (b, 0, col_fn(*r)))
    return pl.BlockSpec((1, tm, width), lambda b, m, *r: (b, m, col_fn(*r)))


def _ada_body(c_ref, w_ref, b_ref, o_ref):
    c = c_ref[...]
    cond = c * _sigmoid(c)
    o_ref[0] = jnp.dot(cond.astype(BF16), w_ref[0].astype(BF16), preferred_element_type=F32) + b_ref[0]


def ada_mods(c_all, ada_w, ada_b):
    rows = c_all.shape[0]
    n_out = ada_w.shape[2]
    tn = 1024
    return pl.pallas_call(
        _ada_body,
        grid=(DEPTH, n_out // tn),
        in_specs=[
            pl.BlockSpec((rows, D_MODEL), lambda i, n: (0, 0)),
            pl.BlockSpec((1, D_MODEL, tn), lambda i, n: (i, 0, n)),
            pl.BlockSpec((1, 1, tn), lambda i, n: (i, 0, n)),
        ],
        out_specs=pl.BlockSpec((1, rows, tn), lambda i, n: (i, 0, n)),
        out_shape=jax.ShapeDtypeStruct((DEPTH, rows, n_out), F32),
        compiler_params=_cparams(2),
        name="ada_mods",
    )(c_all, ada_w, ada_b.reshape(DEPTH, 1, n_out))


def _mm_body(*refs, has_mod, epi, w_t, out_t, scale):
    it = iter(refs)
    x_ref = next(it)
    if has_mod:
        sh_ref, sc_ref = next(it), next(it)
    w_ref = next(it)
    if epi == "rot":
        cos_ref, sin_ref = next(it), next(it)
    elif epi == "res":
        res_ref, gate_ref = next(it), next(it)
    elif epi == "logsig":
        b_ref = next(it)
    o_ref = next(it)
    n = pl.program_id(2)
    if has_mod:
        h_scr = next(it)

        @pl.when(n == 0)
        def _():
            h_scr[...] = _modulate(x_ref[0], sh_ref[0], sc_ref[0]).astype(BF16)

        lhs = h_scr[...]
    else:
        lhs = x_ref[0].astype(BF16)
    w = w_ref[0].astype(BF16)
    last = (((1,), (1,)), ((), ()))
    if not w_t:
        acc = jnp.dot(lhs, w, preferred_element_type=F32)
    elif not out_t:
        acc = lax.dot_general(lhs, w, last, preferred_element_type=F32)
    else:
        acc = lax.dot_general(w, lhs, last, preferred_element_type=F32)
    if epi == "rot":
        acc = _rotary(acc, cos_ref[...], sin_ref[...])
    elif epi == "res":
        acc = res_ref[0] + gate_ref[0] * acc
    elif epi == "logsig":
        z = acc + b_ref[...]
        acc = jnp.minimum(z, 0.0) - jnp.log1p(jnp.exp(-jnp.abs(z)))
    if scale != 1.0:
        acc = acc * scale
    o_ref[0] = acc.astype(o_ref.dtype)


def mm(x3, w3, wl, *, n0, n_out, out_dtype, mod=None, epi="plain", w_t=False, out_t=False, scale=1.0,
       rot=None, res=None, bias=None, tm=None, tn=None, name="mm"):
    bx, L, K = x3.shape
    tm = tm or min(L, 1024 if (epi in ("rot", "res") or K > 1024) else 2048)
    tn = tn or min(n_out, 512 if K > 1024 else 1024)
    assert L % tm == 0 and n_out % tn == 0 and n0 % tn == 0
    nb0 = n0 // tn
    has_mod = mod is not None
    in_specs = [pl.BlockSpec((1, tm, K), lambda b, m, n: (b, m, 0))]
    args = [x3]
    if has_mod:
        for a in mod:
            in_specs.append(_row_spec(a, tm, K, lambda n: 0))
            args.append(a)
    if w_t:
        in_specs.append(pl.BlockSpec((1, tn, K), lambda b, m, n: (wl, nb0 + n, 0)))
    else:
        in_specs.append(pl.BlockSpec((1, K, tn), lambda b, m, n: (wl, 0, nb0 + n)))
    args.append(w3)
    if epi == "rot":
        for a in rot:
            in_specs.append(pl.BlockSpec((tm, tn), lambda b, m, n: (m, n)))
            args.append(a)
    elif epi == "res":
        in_specs.append(pl.BlockSpec((1, tm, tn), lambda b, m, n: (b, m, n)))
        in_specs.append(_row_spec(res[1], tm, tn, lambda n: n))
        args.extend(res)
    elif epi == "logsig":
        in_specs.append(pl.BlockSpec(bias.shape, lambda b, m, n: (0, 0)))
        args.append(bias)
    if out_t:
        out_spec = pl.BlockSpec((1, tn, tm), lambda b, m, n: (b, n, m))
        out_shape = jax.ShapeDtypeStruct((bx, n_out, L), out_dtype)
    else:
        out_spec = pl.BlockSpec((1, tm, tn), lambda b, m, n: (b, m, n))
        out_shape = jax.ShapeDtypeStruct((bx, L, n_out), out_dtype)
    return pl.pallas_call(
        functools.partial(_mm_body, has_mod=has_mod, epi=epi, w_t=w_t, out_t=out_t, scale=scale),
        grid=(bx, L // tm, n_out // tn),
        in_specs=in_specs,
        out_specs=out_spec,
        out_shape=out_shape,
        scratch_shapes=[pltpu.VMEM((tm, K), BF16)] if has_mod else [],
        compiler_params=_cparams(3),
        name=name,
    )(*args)


def _rotary(acc, cos, sin_signed):
    width = acc.shape[1]
    lane = lax.broadcasted_iota(jnp.int32, acc.shape, 1)
    partner = jnp.where(lane % 2 == 0, pltpu.roll(acc, width - 1, axis=1), pltpu.roll(acc, 1, axis=1))
    return acc * cos + partner * sin_signed


def _ffn_body(*refs, moe, nf, ne):
    it = iter(refs)
    x_ref, sh_ref, sc_ref, gate_ref = next(it), next(it), next(it), next(it)
    gw_ref = next(it) if moe else None
    wg_ref, wu_ref, wd_ref, o_ref, h_scr, acc_scr = next(it), next(it), next(it), next(it), next(it), next(it)
    tot_scr = next(it) if moe else None
    e = pl.program_id(2)
    f = pl.program_id(3)

    @pl.when((e == 0) & (f == 0))
    def _():
        h_scr[...] = _modulate(x_ref[0], sh_ref[0], sc_ref[0]).astype(BF16)

    @pl.when(f == 0)
    def _():
        acc_scr[...] = jnp.zeros_like(acc_scr)

    if moe:

        @pl.when((e == 0) & (f == 0))
        def _():
            tot_scr[...] = jnp.zeros_like(tot_scr)

    h = h_scr[...]
    a = jnp.dot(h, wg_ref[0].astype(BF16), preferred_element_type=F32)
    b = jnp.dot(h, wu_ref[0].astype(BF16), preferred_element_type=F32)
    mid = (a * _sigmoid(a) * b).astype(BF16)
    acc_scr[...] += jnp.dot(mid, wd_ref[0].astype(BF16), preferred_element_type=F32)

    if not moe:

        @pl.when(f == nf - 1)
        def _():
            o_ref[0] = x_ref[0] + gate_ref[0] * acc_scr[...]

    else:

        @pl.when(f == nf - 1)
        def _():
            gw = gw_ref[0]
            lane = lax.broadcasted_iota(jnp.int32, gw.shape, 1)
            col = jnp.sum(jnp.where(lane == e, gw, 0.0), axis=1, keepdims=True)
            tot_scr[...] += col * acc_scr[...]

            @pl.when(e == ne - 1)
            def _():
                o_ref[0] = x_ref[0] + gate_ref[0] * tot_scr[...]


def ffn(x3, mod, gate, w_gu3, w_down3, wl, *, gates=None, tm=None, name="ffn"):
    bx, L, D = x3.shape
    moe = gates is not None
    ne = N_EXPERTS if moe else 1
    tm = tm or min(L, 1024 if moe else 2048)
    tf = FFN_TF if tm > 256 else MOE_TF
    nf = FFN_DIM // tf
    assert L % tm == 0 and FFN_DIM % tf == 0
    in_specs = [pl.BlockSpec((1, tm, D), lambda b, m, e, f: (b, m, 0))]
    args = [x3]
    for a in (*mod, gate):
        in_specs.append(_row_spec(a, tm, D, lambda e, f: 0))
        args.append(a)
    if moe:
        in_specs.append(pl.BlockSpec((1, tm, LANES), lambda b, m, e, f: (b, m, 0)))
        args.append(gates)
    in_specs += [
        pl.BlockSpec((1, D, tf), lambda b, m, e, f: (wl + e, 0, f)),
        pl.BlockSpec((1, D, tf), lambda b, m, e, f: (wl + e, 0, nf + f)),
        pl.BlockSpec((1, tf, D), lambda b, m, e, f: (wl + e, f, 0)),
    ]
    args += [w_gu3, w_gu3, w_down3]
    scratch = [pltpu.VMEM((tm, D), BF16), pltpu.VMEM((tm, D), F32)]
    if moe:
        scratch.append(pltpu.VMEM((tm, D), F32))
    return pl.pallas_call(
        functools.partial(_ffn_body, moe=moe, nf=nf, ne=ne),
        grid=(bx, L // tm, ne, nf),
        in_specs=in_specs,
        out_specs=pl.BlockSpec((1, tm, D), lambda b, m, e, f: (b, m, 0)),
        out_shape=jax.ShapeDtypeStruct((bx, L, D), F32),
        scratch_shapes=scratch,
        compiler_params=_cparams(4),
        name=name,
    )(*args)


def _router_body(x_ref, sh_ref, sc_ref, w_ref, b_ref, o_ref):
    h = _modulate(x_ref[0], sh_ref[0], sc_ref[0]).astype(BF16)
    logits = lax.dot_general(h, w_ref[0].astype(BF16), (((1,), (1,)), ((), ())), preferred_element_type=F32)
    logits = logits + b_ref[0]
    lane = lax.broadcasted_iota(jnp.int32, logits.shape, 1).astype(F32)
    lg = jnp.where(lane < N_EXPERTS, logits, -jnp.inf)
    m1 = jnp.max(lg, axis=1, keepdims=True)
    i1 = jnp.min(jnp.where(lg == m1, lane, float(LANES)), axis=1, keepdims=True)
    lg2 = jnp.where(lane == i1, -jnp.inf, lg)
    m2 = jnp.max(lg2, axis=1, keepdims=True)
    i2 = jnp.min(jnp.where(lg2 == m2, lane, float(LANES)), axis=1, keepdims=True)
    e2 = jnp.exp(m2 - m1)
    den = 1.0 + e2
    o_ref[0] = jnp.where(lane == i1, 1.0 / den, 0.0) + jnp.where(lane == i2, e2 / den, 0.0)


def router(x3, mod, w_rt_pad, b_pad, wl, *, tm=None):
    bx, L, D = x3.shape
    tm = tm or min(L, 1024)
    in_specs = [pl.BlockSpec((1, tm, D), lambda b, m: (b, m, 0))]
    args = [x3]
    for a in mod:
        in_specs.append(_row_spec(a, tm, D, lambda: 0))
        args.append(a)
    in_specs += [
        pl.BlockSpec((1, LANES, D), lambda b, m: (wl, 0, 0)),
        pl.BlockSpec((1, 1, LANES), lambda b, m: (wl, 0, 0)),
    ]
    args += [w_rt_pad, b_pad]
    return pl.pallas_call(
        _router_body,
        grid=(bx, L // tm),
        in_specs=in_specs,
        out_specs=pl.BlockSpec((1, tm, LANES), lambda b, m: (b, m, 0)),
        out_shape=jax.ShapeDtypeStruct((bx, L, LANES), F32),
        compiler_params=_cparams(2),
        name="router",
    )(*args)


MOE_T = 256
MOE_CH = 128
MOE_TM = 512
MOE_TF = FFN_DIM // 2
MOE_ALIGN = 8
MOE_STAGE = -(-(2 * MOE_T + N_EXPERTS * (MOE_ALIGN - 1) + MOE_CH) // 8) * 8


def _moe_plan(gates2, n_tokens):
    nb = n_tokens // MOE_T
    routed = (gates2[:, :N_EXPERTS] > 0).reshape(nb, MOE_T, N_EXPERTS)
    ri = routed.astype(jnp.int32)
    t_i = jnp.arange(MOE_T)
    before = (t_i[None, :] < t_i[:, None]).astype(F32)
    rank = jnp.einsum("ts,bse->bte", before, routed.astype(F32)).astype(jnp.int32)
    cnt = jnp.sum(ri, axis=1)
    cnt_al = -(-cnt // MOE_ALIGN) * MOE_ALIGN
    lo = jnp.cumsum(cnt_al, axis=1) - cnt_al
    total = jnp.sum(cnt_al, axis=0)
    region = -(-(total + MOE_CH) // MOE_TM) * MOE_TM
    ends = jnp.cumsum(region)
    off = ends - region
    pos = off[None, :] + jnp.cumsum(cnt_al, axis=0) - cnt_al
    nch = -(-cnt_al // MOE_CH)
    m_pad = -(-(2 * n_tokens + nb * N_EXPERTS * (MOE_ALIGN - 1) + N_EXPERTS * (MOE_CH + MOE_TM)) // MOE_TM) * MOE_TM
    n_tiles = m_pad // MOE_TM
    tile_start = jnp.arange(n_tiles, dtype=jnp.int32) * MOE_TM
    tile_e = jnp.minimum(jnp.sum(tile_start[:, None] >= ends[None, :], axis=1), N_EXPERTS - 1).astype(jnp.int32)
    n_used = (ends[-1] // MOE_TM).astype(jnp.int32).reshape(1)
    dest = jnp.where(routed, lo[:, None, :] + rank, -1)
    d_hi = jnp.max(dest, axis=2)
    d_lo = jnp.min(jnp.where(routed, dest, MOE_STAGE), axis=2)
    dd = jnp.stack([d_hi, d_lo], axis=1).astype(jnp.int32)
    eidx = jnp.arange(N_EXPERTS, dtype=jnp.int32)
    e_a = jnp.min(jnp.where(routed, eidx, N_EXPERTS), axis=2)
    e_b = jnp.max(jnp.where(routed, eidx, -1), axis=2)
    g3 = gates2[:, :N_EXPERTS].reshape(nb, MOE_T, N_EXPERTS)
    at = lambda a, e: jnp.sum(jnp.where(eidx == e[..., None], a, 0), axis=2)
    r_a, r_b = at(rank, e_a), at(rank, e_b)
    w_a, w_b = at(g3, e_a), jnp.where(e_b != e_a, at(g3, e_b), 0.0)
    col = lambda e, r, c: jnp.where(r // MOE_CH == c, e * MOE_CH + r % MOE_CH, -1).astype(F32)
    cmeta = jnp.stack([col(e_a, r_a, 0), col(e_b, r_b, 0), col(e_a, r_a, 1), col(e_b, r_b, 1), w_a, w_b], axis=-1)
    cmeta = jnp.pad(cmeta.reshape(n_tokens, 6), ((0, 0), (0, LANES - 6)))
    two = (jnp.max(nch, axis=1) > 1).astype(jnp.int32)
    flat = lambda a: a.reshape(-1).astype(jnp.int32)
    return dict(lo=flat(lo // MOE_ALIGN), pos=flat(pos // MOE_ALIGN), nch=flat(nch), tile_e=tile_e, n_used=n_used,
                dd=dd, cmeta=cmeta, two=two, m_pad=m_pad, n_tiles=n_tiles, nb=nb)


def _dispatch_body(lo_ref, pos_ref, nch_ref, x_ref, sh_ref, sc_ref, dd_ref, xs_in_ref, xs_ref, stage_scr, sem,
                   *, nb):
    del xs_in_ref
    b = pl.program_id(0)
    slot = b % 2
    h = _modulate(x_ref[...], sh_ref[0], sc_ref[0]).astype(BF16)
    r = lax.broadcasted_iota(jnp.int32, (MOE_STAGE, MOE_T), 0)
    dd = dd_ref[0]
    onehot = jnp.where((r == dd[0:1, :]) | (r == dd[1:2, :]), 1.0, 0.0).astype(BF16)
    stage_scr[slot] = jnp.dot(onehot, h, preferred_element_type=F32)

    def seg_copy(blk, e, c):
        src0 = pl.multiple_of(lo_ref[blk * N_EXPERTS + e] * MOE_ALIGN + c * MOE_CH, MOE_ALIGN)
        dst0 = pl.multiple_of(pos_ref[blk * N_EXPERTS + e] * MOE_ALIGN + c * MOE_CH, MOE_ALIGN)
        return pltpu.make_async_copy(stage_scr.at[blk % 2, pl.ds(src0, MOE_CH)], xs_ref.at[pl.ds(dst0, MOE_CH)],
                                     sem.at[blk % 2, e, c])

    def for_segments(blk, fn):
        for e in range(N_EXPERTS):
            for c in range(2):
                @pl.when(c < nch_ref[blk * N_EXPERTS + e])
                def _():
                    fn(seg_copy(blk, e, c))

    @pl.when(b > 0)
    def _():
        for_segments(b - 1, lambda cp: cp.wait())

    for_segments(b, lambda cp: cp.start())

    @pl.when(b == nb - 1)
    def _():
        for_segments(b, lambda cp: cp.wait())


def moe_dispatch(x2, mod, plan, seq_len, buf=None):
    n_tokens, D = x2.shape
    if buf is None:
        buf = jnp.zeros((plan["m_pad"], D), F32)
    per_seq = seq_len // MOE_T
    nb = plan["nb"]
    vec = lambda: pl.BlockSpec((1, 1, D), lambda b, *_: (b // per_seq, 0, 0))
    grid_spec = pltpu.PrefetchScalarGridSpec(
        num_scalar_prefetch=3,
        grid=(nb,),
        in_specs=[
            pl.BlockSpec((MOE_T, D), lambda b, *_: (b, 0)),
            vec(), vec(),
            pl.BlockSpec((1, 2, MOE_T), lambda b, *_: (b, 0, 0)),
            pl.BlockSpec(memory_space=pltpu.MemorySpace.HBM),
        ],
        out_specs=pl.BlockSpec(memory_space=pltpu.MemorySpace.HBM),
        scratch_shapes=[pltpu.VMEM((2, MOE_STAGE, D), F32), pltpu.SemaphoreType.DMA((2, N_EXPERTS, 2))],
    )
    return pl.pallas_call(
        functools.partial(_dispatch_body, nb=nb),
        grid_spec=grid_spec,
        out_shape=jax.ShapeDtypeStruct((plan["m_pad"], D), F32),
        input_output_aliases={7: 0},
        compiler_params=_cparams(1),
        name="moe_dispatch",
    )(plan["lo"], plan["pos"], plan["nch"], x2, mod[0], mod[1], plan["dd"], buf)


def _gffn_body(te_ref, nu_ref, *refs, has_prev):
    it = iter(refs)
    xs_ref = next(it)
    yp_ref = next(it) if has_prev else None
    wg_ref, wu_ref, wd_ref, o_ref, wg_scr, wu_scr, wd_scr = (next(it) for _ in range(7))
    t = pl.program_id(0)
    e_here = te_ref[t]
    e_prev = te_ref[jnp.maximum(t - 1, 0)]

    @pl.when(t < nu_ref[0])
    def _():
        @pl.when((t == 0) | (e_here != e_prev))
        def _():
            wg_scr[...] = wg_ref[0].astype(BF16)
            wu_scr[...] = wu_ref[0].astype(BF16)
            wd_scr[...] = wd_ref[0].astype(BF16)

        h = xs_ref[...].astype(BF16)
        a = jnp.dot(h, wg_scr[...], preferred_element_type=F32)
        b = jnp.dot(h, wu_scr[...], preferred_element_type=F32)
        mid = (a * _sigmoid(a) * b).astype(BF16)
        y = jnp.dot(mid, wd_scr[...], preferred_element_type=F32)
        o_ref[...] = (yp_ref[...] + y) if has_prev else y

    @pl.when(t >= nu_ref[0])
    def _():
        o_ref[...] = jnp.zeros_like(o_ref)


def moe_grouped_ffn(xs, w_gu3, w_down3, wl, plan, f, y_prev=None):
    m_pad, D = xs.shape
    nf = FFN_DIM // MOE_TF
    has_prev = y_prev is not None
    row = lambda t, te, nu: (jnp.minimum(t, nu[0] - 1), 0)
    tile = lambda: pl.BlockSpec((MOE_TM, D), row)
    in_specs = [tile()] + ([tile()] if has_prev else [])
    in_specs += [
        pl.BlockSpec((1, D, MOE_TF), lambda t, te, nu: (wl + te[t], 0, f)),
        pl.BlockSpec((1, D, MOE_TF), lambda t, te, nu: (wl + te[t], 0, nf + f)),
        pl.BlockSpec((1, MOE_TF, D), lambda t, te, nu: (wl + te[t], f, 0), pipeline_mode=pl.Buffered(1)),
    ]
    grid_spec = pltpu.PrefetchScalarGridSpec(
        num_scalar_prefetch=2,
        grid=(plan["n_tiles"],),
        in_specs=in_specs,
        out_specs=pl.BlockSpec((MOE_TM, D), lambda t, te, nu: (t, 0)),
        scratch_shapes=[pltpu.VMEM((D, MOE_TF), BF16), pltpu.VMEM((D, MOE_TF), BF16), pltpu.VMEM((MOE_TF, D), BF16)],
    )
    args = [xs] + ([y_prev] if has_prev else []) + [w_gu3, w_gu3, w_down3]
    return pl.pallas_call(
        functools.partial(_gffn_body, has_prev=has_prev),
        grid_spec=grid_spec,
        out_shape=jax.ShapeDtypeStruct((m_pad, D), F32),
        compiler_params=_cparams(1),
        name=f"moe_grouped_ffn{f}",
    )(plan["tile_e"], plan["n_used"], *args)


def _combine_body(pos_ref, nch_ref, two_ref, x_ref, gate_ref, cm_ref, *refs, final):
    y_refs = refs[:N_EXPERTS]
    refs = refs[N_EXPERTS:]
    y_hbm, fg_ref = (refs[0], refs[1]) if final else (refs[0], None)
    o_ref, yhi_scr, ylo_scr, over_scr, sem = refs[2 if final else 1:]
    b = pl.program_id(0)

    def gathered(window, chunk):
        for e in range(N_EXPERTS):
            y = window(e)
            y_hi = y.astype(BF16)
            yhi_scr[e * MOE_CH:(e + 1) * MOE_CH, :] = y_hi
            ylo_scr[e * MOE_CH:(e + 1) * MOE_CH, :] = (y - y_hi.astype(F32)).astype(BF16)
        cm = cm_ref[...]
        lane = lax.broadcasted_iota(jnp.int32, (MOE_T, N_EXPERTS * MOE_CH), 1).astype(F32)
        picks = [jnp.where(lane == cm[:, 2 * chunk + k:2 * chunk + k + 1], 1.0, 0.0).astype(BF16) for k in range(2)]
        onehot = jnp.concatenate(picks, axis=0)
        rows = (jnp.dot(onehot, yhi_scr[...], preferred_element_type=F32)
                + jnp.dot(onehot, ylo_scr[...], preferred_element_type=F32))
        return cm[:, 4:5] * rows[:MOE_T] + cm[:, 5:6] * rows[MOE_T:]

    o_ref[...] = x_ref[...] + gate_ref[0] * gathered(lambda e: y_refs[e][...], 0)

    @pl.when(two_ref[b] > 0)
    def _():
        for e in range(N_EXPERTS):
            s = b * N_EXPERTS + e

            @pl.when(nch_ref[s] > 1)
            def _():
                start = pl.multiple_of(pos_ref[s] * MOE_ALIGN + MOE_CH, MOE_ALIGN)
                cp = pltpu.make_async_copy(y_hbm.at[pl.ds(start, MOE_CH)], over_scr.at[e], sem.at[e])
                cp.start()
                cp.wait()

            @pl.when(nch_ref[s] <= 1)
            def _():
                over_scr[e] = jnp.zeros((MOE_CH, over_scr.shape[2]), F32)

        o_ref[...] += gate_ref[0] * gathered(lambda e: over_scr[e], 1)

    if final:
        xo = o_ref[...]
        o_ref[...] = xo * lax.rsqrt(jnp.mean(xo * xo, axis=-1, keepdims=True) + EPS) * fg_ref[...]


def moe_combine(x2, gate, y, plan, seq_len, final_g=None):
    n_tokens, D = x2.shape
    per_seq = seq_len // MOE_T
    final = final_g is not None
    tail_specs = [pl.BlockSpec(memory_space=pltpu.MemorySpace.HBM)]
    tail_args = [y]
    if final:
        tail_specs.append(pl.BlockSpec((1, D), lambda b, *_: (0, 0)))
        tail_args.append(final_g.reshape(1, D))

    def window(e):
        return pl.BlockSpec((pl.Element(MOE_CH), pl.Element(D)),
                            lambda b, pos, nch, two: (pos[b * N_EXPERTS + e] * MOE_ALIGN, 0))

    blk = lambda w: pl.BlockSpec((MOE_T, w), lambda b, *_: (b, 0))
    grid_spec = pltpu.PrefetchScalarGridSpec(
        num_scalar_prefetch=3,
        grid=(plan["nb"],),
        in_specs=[blk(D), pl.BlockSpec((1, 1, D), lambda b, *_: (b // per_seq, 0, 0)), blk(LANES)]
        + [window(e) for e in range(N_EXPERTS)] + tail_specs,
        out_specs=blk(D),
        scratch_shapes=[pltpu.VMEM((N_EXPERTS * MOE_CH, D), BF16), pltpu.VMEM((N_EXPERTS * MOE_CH, D), BF16),
                        pltpu.VMEM((N_EXPERTS, MOE_CH, D), F32), pltpu.SemaphoreType.DMA((N_EXPERTS,))],
    )
    return pl.pallas_call(
        functools.partial(_combine_body, final=final),
        grid_spec=grid_spec,
        out_shape=jax.ShapeDtypeStruct((n_tokens, D), F32),
        compiler_params=_cparams(1),
        name="moe_combine",
    )(plan["pos"], plan["nch"], plan["two"], x2, gate, plan["cmeta"], *([y] * N_EXPERTS), *tail_args)


def moe_sparse(x3, mod, gate, gates, w_gu3, w_down3, wl, final_g=None, buf=None):
    B, L, D = x3.shape
    n_tokens = B * L
    assert L % MOE_T == 0 and FFN_DIM % MOE_TF == 0
    x2 = x3.reshape(n_tokens, D)
    gates2 = gates.reshape(n_tokens, LANES)
    plan = _moe_plan(gates2, n_tokens)
    xs = moe_dispatch(x2, mod, plan, L, buf)
    y = None
    for f in range(FFN_DIM // MOE_TF):
        y = moe_grouped_ffn(xs, w_gu3, w_down3, wl, plan, f, y_prev=y)
    return moe_combine(x2, gate, y, plan, L, final_g).reshape(B, L, D), xs


RET_CHUNKS_PER_STEP = 8


def _ret_body(*refs, zero_init, nc, chunk, per_step, heads):
    it = iter(refs)
    q_ref, k_ref, v_ref, g_ref = next(it), next(it), next(it), next(it)
    s0_ref = None if zero_init else next(it)
    inner_ref, qd_ref, kd_ref, cd_ref = next(it), next(it), next(it), next(it)
    o_ref, sout_ref, s_scr = next(it), next(it), next(it)
    c = pl.program_id(2)

    @pl.when(c == 0)
    def _():
        if zero_init:
            s_scr[...] = jnp.zeros_like(s_scr)
        else:
            s_scr[...] = s0_ref[0, 0]

    for hi in range(heads):
        kcols = slice(hi * RET_DK, (hi + 1) * RET_DK)
        vcols = slice(hi * RET_DV, (hi + 1) * RET_DV)
        s = s_scr[hi]
        for ci in range(per_step):
            rows = slice(ci * chunk, (ci + 1) * chunk)
            q = q_ref[0, rows, kcols]
            k = k_ref[0, rows, kcols]
            v = v_ref[0, rows, vcols]
            att = lax.dot_general(q, k, (((1,), (1,)), ((), ())), preferred_element_type=F32) * inner_ref[hi]
            inner = jnp.dot(att.astype(BF16), v, preferred_element_type=F32)
            cross = jnp.dot(q, s.astype(BF16), preferred_element_type=F32) * qd_ref[hi]
            kdt = (k.astype(F32) * kd_ref[hi]).T.astype(BF16)
            s = s * cd_ref[hi] + jnp.dot(kdt, v, preferred_element_type=F32)
            o = inner + cross
            on = o * lax.rsqrt(jnp.mean(o * o, axis=-1, keepdims=True) + EPS)
            g = g_ref[0, rows, vcols]
            o_ref[0, rows, vcols] = (g * _sigmoid(g) * on).astype(o_ref.dtype)
        s_scr[hi] = s

    @pl.when(c == nc - 1)
    def _():
        sout_ref[0] = s_scr[...]


def retention_scan(q, k, v, g, tables, chunk, *, s0=None, s0_layer=0, heads=1):
    B, L, _ = q.shape
    per_step = math.gcd(L // chunk, RET_CHUNKS_PER_STEP)
    rows = per_step * chunk
    nc = L // rows
    inner, qd, kd, cd = tables
    zero_init = s0 is None
    in_specs = [
        pl.BlockSpec((1, rows, heads * RET_DK), lambda h, b, c: (b, c, h)),
        pl.BlockSpec((1, rows, heads * RET_DK), lambda h, b, c: (b, c, h)),
        pl.BlockSpec((1, rows, heads * RET_DV), lambda h, b, c: (b, c, h)),
        pl.BlockSpec((1, rows, heads * RET_DV), lambda h, b, c: (b, c, h)),
    ]
    args = [q, k, v, g]
    if not zero_init:
        in_specs.append(pl.BlockSpec((1, 1, heads, RET_DK, RET_DV), lambda h, b, c: (s0_layer, b, h, 0, 0)))
        args.append(s0)
    in_specs += [
        pl.BlockSpec((heads, chunk, chunk), lambda h, b, c: (h, 0, 0)),
        pl.BlockSpec((heads, chunk, RET_DV), lambda h, b, c: (h, 0, 0)),
        pl.BlockSpec((heads, chunk, RET_DK), lambda h, b, c: (h, 0, 0)),
        pl.BlockSpec((heads, 1, RET_DV), lambda h, b, c: (h, 0, 0)),
    ]
    args += [inner, qd, kd, cd]
    return pl.pallas_call(
        functools.partial(_ret_body, zero_init=zero_init, nc=nc, chunk=chunk, per_step=per_step, heads=heads),
        grid=(RET_HEADS // heads, B, nc),
        in_specs=in_specs,
        out_specs=[
            pl.BlockSpec((1, rows, heads * RET_DV), lambda h, b, c: (b, c, h)),
            pl.BlockSpec((1, heads, RET_DK, RET_DV), lambda h, b, c: (b, h, 0, 0)),
        ],
        out_shape=[
            jax.ShapeDtypeStruct((B, L, RET_V), BF16),
            jax.ShapeDtypeStruct((B, RET_HEADS, RET_DK, RET_DV), F32),
        ],
        scratch_shapes=[pltpu.VMEM((heads, RET_DK, RET_DV), F32)],
        compiler_params=_cparams(3),
        name="retention_scan",
    )(*args)


def retention_tables(n_real, n_pad):
    log_gamma = jnp.log1p(-(2.0 ** (-5.0 - jnp.arange(RET_HEADS, dtype=F32))))
    idx = jnp.arange(n_pad, dtype=F32)
    valid = idx < n_real
    diff = idx[:, None] - idx[None, :]
    ok = (diff >= 0) & valid[:, None] & valid[None, :]
    inner = jnp.where(ok[None], jnp.exp(log_gamma[:, None, None] * jnp.maximum(diff, 0.0)[None]), 0.0)
    qd = jnp.exp(log_gamma[:, None] * (idx[None, :] + 1.0))
    kd = jnp.where(valid[None, :], jnp.exp(log_gamma[:, None] * (n_real - 1.0 - idx[None, :])), 0.0)
    cd = jnp.exp(log_gamma * n_real)
    return (inner,
            jnp.broadcast_to(qd[:, :, None], (RET_HEADS, n_pad, RET_DV)),
            jnp.broadcast_to(kd[:, :, None], (RET_HEADS, n_pad, RET_DK)),
            jnp.broadcast_to(cd[:, None, None], (RET_HEADS, 1, RET_DV)))


def rotary_tables(pos, reps):
    inv_freq = ROPE_BASE ** (-jnp.arange(0, RET_DK, 2, dtype=F32) / RET_DK)
    ang = pos.astype(F32)[:, None] * inv_freq[None, :]
    cos = jnp.repeat(jnp.cos(ang), 2, axis=1)
    sin = jnp.sin(ang)
    sin_signed = jnp.stack([-sin, sin], axis=-1).reshape(ang.shape[0], RET_DK)
    return jnp.tile(cos, (1, reps)), jnp.tile(sin_signed, (1, reps))


def _pool_body(x_ref, xp_ref, buf_ref, sh_ref, sc_ref, gate_ref, w_ref, cs_ref, o_ref, tail_ref, ext_scr,
               *, tm, pos0, has_prev):
    m = pl.program_id(1)
    sh, sc = sh_ref[0], sc_ref[0]
    h = _modulate(x_ref[0], sh, sc)

    @pl.when(m == 0)
    def _():
        ext_scr[0:POOL_HALO, :] = buf_ref[0]

    if has_prev:

        @pl.when(m > 0)
        def _():
            ext_scr[0:POOL_HALO, :] = _modulate(xp_ref[0], sh, sc)

    ext_scr[POOL_HALO:POOL_HALO + tm, :] = h
    tail_ref[0] = ext_scr[tm:tm + POOL_HALO, :]
    row = lax.broadcasted_iota(jnp.int32, (tm, 1), 0)
    pos1 = (pos0 + m * tm + row + 1).astype(F32)
    rows = max(tm, 16)
    ys = []
    for gi, w in enumerate(POOL_WINDOWS):
        c0, c1 = gi * POOL_GW, (gi + 1) * POOL_GW
        win = ext_scr[POOL_HALO:POOL_HALO + tm, c0:c1]
        for j in range(1, w):
            win = win + ext_scr[POOL_HALO - j:POOL_HALO - j + tm, c0:c1]
        d = win / jnp.minimum(jnp.float32(w), pos1) - h[:, c0:c1]
        if rows != tm:
            d = jnp.concatenate([d, jnp.zeros((rows - tm, POOL_GW), F32)], axis=0)
        y = jnp.dot(d.astype(BF16), w_ref[0, gi].astype(BF16), preferred_element_type=F32)
        ys.append(y[0:tm])
    y = jnp.concatenate(ys, axis=1) * cs_ref[...]
    o_ref[0] = x_ref[0] + gate_ref[0] * y


def pool_layer(x3, buf16, mod, gate, pool_w, pool_scale, wl, pos0, *, tm):
    B, L, D = x3.shape
    assert L % tm == 0 and (L == tm or tm % POOL_HALO == 0)
    has_prev = L > tm
    ph = POOL_HALO if has_prev else min(L, POOL_HALO)
    per = tm // POOL_HALO if has_prev else 1
    vec = lambda: pl.BlockSpec((1, 1, D), lambda b, m: (b, 0, 0))
    return pl.pallas_call(
        functools.partial(_pool_body, tm=tm, pos0=pos0, has_prev=has_prev),
        grid=(B, L // tm),
        in_specs=[
            pl.BlockSpec((1, tm, D), lambda b, m: (b, m, 0)),
            pl.BlockSpec((1, ph, D), lambda b, m: (b, jnp.maximum(m * per - 1, 0), 0)),
            pl.BlockSpec((1, POOL_HALO, D), lambda b, m: (b, 0, 0)),
            vec(), vec(), vec(),
            pl.BlockSpec((1,) + pool_w.shape[1:], lambda b, m: (wl, 0, 0, 0)),
            pl.BlockSpec((1, D), lambda b, m: (wl, 0)),
        ],
        out_specs=[
            pl.BlockSpec((1, tm, D), lambda b, m: (b, m, 0)),
            pl.BlockSpec((1, POOL_HALO, D), lambda b, m: (b, 0, 0)),
        ],
        out_shape=[
            jax.ShapeDtypeStruct((B, L, D), F32),
            jax.ShapeDtypeStruct((B, POOL_HALO, D), F32),
        ],
        scratch_shapes=[pltpu.VMEM((tm + POOL_HALO, D), F32)],
        compiler_params=_cparams(2),
        name="pool_layer",
    )(x3, x3, buf16, mod[0], mod[1], gate, pool_w, pool_scale)


def _split3(x):
    p0 = x.astype(BF16)
    r1 = x - p0.astype(F32)
    p1 = r1.astype(BF16)
    p2 = (r1 - p1.astype(F32)).astype(BF16)
    return p0, p1, p2


def _lane_cumsum(x, tri):
    p0, p1, p2 = _split3(x)
    dot = lambda p: jnp.dot(p, tri, preferred_element_type=F32)
    return (dot(p0) + dot(p1)) + dot(p2)


def _upper_tri(t):
    r = lax.broadcasted_iota(jnp.int32, (t, t), 0)
    c = lax.broadcasted_iota(jnp.int32, (t, t), 1)
    return jnp.where(r <= c, 1.0, 0.0).astype(BF16)


BIAS_PIECES = 3


def _fbias_body(lf_ref, o_ref, carry_scr, *, tc):
    @pl.when(pl.program_id(1) == 0)
    def _():
        carry_scr[...] = jnp.zeros_like(carry_scr)

    r = lax.broadcasted_iota(jnp.int32, (tc, tc), 0)
    c = lax.broadcasted_iota(jnp.int32, (tc, tc), 1)
    tril = jnp.where(c <= r, 1.0, 0.0).astype(BF16)
    p0, p1, p2 = _split3(lf_ref[0])
    dot = lambda p: jnp.dot(tril, p, preferred_element_type=F32)
    f = carry_scr[...] + ((dot(p0) + dot(p1)) + dot(p2))
    carry_scr[...] = f[tc - 1:tc, :]
    head = lax.broadcasted_iota(jnp.int32, (FOX_HEADS, LANES), 0)
    lane = lax.broadcasted_iota(jnp.int32, (FOX_HEADS, LANES), 1)
    out = None
    for p, piece in enumerate(_split3(-f)):
        place = jnp.where(lane == BIAS_PIECES * head + p, 1.0, 0.0).astype(BF16)
        term = jnp.dot(piece, place, preferred_element_type=F32)
        out = term if out is None else out + term
    o_ref[0] = out.astype(BF16)


def fox_bias_features(lf, *, tc=512):
    B, L, H = lf.shape
    tc = min(tc, L)
    return pl.pallas_call(
        functools.partial(_fbias_body, tc=tc),
        grid=(B, L // tc),
        in_specs=[pl.BlockSpec((1, tc, H), lambda b, c: (b, c, 0))],
        out_specs=pl.BlockSpec((1, tc, LANES), lambda b, c: (b, c, 0)),
        out_shape=jax.ShapeDtypeStruct((B, L, LANES), BF16),
        scratch_shapes=[pltpu.VMEM((1, H), F32)],
        compiler_params=_cparams(2),
        name="fox_bias_features",
    )(lf)


DEN_ROWS = 16


def _flash_body(qt_ref, k_ref, fb_ref, vt_ref, o_ref, *, tq, tk):
    hp = pl.program_id(1)
    qi = pl.program_id(2)
    pair = 2 * FOX_DH
    row = lax.broadcasted_iota(jnp.int32, (pair, tq), 0)
    qt = qt_ref[0]
    qaug = []
    for i in range(2):
        q_head = jnp.where(row // FOX_DH == i, qt, jnp.zeros_like(qt))
        pick = jnp.where(row // BIAS_PIECES == 2 * hp + i, 1.0, 0.0).astype(BF16)
        qaug.append(jnp.concatenate([q_head, pick], axis=0))
    key_i = lax.broadcasted_iota(jnp.int32, (tk, tq), 0)
    qry_i = lax.broadcasted_iota(jnp.int32, (tk, tq), 1)
    per_q = tq // tk

    def step(j, carry, diag):
        k0 = pl.multiple_of(j * tk, tk)
        kaug = jnp.concatenate([k_ref[0, pl.ds(k0, tk), :], fb_ref[0, pl.ds(k0, tk), :]], axis=1)
        ones = jnp.ones((DEN_ROWS, tk), BF16)
        new = []
        for i in range(2):
            m_old, acc = carry[i]
            st = jnp.dot(kaug, qaug[i], preferred_element_type=F32)
            if diag is not None:
                st = jnp.where(key_i + diag * tk <= qry_i, st, NEG_BIG)
            m_new = jnp.maximum(m_old, jnp.max(st, axis=0, keepdims=True))
            alpha = jnp.exp(m_old - m_new)
            p = jnp.exp(st - m_new).astype(BF16)
            vt = jnp.concatenate([vt_ref[0, i * FOX_DH:(i + 1) * FOX_DH, pl.ds(k0, tk)].astype(BF16), ones], axis=0)
            acc = alpha * acc + jnp.dot(vt, p, preferred_element_type=F32)
            new.append((m_new, acc))
        return tuple(new)

    init = tuple((jnp.full((1, tq), NEG_BIG, F32), jnp.zeros((FOX_DH + DEN_ROWS, tq), F32)) for _ in range(2))
    carry = lax.fori_loop(0, qi * per_q, lambda j, c: step(j, c, None), init)
    for d in range(per_q):
        carry = step(qi * per_q + d, carry, d)
    o_ref[0] = jnp.concatenate([(acc[:FOX_DH] / acc[FOX_DH:FOX_DH + 1]).T for _, acc in carry],
                               axis=1).astype(o_ref.dtype)


def fox_flash(qt, k, fb, vt, *, tq=1024, tk=1024):
    B, L, D = k.shape
    tq = min(tq, L)
    tk = min(tk, tq)
    assert L % tq == 0 and tq % tk == 0
    pair = 2 * FOX_DH
    return pl.pallas_call(
        functools.partial(_flash_body, tq=tq, tk=tk),
        grid=(B, FOX_HEADS // 2, L // tq),
        in_specs=[
            pl.BlockSpec((1, pair, tq), lambda b, hp, qi: (b, hp, qi)),
            pl.BlockSpec((1, L, pair), lambda b, hp, qi: (b, 0, hp)),
            pl.BlockSpec((1, L, LANES), lambda b, hp, qi: (b, 0, 0)),
            pl.BlockSpec((1, pair, L), lambda b, hp, qi: (b, hp, 0)),
        ],
        out_specs=pl.BlockSpec((1, tq, pair), lambda b, hp, qi: (b, qi, hp)),
        out_shape=jax.ShapeDtypeStruct((B, L, D), BF16),
        compiler_params=_cparams(3),
        name="fox_flash",
    )(qt, k, fb, vt)


MAX_DECODE_PAGES = 16


def _decode_body(pt_ref, q_ref, kn_ref, vn_ref, lfn_ref, *refs, n_steps, lq, pages):
    kc = refs[0:pages]
    vc = refs[pages:2 * pages]
    lc = refs[2 * pages:3 * pages]
    o_ref, qbd_scr, m_scr, l_scr, acc_scr, carry_scr = refs[3 * pages:]
    b = pl.program_id(0)
    st = pl.program_id(1)
    rows = lq * FOX_HEADS
    last = (((1,), (1,)), ((), ()))
    tri = _upper_tri(PAGE_SIZE)

    @pl.when(st == 0)
    def _():
        head = lax.broadcasted_iota(jnp.int32, (FOX_HEADS, D_MODEL), 0)
        lane_head = lax.broadcasted_iota(jnp.int32, (FOX_HEADS, D_MODEL), 1) // FOX_DH
        blocks = [jnp.where(head == lane_head, jnp.broadcast_to(q_ref[0, t:t + 1, :], (FOX_HEADS, D_MODEL)), 0.0)
                  for t in range(lq)]
        qbd_scr[...] = jnp.concatenate(blocks, axis=0).astype(BF16)
        m_scr[...] = jnp.full_like(m_scr, NEG_BIG)
        l_scr[...] = jnp.zeros_like(l_scr)
        acc_scr[...] = jnp.zeros_like(acc_scr)
        carry_scr[...] = jnp.zeros_like(carry_scr)

    def absorb(s, v_mat, v_is_t):
        m_old = m_scr[...]
        m_new = jnp.maximum(m_old, jnp.max(s, axis=1, keepdims=True))
        alpha = jnp.exp(m_old - m_new)
        p = jnp.exp(s - m_new)
        l_scr[...] = alpha * l_scr[...] + jnp.sum(p, axis=1, keepdims=True)
        if v_is_t:
            pv = lax.dot_general(p.astype(BF16), v_mat, last, preferred_element_type=F32)
        else:
            pv = jnp.dot(p.astype(BF16), v_mat, preferred_element_type=F32)
        acc_scr[...] = alpha * acc_scr[...] + pv
        m_scr[...] = m_new

    @pl.when(st < n_steps)
    def _():
        kt = jnp.concatenate([kc[i][0].reshape(D_MODEL, PAGE_SIZE).astype(BF16) for i in range(pages)], axis=1)
        vt = jnp.concatenate([vc[i][0].reshape(D_MODEL, PAGE_SIZE).astype(BF16) for i in range(pages)], axis=1)
        within = _lane_cumsum(jnp.concatenate([lc[i][0] for i in range(pages)], axis=0), tri)
        f = carry_scr[...]
        biases = []
        for i in range(pages):
            f_page = f + within[i * FOX_HEADS:(i + 1) * FOX_HEADS, :]
            biases.append(jnp.tile(f_page, (lq, 1)))
            f = jnp.broadcast_to(f_page[:, PAGE_SIZE - 1:PAGE_SIZE], f_page.shape)
        carry_scr[...] = f
        s = jnp.dot(qbd_scr[...], kt, preferred_element_type=F32) - jnp.concatenate(biases, axis=1)
        absorb(s, vt, True)

    @pl.when(st == n_steps)
    def _():
        pad = jnp.zeros((PAGE_SIZE - lq, D_MODEL), F32)
        kn = jnp.concatenate([kn_ref[0], pad], axis=0).astype(BF16)
        vn = jnp.concatenate([vn_ref[0], pad], axis=0).astype(BF16)
        n_tok = lfn_ref.shape[1]
        tok = lax.broadcasted_iota(jnp.int32, (n_tok, PAGE_SIZE), 0)
        key = lax.broadcasted_iota(jnp.int32, (n_tok, PAGE_SIZE), 1)
        sel = jnp.where((tok // lq == b) & (tok % lq <= key) & (key < lq), 1.0, 0.0).astype(BF16)
        p0, p1, p2 = _split3(lfn_ref[...])
        dot = lambda p: jnp.dot(p, sel, preferred_element_type=F32)
        f = carry_scr[...] + ((dot(p0) + dot(p1)) + dot(p2))
        s = lax.dot_general(qbd_scr[...], kn, last, preferred_element_type=F32) - jnp.tile(f, (lq, 1))
        rq = lax.broadcasted_iota(jnp.int32, (rows, PAGE_SIZE), 0) // FOX_HEADS
        kk = lax.broadcasted_iota(jnp.int32, (rows, PAGE_SIZE), 1)
        s = jnp.where(kk <= rq, s, NEG_BIG)
        absorb(s, vn, False)
        o = acc_scr[...] / l_scr[...]
        head = lax.broadcasted_iota(jnp.int32, (FOX_HEADS, D_MODEL), 0)
        lane_head = lax.broadcasted_iota(jnp.int32, (FOX_HEADS, D_MODEL), 1) // FOX_DH
        outs = [jnp.sum(jnp.where(head == lane_head, o[t * FOX_HEADS:(t + 1) * FOX_HEADS, :], 0.0), axis=0,
                        keepdims=True) for t in range(lq)]
        o_ref[0] = jnp.concatenate(outs, axis=0).astype(o_ref.dtype)


def fox_decode(q, k_new, v_new, lft_new, cache_kt, cache_vt, cache_lt, page_table):
    B, lq, D = q.shape
    n_pages = page_table.shape[1]
    pages = math.gcd(n_pages, MAX_DECODE_PAGES)
    n_steps = n_pages // pages
    rows = lq * FOX_HEADS

    def page_idx(i):
        return lambda b, s, pt: (pt[b * n_pages + jnp.minimum(s, n_steps - 1) * pages + i], 0, 0, 0)

    def page_idx3(i):
        return lambda b, s, pt: (pt[b * n_pages + jnp.minimum(s, n_steps - 1) * pages + i], 0, 0)

    seq = lambda: pl.BlockSpec((1, lq, D), lambda b, s, pt: (b, 0, 0))
    in_specs = [seq(), seq(), seq(), pl.BlockSpec(lft_new.shape, lambda b, s, pt: (0, 0))]
    in_specs += [pl.BlockSpec((1, FOX_HEADS, FOX_DH, PAGE_SIZE), page_idx(i)) for i in range(pages)]
    in_specs += [pl.BlockSpec((1, FOX_HEADS, FOX_DH, PAGE_SIZE), page_idx(i)) for i in range(pages)]
    in_specs += [pl.BlockSpec((1, FOX_HEADS, PAGE_SIZE), page_idx3(i)) for i in range(pages)]
    grid_spec = pltpu.PrefetchScalarGridSpec(
        num_scalar_prefetch=1,
        grid=(B, n_steps + 1),
        in_specs=in_specs,
        out_specs=pl.BlockSpec((1, lq, D), lambda b, s, pt: (b, 0, 0)),
        scratch_shapes=[
            pltpu.VMEM((rows, D), BF16),
            pltpu.VMEM((rows, 1), F32),
            pltpu.VMEM((rows, 1), F32),
            pltpu.VMEM((rows, D), F32),
            pltpu.VMEM((FOX_HEADS, PAGE_SIZE), F32),
        ],
    )
    return pl.pallas_call(
        functools.partial(_decode_body, n_steps=n_steps, lq=lq, pages=pages),
        grid_spec=grid_spec,
        out_shape=jax.ShapeDtypeStruct((B, lq, D), F32),
        compiler_params=_cparams(2),
        name="fox_decode",
    )(page_table.reshape(-1), q, k_new, v_new, lft_new,
      *([cache_kt] * pages), *([cache_vt] * pages), *([cache_lt] * pages))


def _final_body(x_ref, g_ref, o_ref):
    xf = x_ref[0]
    o_ref[0] = xf * lax.rsqrt(jnp.mean(xf * xf, axis=-1, keepdims=True) + EPS) * g_ref[...]


def final_norm(x3, final_g, *, tm=None):
    bx, L, D = x3.shape
    tm = tm or min(L, 1024)
    return pl.pallas_call(
        _final_body,
        grid=(bx, L // tm),
        in_specs=[pl.BlockSpec((1, tm, D), lambda b, m: (b, m, 0)), pl.BlockSpec((1, D), lambda b, m: (0, 0))],
        out_specs=pl.BlockSpec((1, tm, D), lambda b, m: (b, m, 0)),
        out_shape=jax.ShapeDtypeStruct((bx, L, D), F32),
        compiler_params=_cparams(2),
        name="final_norm",
    )(x3, final_g.reshape(1, D))


def _trunk(x, mods, ret_state, pool_state, fox_past, pos0, params):
    (ret_w_in, ret_w_out, pool_w, pool_scale, fox_wt, fox_b_f, fox_w_out, ffn_w_gu, ffn_w_down,
     w_rt_pad, b_rt_pad, moe_w_gu, moe_w_down, final_g) = params
    B, L, D = x.shape
    decode = fox_past is not None
    if decode:
        x3 = x.reshape(1, B * L, D)
        expand = lambda v: jnp.repeat(v, L, axis=0)[None]
    else:
        x3 = x
        expand = lambda v: v[:, None, :]
    n_rows = x3.shape[1]
    pos = pos0 + jnp.arange(L)
    cos, sin = rotary_tables(pos, RET_HEADS)
    if decode:
        cos, sin = jnp.tile(cos, (B, 1)), jnp.tile(sin, (B, 1))
    chunk = min(L, 256)
    chunk_pad = max(chunk, PAGE_SIZE)
    tables = retention_tables(chunk, chunk_pad)
    ret_new, extras = [], {}
    moe_buf = None
    for i in range(DEPTH):
        sh_a, sc_a, g_a, sh_f, sc_f, g_f = [expand(v) for v in jnp.split(mods[i], 6, axis=-1)]
        kind, j = i % N_MIXERS, i // N_MIXERS
        if kind == 0:
            proj = functools.partial(mm, x3, ret_w_in, j, mod=(sh_a, sc_a))
            q = proj(n0=0, n_out=RET_QK, out_dtype=BF16, epi="rot", rot=(cos, sin), name="ret_q")
            k = proj(n0=RET_QK, n_out=RET_QK, out_dtype=BF16, epi="rot", rot=(cos, sin), scale=RET_DK ** -0.5,
                     name="ret_k")
            v = proj(n0=2 * RET_QK, n_out=RET_V, out_dtype=BF16, name="ret_v")
            g = proj(n0=2 * RET_QK + RET_V, n_out=RET_V, out_dtype=F32, name="ret_g")
            if decode:
                padr = lambda t: jnp.pad(t.reshape(B, L, -1), ((0, 0), (0, chunk_pad - L), (0, 0)))
                o, s = retention_scan(padr(q), padr(k), padr(v), padr(g), tables, chunk_pad, s0=ret_state,
                                      s0_layer=j, heads=RET_HEADS)
                o = o[:, :L].reshape(1, n_rows, RET_V)
            else:
                o, s = retention_scan(q, k, v, g, tables, chunk_pad)
            ret_new.append(s)
            x3 = mm(o, ret_w_out, j, n0=0, n_out=D, out_dtype=F32, epi="res", res=(x3, g_a), name="ret_out")
        elif kind == 1:
            vecs = [v[:, None, :] for v in jnp.split(mods[i], 6, axis=-1)[:3]]
            if decode:
                buf16 = jnp.pad(pool_state[j], ((0, 0), (1, 0), (0, 0)))
                tm = L
            else:
                buf16 = jnp.zeros((B, POOL_HALO, D), F32)
                tm = min(L, 1024)
            xn, tail = pool_layer(x3.reshape(B, L, D), buf16, (vecs[0], vecs[1]), vecs[2], pool_w, pool_scale, j,
                                  pos0, tm=tm)
            x3 = xn.reshape(x3.shape)
            extras["pool"] = tail[:, 1:, :]
        else:
            fproj = functools.partial(mm, x3, fox_wt, j, mod=(sh_a, sc_a), w_t=True)
            if decode:
                ck, cv, cl, pt = fox_past
                q = fproj(n0=0, n_out=D, out_dtype=F32, scale=FOX_DH ** -0.5, name="fox_q")
                lft = fproj(n0=3 * D, n_out=FOX_HEADS, out_dtype=F32, out_t=True, epi="logsig",
                            bias=fox_b_f[j].reshape(FOX_HEADS, 1), name="fox_logft")
                k = fproj(n0=D, n_out=D, out_dtype=F32, name="fox_k")
                v = fproj(n0=2 * D, n_out=D, out_dtype=F32, name="fox_v")
                o = fox_decode(q.reshape(B, L, D), k.reshape(B, L, D), v.reshape(B, L, D), lft[0],
                               jnp.transpose(ck[j], (0, 2, 3, 1)), jnp.transpose(cv[j], (0, 2, 3, 1)),
                               jnp.transpose(cl[j], (0, 2, 1)), pt)
                o = o.reshape(1, n_rows, D)
                extras["k"] = k.reshape(B, L, FOX_HEADS, FOX_DH)
                extras["v"] = v.reshape(B, L, FOX_HEADS, FOX_DH)
                extras["l"] = jnp.transpose(lft[0].reshape(FOX_HEADS, B, L), (1, 2, 0))
            else:
                qt = fproj(n0=0, n_out=D, out_dtype=BF16, out_t=True, scale=FOX_DH ** -0.5, name="fox_qt")
                kb = fproj(n0=D, n_out=D, out_dtype=BF16, name="fox_kb")
                kt = fproj(n0=D, n_out=D, out_dtype=F32, out_t=True, name="fox_kt")
                vt = fproj(n0=2 * D, n_out=D, out_dtype=F32, out_t=True, name="fox_vt")
                lf = fproj(n0=3 * D, n_out=FOX_HEADS, out_dtype=F32, epi="logsig",
                           bias=fox_b_f[j].reshape(1, FOX_HEADS), name="fox_logf")
                o = fox_flash(qt, kb, fox_bias_features(lf), vt)
                unt = lambda t: jnp.transpose(t.reshape(B, FOX_HEADS, FOX_DH, L), (0, 3, 1, 2))
                extras["k"], extras["v"] = unt(kt), unt(vt)
                extras["l"] = lf
            x3 = mm(o, fox_w_out, j, n0=0, n_out=D, out_dtype=F32, epi="res", res=(x3, g_a), name="fox_out")
        ml = i // 2
        if i % 2 == 0:
            x3 = ffn(x3, (sh_f, sc_f), g_f, ffn_w_gu, ffn_w_down, ml, name="ffn_dense")
        else:
            gates = router(x3, (sh_f, sc_f), w_rt_pad, b_rt_pad, ml)
            if decode:
                x3 = ffn(x3, (sh_f, sc_f), g_f, moe_w_gu, moe_w_down, ml * N_EXPERTS, gates=gates, name="ffn_moe")
            else:
                closing = final_g if i == DEPTH - 1 else None
                x3, moe_buf = moe_sparse(x3, (sh_f, sc_f), g_f, gates, moe_w_gu, moe_w_down, ml * N_EXPERTS, closing,
                                         moe_buf)
    fused_final = (not decode) and (DEPTH - 1) % 2 == 1
    out = (x3 if fused_final else final_norm(x3, final_g)).reshape(B, L, D)
    return (out, jnp.stack(ret_new), extras["pool"][None], extras["k"][None], extras["v"][None], extras["l"][None])


def kernel(x_prompt, x_sample, state_ret, state_pool, cache_fox_k, cache_fox_v, cache_fox_logf, page_table,
           c_prompt, c_sample, ada_w, ada_b, ret_w_in, ret_w_out, pool_w, pool_scale, fox_w_in, fox_b_f, fox_w_out,
           ffn_w_gu, ffn_w_down, moe_w_router, moe_b_router, moe_w_gu, moe_w_down, final_g):
    bp, bs = x_prompt.shape[0], x_sample.shape[0]
    rows = -(-(bp + bs) // 8) * 8
    c_all = jnp.concatenate([c_prompt, c_sample, jnp.zeros((rows - bp - bs, D_MODEL), F32)], axis=0)
    mods = ada_mods(c_all, ada_w, ada_b)
    n_moe = moe_w_router.shape[0]
    params = (
        ret_w_in, ret_w_out, pool_w, pool_scale,
        jnp.swapaxes(fox_w_in, 1, 2),
        fox_b_f, fox_w_out, ffn_w_gu, ffn_w_down,
        jnp.pad(jnp.swapaxes(moe_w_router, 1, 2), ((0, 0), (0, LANES - N_EXPERTS), (0, 0))),
        jnp.pad(moe_b_router, ((0, 0), (0, LANES - N_EXPERTS))).reshape(n_moe, 1, LANES),
        moe_w_gu.reshape((n_moe * N_EXPERTS,) + moe_w_gu.shape[2:]),
        moe_w_down.reshape((n_moe * N_EXPERTS,) + moe_w_down.shape[2:]),
        final_g,
    )
    y_p, ret_p, pool_p, k_p, v_p, l_p = _trunk(x_prompt, mods[:, :bp], None, None, None, 0, params)
    n_past = page_table.shape[1] * PAGE_SIZE
    y_s, ret_s, pool_s, k_s, v_s, l_s = _trunk(
        x_sample, mods[:, bp:bp + bs], state_ret, state_pool,
        (cache_fox_k, cache_fox_v, cache_fox_logf, page_table), n_past, params)
    return (y_p, y_s, ret_p, ret_s, pool_p, pool_s, k_p, k_s, v_p, v_s, l_p, l_s)
```

```python
import functools
import math

import jax
import jax.numpy as jnp
from jax import lax
from jax.experimental import pallas as pl
from jax.experimental.pallas import tpu as pltpu

F32 = jnp.float32
BF16 = jnp.bfloat16

D_MODEL = 1024
DEPTH = 4
PAGE_SIZE = 128
N_MIXERS = 3
RET_HEADS = 4
RET_DK = D_MODEL // RET_HEADS
RET_DV = 2 * D_MODEL // RET_HEADS
RET_QK = RET_HEADS * RET_DK
RET_V = RET_HEADS * RET_DV
ROPE_BASE = 10000.0
POOL_WINDOWS = (2, 4, 8, 16)
POOL_GW = D_MODEL // len(POOL_WINDOWS)
POOL_BUF = max(POOL_WINDOWS) - 1
POOL_HALO = POOL_BUF + 1
FOX_HEADS = 16
FOX_DH = D_MODEL // FOX_HEADS
FFN_DIM = 2816
N_EXPERTS = 8
EPS = 1e-6
NEG_BIG = -1e30

V7X_VMEM_BYTES = 64 * 1024 * 1024
VMEM_LIMIT = V7X_VMEM_BYTES - 8 * 1024 * 1024
LANES = 128
FFN_TF = 256


def _cparams(n_axes):
    return pltpu.CompilerParams(dimension_semantics=("arbitrary",) * n_axes, vmem_limit_bytes=VMEM_LIMIT)


def _sigmoid(x):
    return 1.0 / (1.0 + jnp.exp(-x))


def _modulate(x, shift, scale):
    xf = x.astype(F32)
    ms = jnp.mean(xf * xf, axis=-1, keepdims=True)
    return (xf * lax.rsqrt(ms + EPS)) * (1.0 + scale) + shift


def _row_spec(arr, tm, width, col_fn):
    if arr.shape[1] == 1:
        return pl.BlockSpec((1, 1, width), lambda b, m, *r: (b, 0, col_fn(*r)))
    return pl.BlockSpec((1, tm, width), lambda b, m, *r: (b, m, col_fn(*r)))


def _ada_body(c_ref, w_ref, b_ref, o_ref):
    c = c_ref[...]
    cond = c * _sigmoid(c)
    o_ref[0] = jnp.dot(cond.astype(BF16), w_ref[0].astype(BF16), preferred_element_type=F32) + b_ref[0]


def ada_mods(c_all, ada_w, ada_b):
    rows = c_all.shape[0]
    n_out = ada_w.shape[2]
    tn = 1024
    return pl.pallas_call(
        _ada_body,
        grid=(DEPTH, n_out // tn),
        in_specs=[
            pl.BlockSpec((rows, D_MODEL), lambda i, n: (0, 0)),
            pl.BlockSpec((1, D_MODEL, tn), lambda i, n: (i, 0, n)),
            pl.BlockSpec((1, 1, tn), lambda i, n: (i, 0, n)),
        ],
        out_specs=pl.BlockSpec((1, rows, tn), lambda i, n: (i, 0, n)),
        out_shape=jax.ShapeDtypeStruct((DEPTH, rows, n_out), F32),
        compiler_params=_cparams(2),
        name="ada_mods",
    )(c_all, ada_w, ada_b.reshape(DEPTH, 1, n_out))


def _mm_body(*refs, has_mod, epi, w_t, out_t, scale):
    it = iter(refs)
    x_ref = next(it)
    if has_mod:
        sh_ref, sc_ref = next(it), next(it)
    w_ref = next(it)
    if epi == "rot":
        cos_ref, sin_ref = next(it), next(it)
    elif epi == "res":
        res_ref, gate_ref = next(it), next(it)
    elif epi == "logsig":
        b_ref = next(it)
    o_ref = next(it)
    n = pl.program_id(2)
    if has_mod:
        h_scr = next(it)

        @pl.when(n == 0)
        def _():
            h_scr[...] = _modulate(x_ref[0], sh_ref[0], sc_ref[0]).astype(BF16)

        lhs = h_scr[...]
    else:
        lhs = x_ref[0].astype(BF16)
    w = w_ref[0].astype(BF16)
    last = (((1,), (1,)), ((), ()))
    if not w_t:
        acc = jnp.dot(lhs, w, preferred_element_type=F32)
    elif not out_t:
        acc = lax.dot_general(lhs, w, last, preferred_element_type=F32)
    else:
        acc = lax.dot_general(w, lhs, last, preferred_element_type=F32)
    if epi == "rot":
        acc = _rotary(acc, cos_ref[...], sin_ref[...])
    elif epi == "res":
        acc = res_ref[0] + gate_ref[0] * acc
    elif epi == "logsig":
        z = acc + b_ref[...]
        acc = jnp.minimum(z, 0.0) - jnp.log1p(jnp.exp(-jnp.abs(z)))
    if scale != 1.0:
        acc = acc * scale
    o_ref[0] = acc.astype(o_ref.dtype)


def mm(x3, w3, wl, *, n0, n_out, out_dtype, mod=None, epi="plain", w_t=False, out_t=False, scale=1.0,
       rot=None, res=None, bias=None, tm=None, tn=None, name="mm"):
    bx, L, K = x3.shape
    tm = tm or min(L, 1024 if (epi in ("rot", "res") or K > 1024) else 2048)
    tn = tn or min(n_out, 512 if K > 1024 else 1024)
    assert L % tm == 0 and n_out % tn == 0 and n0 % tn == 0
    nb0 = n0 // tn
    has_mod = mod is not None
    in_specs = [pl.BlockSpec((1, tm, K), lambda b, m, n: (b, m, 0))]
    args = [x3]
    if has_mod:
        for a in mod:
            in_specs.append(_row_spec(a, tm, K, lambda n: 0))
            args.append(a)
    if w_t:
        in_specs.append(pl.BlockSpec((1, tn, K), lambda b, m, n: (wl, nb0 + n, 0)))
    else:
        in_specs.append(pl.BlockSpec((1, K, tn), lambda b, m, n: (wl, 0, nb0 + n)))
    args.append(w3)
    if epi == "rot":
        for a in rot:
            in_specs.append(pl.BlockSpec((tm, tn), lambda b, m, n: (m, n)))
            args.append(a)
    elif epi == "res":
        in_specs.append(pl.BlockSpec((1, tm, tn), lambda b, m, n: (b, m, n)))
        in_specs.append(_row_spec(res[1], tm, tn, lambda n: n))
        args.extend(res)
    elif epi == "logsig":
        in_specs.append(pl.BlockSpec(bias.shape, lambda b, m, n: (0, 0)))
        args.append(bias)
    if out_t:
        out_spec = pl.BlockSpec((1, tn, tm), lambda b, m, n: (b, n, m))
        out_shape = jax.ShapeDtypeStruct((bx, n_out, L), out_dtype)
    else:
        out_spec = pl.BlockSpec((1, tm, tn), lambda b, m, n: (b, m, n))
        out_shape = jax.ShapeDtypeStruct((bx, L, n_out), out_dtype)
    return pl.pallas_call(
        functools.partial(_mm_body, has_mod=has_mod, epi=epi, w_t=w_t, out_t=out_t, scale=scale),
        grid=(bx, L // tm, n_out // tn),
        in_specs=in_specs,
        out_specs=out_spec,
        out_shape=out_shape,
        scratch_shapes=[pltpu.VMEM((tm, K), BF16)] if has_mod else [],
        compiler_params=_cparams(3),
        name=name,
    )(*args)


def _rotary(acc, cos, sin_signed):
    width = acc.shape[1]
    lane = lax.broadcasted_iota(jnp.int32, acc.shape, 1)
    partner = jnp.where(lane % 2 == 0, pltpu.roll(acc, width - 1, axis=1), pltpu.roll(acc, 1, axis=1))
    return acc * cos + partner * sin_signed


def _ffn_body(*refs, moe, nf, ne):
    it = iter(refs)
    x_ref, sh_ref, sc_ref, gate_ref = next(it), next(it), next(it), next(it)
    gw_ref = next(it) if moe else None
    wg_ref, wu_ref, wd_ref, o_ref, h_scr, acc_scr = next(it), next(it), next(it), next(it), next(it), next(it)
    tot_scr = next(it) if moe else None
    e = pl.program_id(2)
    f = pl.program_id(3)

    @pl.when((e == 0) & (f == 0))
    def _():
        h_scr[...] = _modulate(x_ref[0], sh_ref[0], sc_ref[0]).astype(BF16)

    @pl.when(f == 0)
    def _():
        acc_scr[...] = jnp.zeros_like(acc_scr)

    if moe:

        @pl.when((e == 0) & (f == 0))
        def _():
            tot_scr[...] = jnp.zeros_like(tot_scr)

    h = h_scr[...]
    a = jnp.dot(h, wg_ref[0].astype(BF16), preferred_element_type=F32)
    b = jnp.dot(h, wu_ref[0].astype(BF16), preferred_element_type=F32)
    mid = (a * _sigmoid(a) * b).astype(BF16)
    acc_scr[...] += jnp.dot(mid, wd_ref[0].astype(BF16), preferred_element_type=F32)

    if not moe:

        @pl.when(f == nf - 1)
        def _():
            o_ref[0] = x_ref[0] + gate_ref[0] * acc_scr[...]

    else:

        @pl.when(f == nf - 1)
        def _():
            gw = gw_ref[0]
            lane = lax.broadcasted_iota(jnp.int32, gw.shape, 1)
            col = jnp.sum(jnp.where(lane == e, gw, 0.0), axis=1, keepdims=True)
            tot_scr[...] += col * acc_scr[...]

            @pl.when(e == ne - 1)
            def _():
                o_ref[0] = x_ref[0] + gate_ref[0] * tot_scr[...]


def ffn(x3, mod, gate, w_gu3, w_down3, wl, *, gates=None, tm=None, name="ffn"):
    bx, L, D = x3.shape
    moe = gates is not None
    ne = N_EXPERTS if moe else 1
    tm = tm or min(L, 1024 if moe else 2048)
    tf = FFN_TF if tm > 256 else MOE_TF
    nf = FFN_DIM // tf
    assert L % tm == 0 and FFN_DIM % tf == 0
    in_specs = [pl.BlockSpec((1, tm, D), lambda b, m, e, f: (b, m, 0))]
    args = [x3]
    for a in (*mod, gate):
        in_specs.append(_row_spec(a, tm, D, lambda e, f: 0))
        args.append(a)
    if moe:
        in_specs.append(pl.BlockSpec((1, tm, LANES), lambda b, m, e, f: (b, m, 0)))
        args.append(gates)
    in_specs += [
        pl.BlockSpec((1, D, tf), lambda b, m, e, f: (wl + e, 0, f)),
        pl.BlockSpec((1, D, tf), lambda b, m, e, f: (wl + e, 0, nf + f)),
        pl.BlockSpec((1, tf, D), lambda b, m, e, f: (wl + e, f, 0)),
    ]
    args += [w_gu3, w_gu3, w_down3]
    scratch = [pltpu.VMEM((tm, D), BF16), pltpu.VMEM((tm, D), F32)]
    if moe:
        scratch.append(pltpu.VMEM((tm, D), F32))
    return pl.pallas_call(
        functools.partial(_ffn_body, moe=moe, nf=nf, ne=ne),
        grid=(bx, L // tm, ne, nf),
        in_specs=in_specs,
        out_specs=pl.BlockSpec((1, tm, D), lambda b, m, e, f: (b, m, 0)),
        out_shape=jax.ShapeDtypeStruct((bx, L, D), F32),
        scratch_shapes=scratch,
        compiler_params=_cparams(4),
        name=name,
    )(*args)


def _router_body(x_ref, sh_ref, sc_ref, w_ref, b_ref, o_ref):
    h = _modulate(x_ref[0], sh_ref[0], sc_ref[0]).astype(BF16)
    logits = lax.dot_general(h, w_ref[0].astype(BF16), (((1,), (1,)), ((), ())), preferred_element_type=F32)
    logits = logits + b_ref[0]
    lane = lax.broadcasted_iota(jnp.int32, logits.shape, 1).astype(F32)
    lg = jnp.where(lane < N_EXPERTS, logits, -jnp.inf)
    m1 = jnp.max(lg, axis=1, keepdims=True)
    i1 = jnp.min(jnp.where(lg == m1, lane, float(LANES)), axis=1, keepdims=True)
    lg2 = jnp.where(lane == i1, -jnp.inf, lg)
    m2 = jnp.max(lg2, axis=1, keepdims=True)
    i2 = jnp.min(jnp.where(lg2 == m2, lane, float(LANES)), axis=1, keepdims=True)
    e2 = jnp.exp(m2 - m1)
    den = 1.0 + e2
    o_ref[0] = jnp.where(lane == i1, 1.0 / den, 0.0) + jnp.where(lane == i2, e2 / den, 0.0)


def router(x3, mod, w_rt_pad, b_pad, wl, *, tm=None):
    bx, L, D = x3.shape
    tm = tm or min(L, 1024)
    in_specs = [pl.BlockSpec((1, tm, D), lambda b, m: (b, m, 0))]
    args = [x3]
    for a in mod:
        in_specs.append(_row_spec(a, tm, D, lambda: 0))
        args.append(a)
    in_specs += [
        pl.BlockSpec((1, LANES, D), lambda b, m: (wl, 0, 0)),
        pl.BlockSpec((1, 1, LANES), lambda b, m: (wl, 0, 0)),
    ]
    args += [w_rt_pad, b_pad]
    return pl.pallas_call(
        _router_body,
        grid=(bx, L // tm),
        in_specs=in_specs,
        out_specs=pl.BlockSpec((1, tm, LANES), lambda b, m: (b, m, 0)),
        out_shape=jax.ShapeDtypeStruct((bx, L, LANES), F32),
        compiler_params=_cparams(2),
        name="router",
    )(*args)


MOE_T = 256
MOE_CH = 128
MOE_TM = 512
MOE_TF = FFN_DIM // 2
MOE_ALIGN = 8
MOE_STAGE = -(-(2 * MOE_T + N_EXPERTS * (MOE_ALIGN - 1) + MOE_CH) // 8) * 8


def _moe_plan(gates2, n_tokens):
    nb = n_tokens // MOE_T
    routed = (gates2[:, :N_EXPERTS] > 0).reshape(nb, MOE_T, N_EXPERTS)
    ri = routed.astype(jnp.int32)
    t_i = jnp.arange(MOE_T)
    before = (t_i[None, :] < t_i[:, None]).astype(F32)
    rank = jnp.einsum("ts,bse->bte", before, routed.astype(F32)).astype(jnp.int32)
    cnt = jnp.sum(ri, axis=1)
    cnt_al = -(-cnt // MOE_ALIGN) * MOE_ALIGN
    lo = jnp.cumsum(cnt_al, axis=1) - cnt_al
    total = jnp.sum(cnt_al, axis=0)
    region = -(-(total + MOE_CH) // MOE_TM) * MOE_TM
    ends = jnp.cumsum(region)
    off = ends - region
    pos = off[None, :] + jnp.cumsum(cnt_al, axis=0) - cnt_al
    nch = -(-cnt_al // MOE_CH)
    m_pad = -(-(2 * n_tokens + nb * N_EXPERTS * (MOE_ALIGN - 1) + N_EXPERTS * (MOE_CH + MOE_TM)) // MOE_TM) * MOE_TM
    n_tiles = m_pad // MOE_TM
    tile_start = jnp.arange(n_tiles, dtype=jnp.int32) * MOE_TM
    tile_e = jnp.minimum(jnp.sum(tile_start[:, None] >= ends[None, :], axis=1), N_EXPERTS - 1).astype(jnp.int32)
    n_used = (ends[-1] // MOE_TM).astype(jnp.int32).reshape(1)
    dest = jnp.where(routed, lo[:, None, :] + rank, -1)
    d_hi = jnp.max(dest, axis=2)
    d_lo = jnp.min(jnp.where(routed, dest, MOE_STAGE), axis=2)
    dd = jnp.stack([d_hi, d_lo], axis=1).astype(jnp.int32)
    eidx = jnp.arange(N_EXPERTS, dtype=jnp.int32)
    e_a = jnp.min(jnp.where(routed, eidx, N_EXPERTS), axis=2)
    e_b = jnp.max(jnp.where(routed, eidx, -1), axis=2)
    g3 = gates2[:, :N_EXPERTS].reshape(nb, MOE_T, N_EXPERTS)
    at = lambda a, e: jnp.sum(jnp.where(eidx == e[..., None], a, 0), axis=2)
    r_a, r_b = at(rank, e_a), at(rank, e_b)
    w_a, w_b = at(g3, e_a), jnp.where(e_b != e_a, at(g3, e_b), 0.0)
    col = lambda e, r, c: jnp.where(r // MOE_CH == c, e * MOE_CH + r % MOE_CH, -1).astype(F32)
    cmeta = jnp.stack([col(e_a, r_a, 0), col(e_b, r_b, 0), col(e_a, r_a, 1), col(e_b, r_b, 1), w_a, w_b], axis=-1)
    cmeta = jnp.pad(cmeta.reshape(n_tokens, 6), ((0, 0), (0, LANES - 6)))
    two = (jnp.max(nch, axis=1) > 1).astype(jnp.int32)
    flat = lambda a: a.reshape(-1).astype(jnp.int32)
    return dict(lo=flat(lo // MOE_ALIGN), pos=flat(pos // MOE_ALIGN), nch=flat(nch), tile_e=tile_e, n_used=n_used,
                dd=dd, cmeta=cmeta, two=two, m_pad=m_pad, n_tiles=n_tiles, nb=nb)


def _dispatch_body(lo_ref, pos_ref, nch_ref, x_ref, sh_ref, sc_ref, dd_ref, xs_in_ref, xs_ref, stage_scr, sem,
                   *, nb):
    del xs_in_ref
    b = pl.program_id(0)
    slot = b % 2
    h = _modulate(x_ref[...], sh_ref[0], sc_ref[0]).astype(BF16)
    r = lax.broadcasted_iota(jnp.int32, (MOE_STAGE, MOE_T), 0)
    dd = dd_ref[0]
    onehot = jnp.where((r == dd[0:1, :]) | (r == dd[1:2, :]), 1.0, 0.0).astype(BF16)
    stage_scr[slot] = jnp.dot(onehot, h, preferred_element_type=F32)

    def seg_copy(blk, e, c):
        src0 = pl.multiple_of(lo_ref[blk * N_EXPERTS + e] * MOE_ALIGN + c * MOE_CH, MOE_ALIGN)
        dst0 = pl.multiple_of(pos_ref[blk * N_EXPERTS + e] * MOE_ALIGN + c * MOE_CH, MOE_ALIGN)
        return pltpu.make_async_copy(stage_scr.at[blk % 2, pl.ds(src0, MOE_CH)], xs_ref.at[pl.ds(dst0, MOE_CH)],
                                     sem.at[blk % 2, e, c])

    def for_segments(blk, fn):
        for e in range(N_EXPERTS):
            for c in range(2):
                @pl.when(c < nch_ref[blk * N_EXPERTS + e])
                def _():
                    fn(seg_copy(blk, e, c), e)

    @pl.when(b > 0)
    def _():
        for_segments(b - 1, lambda cp, e: cp.wait())

    for_segments(b, lambda cp, e: cp.start(priority=e % 2))

    @pl.when(b == nb - 1)
    def _():
        for_segments(b, lambda cp, e: cp.wait())


def moe_dispatch(x2, mod, plan, seq_len, buf=None):
    n_tokens, D = x2.shape
    if buf is None:
        buf = jnp.zeros((plan["m_pad"], D), F32)
    per_seq = seq_len // MOE_T
    nb = plan["nb"]
    vec = lambda: pl.BlockSpec((1, 1, D), lambda b, *_: (b // per_seq, 0, 0))
    grid_spec = pltpu.PrefetchScalarGridSpec(
        num_scalar_prefetch=3,
        grid=(nb,),
        in_specs=[
            pl.BlockSpec((MOE_T, D), lambda b, *_: (b, 0)),
            vec(), vec(),
            pl.BlockSpec((1, 2, MOE_T), lambda b, *_: (b, 0, 0)),
            pl.BlockSpec(memory_space=pltpu.MemorySpace.HBM),
        ],
        out_specs=pl.BlockSpec(memory_space=pltpu.MemorySpace.HBM),
        scratch_shapes=[pltpu.VMEM((2, MOE_STAGE, D), F32), pltpu.SemaphoreType.DMA((2, N_EXPERTS, 2))],
    )
    return pl.pallas_call(
        functools.partial(_dispatch_body, nb=nb),
        grid_spec=grid_spec,
        out_shape=jax.ShapeDtypeStruct((plan["m_pad"], D), F32),
        input_output_aliases={7: 0},
        compiler_params=_cparams(1),
        name="moe_dispatch",
    )(plan["lo"], plan["pos"], plan["nch"], x2, mod[0], mod[1], plan["dd"], buf)


def _gffn_body(te_ref, nu_ref, *refs, has_prev):
    it = iter(refs)
    xs_ref = next(it)
    yp_ref = next(it) if has_prev else None
    wg_ref, wu_ref, wd_ref, o_ref, wg_scr, wu_scr, wd_scr = (next(it) for _ in range(7))
    t = pl.program_id(0)
    e_here = te_ref[t]
    e_prev = te_ref[jnp.maximum(t - 1, 0)]

    @pl.when(t < nu_ref[0])
    def _():
        @pl.when((t == 0) | (e_here != e_prev))
        def _():
            wg_scr[...] = wg_ref[0].astype(BF16)
            wu_scr[...] = wu_ref[0].astype(BF16)
            wd_scr[...] = wd_ref[0].astype(BF16)

        h = xs_ref[...].astype(BF16)
        a = jnp.dot(h, wg_scr[...], preferred_element_type=F32)
        b = jnp.dot(h, wu_scr[...], preferred_element_type=F32)
        mid = (a * _sigmoid(a) * b).astype(BF16)
        y = jnp.dot(mid, wd_scr[...], preferred_element_type=F32)
        o_ref[...] = (yp_ref[...] + y) if has_prev else y

    @pl.when(t >= nu_ref[0])
    def _():
        o_ref[...] = jnp.zeros_like(o_ref)


def moe_grouped_ffn(xs, w_gu3, w_down3, wl, plan, f, y_prev=None):
    m_pad, D = xs.shape
    nf = FFN_DIM // MOE_TF
    has_prev = y_prev is not None
    row = lambda t, te, nu: (jnp.minimum(t, nu[0] - 1), 0)
    tile = lambda: pl.BlockSpec((MOE_TM, D), row)
    in_specs = [tile()] + ([tile()] if has_prev else [])
    in_specs += [
        pl.BlockSpec((1, D, MOE_TF), lambda t, te, nu: (wl + te[t], 0, f)),
        pl.BlockSpec((1, D, MOE_TF), lambda t, te, nu: (wl + te[t], 0, nf + f)),
        pl.BlockSpec((1, MOE_TF, D), lambda t, te, nu: (wl + te[t], f, 0), pipeline_mode=pl.Buffered(1)),
    ]
    grid_spec = pltpu.PrefetchScalarGridSpec(
        num_scalar_prefetch=2,
        grid=(plan["n_tiles"],),
        in_specs=in_specs,
        out_specs=pl.BlockSpec((MOE_TM, D), lambda t, te, nu: (t, 0)),
        scratch_shapes=[pltpu.VMEM((D, MOE_TF), BF16), pltpu.VMEM((D, MOE_TF), BF16), pltpu.VMEM((MOE_TF, D), BF16)],
    )
    args = [xs] + ([y_prev] if has_prev else []) + [w_gu3, w_gu3, w_down3]
    return pl.pallas_call(
        functools.partial(_gffn_body, has_prev=has_prev),
        grid_spec=grid_spec,
        out_shape=jax.ShapeDtypeStruct((m_pad, D), F32),
        compiler_params=_cparams(1),
        name=f"moe_grouped_ffn{f}",
    )(plan["tile_e"], plan["n_used"], *args)


def _combine_body(pos_ref, nch_ref, two_ref, x_ref, gate_ref, cm_ref, *refs, final):
    y_refs = refs[:N_EXPERTS]
    refs = refs[N_EXPERTS:]
    y_hbm, fg_ref = (refs[0], refs[1]) if final else (refs[0], None)
    o_ref, yhi_scr, ylo_scr, over_scr, sem = refs[2 if final else 1:]
    b = pl.program_id(0)

    def gathered(window, chunk):
        for e in range(N_EXPERTS):
            y = window(e)
            y_hi = y.astype(BF16)
            yhi_scr[e * MOE_CH:(e + 1) * MOE_CH, :] = y_hi
            ylo_scr[e * MOE_CH:(e + 1) * MOE_CH, :] = (y - y_hi.astype(F32)).astype(BF16)
        cm = cm_ref[...]
        lane = lax.broadcasted_iota(jnp.int32, (MOE_T, N_EXPERTS * MOE_CH), 1).astype(F32)
        picks = [jnp.where(lane == cm[:, 2 * chunk + k:2 * chunk + k + 1], 1.0, 0.0).astype(BF16) for k in range(2)]
        onehot = jnp.concatenate(picks, axis=0)
        rows = (jnp.dot(onehot, yhi_scr[...], preferred_element_type=F32)
                + jnp.dot(onehot, ylo_scr[...], preferred_element_type=F32))
        return cm[:, 4:5] * rows[:MOE_T] + cm[:, 5:6] * rows[MOE_T:]

    o_ref[...] = x_ref[...] + gate_ref[0] * gathered(lambda e: y_refs[e][...], 0)

    @pl.when(two_ref[b] > 0)
    def _():
        for e in range(N_EXPERTS):
            s = b * N_EXPERTS + e

            @pl.when(nch_ref[s] > 1)
            def _():
                start = pl.multiple_of(pos_ref[s] * MOE_ALIGN + MOE_CH, MOE_ALIGN)
                cp = pltpu.make_async_copy(y_hbm.at[pl.ds(start, MOE_CH)], over_scr.at[e], sem.at[e])
                cp.start()
                cp.wait()

            @pl.when(nch_ref[s] <= 1)
            def _():
                over_scr[e] = jnp.zeros((MOE_CH, over_scr.shape[2]), F32)

        o_ref[...] += gate_ref[0] * gathered(lambda e: over_scr[e], 1)

    if final:
        xo = o_ref[...]
        o_ref[...] = xo * lax.rsqrt(jnp.mean(xo * xo, axis=-1, keepdims=True) + EPS) * fg_ref[...]


def moe_combine(x2, gate, y, plan, seq_len, final_g=None):
    n_tokens, D = x2.shape
    per_seq = seq_len // MOE_T
    final = final_g is not None
    tail_specs = [pl.BlockSpec(memory_space=pltpu.MemorySpace.HBM)]
    tail_args = [y]
    if final:
        tail_specs.append(pl.BlockSpec((1, D), lambda b, *_: (0, 0)))
        tail_args.append(final_g.reshape(1, D))

    def window(e):
        return pl.BlockSpec((pl.Element(MOE_CH), pl.Element(D)),
                            lambda b, pos, nch, two: (pos[b * N_EXPERTS + e] * MOE_ALIGN, 0))

    blk = lambda w: pl.BlockSpec((MOE_T, w), lambda b, *_: (b, 0))
    grid_spec = pltpu.PrefetchScalarGridSpec(
        num_scalar_prefetch=3,
        grid=(plan["nb"],),
        in_specs=[blk(D), pl.BlockSpec((1, 1, D), lambda b, *_: (b // per_seq, 0, 0)), blk(LANES)]
        + [window(e) for e in range(N_EXPERTS)] + tail_specs,
        out_specs=blk(D),
        scratch_shapes=[pltpu.VMEM((N_EXPERTS * MOE_CH, D), BF16), pltpu.VMEM((N_EXPERTS * MOE_CH, D), BF16),
                        pltpu.VMEM((N_EXPERTS, MOE_CH, D), F32), pltpu.SemaphoreType.DMA((N_EXPERTS,))],
    )
    return pl.pallas_call(
        functools.partial(_combine_body, final=final),
        grid_spec=grid_spec,
        out_shape=jax.ShapeDtypeStruct((n_tokens, D), F32),
        compiler_params=_cparams(1),
        name="moe_combine",
    )(plan["pos"], plan["nch"], plan["two"], x2, gate, plan["cmeta"], *([y] * N_EXPERTS), *tail_args)


def moe_sparse(x3, mod, gate, gates, w_gu3, w_down3, wl, final_g=None, buf=None):
    B, L, D = x3.shape
    n_tokens = B * L
    assert L % MOE_T == 0 and FFN_DIM % MOE_TF == 0
    x2 = x3.reshape(n_tokens, D)
    gates2 = gates.reshape(n_tokens, LANES)
    plan = _moe_plan(gates2, n_tokens)
    xs = moe_dispatch(x2, mod, plan, L, buf)
    y = None
    for f in range(FFN_DIM // MOE_TF):
        y = moe_grouped_ffn(xs, w_gu3, w_down3, wl, plan, f, y_prev=y)
    return moe_combine(x2, gate, y, plan, L, final_g).reshape(B, L, D), xs


RET_CHUNKS_PER_STEP = 8


def _ret_body(*refs, zero_init, nc, chunk, per_step, heads):
    it = iter(refs)
    q_ref, k_ref, v_ref, g_ref = next(it), next(it), next(it), next(it)
    s0_ref = None if zero_init else next(it)
    inner_ref, qd_ref, kd_ref, cd_ref = next(it), next(it), next(it), next(it)
    o_ref, sout_ref, s_scr = next(it), next(it), next(it)
    c = pl.program_id(2)

    @pl.when(c == 0)
    def _():
        if zero_init:
            s_scr[...] = jnp.zeros_like(s_scr)
        else:
            s_scr[...] = s0_ref[0, 0]

    for hi in range(heads):
        kcols = slice(hi * RET_DK, (hi + 1) * RET_DK)
        vcols = slice(hi * RET_DV, (hi + 1) * RET_DV)
        s = s_scr[hi]
        for ci in range(per_step):
            rows = slice(ci * chunk, (ci + 1) * chunk)
            q = q_ref[0, rows, kcols]
            k = k_ref[0, rows, kcols]
            v = v_ref[0, rows, vcols]
            att = lax.dot_general(q, k, (((1,), (1,)), ((), ())), preferred_element_type=F32) * inner_ref[hi]
            inner = jnp.dot(att.astype(BF16), v, preferred_element_type=F32)
            cross = jnp.dot(q, s.astype(BF16), preferred_element_type=F32) * qd_ref[hi]
            kdt = (k.astype(F32) * kd_ref[hi]).T.astype(BF16)
            s = s * cd_ref[hi] + jnp.dot(kdt, v, preferred_element_type=F32)
            o = inner + cross
            on = o * lax.rsqrt(jnp.mean(o * o, axis=-1, keepdims=True) + EPS)
            g = g_ref[0, rows, vcols]
            o_ref[0, rows, vcols] = (g * _sigmoid(g) * on).astype(o_ref.dtype)
        s_scr[hi] = s

    @pl.when(c == nc - 1)
    def _():
        sout_ref[0] = s_scr[...]


def retention_scan(q, k, v, g, tables, chunk, *, s0=None, s0_layer=0, heads=1):
    B, L, _ = q.shape
    per_step = math.gcd(L // chunk, RET_CHUNKS_PER_STEP)
    rows = per_step * chunk
    nc = L // rows
    inner, qd, kd, cd = tables
    zero_init = s0 is None
    in_specs = [
        pl.BlockSpec((1, rows, heads * RET_DK), lambda h, b, c: (b, c, h)),
        pl.BlockSpec((1, rows, heads * RET_DK), lambda h, b, c: (b, c, h)),
        pl.BlockSpec((1, rows, heads * RET_DV), lambda h, b, c: (b, c, h)),
        pl.BlockSpec((1, rows, heads * RET_DV), lambda h, b, c: (b, c, h)),
    ]
    args = [q, k, v, g]
    if not zero_init:
        in_specs.append(pl.BlockSpec((1, 1, heads, RET_DK, RET_DV), lambda h, b, c: (s0_layer, b, h, 0, 0)))
        args.append(s0)
    in_specs += [
        pl.BlockSpec((heads, chunk, chunk), lambda h, b, c: (h, 0, 0)),
        pl.BlockSpec((heads, chunk, RET_DV), lambda h, b, c: (h, 0, 0)),
        pl.BlockSpec((heads, chunk, RET_DK), lambda h, b, c: (h, 0, 0)),
        pl.BlockSpec((heads, 1, RET_DV), lambda h, b, c: (h, 0, 0)),
    ]
    args += [inner, qd, kd, cd]
    return pl.pallas_call(
        functools.partial(_ret_body, zero_init=zero_init, nc=nc, chunk=chunk, per_step=per_step, heads=heads),
        grid=(RET_HEADS // heads, B, nc),
        in_specs=in_specs,
        out_specs=[
            pl.BlockSpec((1, rows, heads * RET_DV), lambda h, b, c: (b, c, h)),
            pl.BlockSpec((1, heads, RET_DK, RET_DV), lambda h, b, c: (b, h, 0, 0)),
        ],
        out_shape=[
            jax.ShapeDtypeStruct((B, L, RET_V), BF16),
            jax.ShapeDtypeStruct((B, RET_HEADS, RET_DK, RET_DV), F32),
        ],
        scratch_shapes=[pltpu.VMEM((heads, RET_DK, RET_DV), F32)],
        compiler_params=_cparams(3),
        name="retention_scan",
    )(*args)


def retention_tables(n_real, n_pad):
    log_gamma = jnp.log1p(-(2.0 ** (-5.0 - jnp.arange(RET_HEADS, dtype=F32))))
    idx = jnp.arange(n_pad, dtype=F32)
    valid = idx < n_real
    diff = idx[:, None] - idx[None, :]
    ok = (diff >= 0) & valid[:, None] & valid[None, :]
    inner = jnp.where(ok[None], jnp.exp(log_gamma[:, None, None] * jnp.maximum(diff, 0.0)[None]), 0.0)
    qd = jnp.exp(log_gamma[:, None] * (idx[None, :] + 1.0))
    kd = jnp.where(valid[None, :], jnp.exp(log_gamma[:, None] * (n_real - 1.0 - idx[None, :])), 0.0)
    cd = jnp.exp(log_gamma * n_real)
    return (inner,
            jnp.broadcast_to(qd[:, :, None], (RET_HEADS, n_pad, RET_DV)),
            jnp.broadcast_to(kd[:, :, None], (RET_HEADS, n_pad, RET_DK)),
            jnp.broadcast_to(cd[:, None, None], (RET_HEADS, 1, RET_DV)))


def rotary_tables(pos, reps):
    inv_freq = ROPE_BASE ** (-jnp.arange(0, RET_DK, 2, dtype=F32) / RET_DK)
    ang = pos.astype(F32)[:, None] * inv_freq[None, :]
    cos = jnp.repeat(jnp.cos(ang), 2, axis=1)
    sin = jnp.sin(ang)
    sin_signed = jnp.stack([-sin, sin], axis=-1).reshape(ang.shape[0], RET_DK)
    return jnp.tile(cos, (1, reps)), jnp.tile(sin_signed, (1, reps))


def _pool_body(x_ref, xp_ref, buf_ref, sh_ref, sc_ref, gate_ref, w_ref, cs_ref, o_ref, tail_ref, ext_scr,
               *, tm, pos0, has_prev):
    m = pl.program_id(1)
    sh, sc = sh_ref[0], sc_ref[0]
    h = _modulate(x_ref[0], sh, sc)

    @pl.when(m == 0)
    def _():
        ext_scr[0:POOL_HALO, :] = buf_ref[0]

    if has_prev:

        @pl.when(m > 0)
        def _():
            ext_scr[0:POOL_HALO, :] = _modulate(xp_ref[0], sh, sc)

    ext_scr[POOL_HALO:POOL_HALO + tm, :] = h
    tail_ref[0] = ext_scr[tm:tm + POOL_HALO, :]
    row = lax.broadcasted_iota(jnp.int32, (tm, 1), 0)
    pos1 = (pos0 + m * tm + row + 1).astype(F32)
    rows = max(tm, 16)
    ys = []
    for gi, w in enumerate(POOL_WINDOWS):
        c0, c1 = gi * POOL_GW, (gi + 1) * POOL_GW
        win = ext_scr[POOL_HALO:POOL_HALO + tm, c0:c1]
        for j in range(1, w):
            win = win + ext_scr[POOL_HALO - j:POOL_HALO - j + tm, c0:c1]
        d = win / jnp.minimum(jnp.float32(w), pos1) - h[:, c0:c1]
        if rows != tm:
            d = jnp.concatenate([d, jnp.zeros((rows - tm, POOL_GW), F32)], axis=0)
        y = jnp.dot(d.astype(BF16), w_ref[0, gi].astype(BF16), preferred_element_type=F32)
        ys.append(y[0:tm])
    y = jnp.concatenate(ys, axis=1) * cs_ref[...]
    o_ref[0] = x_ref[0] + gate_ref[0] * y


def pool_layer(x3, buf16, mod, gate, pool_w, pool_scale, wl, pos0, *, tm):
    B, L, D = x3.shape
    assert L % tm == 0 and (L == tm or tm % POOL_HALO == 0)
    has_prev = L > tm
    ph = POOL_HALO if has_prev else min(L, POOL_HALO)
    per = tm // POOL_HALO if has_prev else 1
    vec = lambda: pl.BlockSpec((1, 1, D), lambda b, m: (b, 0, 0))
    return pl.pallas_call(
        functools.partial(_pool_body, tm=tm, pos0=pos0, has_prev=has_prev),
        grid=(B, L // tm),
        in_specs=[
            pl.BlockSpec((1, tm, D), lambda b, m: (b, m, 0)),
            pl.BlockSpec((1, ph, D), lambda b, m: (b, jnp.maximum(m * per - 1, 0), 0)),
            pl.BlockSpec((1, POOL_HALO, D), lambda b, m: (b, 0, 0)),
            vec(), vec(), vec(),
            pl.BlockSpec((1,) + pool_w.shape[1:], lambda b, m: (wl, 0, 0, 0)),
            pl.BlockSpec((1, D), lambda b, m: (wl, 0)),
        ],
        out_specs=[
            pl.BlockSpec((1, tm, D), lambda b, m: (b, m, 0)),
            pl.BlockSpec((1, POOL_HALO, D), lambda b, m: (b, 0, 0)),
        ],
        out_shape=[
            jax.ShapeDtypeStruct((B, L, D), F32),
            jax.ShapeDtypeStruct((B, POOL_HALO, D), F32),
        ],
        scratch_shapes=[pltpu.VMEM((tm + POOL_HALO, D), F32)],
        compiler_params=_cparams(2),
        name="pool_layer",
    )(x3, x3, buf16, mod[0], mod[1], gate, pool_w, pool_scale)


def _split3(x):
    p0 = x.astype(BF16)
    r1 = x - p0.astype(F32)
    p1 = r1.astype(BF16)
    p2 = (r1 - p1.astype(F32)).astype(BF16)
    return p0, p1, p2


def _lane_cumsum(x, tri):
    p0, p1, p2 = _split3(x)
    dot = lambda p: jnp.dot(p, tri, preferred_element_type=F32)
    return (dot(p0) + dot(p1)) + dot(p2)


def _upper_tri(t):
    r = lax.broadcasted_iota(jnp.int32, (t, t), 0)
    c = lax.broadcasted_iota(jnp.int32, (t, t), 1)
    return jnp.where(r <= c, 1.0, 0.0).astype(BF16)


BIAS_PIECES = 3


def _fbias_body(lf_ref, o_ref, carry_scr, *, tc):
    @pl.when(pl.program_id(1) == 0)
    def _():
        carry_scr[...] = jnp.zeros_like(carry_scr)

    r = lax.broadcasted_iota(jnp.int32, (tc, tc), 0)
    c = lax.broadcasted_iota(jnp.int32, (tc, tc), 1)
    tril = jnp.where(c <= r, 1.0, 0.0).astype(BF16)
    p0, p1, p2 = _split3(lf_ref[0])
    dot = lambda p: jnp.dot(tril, p, preferred_element_type=F32)
    f = carry_scr[...] + ((dot(p0) + dot(p1)) + dot(p2))
    carry_scr[...] = f[tc - 1:tc, :]
    head = lax.broadcasted_iota(jnp.int32, (FOX_HEADS, LANES), 0)
    lane = lax.broadcasted_iota(jnp.int32, (FOX_HEADS, LANES), 1)
    out = None
    for p, piece in enumerate(_split3(-f)):
        place = jnp.where(lane == BIAS_PIECES * head + p, 1.0, 0.0).astype(BF16)
        term = jnp.dot(piece, place, preferred_element_type=F32)
        out = term if out is None else out + term
    o_ref[0] = out.astype(BF16)


def fox_bias_features(lf, *, tc=512):
    B, L, H = lf.shape
    tc = min(tc, L)
    return pl.pallas_call(
        functools.partial(_fbias_body, tc=tc),
        grid=(B, L // tc),
        in_specs=[pl.BlockSpec((1, tc, H), lambda b, c: (b, c, 0))],
        out_specs=pl.BlockSpec((1, tc, LANES), lambda b, c: (b, c, 0)),
        out_shape=jax.ShapeDtypeStruct((B, L, LANES), BF16),
        scratch_shapes=[pltpu.VMEM((1, H), F32)],
        compiler_params=_cparams(2),
        name="fox_bias_features",
    )(lf)


DEN_ROWS = 16


def _flash_body(qt_ref, k_ref, fb_ref, vt_ref, o_ref, *, tq, tk):
    hp = pl.program_id(1)
    qi = pl.program_id(2)
    pair = 2 * FOX_DH
    row = lax.broadcasted_iota(jnp.int32, (pair, tq), 0)
    qt = qt_ref[0]
    qaug = []
    for i in range(2):
        q_head = jnp.where(row // FOX_DH == i, qt, jnp.zeros_like(qt))
        pick = jnp.where(row // BIAS_PIECES == 2 * hp + i, 1.0, 0.0).astype(BF16)
        qaug.append(jnp.concatenate([q_head, pick], axis=0))
    key_i = lax.broadcasted_iota(jnp.int32, (tk, tq), 0)
    qry_i = lax.broadcasted_iota(jnp.int32, (tk, tq), 1)
    per_q = tq // tk

    def step(j, carry, diag):
        k0 = pl.multiple_of(j * tk, tk)
        kaug = jnp.concatenate([k_ref[0, pl.ds(k0, tk), :], fb_ref[0, pl.ds(k0, tk), :]], axis=1)
        ones = jnp.ones((DEN_ROWS, tk), BF16)
        new = []
        for i in range(2):
            m_old, acc = carry[i]
            st = jnp.dot(kaug, qaug[i], preferred_element_type=F32)
            if diag is not None:
                st = jnp.where(key_i + diag * tk <= qry_i, st, NEG_BIG)
            m_new = jnp.maximum(m_old, jnp.max(st, axis=0, keepdims=True))
            alpha = jnp.exp(m_old - m_new)
            p = jnp.exp(st - m_new).astype(BF16)
            vt = jnp.concatenate([vt_ref[0, i * FOX_DH:(i + 1) * FOX_DH, pl.ds(k0, tk)].astype(BF16), ones], axis=0)
            acc = alpha * acc + jnp.dot(vt, p, preferred_element_type=F32)
            new.append((m_new, acc))
        return tuple(new)

    init = tuple((jnp.full((1, tq), NEG_BIG, F32), jnp.zeros((FOX_DH + DEN_ROWS, tq), F32)) for _ in range(2))
    carry = lax.fori_loop(0, qi * per_q, lambda j, c: step(j, c, None), init)
    for d in range(per_q):
        carry = step(qi * per_q + d, carry, d)
    o_ref[0] = jnp.concatenate([(acc[:FOX_DH] / acc[FOX_DH:FOX_DH + 1]).T for _, acc in carry],
                               axis=1).astype(o_ref.dtype)


def fox_flash(qt, k, fb, vt, *, tq=1024, tk=1024):
    B, L, D = k.shape
    tq = min(tq, L)
    tk = min(tk, tq)
    assert L % tq == 0 and tq % tk == 0
    pair = 2 * FOX_DH
    return pl.pallas_call(
        functools.partial(_flash_body, tq=tq, tk=tk),
        grid=(B, FOX_HEADS // 2, L // tq),
        in_specs=[
            pl.BlockSpec((1, pair, tq), lambda b, hp, qi: (b, hp, qi)),
            pl.BlockSpec((1, L, pair), lambda b, hp, qi: (b, 0, hp)),
            pl.BlockSpec((1, L, LANES), lambda b, hp, qi: (b, 0, 0)),
            pl.BlockSpec((1, pair, L), lambda b, hp, qi: (b, hp, 0)),
        ],
        out_specs=pl.BlockSpec((1, tq, pair), lambda b, hp, qi: (b, qi, hp)),
        out_shape=jax.ShapeDtypeStruct((B, L, D), BF16),
        compiler_params=_cparams(3),
        name="fox_flash",
    )(qt, k, fb, vt)


MAX_DECODE_PAGES = 16


def _decode_body(pt_ref, q_ref, kn_ref, vn_ref, lfn_ref, *refs, n_steps, lq, pages):
    kc = refs[0:pages]
    vc = refs[pages:2 * pages]
    lc = refs[2 * pages:3 * pages]
    o_ref, qbd_scr, m_scr, l_scr, acc_scr, carry_scr = refs[3 * pages:]
    b = pl.program_id(0)
    st = pl.program_id(1)
    rows = lq * FOX_HEADS
    last = (((1,), (1,)), ((), ()))
    tri = _upper_tri(PAGE_SIZE)

    @pl.when(st == 0)
    def _():
        head = lax.broadcasted_iota(jnp.int32, (FOX_HEADS, D_MODEL), 0)
        lane_head = lax.broadcasted_iota(jnp.int32, (FOX_HEADS, D_MODEL), 1) // FOX_DH
        blocks = [jnp.where(head == lane_head, jnp.broadcast_to(q_ref[0, t:t + 1, :], (FOX_HEADS, D_MODEL)), 0.0)
                  for t in range(lq)]
        qbd_scr[...] = jnp.concatenate(blocks, axis=0).astype(BF16)
        m_scr[...] = jnp.full_like(m_scr, NEG_BIG)
        l_scr[...] = jnp.zeros_like(l_scr)
        acc_scr[...] = jnp.zeros_like(acc_scr)
        carry_scr[...] = jnp.zeros_like(carry_scr)

    def absorb(s, v_mat, v_is_t):
        m_old = m_scr[...]
        m_new = jnp.maximum(m_old, jnp.max(s, axis=1, keepdims=True))
        alpha = jnp.exp(m_old - m_new)
        p = jnp.exp(s - m_new)
        l_scr[...] = alpha * l_scr[...] + jnp.sum(p, axis=1, keepdims=True)
        if v_is_t:
            pv = lax.dot_general(p.astype(BF16), v_mat, last, preferred_element_type=F32)
        else:
            pv = jnp.dot(p.astype(BF16), v_mat, preferred_element_type=F32)
        acc_scr[...] = alpha * acc_scr[...] + pv
        m_scr[...] = m_new

    @pl.when(st < n_steps)
    def _():
        kt = jnp.concatenate([kc[i][0].reshape(D_MODEL, PAGE_SIZE).astype(BF16) for i in range(pages)], axis=1)
        vt = jnp.concatenate([vc[i][0].reshape(D_MODEL, PAGE_SIZE).astype(BF16) for i in range(pages)], axis=1)
        within = _lane_cumsum(jnp.concatenate([lc[i][0] for i in range(pages)], axis=0), tri)
        f = carry_scr[...]
        biases = []
        for i in range(pages):
            f_page = f + within[i * FOX_HEADS:(i + 1) * FOX_HEADS, :]
            biases.append(jnp.tile(f_page, (lq, 1)))
            f = jnp.broadcast_to(f_page[:, PAGE_SIZE - 1:PAGE_SIZE], f_page.shape)
        carry_scr[...] = f
        s = jnp.dot(qbd_scr[...], kt, preferred_element_type=F32) - jnp.concatenate(biases, axis=1)
        absorb(s, vt, True)

    @pl.when(st == n_steps)
    def _():
        pad = jnp.zeros((PAGE_SIZE - lq, D_MODEL), F32)
        kn = jnp.concatenate([kn_ref[0], pad], axis=0).astype(BF16)
        vn = jnp.concatenate([vn_ref[0], pad], axis=0).astype(BF16)
        n_tok = lfn_ref.shape[1]
        tok = lax.broadcasted_iota(jnp.int32, (n_tok, PAGE_SIZE), 0)
        key = lax.broadcasted_iota(jnp.int32, (n_tok, PAGE_SIZE), 1)
        sel = jnp.where((tok // lq == b) & (tok % lq <= key) & (key < lq), 1.0, 0.0).astype(BF16)
        p0, p1, p2 = _split3(lfn_ref[...])
        dot = lambda p: jnp.dot(p, sel, preferred_element_type=F32)
        f = carry_scr[...] + ((dot(p0) + dot(p1)) + dot(p2))
        s = lax.dot_general(qbd_scr[...], kn, last, preferred_element_type=F32) - jnp.tile(f, (lq, 1))
        rq = lax.broadcasted_iota(jnp.int32, (rows, PAGE_SIZE), 0) // FOX_HEADS
        kk = lax.broadcasted_iota(jnp.int32, (rows, PAGE_SIZE), 1)
        s = jnp.where(kk <= rq, s, NEG_BIG)
        absorb(s, vn, False)
        o = acc_scr[...] / l_scr[...]
        head = lax.broadcasted_iota(jnp.int32, (FOX_HEADS, D_MODEL), 0)
        lane_head = lax.broadcasted_iota(jnp.int32, (FOX_HEADS, D_MODEL), 1) // FOX_DH
        outs = [jnp.sum(jnp.where(head == lane_head, o[t * FOX_HEADS:(t + 1) * FOX_HEADS, :], 0.0), axis=0,
                        keepdims=True) for t in range(lq)]
        o_ref[0] = jnp.concatenate(outs, axis=0).astype(o_ref.dtype)


def fox_decode(q, k_new, v_new, lft_new, cache_kt, cache_vt, cache_lt, page_table):
    B, lq, D = q.shape
    n_pages = page_table.shape[1]
    pages = math.gcd(n_pages, MAX_DECODE_PAGES)
    n_steps = n_pages // pages
    rows = lq * FOX_HEADS

    def page_idx(i):
        return lambda b, s, pt: (pt[b * n_pages + jnp.minimum(s, n_steps - 1) * pages + i], 0, 0, 0)

    def page_idx3(i):
        return lambda b, s, pt: (pt[b * n_pages + jnp.minimum(s, n_steps - 1) * pages + i], 0, 0)

    seq = lambda: pl.BlockSpec((1, lq, D), lambda b, s, pt: (b, 0, 0))
    in_specs = [seq(), seq(), seq(), pl.BlockSpec(lft_new.shape, lambda b, s, pt: (0, 0))]
    in_specs += [pl.BlockSpec((1, FOX_HEADS, FOX_DH, PAGE_SIZE), page_idx(i)) for i in range(pages)]
    in_specs += [pl.BlockSpec((1, FOX_HEADS, FOX_DH, PAGE_SIZE), page_idx(i)) for i in range(pages)]
    in_specs += [pl.BlockSpec((1, FOX_HEADS, PAGE_SIZE), page_idx3(i)) for i in range(pages)]
    grid_spec = pltpu.PrefetchScalarGridSpec(
        num_scalar_prefetch=1,
        grid=(B, n_steps + 1),
        in_specs=in_specs,
        out_specs=pl.BlockSpec((1, lq, D), lambda b, s, pt: (b, 0, 0)),
        scratch_shapes=[
            pltpu.VMEM((rows, D), BF16),
            pltpu.VMEM((rows, 1), F32),
            pltpu.VMEM((rows, 1), F32),
            pltpu.VMEM((rows, D), F32),
            pltpu.VMEM((FOX_HEADS, PAGE_SIZE), F32),
        ],
    )
    return pl.pallas_call(
        functools.partial(_decode_body, n_steps=n_steps, lq=lq, pages=pages),
        grid_spec=grid_spec,
        out_shape=jax.ShapeDtypeStruct((B, lq, D), F32),
        compiler_params=_cparams(2),
        name="fox_decode",
    )(page_table.reshape(-1), q, k_new, v_new, lft_new,
      *([cache_kt] * pages), *([cache_vt] * pages), *([cache_lt] * pages))


def _final_body(x_ref, g_ref, o_ref):
    xf = x_ref[0]
    o_ref[0] = xf * lax.rsqrt(jnp.mean(xf * xf, axis=-1, keepdims=True) + EPS) * g_ref[...]


def final_norm(x3, final_g, *, tm=None):
    bx, L, D = x3.shape
    tm = tm or min(L, 1024)
    return pl.pallas_call(
        _final_body,
        grid=(bx, L // tm),
        in_specs=[pl.BlockSpec((1, tm, D), lambda b, m: (b, m, 0)), pl.BlockSpec((1, D), lambda b, m: (0, 0))],
        out_specs=pl.BlockSpec((1, tm, D), lambda b, m: (b, m, 0)),
        out_shape=jax.ShapeDtypeStruct((bx, L, D), F32),
        compiler_params=_cparams(2),
        name="final_norm",
    )(x3, final_g.reshape(1, D))


def _trunk(x, mods, ret_state, pool_state, fox_past, pos0, params):
    (ret_w_in, ret_w_out, pool_w, pool_scale, fox_wt, fox_b_f, fox_w_out, ffn_w_gu, ffn_w_down,
     w_rt_pad, b_rt_pad, moe_w_gu, moe_w_down, final_g) = params
    B, L, D = x.shape
    decode = fox_past is not None
    if decode:
        x3 = x.reshape(1, B * L, D)
        expand = lambda v: jnp.repeat(v, L, axis=0)[None]
    else:
        x3 = x
        expand = lambda v: v[:, None, :]
    n_rows = x3.shape[1]
    pos = pos0 + jnp.arange(L)
    cos, sin = rotary_tables(pos, RET_HEADS)
    if decode:
        cos, sin = jnp.tile(cos, (B, 1)), jnp.tile(sin, (B, 1))
    chunk = min(L, 256)
    chunk_pad = max(chunk, PAGE_SIZE)
    tables = retention_tables(chunk, chunk_pad)
    ret_new, extras = [], {}
    moe_buf = None
    for i in range(DEPTH):
        sh_a, sc_a, g_a, sh_f, sc_f, g_f = [expand(v) for v in jnp.split(mods[i], 6, axis=-1)]
        kind, j = i % N_MIXERS, i // N_MIXERS
        if kind == 0:
            proj = functools.partial(mm, x3, ret_w_in, j, mod=(sh_a, sc_a))
            q = proj(n0=0, n_out=RET_QK, out_dtype=BF16, epi="rot", rot=(cos, sin), name="ret_q")
            k = proj(n0=RET_QK, n_out=RET_QK, out_dtype=BF16, epi="rot", rot=(cos, sin), scale=RET_DK ** -0.5,
                     name="ret_k")
            v = proj(n0=2 * RET_QK, n_out=RET_V, out_dtype=BF16, name="ret_v")
            g = proj(n0=2 * RET_QK + RET_V, n_out=RET_V, out_dtype=F32, name="ret_g")
            if decode:
                padr = lambda t: jnp.pad(t.reshape(B, L, -1), ((0, 0), (0, chunk_pad - L), (0, 0)))
                o, s = retention_scan(padr(q), padr(k), padr(v), padr(g), tables, chunk_pad, s0=ret_state,
                                      s0_layer=j, heads=RET_HEADS)
                o = o[:, :L].reshape(1, n_rows, RET_V)
            else:
                o, s = retention_scan(q, k, v, g, tables, chunk_pad)
            ret_new.append(s)
            x3 = mm(o, ret_w_out, j, n0=0, n_out=D, out_dtype=F32, epi="res", res=(x3, g_a), name="ret_out")
        elif kind == 1:
            vecs = [v[:, None, :] for v in jnp.split(mods[i], 6, axis=-1)[:3]]
            if decode:
                buf16 = jnp.pad(pool_state[j], ((0, 0), (1, 0), (0, 0)))
                tm = L
            else:
                buf16 = jnp.zeros((B, POOL_HALO, D), F32)
                tm = min(L, 1024)
            xn, tail = pool_layer(x3.reshape(B, L, D), buf16, (vecs[0], vecs[1]), vecs[2], pool_w, pool_scale, j,
                                  pos0, tm=tm)
            x3 = xn.reshape(x3.shape)
            extras["pool"] = tail[:, 1:, :]
        else:
            fproj = functools.partial(mm, x3, fox_wt, j, mod=(sh_a, sc_a), w_t=True)
            if decode:
                ck, cv, cl, pt = fox_past
                q = fproj(n0=0, n_out=D, out_dtype=F32, scale=FOX_DH ** -0.5, name="fox_q")
                lft = fproj(n0=3 * D, n_out=FOX_HEADS, out_dtype=F32, out_t=True, epi="logsig",
                            bias=fox_b_f[j].reshape(FOX_HEADS, 1), name="fox_logft")
                k = fproj(n0=D, n_out=D, out_dtype=F32, name="fox_k")
                v = fproj(n0=2 * D, n_out=D, out_dtype=F32, name="fox_v")
                o = fox_decode(q.reshape(B, L, D), k.reshape(B, L, D), v.reshape(B, L, D), lft[0],
                               jnp.transpose(ck[j], (0, 2, 3, 1)), jnp.transpose(cv[j], (0, 2, 3, 1)),
                               jnp.transpose(cl[j], (0, 2, 1)), pt)
                o = o.reshape(1, n_rows, D)
                extras["k"] = k.reshape(B, L, FOX_HEADS, FOX_DH)
                extras["v"] = v.reshape(B, L, FOX_HEADS, FOX_DH)
                extras["l"] = jnp.transpose(lft[0].reshape(FOX_HEADS, B, L), (1, 2, 0))
            else:
                qt = fproj(n0=0, n_out=D, out_dtype=BF16, out_t=True, scale=FOX_DH ** -0.5, name="fox_qt")
                kb = fproj(n0=D, n_out=D, out_dtype=BF16, name="fox_kb")
                kt = fproj(n0=D, n_out=D, out_dtype=F32, out_t=True, name="fox_kt")
                vt = fproj(n0=2 * D, n_out=D, out_dtype=F32, out_t=True, name="fox_vt")
                lf = fproj(n0=3 * D, n_out=FOX_HEADS, out_dtype=F32, epi="logsig",
                           bias=fox_b_f[j].reshape(1, FOX_HEADS), name="fox_logf")
                o = fox_flash(qt, kb, fox_bias_features(lf), vt)
                unt = lambda t: jnp.transpose(t.reshape(B, FOX_HEADS, FOX_DH, L), (0, 3, 1, 2))
                extras["k"], extras["v"] = unt(kt), unt(vt)
                extras["l"] = lf
            x3 = mm(o, fox_w_out, j, n0=0, n_out=D, out_dtype=F32, epi="res", res=(x3, g_a), name="fox_out")
        ml = i // 2
        if i % 2 == 0:
            x3 = ffn(x3, (sh_f, sc_f), g_f, ffn_w_gu, ffn_w_down, ml, name="ffn_dense")
        else:
            gates = router(x3, (sh_f, sc_f), w_rt_pad, b_rt_pad, ml)
            if decode:
                x3 = ffn(x3, (sh_f, sc_f), g_f, moe_w_gu, moe_w_down, ml * N_EXPERTS, gates=gates, name="ffn_moe")
            else:
                closing = final_g if i == DEPTH - 1 else None
                x3, moe_buf = moe_sparse(x3, (sh_f, sc_f), g_f, gates, moe_w_gu, moe_w_down, ml * N_EXPERTS, closing,
                                         moe_buf)
    fused_final = (not decode) and (DEPTH - 1) % 2 == 1
    out = (x3 if fused_final else final_norm(x3, final_g)).reshape(B, L, D)
    return (out, jnp.stack(ret_new), extras["pool"][None], extras["k"][None], extras["v"][None], extras["l"][None])


def kernel(x_prompt, x_sample, state_ret, state_pool, cache_fox_k, cache_fox_v, cache_fox_logf, page_table,
           c_prompt, c_sample, ada_w, ada_b, ret_w_in, ret_w_out, pool_w, pool_scale, fox_w_in, fox_b_f, fox_w_out,
           ffn_w_gu, ffn_w_down, moe_w_router, moe_b_router, moe_w_gu, moe_w_down, final_g):
    bp, bs = x_prompt.shape[0], x_sample.shape[0]
    rows = -(-(bp + bs) // 8) * 8
    c_all = jnp.concatenate([c_prompt, c_sample, jnp.zeros((rows - bp - bs, D_MODEL), F32)], axis=0)
    mods = ada_mods(c_all, ada_w, ada_b)
    n_moe = moe_w_router.shape[0]
    params = (
        ret_w_in, ret_w_out, pool_w, pool_scale,
        jnp.swapaxes(fox_w_in, 1, 2),
        fox_b_f, fox_w_out, ffn_w_gu, ffn_w_down,
        jnp.pad(jnp.swapaxes(moe_w_router, 1, 2), ((0, 0), (0, LANES - N_EXPERTS), (0, 0))),
        jnp.pad(moe_b_router, ((0, 0), (0, LANES - N_EXPERTS))).reshape(n_moe, 1, LANES),
        moe_w_gu.reshape((n_moe * N_EXPERTS,) + moe_w_gu.shape[2:]),
        moe_w_down.reshape((n_moe * N_EXPERTS,) + moe_w_down.shape[2:]),
        final_g,
    )
    y_p, ret_p, pool_p, k_p, v_p, l_p = _trunk(x_prompt, mods[:, :bp], None, None, None, 0, params)
    n_past = page_table.shape[1] * PAGE_SIZE
    y_s, ret_s, pool_s, k_s, v_s, l_s = _trunk(
        x_sample, mods[:, bp:bp + bs], state_ret, state_pool,
        (cache_fox_k, cache_fox_v, cache_fox_logf, page_table), n_past, params)
    return (y_p, y_s, ret_p, ret_s, pool_p, pool_s, k_p, k_s, v_p, v_s, l_p, l_s)
```
